```python
import math
import jax
import jax.numpy as jnp
from jax import lax
import numpy as np

D_MODEL = 1024
BATCH = 4
SEQ = 4096
DEPTH = 1

CTX_LEN = 256
GRID_W = 64
GDN_HEADS = 4
GDN_DK = 128
GDN_DV = 128
GDN_CONV = 5
GDN_CHUNK = 64
DIFF_HEADS = 4
DIFF_DK = 64
DIFF_DV = 2 * DIFF_DK
Q_BLOCK = 128
ROPE_THETA = 10000.0
D_MIX = GDN_HEADS * GDN_DV + DIFF_HEADS * DIFF_DV
GDN_QK = GDN_HEADS * GDN_DK
GDN_QKV = 2 * GDN_QK + GDN_HEADS * GDN_DV
GDN_Z = GDN_HEADS * GDN_DV
GDN_GATES = 2 * GDN_HEADS
GDN_COLS = GDN_QKV + GDN_Z + 2 * GDN_GATES
DIFF_QK = DIFF_HEADS * 2 * DIFF_DK
DIFF_COLS = 2 * DIFF_QK + DIFF_HEADS * DIFF_DV
IN_COLS = GDN_COLS + DIFF_COLS
MOE_GROUPS = 4
EXPERTS_PER_GROUP = 8
N_EXPERTS = MOE_GROUPS * EXPERTS_PER_GROUP
TOPK_IN_GROUP = 2
D_EXPERT = 512
EXPERT_BLOCK = 128
LN_EPS = 1e-5
NORM_EPS = 1e-6

kernel_name = 'hybrid_gdn_diffattn_hmoe_block'


def normalize(u):
    uf = u.astype(jnp.float32)
    mu = jnp.mean(uf, axis=-1, keepdims=True)
    var = jnp.mean(jnp.square(uf - mu), axis=-1, keepdims=True)
    return ((uf - mu) * lax.rsqrt(var + LN_EPS)).astype(u.dtype)


def layer_norm(u, g, b):
    return normalize(u) * g + b


def modulate(u, shift, scale):
    return normalize(u) * (1 + scale) + shift


def rms_norm(u, w):
    uf = u.astype(jnp.float32)
    y = uf * lax.rsqrt(jnp.mean(jnp.square(uf), axis=-1, keepdims=True) + NORM_EPS)
    return y.astype(w.dtype) * w


def l2_normalize(u):
    uf = u.astype(jnp.float32)
    return uf * lax.rsqrt(jnp.sum(jnp.square(uf), axis=-1, keepdims=True) + NORM_EPS)


def centred_dwconv(u, w):
    taps = w.shape[0]
    pad = taps // 2
    seq_len = u.shape[1]
    up = jnp.pad(u, ((0, 0), (pad, pad), (0, 0)))
    return sum(up[:, j:j + seq_len] * w[j] for j in range(taps))


def axial_rope(u):
    seq_len = u.shape[1]
    rows = seq_len // GRID_W
    row = jnp.repeat(jnp.arange(rows), GRID_W)
    col = jnp.tile(jnp.arange(GRID_W), rows)
    half = u.shape[-1] // 2
    inv_freq = ROPE_THETA ** (-jnp.arange(0, half, 2, dtype=jnp.float32) / half)

    def rot(v, pos):
        ang = pos.astype(jnp.float32)[:, None] * inv_freq
        ang = ang.reshape((1, seq_len) + (1,) * (v.ndim - 3) + (half // 2,))
        cos, sin = jnp.cos(ang).astype(v.dtype), jnp.sin(ang).astype(v.dtype)
        v1, v2 = jnp.split(v, 2, axis=-1)
        return jnp.concatenate([v1 * cos - v2 * sin, v2 * cos + v1 * sin], axis=-1)

    return jnp.concatenate([rot(u[..., :half], row), rot(u[..., half:], col)], axis=-1)


def gated_delta_chunked(q, k, v, g, beta, s0, with_out):
    f32 = jnp.float32
    bn, seq_len, heads, _ = q.shape
    dv = v.shape[-1]
    n_chunks = seq_len // GDN_CHUNK

    def chunks(t):
        t = t.astype(f32).reshape((bn, n_chunks, GDN_CHUNK) + t.shape[2:])
        return jnp.moveaxis(t, 3, 2)

    q, k, v, beta = chunks(q), chunks(k), chunks(v), chunks(beta)
    gc = jnp.cumsum(chunks(g), axis=-1)
    tri = jnp.tril(jnp.ones((GDN_CHUNK, GDN_CHUNK), dtype=bool))
    strict = jnp.tril(jnp.ones((GDN_CHUNK, GDN_CHUNK), dtype=bool), k=-1)
    decay = jnp.exp(jnp.where(tri, gc[..., :, None] - gc[..., None, :], -jnp.inf))
    k_beta = k * beta[..., None]
    a_kk = jnp.where(strict, jnp.einsum('bnhid,bnhjd->bnhij', k_beta, k) * decay, 0.0)
    eye = jnp.eye(GDN_CHUNK, dtype=f32)
    t_inv = lax.linalg.triangular_solve(a_kk + eye, jnp.broadcast_to(eye, a_kk.shape),
                                        left_side=True, lower=True, unit_diagonal=True)
    u = t_inv @ (v * beta[..., None])
    w = t_inv @ (k_beta * jnp.exp(gc)[..., None])
    g_last = gc[..., -1]
    k_tail = k * jnp.exp(g_last[..., None] - gc)[..., None]
    xs = [u, w, k_tail, g_last]
    if with_out:
        xs += [q * jnp.exp(gc)[..., None], jnp.einsum('bnhid,bnhjd->bnhij', q, k) * decay]
    xs = [jnp.moveaxis(t, 1, 0) for t in xs]

    def step(s, inp):
        v_new = inp[0] - inp[1] @ s
        s_next = s * jnp.exp(inp[3])[..., None, None] + jnp.einsum('bhcd,bhce->bhde', inp[2], v_new)
        o = inp[4] @ s + inp[5] @ v_new if with_out else None
        return s_next, o

    s_final, o = lax.scan(step, s0.astype(f32), xs)
    if with_out:
        o = jnp.moveaxis(jnp.moveaxis(o, 0, 1), 2, 3).reshape(bn, seq_len, heads, dv)
    return s_final, o


def gdn_inputs(p, conv_w, a_log, dt_bias):
    bn, seq_len, _ = p.shape
    qkv = jax.nn.silu(centred_dwconv(p[..., :GDN_QKV], conv_w))
    q = l2_normalize(qkv[..., :GDN_QK].reshape(bn, seq_len, GDN_HEADS, GDN_DK)) * GDN_DK ** -0.5
    k = l2_normalize(qkv[..., GDN_QK:2 * GDN_QK].reshape(bn, seq_len, GDN_HEADS, GDN_DK))
    v = qkv[..., 2 * GDN_QK:].reshape(bn, seq_len, GDN_HEADS, GDN_DV)
    off = GDN_QKV + GDN_Z
    beta = jax.nn.sigmoid(p[..., off:off + GDN_GATES].astype(jnp.float32)).reshape(bn, seq_len, 2, GDN_HEADS)
    a = p[..., off + GDN_GATES:GDN_COLS].astype(jnp.float32).reshape(bn, seq_len, 2, GDN_HEADS)
    g = -jnp.exp(a_log.astype(jnp.float32)) * jax.nn.softplus(a + dt_bias.astype(jnp.float32))
    return q, k, v, beta, g


def gdn_direction(q, k, v, beta, g, s0, reverse, with_out):
    if reverse:
        q, k, v, beta, g = (jnp.flip(t, axis=1) for t in (q, k, v, beta, g))
    s, o = gated_delta_chunked(q, k, v, g, beta, s0, with_out)
    if reverse and with_out:
        o = jnp.flip(o, axis=1)
    return s, o


def gated_out(o, z, norm_w):
    bn, seq_len = o.shape[:2]
    zh = z.reshape(bn, seq_len, GDN_HEADS, GDN_DV)
    return (rms_norm(o, norm_w) * jax.nn.silu(zh)).reshape(bn, seq_len, GDN_Z)


def gdn_group(p, pc, conv_w, a_log, dt_bias, norm_w, need_ctx):
    lat = gdn_inputs(p, conv_w, a_log, dt_bias)
    con = gdn_inputs(pc, conv_w, a_log, dt_bias)
    s0 = jnp.zeros((p.shape[0], GDN_HEADS, GDN_DK, GDN_DV), jnp.float32)
    outs_lat, outs_ctx = [], []
    for d, reverse in enumerate((False, True)):
        qc, kc, vc, bc, gc = con
        s_ctx, oc = gdn_direction(qc, kc, vc, bc[:, :, d], gc[:, :, d], s0, reverse, need_ctx)
        q, k, v, b, g = lat
        _, o = gdn_direction(q, k, v, b[:, :, d], g[:, :, d], s_ctx, reverse, True)
        outs_lat.append(o)
        outs_ctx.append(oc)
    y = gated_out(outs_lat[0] + outs_lat[1], p[..., GDN_QKV:GDN_QKV + GDN_Z], norm_w)
    yc = gated_out(outs_ctx[0] + outs_ctx[1], pc[..., GDN_QKV:GDN_QKV + GDN_Z], norm_w) if need_ctx else None
    return y, yc


def diff_attend(q, k, v, lam):
    bn, lq, heads = q.shape[:3]
    dv = v.shape[-1]
    nb = lq // Q_BLOCK
    qb = jnp.moveaxis(q.reshape((bn, nb, Q_BLOCK) + q.shape[2:]), 1, 0)
    scale = DIFF_DK ** -0.5

    def one(q_blk):
        s = jnp.einsum('bqhcd,bkhcd->bhcqk', q_blk, k).astype(jnp.float32) * scale
        pr = jax.nn.softmax(s, axis=-1)
        a = pr[:, :, 0] - lam * pr[:, :, 1]
        return jnp.einsum('bhqk,bkhd->bqhd', a.astype(v.dtype), v)

    o = lax.map(one, qb)
    return jnp.moveaxis(o, 0, 1).reshape(bn, lq, heads, dv)


def diff_heads(u):
    bn, seq_len, _ = u.shape
    q = u[..., :DIFF_QK].reshape(bn, seq_len, DIFF_HEADS, 2, DIFF_DK)
    k = u[..., DIFF_QK:2 * DIFF_QK].reshape(bn, seq_len, DIFF_HEADS, 2, DIFF_DK)
    v = u[..., 2 * DIFF_QK:].reshape(bn, seq_len, DIFF_HEADS, DIFF_DV)
    return q, k, v


def diff_group(p, pc, lam, lam_init, norm_w, need_ctx):
    q, k, v = diff_heads(p)
    qc, kc, vc = diff_heads(pc)
    q, k = axial_rope(q), axial_rope(k)
    lf = lam.astype(jnp.float32)
    lam_full = jnp.exp(jnp.sum(lf[0] * lf[1])) - jnp.exp(jnp.sum(lf[2] * lf[3])) + lam_init
    bn, seq_len = p.shape[:2]
    o = diff_attend(q, jnp.concatenate([k, kc], axis=1), jnp.concatenate([v, vc], axis=1), lam_full)
    y = (rms_norm(o, norm_w) * (1 - lam_init)).reshape(bn, seq_len, DIFF_HEADS * DIFF_DV)
    if need_ctx:
        oc = diff_attend(qc, kc, vc, lam_full)
        yc = (rms_norm(oc, norm_w) * (1 - lam_init)).reshape(bn, pc.shape[1], DIFF_HEADS * DIFF_DV)
    else:
        yc = None
    return y, yc


def hybrid_mixer(h, hc, w_in, conv_w, a_log, dt_bias, gdn_norm_w, lam, lam_init, diff_norm_w, w_out, need_ctx):
    p = h @ w_in
    pc = hc @ w_in
    y_gdn, yc_gdn = gdn_group(p[..., :GDN_COLS], pc[..., :GDN_COLS], conv_w, a_log, dt_bias, gdn_norm_w, need_ctx)
    y_diff, yc_diff = diff_group(p[..., GDN_COLS:], pc[..., GDN_COLS:], lam, lam_init, diff_norm_w, need_ctx)
    y = jnp.concatenate([y_gdn, y_diff], axis=-1) @ w_out
    yc = jnp.concatenate([yc_gdn, yc_diff], axis=-1) @ w_out if need_ctx else None
    return y, yc


def routed_experts(t, expert, gate, w_gate, w_up, w_down):
    n_tok, d = t.shape
    n_assign = n_tok * TOPK_IN_GROUP
    e_flat = expert.reshape(-1)
    tok = jnp.repeat(jnp.arange(n_tok), TOPK_IN_GROUP)
    order = jnp.argsort(e_flat)
    e_s, tok_s, g_s = e_flat[order], tok[order], gate.reshape(-1)[order]
    counts = jnp.bincount(e_flat, length=N_EXPERTS)
    start = jnp.cumsum(counts) - counts
    padded = (counts + EXPERT_BLOCK - 1) // EXPERT_BLOCK * EXPERT_BLOCK
    p_end = jnp.cumsum(padded)
    p_start = p_end - padded
    dest = p_start[e_s] + (jnp.arange(n_assign) - start[e_s])
    n_blocks = (n_assign + EXPERT_BLOCK - 1) // EXPERT_BLOCK + N_EXPERTS
    buf = jnp.zeros((n_blocks * EXPERT_BLOCK, d), t.dtype).at[dest].set(t[tok_s])
    block_e = jnp.clip(jnp.searchsorted(p_end, jnp.arange(n_blocks) * EXPERT_BLOCK, side='right'), 0, N_EXPERTS - 1)

    def block_ffn(args):
        xb, e = args
        return (jax.nn.silu(xb @ w_gate[e]) * (xb @ w_up[e])) @ w_down[e]

    yb = lax.map(block_ffn, (buf.reshape(n_blocks, EXPERT_BLOCK, d), block_e))
    y = yb.reshape(n_blocks * EXPERT_BLOCK, d)[dest] * g_s[:, None]
    return jnp.zeros((n_tok, d), y.dtype).at[tok_s].add(y)


def hier_moe(h, w_rg, b_rg, w_re, b_re, w_gate, w_up, w_down):
    bn, seq_len, d = h.shape
    t = h.reshape(bn * seq_len, d)
    n_tok = t.shape[0]
    p_group = jax.nn.softmax((t @ w_rg + b_rg).astype(jnp.float32), axis=-1)
    p_g, grp = lax.top_k(p_group, 1)
    logits_e = (t @ w_re + b_re).astype(jnp.float32).reshape(n_tok, MOE_GROUPS, EXPERTS_PER_GROUP)
    logits_in = jnp.take_along_axis(logits_e, grp[:, :, None], axis=1)[:, 0]
    p_e, local = lax.top_k(jax.nn.softmax(logits_in, axis=-1), TOPK_IN_GROUP)
    gate = p_g * p_e / jnp.sum(p_e, axis=-1, keepdims=True)
    expert = grp * EXPERTS_PER_GROUP + local
    y = routed_experts(t, expert, gate.astype(t.dtype), w_gate, w_up, w_down)
    return y.reshape(bn, seq_len, d)


def setup_inputs(seed: int = 0) -> dict:
    key = jax.random.key(seed)
    ks = jax.random.split(key, 26)
    f32 = jnp.float32

    def nrm(k, shape, s):
        return jax.random.normal(k, shape, f32) * s

    beta_dn = (8.0 * DEPTH) ** -0.25
    dt = jnp.exp(jax.random.uniform(ks[9], (DEPTH, 2, GDN_HEADS), f32, math.log(1e-3), math.log(1e-1)))
    return {
        'x': nrm(ks[0], (BATCH, SEQ, D_MODEL), 1.0),
        'c': nrm(ks[1], (BATCH, D_MODEL), 1.0),
        'ctx': nrm(ks[2], (BATCH, CTX_LEN, D_MODEL), 1.0),
        'c_ctx': nrm(ks[3], (D_MODEL,), 1.0),
        'w_ada': nrm(ks[4], (DEPTH, D_MODEL, 6 * D_MODEL), D_MODEL ** -0.5),
        'b_ada': nrm(ks[5], (DEPTH, 6 * D_MODEL), 0.01),
        'w_in': nrm(ks[6], (DEPTH, D_MODEL, IN_COLS), D_MODEL ** -0.5),
        'conv_w': nrm(ks[7], (DEPTH, GDN_CONV, GDN_QKV), GDN_CONV ** -0.5),
        'gdn_a_log': jnp.log(jax.random.uniform(ks[8], (DEPTH, 2, GDN_HEADS), f32, 1.0, 16.0)),
        'gdn_dt_bias': dt + jnp.log(-jnp.expm1(-dt)),
        'gdn_norm_w': 1.0 + nrm(ks[10], (DEPTH, GDN_DV), 0.02),
        'diff_lambda': nrm(ks[11], (DEPTH, 4, DIFF_DK), 0.1),
        'diff_norm_w': 1.0 + nrm(ks[12], (DEPTH, DIFF_DV), 0.02),
        'w_out': nrm(ks[13], (DEPTH, D_MIX, D_MODEL), D_MIX ** -0.5 * beta_dn),
        'ln1_g': 1.0 + nrm(ks[14], (DEPTH, D_MODEL), 0.02),
        'ln1_b': nrm(ks[15], (DEPTH, D_MODEL), 0.02),
        'w_router_group': nrm(ks[16], (DEPTH, D_MODEL, MOE_GROUPS), D_MODEL ** -0.5),
        'b_router_group': nrm(ks[17], (DEPTH, MOE_GROUPS), 0.01),
        'w_router_expert': nrm(ks[18], (DEPTH, D_MODEL, N_EXPERTS), D_MODEL ** -0.5),
        'b_router_expert': nrm(ks[19], (DEPTH, N_EXPERTS), 0.01),
        'w_expert_gate': nrm(ks[20], (DEPTH, N_EXPERTS, D_MODEL, D_EXPERT), D_MODEL ** -0.5),
        'w_expert_up': nrm(ks[21], (DEPTH, N_EXPERTS, D_MODEL, D_EXPERT), D_MODEL ** -0.5),
        'w_expert_down': nrm(ks[22], (DEPTH, N_EXPERTS, D_EXPERT, D_MODEL), D_EXPERT ** -0.5 * beta_dn),
        'ln2_g': 1.0 + nrm(ks[23], (DEPTH, D_MODEL), 0.02),
        'ln2_b': nrm(ks[24], (DEPTH, D_MODEL), 0.02),
    }


def reference(x, c, ctx, c_ctx, w_ada, b_ada, w_in, conv_w, gdn_a_log, gdn_dt_bias, gdn_norm_w,
              diff_lambda, diff_norm_w, w_out, ln1_g, ln1_b, w_router_group, b_router_group,
              w_router_expert, b_router_expert, w_expert_gate, w_expert_up, w_expert_down, ln2_g, ln2_b):
    alpha = (2.0 * DEPTH) ** 0.25
    for i in range(DEPTH):
        need_ctx = i < DEPTH - 1
        mod = jax.nn.silu(c) @ w_ada[i] + b_ada[i]
        mod_c = jax.nn.silu(c_ctx) @ w_ada[i] + b_ada[i]
        sh1, sc1, g1, sh2, sc2, g2 = jnp.split(mod[:, None, :], 6, axis=-1)
        csh1, csc1, cg1, csh2, csc2, cg2 = jnp.split(mod_c, 6, axis=-1)
        lam_init = 0.8 - 0.6 * math.exp(-0.3 * i)
        y, yc = hybrid_mixer(modulate(x, sh1, sc1), modulate(ctx, csh1, csc1), w_in[i], conv_w[i],
                             gdn_a_log[i], gdn_dt_bias[i], gdn_norm_w[i], diff_lambda[i], lam_init,
                             diff_norm_w[i], w_out[i], need_ctx)
        moe_w = (w_router_group[i], b_router_group[i], w_router_expert[i], b_router_expert[i],
                 w_expert_gate[i], w_expert_up[i], w_expert_down[i])
        x = layer_norm(alpha * x + g1 * y, ln1_g[i], ln1_b[i])
        x = layer_norm(alpha * x + g2 * hier_moe(modulate(x, sh2, sc2), *moe_w), ln2_g[i], ln2_b[i])
        if need_ctx:
            ctx = layer_norm(alpha * ctx + cg1 * yc, ln1_g[i], ln1_b[i])
            ctx = layer_norm(alpha * ctx + cg2 * hier_moe(modulate(ctx, csh2, csc2), *moe_w), ln2_g[i], ln2_b[i])
    return x
```

```python
import functools
import math

import numpy as np
import jax
import jax.numpy as jnp
from jax import lax
from jax.experimental import pallas as pl
from jax.experimental.pallas import tpu as pltpu

F32 = jnp.float32
BF16 = jnp.bfloat16
HIGHEST = lax.Precision.HIGHEST

D = 1024
GRID_W = 64
ROPE_THETA = 10000.0
GH = 4
GDK = 128
GDV = 128
GCONV = 5
DH = 4
DDK = 64
DDV = 128
G_QK = GH * GDK
G_QKV = 2 * G_QK + GH * GDV
G_Z = GH * GDV
G_GATES = 2 * GH
G_COLS = G_QKV + G_Z + 2 * G_GATES
D_QK = DH * 2 * DDK
N_GROUPS = 4
E_PER_GROUP = 8
N_EXPERTS = N_GROUPS * E_PER_GROUP
D_EXPERT = 512
LN_EPS = 1e-5
NORM_EPS = 1e-6

LANES = 128
ROW_TILE = 256
CHUNK = 64
FFN_BLOCK = 256
VMEM_LIMIT = 56 * 1024 * 1024

C_QKV = 0
C_Z = C_QKV + G_QKV
C_GB = C_Z + G_Z
C_GA = C_GB + LANES
C_DQ = C_GA + LANES
C_DK = C_DQ + D_QK
C_DV = C_DK + D_QK
C_END = C_DV + DH * DDV


def _cparams(sem):
    return pltpu.CompilerParams(dimension_semantics=sem, vmem_limit_bytes=VMEM_LIMIT)


def _dot(a, b):
    return jnp.dot(a, b, preferred_element_type=F32)


def _dot_nt(a, b):
    return lax.dot_general(a, b, (((1,), (1,)), ((), ())), preferred_element_type=F32)


def _dot_tn(a, b):
    return lax.dot_general(a, b, (((0,), (0,)), ((), ())), preferred_element_type=F32)


def _normalize(x):
    mu = jnp.mean(x, axis=-1, keepdims=True)
    xc = x - mu
    var = jnp.mean(xc * xc, axis=-1, keepdims=True)
    return xc * lax.rsqrt(var + LN_EPS)


def _silu(x):
    return x * jax.nn.sigmoid(x)


def _split3(x):
    hi = x.astype(BF16)
    r = x - hi.astype(F32)
    mid = r.astype(BF16)
    lo = (r - mid.astype(F32)).astype(BF16)
    return hi, mid, lo


def _ada_kernel(c_ref, w_ref, b_ref, o_ref):
    s = _silu(c_ref[...])
    o_ref[...] = jnp.dot(s, w_ref[...], precision=HIGHEST, preferred_element_type=F32) + b_ref[...]


def _ada(cc, w_ada, b_ada):
    n = w_ada.shape[1]
    bn = 1024
    return pl.pallas_call(
        _ada_kernel,
        grid=(n // bn,),
        in_specs=[pl.BlockSpec((8, D), lambda j: (0, 0)),
                  pl.BlockSpec((D, bn), lambda j: (0, j)),
                  pl.BlockSpec((1, bn), lambda j: (0, j))],
        out_specs=pl.BlockSpec((8, bn), lambda j: (0, j)),
        out_shape=jax.ShapeDtypeStruct((8, n), F32),
        compiler_params=_cparams(("arbitrary",)),
        name="ada",
    )(cc, w_ada, b_ada.reshape(1, n))


def _rope(v, cos, sin, low_half):
    fwd = pltpu.roll(v, LANES - 16, axis=1)
    bwd = pltpu.roll(v, 16, axis=1)
    return v * cos + jnp.where(low_half, fwd, bwd) * sin


def _proj_kernel(x_ref, sh_ref, sc_ref, cos_ref, sin_ref, w_ref,
                 qkv_ref, z_ref, gt_ref, dq_ref, dk_ref, dv_ref):
    h = (_normalize(x_ref[0]) * (1.0 + sc_ref[0]) + sh_ref[0]).astype(BF16)
    qkv_ref[0] = _dot(h, w_ref[:, C_QKV:C_Z])
    z_ref[0] = _dot(h, w_ref[:, C_Z:C_GB])
    gt_ref[0] = _dot(h, w_ref[:, C_GB:C_DQ])
    cos = cos_ref[...]
    sin = sin_ref[...]
    lane = lax.broadcasted_iota(jnp.int32, cos.shape, 1)
    low_half = (lane % 32) < 16
    for j in range(D_QK // LANES):
        q = _dot(h, w_ref[:, C_DQ + j * LANES:C_DQ + (j + 1) * LANES])
        dq_ref[0, :, j * LANES:(j + 1) * LANES] = (_rope(q, cos, sin, low_half) * (DDK ** -0.5)).astype(BF16)
        k = _dot(h, w_ref[:, C_DK + j * LANES:C_DK + (j + 1) * LANES])
        dk_ref[0, :, j * LANES:(j + 1) * LANES] = _rope(k, cos, sin, low_half).astype(BF16)
    dv_ref[0] = _dot(h, w_ref[:, C_DV:C_END]).astype(BF16)


def _proj(xc, mod3, cos_t, sin_t, w_all, n_ctx_tiles):
    bsz, lc, _ = xc.shape
    nt = lc // ROW_TILE
    ctx_row = mod3.shape[0] - 1

    def mod_idx(col):
        return lambda b, i: (jnp.where(i < n_ctx_tiles, ctx_row, b), 0, col)

    def row_spec(width):
        return pl.BlockSpec((1, ROW_TILE, width), lambda b, i: (b, i, 0))

    outs = [(G_QKV, F32), (G_Z, F32), (2 * LANES, F32), (D_QK, BF16), (D_QK, BF16), (DH * DDV, BF16)]
    return pl.pallas_call(
        _proj_kernel,
        grid=(bsz, nt),
        in_specs=[row_spec(D),
                  pl.BlockSpec((1, 1, D), mod_idx(0)),
                  pl.BlockSpec((1, 1, D), mod_idx(1)),
                  pl.BlockSpec((ROW_TILE, LANES), lambda b, i: (i, 0)),
                  pl.BlockSpec((ROW_TILE, LANES), lambda b, i: (i, 0)),
                  pl.BlockSpec((D, C_END), lambda b, i: (0, 0))],
        out_specs=[row_spec(w) for w, _ in outs],
        out_shape=[jax.ShapeDtypeStruct((bsz, lc, w), dt) for w, dt in outs],
        compiler_params=_cparams(("arbitrary", "arbitrary")),
        name="proj",
    )(xc, mod3, mod3, cos_t, sin_t, w_all)


def _prep_kernel(n_ctx_tiles, nt, main_ref, prev_ref, next_ref, gt_ref, cw_ref, alog_ref, dtb_ref,
                 lo_ref, up_ref, q_ref, k_ref, v_ref, beta_ref, gc_ref, ext_ref):
    i = pl.program_id(1)
    halo = prev_ref.shape[1]
    has_prev = jnp.logical_and(i != 0, i != n_ctx_tiles)
    has_next = jnp.logical_and(i != n_ctx_tiles - 1, i != nt - 1)
    ext_ref[0:halo, :] = prev_ref[0] * has_prev.astype(F32)
    ext_ref[halo:halo + ROW_TILE, :] = main_ref[0]
    ext_ref[halo + ROW_TILE:, :] = next_ref[0] * has_next.astype(F32)
    pad = GCONV // 2
    acc = None
    for j in range(GCONV):
        start = halo - pad + j
        term = ext_ref[start:start + ROW_TILE, :] * cw_ref[j:j + 1, :]
        acc = term if acc is None else acc + term
    qkv = _silu(acc)
    for h in range(GH):
        q = qkv[:, h * GDK:(h + 1) * GDK]
        q_ref[0, :, h * GDK:(h + 1) * GDK] = q * (lax.rsqrt(jnp.sum(q * q, axis=-1, keepdims=True) + NORM_EPS)
                                                * (GDK ** -0.5))
        k = qkv[:, G_QK + h * GDK:G_QK + (h + 1) * GDK]
        k_ref[0, :, h * GDK:(h + 1) * GDK] = k * lax.rsqrt(jnp.sum(k * k, axis=-1, keepdims=True) + NORM_EPS)
    v_ref[0] = qkv[:, 2 * G_QK:]
    gt = gt_ref[0]
    beta_ref[0] = jax.nn.sigmoid(gt[:, :LANES])
    a = gt[:, LANES:] + dtb_ref[...]
    softplus = jnp.maximum(a, 0.0) + jnp.log(1.0 + jnp.exp(-jnp.abs(a)))
    g = -jnp.exp(alog_ref[...]) * softplus
    lo = lo_ref[...]
    up = up_ref[...]
    fwd = None
    bwd = None
    for part in _split3(g):
        f = _dot(lo, part)
        r = _dot(up, part)
        fwd = f if fwd is None else fwd + f
        bwd = r if bwd is None else bwd + r
    lane = lax.broadcasted_iota(jnp.int32, g.shape, 1)
    gc_ref[0] = jnp.where(lane < GH, fwd, bwd)


def _prep(qkv, gt, conv_w8, alog_v, dtb_v, tri_lo, tri_up, n_ctx_tiles):
    bsz, lc, _ = qkv.shape
    nt = lc // ROW_TILE
    halo = 8
    per = ROW_TILE // halo
    last = lc // halo - 1

    def row_spec(width):
        return pl.BlockSpec((1, ROW_TILE, width), lambda b, i: (b, i, 0))

    def const_spec(shape):
        return pl.BlockSpec(shape, lambda b, i: tuple(0 for _ in shape))

    outs = [(G_QK, F32), (G_QK, F32), (GH * GDV, F32), (LANES, F32), (LANES, F32)]
    return pl.pallas_call(
        functools.partial(_prep_kernel, n_ctx_tiles, nt),
        grid=(bsz, nt),
        in_specs=[row_spec(G_QKV),
                  pl.BlockSpec((1, halo, G_QKV), lambda b, i: (b, jnp.maximum(i * per - 1, 0), 0)),
                  pl.BlockSpec((1, halo, G_QKV), lambda b, i: (b, jnp.minimum((i + 1) * per, last), 0)),
                  row_spec(2 * LANES),
                  const_spec((8, G_QKV)), const_spec((1, LANES)), const_spec((1, LANES)),
                  const_spec((ROW_TILE, ROW_TILE)), const_spec((ROW_TILE, ROW_TILE))],
        out_specs=[row_spec(w) for w, _ in outs],
        out_shape=[jax.ShapeDtypeStruct((bsz, lc, w), dt) for w, dt in outs],
        scratch_shapes=[pltpu.VMEM((ROW_TILE + 2 * halo, G_QKV), F32)],
        compiler_params=_cparams(("arbitrary", "arbitrary")),
        name="prep",
    )(qkv, qkv, qkv, gt, conv_w8, alog_v, dtb_v, tri_lo, tri_up)


def _chunk_masks(n, rev):
    row = lax.broadcasted_iota(jnp.int32, (n, n), 0)
    col = lax.broadcasted_iota(jnp.int32, (n, n), 1)
    same = (row // CHUNK) == (col // CHUNK)
    incl = jnp.logical_and(same, (col >= row) if rev else (col <= row))
    strict = jnp.logical_and(same, (col > row) if rev else (col < row))
    return incl, strict


def _decay(gcol, grow, incl):
    return jnp.where(incl, jnp.exp(jnp.minimum(gcol - grow, 0.0)), 0.0)


def _chunk_kernel(k_ref, v_ref, beta_ref, gc_ref, uw_ref):
    n = k_ref.shape[1]
    gc = gc_ref[0]
    gct = gc.T
    beta = beta_ref[0]
    row = lax.broadcasted_iota(jnp.int32, (n, n), 0)
    col = lax.broadcasted_iota(jnp.int32, (n, n), 1)
    eye = (row == col).astype(F32)
    sizes = [2 ** j for j in range(1, int(math.log2(CHUNK)))]
    same = {r: (row // r) == (col // r) for r in sizes + [CHUNK]}
    for d in range(2):
        incl, strict = _chunk_masks(n, d == 1)
        for h in range(GH):
            c = d * GH + h
            k = k_ref[0, :, h * GDK:(h + 1) * GDK]
            v = v_ref[0, :, h * GDV:(h + 1) * GDV]
            bcol = beta[:, c:c + 1]
            gcol = gc[:, c:c + 1]
            grow = gct[c:c + 1, :]
            kb = k * bcol
            a = jnp.where(strict, _dot_nt(kb.astype(BF16), k.astype(BF16)) * _decay(gcol, grow, incl), 0.0)
            t = eye - jnp.where(same[2], a, 0.0)
            for r in sizes:
                off = jnp.where(jnp.logical_and(same[2 * r], jnp.logical_not(same[r])), a, 0.0).astype(BF16)
                t_b = t.astype(BF16)
                t = t - _dot(t_b, _dot(off, t_b).astype(BF16))
            x = jnp.concatenate([v * bcol, kb * jnp.exp(gcol)], axis=1).astype(BF16)
            uw_ref[0, c] = _dot(t.astype(BF16), x)


def _chunk(k, v, beta, gc):
    bsz, lc, _ = k.shape
    nt = lc // ROW_TILE

    def row_spec(width):
        return pl.BlockSpec((1, ROW_TILE, width), lambda b, i: (b, i, 0))

    return pl.pallas_call(
        _chunk_kernel,
        grid=(bsz, nt),
        in_specs=[row_spec(G_QK), row_spec(GH * GDV), row_spec(LANES), row_spec(LANES)],
        out_specs=pl.BlockSpec((1, 2 * GH, ROW_TILE, GDV + GDK), lambda b, i: (b, 0, i, 0)),
        out_shape=jax.ShapeDtypeStruct((bsz, 2 * GH, lc, GDV + GDK), F32),
        compiler_params=_cparams(("arbitrary", "arbitrary")),
        name="chunk",
    )(k, v, beta, gc)


def _scan_kernel(qf_ref, kf_ref, gf_ref, uwf_ref, qr_ref, kr_ref, gr_ref, uwr_ref,
                 of_ref, or_ref, s_ref):
    @pl.when(pl.program_id(1) == 0)
    def _():
        s_ref[...] = jnp.zeros_like(s_ref)

    n_chunks = ROW_TILE // CHUNK
    row = lax.broadcasted_iota(jnp.int32, (CHUNK, CHUNK), 0)
    col = lax.broadcasted_iota(jnp.int32, (CHUNK, CHUNK), 1)
    for d, (q_ref, k_ref, g_ref, uw_ref, o_ref) in enumerate(
            ((qf_ref, kf_ref, gf_ref, uwf_ref, of_ref), (qr_ref, kr_ref, gr_ref, uwr_ref, or_ref))):
        rev = d == 1
        incl = (col >= row) if rev else (col <= row)
        for step in range(n_chunks):
            ci = n_chunks - 1 - step if rev else step
            rows = slice(ci * CHUNK, (ci + 1) * CHUNK)
            gc = g_ref[0, rows, :]
            gct = gc.T
            last = 0 if rev else CHUNK - 1
            for h in range(GH):
                c = d * GH + h
                q = q_ref[0, rows, h * GDK:(h + 1) * GDK]
                k = k_ref[0, rows, h * GDK:(h + 1) * GDK]
                u = uw_ref[0, h, rows, :GDV]
                w = uw_ref[0, h, rows, GDV:]
                gcol = gc[:, c:c + 1]
                grow = gct[c:c + 1, :]
                g_last = gcol[last:last + 1, :]
                s = s_ref[c]
                ws = _dot(jnp.concatenate([w, q * jnp.exp(gcol)], axis=0).astype(BF16), s.astype(BF16))
                v_new = (u - ws[:CHUNK]).astype(BF16)
                a_qk = (_dot_nt(q.astype(BF16), k.astype(BF16)) * _decay(gcol, grow, incl)).astype(BF16)
                o_ref[0, rows, h * GDV:(h + 1) * GDV] = ws[CHUNK:] + _dot(a_qk, v_new)
                k_tail = (k * jnp.exp(g_last - gcol)).astype(BF16)
                s_ref[c] = s * jnp.exp(g_last) + _dot_tn(k_tail, v_new)


def _scan(q, k, gc, uw, n_ctx_tiles):
    bsz, lc, _ = q.shape
    nt = lc // ROW_TILE

    def fwd(b, i):
        return i

    def bwd(b, i):
        return jnp.where(i < n_ctx_tiles, n_ctx_tiles - 1 - i, nt - 1 - (i - n_ctx_tiles))

    def specs(tile):
        return [pl.BlockSpec((1, ROW_TILE, G_QK), lambda b, i: (b, tile(b, i), 0)),
                pl.BlockSpec((1, ROW_TILE, G_QK), lambda b, i: (b, tile(b, i), 0)),
                pl.BlockSpec((1, ROW_TILE, LANES), lambda b, i: (b, tile(b, i), 0))]

    def uw_spec(tile, d):
        return pl.BlockSpec((1, GH, ROW_TILE, GDV + GDK), lambda b, i: (b, d, tile(b, i), 0))

    def out_spec(tile):
        return pl.BlockSpec((1, ROW_TILE, GH * GDV), lambda b, i: (b, tile(b, i), 0))

    return pl.pallas_call(
        _scan_kernel,
        grid=(bsz, nt),
        in_specs=specs(fwd) + [uw_spec(fwd, 0)] + specs(bwd) + [uw_spec(bwd, 1)],
        out_specs=[out_spec(fwd), out_spec(bwd)],
        out_shape=[jax.ShapeDtypeStruct((bsz, lc, GH * GDV), F32)] * 2,
        scratch_shapes=[pltpu.VMEM((2 * GH, GDK, GDV), F32)],
        compiler_params=_cparams(("arbitrary", "arbitrary")),
        name="scan",
    )(q, k, gc, uw, q, k, gc, uw)


def _attn_kernel(lam_init, q_ref, k_ref, v_ref, lam_ref, nw_ref, o_ref):
    lam = lam_ref[...]
    lam_full = (jnp.exp(jnp.sum(lam[0:1] * lam[1:2], axis=-1, keepdims=True))
                - jnp.exp(jnp.sum(lam[2:3] * lam[3:4], axis=-1, keepdims=True)) + lam_init)
    q = q_ref[0]
    k = k_ref[0]
    v = v_ref[0]
    lane = lax.broadcasted_iota(jnp.int32, q.shape, 1)
    outs = []
    for comp in range(2):
        in_comp = (lane >= DDK) if comp else (lane < DDK)
        s = _dot_nt(jnp.where(in_comp, q, jnp.zeros_like(q)), k)
        p = jnp.exp(s - jnp.max(s, axis=-1, keepdims=True))
        outs.append(_dot(p.astype(BF16), v) / jnp.sum(p, axis=-1, keepdims=True))
    o = outs[0] - lam_full * outs[1]
    y = o * lax.rsqrt(jnp.mean(o * o, axis=-1, keepdims=True) + NORM_EPS)
    o_ref[0] = y * nw_ref[...] * (1.0 - lam_init)


def _attn(dq, dk, dv, lam, norm_w, lam_init, n_ctx_tiles, tq):
    bsz, lc, _ = dq.shape
    lq = lc - n_ctx_tiles * ROW_TILE
    q_off = n_ctx_tiles * ROW_TILE // tq
    return pl.pallas_call(
        functools.partial(_attn_kernel, lam_init),
        grid=(bsz, DH, lq // tq),
        in_specs=[pl.BlockSpec((1, tq, 2 * DDK), lambda b, h, i: (b, q_off + i, h)),
                  pl.BlockSpec((1, lc, 2 * DDK), lambda b, h, i: (b, 0, h)),
                  pl.BlockSpec((1, lc, DDV), lambda b, h, i: (b, 0, h)),
                  pl.BlockSpec((4, DDK), lambda b, h, i: (0, 0)),
                  pl.BlockSpec((1, DDV), lambda b, h, i: (0, 0))],
        out_specs=pl.BlockSpec((1, tq, DDV), lambda b, h, i: (b, i, h)),
        out_shape=jax.ShapeDtypeStruct((bsz, lq, DH * DDV), F32),
        compiler_params=_cparams(("arbitrary", "arbitrary", "arbitrary")),
        name="attn",
    )(dq, dk, dv, lam, norm_w.reshape(1, DDV))


def _out_kernel(alpha, of_ref, or_ref, z_ref, yd_ref, x_ref, g1_ref, sh2_ref, sc2_ref, gnw_ref,
                wo_ref, lng_ref, lnb_ref, wr_ref, br_ref, x1_ref, h2_ref, lg_ref):
    o = of_ref[0] + or_ref[0]
    z = z_ref[0]
    acc = _dot(yd_ref[0].astype(BF16), wo_ref[G_Z:, :])
    for h in range(GH):
        oh = o[:, h * GDV:(h + 1) * GDV]
        yh = (oh * lax.rsqrt(jnp.mean(oh * oh, axis=-1, keepdims=True) + NORM_EPS) * gnw_ref[...]
              * _silu(z[:, h * GDV:(h + 1) * GDV]))
        acc = acc + _dot(yh.astype(BF16), wo_ref[h * GDV:(h + 1) * GDV, :])
    x1 = _normalize(alpha * x_ref[0] + g1_ref[0] * acc) * lng_ref[...] + lnb_ref[...]
    x1_ref[0] = x1
    h2 = _normalize(x1) * (1.0 + sc2_ref[0]) + sh2_ref[0]
    h2_ref[0] = h2
    lg = br_ref[...]
    for part in _split3(h2):
        for wpart in range(3):
            lg = lg + _dot(part, wr_ref[wpart])
    lg_ref[0] = lg


def _out(o_f, o_r, z, y_diff, xc, mod3, gdn_norm_w, w_out, ln_g, ln_b, w_router3, b_router, alpha, n_ctx_tiles):
    bsz, lq, _ = y_diff.shape
    nt = lq // ROW_TILE

    def lat_spec(width):
        return pl.BlockSpec((1, ROW_TILE, width), lambda b, i: (b, i + n_ctx_tiles, 0))

    def row_spec(width):
        return pl.BlockSpec((1, ROW_TILE, width), lambda b, i: (b, i, 0))

    def mod_spec(col):
        return pl.BlockSpec((1, 1, D), lambda b, i: (b, 0, col))

    def const_spec(shape):
        return pl.BlockSpec(shape, lambda b, i: tuple(0 for _ in shape))

    return pl.pallas_call(
        functools.partial(_out_kernel, alpha),
        grid=(bsz, nt),
        in_specs=[lat_spec(GH * GDV), lat_spec(GH * GDV), lat_spec(G_Z), row_spec(DH * DDV), lat_spec(D),
                  mod_spec(2), mod_spec(3), mod_spec(4),
                  const_spec((1, GDV)), const_spec((G_Z + DH * DDV, D)),
                  const_spec((1, D)), const_spec((1, D)),
                  const_spec((3, D, LANES)), const_spec((1, LANES))],
        out_specs=[row_spec(D), row_spec(D), row_spec(LANES)],
        out_shape=[jax.ShapeDtypeStruct((bsz, lq, D), F32), jax.ShapeDtypeStruct((bsz, lq, D), F32),
                   jax.ShapeDtypeStruct((bsz, lq, LANES), F32)],
        compiler_params=_cparams(("arbitrary", "arbitrary")),
        name="out",
    )(o_f, o_r, z, y_diff, xc, mod3, mod3, mod3, gdn_norm_w.reshape(1, GDV), w_out,
      ln_g.reshape(1, D), ln_b.reshape(1, D), w_router3, b_router)


def _route_kernel(lg_ref, tri_ref, ids_ref, gate_ref, cnt_ref, run_ref):
    @pl.when(pl.program_id(0) == 0)
    def _():
        run_ref[...] = jnp.zeros_like(run_ref)

    lg = lg_ref[...]
    lane_i = lax.broadcasted_iota(jnp.int32, lg.shape, 1)
    lane = lane_i.astype(F32)
    neg = jnp.float32(-jnp.inf)

    def first_max(mask):
        masked = jnp.where(mask, lg, neg)
        m = jnp.max(masked, axis=-1, keepdims=True)
        idx = jnp.min(jnp.where(jnp.logical_and(mask, masked == m), lane, float(LANES)), axis=-1, keepdims=True)
        return m, idx

    is_group = lane_i < N_GROUPS
    m_g, grp = first_max(is_group)
    p_g = 1.0 / jnp.sum(jnp.where(is_group, jnp.exp(lg - m_g), 0.0), axis=-1, keepdims=True)
    e_lane = lane - float(N_GROUPS)
    lo_e = grp * float(E_PER_GROUP)
    in_group = jnp.logical_and(e_lane >= lo_e, e_lane < lo_e + float(E_PER_GROUP))
    l0, i0 = first_max(in_group)
    l1, i1 = first_max(jnp.logical_and(in_group, lane != i0))
    r = jnp.exp(l1 - l0)
    gate0 = p_g / (1.0 + r)
    gate1 = p_g * r / (1.0 + r)
    e0 = i0 - float(N_GROUPS)
    e1 = i1 - float(N_GROUPS)
    oh0 = (lane == e0).astype(F32)
    oh1 = (lane == e1).astype(F32)
    both = oh0 + oh1
    before = _dot(tri_ref[...], both.astype(BF16)) + run_ref[...]
    rank0 = jnp.sum(oh0 * before, axis=-1, keepdims=True)
    rank1 = jnp.sum(oh1 * before, axis=-1, keepdims=True)
    run_ref[...] = run_ref[...] + jnp.sum(both, axis=0, keepdims=True)
    cnt_ref[...] = run_ref[...]
    ids = jnp.where(lane_i == 0, e0, jnp.where(lane_i == 1, e1,
                                               jnp.where(lane_i == 2, rank0, jnp.where(lane_i == 3, rank1, 0.0))))
    ids_ref[...] = ids.astype(jnp.int32)
    gate_ref[...] = jnp.where(lane_i == 0, gate0, jnp.where(lane_i == 1, gate1, 0.0))


def _route(logits, tri_strict):
    t = logits.shape[0]
    tile = tri_strict.shape[0]
    return pl.pallas_call(
        _route_kernel,
        grid=(t // tile,),
        in_specs=[pl.BlockSpec((tile, LANES), lambda i: (i, 0)),
                  pl.BlockSpec((tile, tile), lambda i: (0, 0))],
        out_specs=[pl.BlockSpec((tile, LANES), lambda i: (i, 0)),
                   pl.BlockSpec((tile, LANES), lambda i: (i, 0)),
                   pl.BlockSpec((1, LANES), lambda i: (0, 0))],
        out_shape=[jax.ShapeDtypeStruct((t, LANES), jnp.int32), jax.ShapeDtypeStruct((t, LANES), F32),
                   jax.ShapeDtypeStruct((1, LANES), F32)],
        scratch_shapes=[pltpu.VMEM((1, LANES), F32)],
        compiler_params=_cparams(("arbitrary",)),
        name="route",
    )(logits, tri_strict)


DMA_TOKENS = 512


def _dispatch_kernel(dest_ref, h_hbm, zero_hbm, buf_hbm, sem):
    del zero_hbm
    base = pl.program_id(0) * DMA_TOKENS

    def copy(a):
        return pltpu.make_async_copy(h_hbm.at[pl.ds(base + a // 2, 1)], buf_hbm.at[pl.ds(dest_ref[a], 1)], sem)

    def start(a, carry):
        copy(a).start()
        return carry

    def wait(a, carry):
        copy(a).wait()
        return carry

    lax.fori_loop(0, 2 * DMA_TOKENS, start, 0)
    lax.fori_loop(0, 2 * DMA_TOKENS, wait, 0)


def _dispatch(dest_flat, h2, n_rows):
    t, d = h2.shape
    zero = jnp.zeros((n_rows, d), h2.dtype)
    return pl.pallas_call(
        _dispatch_kernel,
        grid=(t // DMA_TOKENS,),
        in_specs=[pl.BlockSpec((2 * DMA_TOKENS,), lambda i: (i,), memory_space=pltpu.SMEM),
                  pl.BlockSpec(memory_space=pl.ANY),
                  pl.BlockSpec(memory_space=pl.ANY)],
        out_specs=pl.BlockSpec(memory_space=pl.ANY),
        out_shape=jax.ShapeDtypeStruct((n_rows, d), h2.dtype),
        scratch_shapes=[pltpu.SemaphoreType.DMA(())],
        input_output_aliases={2: 0},
        compiler_params=_cparams(("arbitrary",)),
        name="dispatch",
    )(dest_flat, h2, zero)


def _combine_kernel(alpha, dest_ref, ys_hbm, gate_ref, x1_ref, g2_ref, lng_ref, lnb_ref, o_ref, r0_ref, r1_ref, sem):
    n = x1_ref.shape[0]

    def copies(tok):
        return (pltpu.make_async_copy(ys_hbm.at[pl.ds(dest_ref[2 * tok], 1)], r0_ref.at[pl.ds(tok, 1)], sem),
                pltpu.make_async_copy(ys_hbm.at[pl.ds(dest_ref[2 * tok + 1], 1)], r1_ref.at[pl.ds(tok, 1)], sem))

    def start(tok, carry):
        for cp in copies(tok):
            cp.start()
        return carry

    def wait(tok, carry):
        for cp in copies(tok):
            cp.wait()
        return carry

    lax.fori_loop(0, n, start, 0)
    lax.fori_loop(0, n, wait, 0)
    gate = gate_ref[...]
    y = gate[:, 0:1] * r0_ref[...] + gate[:, 1:2] * r1_ref[...]
    o_ref[...] = _normalize(alpha * x1_ref[...] + g2_ref[0] * y) * lng_ref[...] + lnb_ref[...]


def _combine(dest_flat, ys, gate, x1, mod3, ln_g, ln_b, alpha, rows_per_batch):
    t, d = x1.shape
    per_batch = rows_per_batch // DMA_TOKENS
    return pl.pallas_call(
        functools.partial(_combine_kernel, alpha),
        grid=(t // DMA_TOKENS,),
        in_specs=[pl.BlockSpec((2 * DMA_TOKENS,), lambda i: (i,), memory_space=pltpu.SMEM),
                  pl.BlockSpec(memory_space=pl.ANY),
                  pl.BlockSpec((DMA_TOKENS, LANES), lambda i: (i, 0)),
                  pl.BlockSpec((DMA_TOKENS, d), lambda i: (i, 0)),
                  pl.BlockSpec((1, 1, d), lambda i: (i // per_batch, 0, 5)),
                  pl.BlockSpec((1, d), lambda i: (0, 0)),
                  pl.BlockSpec((1, d), lambda i: (0, 0))],
        out_specs=pl.BlockSpec((DMA_TOKENS, d), lambda i: (i, 0)),
        out_shape=jax.ShapeDtypeStruct((t, d), F32),
        scratch_shapes=[pltpu.VMEM((DMA_TOKENS, d), F32), pltpu.VMEM((DMA_TOKENS, d), F32),
                        pltpu.SemaphoreType.DMA(())],
        compiler_params=_cparams(("arbitrary",)),
        name="combine",
    )(dest_flat, ys, gate, x1, mod3, ln_g.reshape(1, d), ln_b.reshape(1, d))


def _ffn_kernel(be_ref, x_ref, wg_ref, wu_ref, wd_ref, o_ref, wgb_ref, wub_ref, wdb_ref):
    i = pl.program_id(0)
    changed = jnp.logical_or(i == 0, be_ref[i] != be_ref[jnp.maximum(i - 1, 0)])

    @pl.when(changed)
    def _():
        wgb_ref[...] = wg_ref[0].astype(BF16)
        wub_ref[...] = wu_ref[0].astype(BF16)
        wdb_ref[...] = wd_ref[0].astype(BF16)

    x = x_ref[...].astype(BF16)
    hidden = _silu(_dot(x, wgb_ref[...])) * _dot(x, wub_ref[...])
    o_ref[...] = _dot(hidden.astype(BF16), wdb_ref[...])


def _ffn(block_e, buf, w_gate, w_up, w_down):
    n_rows, d = buf.shape
    n_blocks = n_rows // FFN_BLOCK
    grid_spec = pltpu.PrefetchScalarGridSpec(
        num_scalar_prefetch=1,
        grid=(n_blocks,),
        in_specs=[pl.BlockSpec((FFN_BLOCK, d), lambda i, be: (i, 0)),
                  pl.BlockSpec((1, d, D_EXPERT), lambda i, be: (be[i], 0, 0)),
                  pl.BlockSpec((1, d, D_EXPERT), lambda i, be: (be[i], 0, 0)),
                  pl.BlockSpec((1, D_EXPERT, d), lambda i, be: (be[i], 0, 0))],
        out_specs=pl.BlockSpec((FFN_BLOCK, d), lambda i, be: (i, 0)),
        scratch_shapes=[pltpu.VMEM((d, D_EXPERT), BF16), pltpu.VMEM((d, D_EXPERT), BF16),
                        pltpu.VMEM((D_EXPERT, d), BF16)],
    )
    return pl.pallas_call(
        _ffn_kernel,
        grid_spec=grid_spec,
        out_shape=jax.ShapeDtypeStruct((n_rows, d), F32),
        compiler_params=_cparams(("arbitrary",)),
        name="ffn",
    )(block_e, buf, w_gate, w_up, w_down)


def _rope_tables(seq_len, n_ctx_rows):
    lane = np.arange(LANES)
    within = lane % DDK
    freq = within % 16
    use_col = within >= DDK // 2
    sign = np.where((lane % 32) < 16, -1.0, 1.0).astype(np.float32)
    half = DDK // 2
    inv_freq = ROPE_THETA ** (-jnp.arange(0, half, 2, dtype=F32) / half)
    t = jnp.arange(seq_len)
    pos = jnp.where(jnp.asarray(use_col)[None, :], (t % GRID_W)[:, None], (t // GRID_W)[:, None]).astype(F32)
    ang = pos * inv_freq[jnp.asarray(freq)][None, :]
    cos = jnp.concatenate([jnp.ones((n_ctx_rows, LANES), F32), jnp.cos(ang)], axis=0)
    sin = jnp.concatenate([jnp.zeros((n_ctx_rows, LANES), F32), jnp.sin(ang) * jnp.asarray(sign)[None, :]], axis=0)
    return cos, sin


def _block_tri(n, block, upper):
    i = np.arange(n)
    same = (i[:, None] // block) == (i[None, :] // block)
    tri = (i[None, :] >= i[:, None]) if upper else (i[None, :] <= i[:, None])
    return jnp.asarray((same & tri).astype(np.float32), dtype=BF16)


def _pack_w_in(w_in):
    pad = jnp.zeros((D, LANES - G_GATES), w_in.dtype)
    off = G_QKV + G_Z
    return jnp.concatenate([
        w_in[:, :G_QKV], w_in[:, G_QKV:off],
        w_in[:, off:off + G_GATES], pad,
        w_in[:, off + G_GATES:G_COLS], pad,
        w_in[:, G_COLS:],
    ], axis=1).astype(BF16)


def kernel(x, c, ctx, c_ctx, w_ada, b_ada, w_in, conv_w, gdn_a_log, gdn_dt_bias, gdn_norm_w, diff_lambda,
           diff_norm_w, w_out, ln1_g, ln1_b, w_router_group, b_router_group, w_router_expert, b_router_expert,
           w_expert_gate, w_expert_up, w_expert_down, ln2_g, ln2_b):
    depth = w_ada.shape[0]
    assert depth == 1, "single-layer block: the context stream never feeds a later layer"
    bsz, seq, _ = x.shape
    n_ctx = ctx.shape[1]
    assert seq % ROW_TILE == 0 and n_ctx % ROW_TILE == 0 and seq % GRID_W == 0 and bsz < 8
    n_ctx_tiles = n_ctx // ROW_TILE
    alpha = (2.0 * depth) ** 0.25
    lam_init = 0.8 - 0.6 * math.exp(-0.3 * 0)
    i = 0

    cc = jnp.zeros((8, D), F32).at[:bsz].set(c).at[7].set(c_ctx)
    mod = _ada(cc, w_ada[i], b_ada[i])
    mod3 = mod.reshape(8, 1, 6 * D)

    xc = jnp.concatenate([ctx, x], axis=1)
    cos_t, sin_t = _rope_tables(seq, n_ctx)
    qkv, z, gt, dq, dk, dv = _proj(xc, mod3, cos_t, sin_t, _pack_w_in(w_in[i]), n_ctx_tiles)

    conv_w8 = jnp.zeros((8, G_QKV), F32).at[:GCONV].set(conv_w[i])
    alog_v = jnp.zeros((1, LANES), F32).at[0, :G_GATES].set(gdn_a_log[i].reshape(-1))
    dtb_v = jnp.zeros((1, LANES), F32).at[0, :G_GATES].set(gdn_dt_bias[i].reshape(-1))
    gq, gk, gv, beta, gc = _prep(qkv, gt, conv_w8, alog_v, dtb_v,
                                 _block_tri(ROW_TILE, CHUNK, False), _block_tri(ROW_TILE, CHUNK, True), n_ctx_tiles)
    uw = _chunk(gk, gv, beta, gc)
    o_f, o_r = _scan(gq, gk, gc, uw, n_ctx_tiles)

    y_diff = _attn(dq, dk, dv, diff_lambda[i], diff_norm_w[i], lam_init, n_ctx_tiles, tq=ROW_TILE)

    w_router = jnp.zeros((D, LANES), F32).at[:, :N_GROUPS].set(w_router_group[i]) \
        .at[:, N_GROUPS:N_GROUPS + N_EXPERTS].set(w_router_expert[i])
    w_router3 = jnp.stack(_split3(w_router))
    b_router = jnp.zeros((1, LANES), F32).at[0, :N_GROUPS].set(b_router_group[i]) \
        .at[0, N_GROUPS:N_GROUPS + N_EXPERTS].set(b_router_expert[i])
    x1, h2, logits = _out(o_f, o_r, z, y_diff, xc, mod3, gdn_norm_w[i], w_out[i].astype(BF16), ln1_g[i], ln1_b[i],
                          w_router3, b_router, alpha, n_ctx_tiles)

    t = bsz * seq
    tri_strict = jnp.asarray(np.tril(np.ones((ROW_TILE, ROW_TILE), np.float32), -1), dtype=BF16)
    ids, gate, counts = _route(logits.reshape(t, LANES), tri_strict)

    cnt = counts[0, :N_EXPERTS].astype(jnp.int32)
    padded = (cnt + FFN_BLOCK - 1) // FFN_BLOCK * FFN_BLOCK
    p_end = jnp.cumsum(padded)
    p_start = p_end - padded
    n_blocks = (2 * t) // FFN_BLOCK + N_EXPERTS
    block_e = jnp.clip(jnp.searchsorted(p_end, jnp.arange(n_blocks) * FFN_BLOCK, side='right'), 0, N_EXPERTS - 1)
    dest = (p_start[ids[:, 0:2]] + ids[:, 2:4]).reshape(-1).astype(jnp.int32)

    buf = _dispatch(dest, h2.reshape(t, D), n_blocks * FFN_BLOCK)
    ys = _ffn(block_e.astype(jnp.int32), buf, w_expert_gate[i], w_expert_up[i], w_expert_down[i])
    out = _combine(dest, ys, gate, x1.reshape(t, D), mod3, ln2_g[i], ln2_b[i], alpha, seq)
    return out.reshape(bsz, seq, D)
```

```python
import functools
import math

import numpy as np
import jax
import jax.numpy as jnp
from jax import lax
from jax.experimental import pallas as pl
from jax.experimental.pallas import tpu as pltpu

F32 = jnp.float32
BF16 = jnp.bfloat16
HIGHEST = lax.Precision.HIGHEST

D = 1024
GRID_W = 64
ROPE_THETA = 10000.0
GH = 4
GDK = 128
GDV = 128
GCONV = 5
DH = 4
DDK = 64
DDV = 128
G_QK = GH * GDK
G_QKV = 2 * G_QK + GH * GDV
G_Z = GH * GDV
G_GATES = 2 * GH
G_COLS = G_QKV + G_Z + 2 * G_GATES
D_QK = DH * 2 * DDK
N_GROUPS = 4
E_PER_GROUP = 8
N_EXPERTS = N_GROUPS * E_PER_GROUP
D_EXPERT = 512
LN_EPS = 1e-5
NORM_EPS = 1e-6

LANES = 128
ROW_TILE = 256
CHUNK = 64
FFN_BLOCK = 256
KEY_TILE = 256
VMEM_LIMIT = 56 * 1024 * 1024

C_QKV = 0
C_Z = C_QKV + G_QKV
C_GB = C_Z + G_Z
C_GA = C_GB + LANES
C_DQ = C_GA + LANES
C_DK = C_DQ + D_QK
C_END = C_DK + D_QK
Q_SCALE = DDK ** -0.5 * math.log2(math.e)


def _cparams(sem):
    return pltpu.CompilerParams(dimension_semantics=sem, vmem_limit_bytes=VMEM_LIMIT)


def _dot(a, b):
    return jnp.dot(a, b, preferred_element_type=F32)


def _dot_nt(a, b):
    return lax.dot_general(a, b, (((1,), (1,)), ((), ())), preferred_element_type=F32)


def _dot_tn(a, b):
    return lax.dot_general(a, b, (((0,), (0,)), ((), ())), preferred_element_type=F32)


def _normalize(x):
    mu = jnp.mean(x, axis=-1, keepdims=True)
    xc = x - mu
    var = jnp.mean(xc * xc, axis=-1, keepdims=True)
    return xc * lax.rsqrt(var + LN_EPS)


def _silu(x):
    return x * jax.nn.sigmoid(x)


def _split3(x):
    hi = x.astype(BF16)
    r = x - hi.astype(F32)
    mid = r.astype(BF16)
    lo = (r - mid.astype(F32)).astype(BF16)
    return hi, mid, lo


def _ada_kernel(c_ref, w_ref, b_ref, o_ref):
    s = _silu(c_ref[...])
    o_ref[...] = jnp.dot(s, w_ref[...], precision=HIGHEST, preferred_element_type=F32) + b_ref[...]


def _ada(cc, w_ada, b_ada):
    n = w_ada.shape[1]
    bn = 1024
    return pl.pallas_call(
        _ada_kernel,
        grid=(n // bn,),
        in_specs=[pl.BlockSpec((8, D), lambda j: (0, 0)),
                  pl.BlockSpec((D, bn), lambda j: (0, j)),
                  pl.BlockSpec((1, bn), lambda j: (0, j))],
        out_specs=pl.BlockSpec((8, bn), lambda j: (0, j)),
        out_shape=jax.ShapeDtypeStruct((8, n), F32),
        compiler_params=_cparams(("arbitrary",)),
        name="ada",
    )(cc, w_ada, b_ada.reshape(1, n))


def _rope(v, cos, sin, low_half):
    fwd = pltpu.roll(v, LANES - 16, axis=1)
    bwd = pltpu.roll(v, 16, axis=1)
    return v * cos + jnp.where(low_half, fwd, bwd) * sin


def _proj_kernel(x_ref, sh_ref, sc_ref, cos_ref, sin_ref, w_ref, wvt_ref,
                 qkv_ref, z_ref, gt_ref, dq_ref, dk_ref, dvt_ref):
    h = (_normalize(x_ref[0]) * (1.0 + sc_ref[0]) + sh_ref[0]).astype(BF16)
    qkv_ref[0] = _dot(h, w_ref[:, C_QKV:C_Z])
    z_ref[0] = _dot(h, w_ref[:, C_Z:C_GB])
    gt_ref[0] = _dot(h, w_ref[:, C_GB:C_DQ])
    cos = cos_ref[...]
    sin = sin_ref[...]
    lane = lax.broadcasted_iota(jnp.int32, cos.shape, 1)
    low_half = (lane % 32) < 16
    for j in range(D_QK // LANES):
        q = _dot(h, w_ref[:, C_DQ + j * LANES:C_DQ + (j + 1) * LANES])
        dq_ref[0, :, j * LANES:(j + 1) * LANES] = (_rope(q, cos, sin, low_half) * Q_SCALE).astype(BF16)
        k = _dot(h, w_ref[:, C_DK + j * LANES:C_DK + (j + 1) * LANES])
        dk_ref[0, :, j * LANES:(j + 1) * LANES] = _rope(k, cos, sin, low_half).astype(BF16)
    dvt_ref[0] = _dot_nt(wvt_ref[...], h).astype(BF16)


def _proj(xc, mod3, cos_t, sin_t, w_all, wv_t, n_ctx_tiles):
    bsz, lc, _ = xc.shape
    nt = lc // ROW_TILE
    ctx_row = mod3.shape[0] - 1

    def mod_idx(col):
        return lambda b, i: (jnp.where(i < n_ctx_tiles, ctx_row, b), 0, col)

    def row_spec(width):
        return pl.BlockSpec((1, ROW_TILE, width), lambda b, i: (b, i, 0))

    outs = [(G_QKV, F32), (G_Z, F32), (2 * LANES, F32), (D_QK, BF16), (D_QK, BF16)]
    return pl.pallas_call(
        _proj_kernel,
        grid=(bsz, nt),
        in_specs=[row_spec(D),
                  pl.BlockSpec((1, 1, D), mod_idx(0)),
                  pl.BlockSpec((1, 1, D), mod_idx(1)),
                  pl.BlockSpec((ROW_TILE, LANES), lambda b, i: (i, 0)),
                  pl.BlockSpec((ROW_TILE, LANES), lambda b, i: (i, 0)),
                  pl.BlockSpec((D, C_END), lambda b, i: (0, 0)),
                  pl.BlockSpec((DH * DDV, D), lambda b, i: (0, 0))],
        out_specs=[row_spec(w) for w, _ in outs]
        + [pl.BlockSpec((1, DH * DDV, ROW_TILE), lambda b, i: (b, 0, i))],
        out_shape=[jax.ShapeDtypeStruct((bsz, lc, w), dt) for w, dt in outs]
        + [jax.ShapeDtypeStruct((bsz, DH * DDV, lc), BF16)],
        compiler_params=_cparams(("arbitrary", "arbitrary")),
        name="proj",
    )(xc, mod3, mod3, cos_t, sin_t, w_all, wv_t)


def _prep_kernel(n_ctx_tiles, nt, main_ref, prev_ref, next_ref, gt_ref, cw_ref, alog_ref, dtb_ref,
                 lo_ref, up_ref, q_ref, k_ref, v_ref, beta_ref, gc_ref, ext_ref):
    i = pl.program_id(1)
    halo = prev_ref.shape[1]
    has_prev = jnp.logical_and(i != 0, i != n_ctx_tiles)
    has_next = jnp.logical_and(i != n_ctx_tiles - 1, i != nt - 1)
    ext_ref[0:halo, :] = prev_ref[0] * has_prev.astype(F32)
    ext_ref[halo:halo + ROW_TILE, :] = main_ref[0]
    ext_ref[halo + ROW_TILE:, :] = next_ref[0] * has_next.astype(F32)
    pad = GCONV // 2
    acc = None
    for j in range(GCONV):
        start = halo - pad + j
        term = ext_ref[start:start + ROW_TILE, :] * cw_ref[j:j + 1, :]
        acc = term if acc is None else acc + term
    qkv = _silu(acc)
    for h in range(GH):
        q = qkv[:, h * GDK:(h + 1) * GDK]
        q_ref[0, :, h * GDK:(h + 1) * GDK] = q * (lax.rsqrt(jnp.sum(q * q, axis=-1, keepdims=True) + NORM_EPS)
                                                * (GDK ** -0.5))
        k = qkv[:, G_QK + h * GDK:G_QK + (h + 1) * GDK]
        k_ref[0, :, h * GDK:(h + 1) * GDK] = k * lax.rsqrt(jnp.sum(k * k, axis=-1, keepdims=True) + NORM_EPS)
    v_ref[0] = qkv[:, 2 * G_QK:]
    gt = gt_ref[0]
    beta_ref[0] = jax.nn.sigmoid(gt[:, :LANES])
    a = gt[:, LANES:] + dtb_ref[...]
    softplus = jnp.maximum(a, 0.0) + jnp.log(1.0 + jnp.exp(-jnp.abs(a)))
    g = -jnp.exp(alog_ref[...]) * softplus
    lo = lo_ref[...]
    up = up_ref[...]
    fwd = None
    bwd = None
    for part in _split3(g):
        f = _dot(lo, part)
        r = _dot(up, part)
        fwd = f if fwd is None else fwd + f
        bwd = r if bwd is None else bwd + r
    lane = lax.broadcasted_iota(jnp.int32, g.shape, 1)
    gc_ref[0] = jnp.where(lane < GH, fwd, bwd)


def _prep(qkv, gt, conv_w8, alog_v, dtb_v, tri_lo, tri_up, n_ctx_tiles):
    bsz, lc, _ = qkv.shape
    nt = lc // ROW_TILE
    halo = 8
    per = ROW_TILE // halo
    last = lc // halo - 1

    def row_spec(width):
        return pl.BlockSpec((1, ROW_TILE, width), lambda b, i: (b, i, 0))

    def const_spec(shape):
        return pl.BlockSpec(shape, lambda b, i: tuple(0 for _ in shape))

    outs = [(G_QK, F32), (G_QK, F32), (GH * GDV, F32), (LANES, F32), (LANES, F32)]
    return pl.pallas_call(
        functools.partial(_prep_kernel, n_ctx_tiles, nt),
        grid=(bsz, nt),
        in_specs=[row_spec(G_QKV),
                  pl.BlockSpec((1, halo, G_QKV), lambda b, i: (b, jnp.maximum(i * per - 1, 0), 0)),
                  pl.BlockSpec((1, halo, G_QKV), lambda b, i: (b, jnp.minimum((i + 1) * per, last), 0)),
                  row_spec(2 * LANES),
                  const_spec((8, G_QKV)), const_spec((1, LANES)), const_spec((1, LANES)),
                  const_spec((ROW_TILE, ROW_TILE)), const_spec((ROW_TILE, ROW_TILE))],
        out_specs=[row_spec(w) for w, _ in outs],
        out_shape=[jax.ShapeDtypeStruct((bsz, lc, w), dt) for w, dt in outs],
        scratch_shapes=[pltpu.VMEM((ROW_TILE + 2 * halo, G_QKV), F32)],
        compiler_params=_cparams(("arbitrary", "arbitrary")),
        name="prep",
    )(qkv, qkv, qkv, gt, conv_w8, alog_v, dtb_v, tri_lo, tri_up)


def _chunk_masks(n, rev):
    row = lax.broadcasted_iota(jnp.int32, (n, n), 0)
    col = lax.broadcasted_iota(jnp.int32, (n, n), 1)
    same = (row // CHUNK) == (col // CHUNK)
    incl = jnp.logical_and(same, (col >= row) if rev else (col <= row))
    strict = jnp.logical_and(same, (col > row) if rev else (col < row))
    return incl, strict


def _decay(gcol, grow, incl):
    return jnp.where(incl, jnp.exp(jnp.minimum(gcol - grow, 0.0)), 0.0)


def _chunk_kernel(k_ref, v_ref, beta_ref, gc_ref, uw_ref):
    n = k_ref.shape[1]
    gc = gc_ref[0]
    gct = gc.T
    beta = beta_ref[0]
    row = lax.broadcasted_iota(jnp.int32, (n, n), 0)
    col = lax.broadcasted_iota(jnp.int32, (n, n), 1)
    eye = (row == col).astype(F32)
    sizes = [2 ** j for j in range(1, int(math.log2(CHUNK)))]
    same = {r: (row // r) == (col // r) for r in sizes + [CHUNK]}
    a_all, t_all, x_all = [], [], []
    for d in range(2):
        incl, strict = _chunk_masks(n, d == 1)
        for h in range(GH):
            c = d * GH + h
            k = k_ref[0, :, h * GDK:(h + 1) * GDK]
            v = v_ref[0, :, h * GDV:(h + 1) * GDV]
            bcol = beta[:, c:c + 1]
            gcol = gc[:, c:c + 1]
            grow = gct[c:c + 1, :]
            kb = k * bcol
            a = jnp.where(strict, _dot_nt(kb.astype(BF16), k.astype(BF16)) * _decay(gcol, grow, incl), 0.0)
            a_all.append(a.astype(BF16))
            t_all.append(eye - jnp.where(same[2], a, 0.0))
            x_all.append(jnp.concatenate([v * bcol, kb * jnp.exp(gcol)], axis=1).astype(BF16))
    n_prob = len(a_all)
    zero = jnp.zeros((n, n), BF16)
    for r in sizes:
        off_mask = jnp.logical_and(same[2 * r], jnp.logical_not(same[r]))
        t_b = [t.astype(BF16) for t in t_all]
        inner = [_dot(jnp.where(off_mask, a_all[c], zero), t_b[c]).astype(BF16) for c in range(n_prob)]
        t_all = [t_all[c] - _dot(t_b[c], inner[c]) for c in range(n_prob)]
    for c in range(n_prob):
        uw_ref[0, c] = _dot(t_all[c].astype(BF16), x_all[c])


def _chunk(k, v, beta, gc):
    bsz, lc, _ = k.shape
    nt = lc // ROW_TILE

    def row_spec(width):
        return pl.BlockSpec((1, ROW_TILE, width), lambda b, i: (b, i, 0))

    return pl.pallas_call(
        _chunk_kernel,
        grid=(bsz, nt),
        in_specs=[row_spec(G_QK), row_spec(GH * GDV), row_spec(LANES), row_spec(LANES)],
        out_specs=pl.BlockSpec((1, 2 * GH, ROW_TILE, GDV + GDK), lambda b, i: (b, 0, i, 0)),
        out_shape=jax.ShapeDtypeStruct((bsz, 2 * GH, lc, GDV + GDK), F32),
        compiler_params=_cparams(("arbitrary", "arbitrary")),
        name="chunk",
    )(k, v, beta, gc)


def _scan_kernel(qf_ref, kf_ref, gf_ref, uwf_ref, qr_ref, kr_ref, gr_ref, uwr_ref,
                 of_ref, or_ref, s_ref):
    @pl.when(pl.program_id(1) == 0)
    def _():
        s_ref[...] = jnp.zeros_like(s_ref)

    n_chunks = ROW_TILE // CHUNK
    row = lax.broadcasted_iota(jnp.int32, (CHUNK, CHUNK), 0)
    col = lax.broadcasted_iota(jnp.int32, (CHUNK, CHUNK), 1)
    state = [s_ref[c] for c in range(2 * GH)]
    dirs = ((qf_ref, kf_ref, gf_ref, uwf_ref, of_ref), (qr_ref, kr_ref, gr_ref, uwr_ref, or_ref))

    def chunk_rows(step, d):
        ci = n_chunks - 1 - step if d == 1 else step
        return slice(ci * CHUNK, (ci + 1) * CHUNK)

    pre = []
    for step in range(n_chunks):
        per_chain = []
        for d, (q_ref, k_ref, g_ref, uw_ref, o_ref) in enumerate(dirs):
            rev = d == 1
            incl = (col >= row) if rev else (col <= row)
            rows = chunk_rows(step, d)
            gc = g_ref[0, rows, :]
            gct = gc.T
            last = 0 if rev else CHUNK - 1
            for h in range(GH):
                c = d * GH + h
                q = q_ref[0, rows, h * GDK:(h + 1) * GDK]
                k = k_ref[0, rows, h * GDK:(h + 1) * GDK]
                gcol = gc[:, c:c + 1]
                grow = gct[c:c + 1, :]
                g_last = gcol[last:last + 1, :]
                a_qk = (_dot_nt(q.astype(BF16), k.astype(BF16)) * _decay(gcol, grow, incl)).astype(BF16)
                wq = jnp.concatenate([uw_ref[0, h, rows, GDV:], q * jnp.exp(gcol)], axis=0).astype(BF16)
                k_tail_t = (k * jnp.exp(g_last - gcol)).T.astype(BF16)
                per_chain.append((a_qk, wq, k_tail_t, jnp.exp(g_last)))
        pre.append(per_chain)

    for step in range(n_chunks):
        ws = [_dot(pre[step][c][1], state[c].astype(BF16)) for c in range(2 * GH)]
        v_new = []
        for d, (q_ref, k_ref, g_ref, uw_ref, o_ref) in enumerate(dirs):
            rows = chunk_rows(step, d)
            for h in range(GH):
                v_new.append((uw_ref[0, h, rows, :GDV] - ws[d * GH + h][:CHUNK]).astype(BF16))
        for d, (q_ref, k_ref, g_ref, uw_ref, o_ref) in enumerate(dirs):
            rows = chunk_rows(step, d)
            for h in range(GH):
                c = d * GH + h
                a_qk, _, k_tail_t, decay_last = pre[step][c]
                o_ref[0, rows, h * GDV:(h + 1) * GDV] = ws[c][CHUNK:] + _dot(a_qk, v_new[c])
                state[c] = state[c] * decay_last + _dot(k_tail_t, v_new[c])
    for c in range(2 * GH):
        s_ref[c] = state[c]


def _scan(q, k, gc, uw, n_ctx_tiles):
    bsz, lc, _ = q.shape
    nt = lc // ROW_TILE

    def fwd(b, i):
        return i

    def bwd(b, i):
        return jnp.where(i < n_ctx_tiles, n_ctx_tiles - 1 - i, nt - 1 - (i - n_ctx_tiles))

    def specs(tile):
        return [pl.BlockSpec((1, ROW_TILE, G_QK), lambda b, i: (b, tile(b, i), 0)),
                pl.BlockSpec((1, ROW_TILE, G_QK), lambda b, i: (b, tile(b, i), 0)),
                pl.BlockSpec((1, ROW_TILE, LANES), lambda b, i: (b, tile(b, i), 0))]

    def uw_spec(tile, d):
        return pl.BlockSpec((1, GH, ROW_TILE, GDV + GDK), lambda b, i: (b, d, tile(b, i), 0))

    def out_spec(tile):
        return pl.BlockSpec((1, ROW_TILE, GH * GDV), lambda b, i: (b, tile(b, i), 0))

    return pl.pallas_call(
        _scan_kernel,
        grid=(bsz, nt),
        in_specs=specs(fwd) + [uw_spec(fwd, 0)] + specs(bwd) + [uw_spec(bwd, 1)],
        out_specs=[out_spec(fwd), out_spec(bwd)],
        out_shape=[jax.ShapeDtypeStruct((bsz, lc, GH * GDV), F32)] * 2,
        scratch_shapes=[pltpu.VMEM((2 * GH, GDK, GDV), F32)],
        compiler_params=_cparams(("arbitrary", "arbitrary")),
        name="scan",
    )(q, k, gc, uw, q, k, gc, uw)


def _attn_kernel(lam_init, q_ref, k_ref, vt_ref, lam_ref, nw_ref, o_ref):
    lam = lam_ref[...]
    lam_full = (jnp.exp(jnp.sum(lam[0:1] * lam[1:2], axis=-1, keepdims=True))
                - jnp.exp(jnp.sum(lam[2:3] * lam[3:4], axis=-1, keepdims=True)) + lam_init)
    q = q_ref[0]
    tq = q.shape[0]
    n_keys = k_ref.shape[1]
    lane = lax.broadcasted_iota(jnp.int32, q.shape, 1)
    zero = jnp.zeros_like(q)
    qc = [jnp.where(lane < DDK, q, zero), jnp.where(lane >= DDK, q, zero)]
    m = [jnp.full((1, tq), -jnp.inf, F32) for _ in range(2)]
    l = [jnp.zeros((1, tq), F32) for _ in range(2)]
    acc = [jnp.zeros((DDV, tq), F32) for _ in range(2)]
    n_tiles = n_keys // KEY_TILE

    def scores(t):
        k_t = k_ref[0, t * KEY_TILE:(t + 1) * KEY_TILE, :]
        return [_dot_nt(k_t, qc[c]) for c in range(2)]

    s_next = scores(0)
    for t in range(n_tiles):
        s_cur = s_next
        if t + 1 < n_tiles:
            s_next = scores(t + 1)
        vt_t = vt_ref[0, :, t * KEY_TILE:(t + 1) * KEY_TILE]
        for c in range(2):
            s = s_cur[c]
            m_new = jnp.maximum(m[c], jnp.max(s, axis=0, keepdims=True))
            p = jnp.exp2(s - m_new)
            scale = jnp.exp2(m[c] - m_new)
            l[c] = scale * l[c] + jnp.sum(p, axis=0, keepdims=True)
            acc[c] = scale * acc[c] + _dot(vt_t, p.astype(BF16))
            m[c] = m_new
    o = acc[0] / l[0] - lam_full * (acc[1] / l[1])
    y = o * lax.rsqrt(jnp.mean(o * o, axis=0, keepdims=True) + NORM_EPS)
    o_ref[0] = y.T * nw_ref[...] * (1.0 - lam_init)


def _attn(dq, dk, dvt, lam, norm_w, lam_init, n_ctx_tiles, tq):
    bsz, lc, _ = dq.shape
    lq = lc - n_ctx_tiles * ROW_TILE
    q_off = n_ctx_tiles * ROW_TILE // tq
    return pl.pallas_call(
        functools.partial(_attn_kernel, lam_init),
        grid=(bsz, DH, lq // tq),
        in_specs=[pl.BlockSpec((1, tq, 2 * DDK), lambda b, h, i: (b, q_off + i, h)),
                  pl.BlockSpec((1, lc, 2 * DDK), lambda b, h, i: (b, 0, h)),
                  pl.BlockSpec((1, DDV, lc), lambda b, h, i: (b, h, 0)),
                  pl.BlockSpec((4, DDK), lambda b, h, i: (0, 0)),
                  pl.BlockSpec((1, DDV), lambda b, h, i: (0, 0))],
        out_specs=pl.BlockSpec((1, tq, DDV), lambda b, h, i: (b, i, h)),
        out_shape=jax.ShapeDtypeStruct((bsz, lq, DH * DDV), F32),
        compiler_params=_cparams(("arbitrary", "arbitrary", "arbitrary")),
        name="attn",
    )(dq, dk, dvt, lam, norm_w.reshape(1, DDV))


def _out_kernel(alpha, of_ref, or_ref, z_ref, yd_ref, x_ref, g1_ref, sh2_ref, sc2_ref, gnw_ref,
                wo_ref, lng_ref, lnb_ref, wr_ref, br_ref, x1_ref, h2_ref, lg_ref):
    o = of_ref[0] + or_ref[0]
    z = z_ref[0]
    acc = _dot(yd_ref[0].astype(BF16), wo_ref[G_Z:, :])
    for h in range(GH):
        oh = o[:, h * GDV:(h + 1) * GDV]
        yh = (oh * lax.rsqrt(jnp.mean(oh * oh, axis=-1, keepdims=True) + NORM_EPS) * gnw_ref[...]
              * _silu(z[:, h * GDV:(h + 1) * GDV]))
        acc = acc + _dot(yh.astype(BF16), wo_ref[h * GDV:(h + 1) * GDV, :])
    x1 = _normalize(alpha * x_ref[0] + g1_ref[0] * acc) * lng_ref[...] + lnb_ref[...]
    x1_ref[0] = x1
    h2 = _normalize(x1) * (1.0 + sc2_ref[0]) + sh2_ref[0]
    h2_ref[0] = h2
    lg = br_ref[...]
    for part in _split3(h2):
        for wpart in range(3):
            lg = lg + _dot(part, wr_ref[wpart])
    lg_ref[0] = lg


def _out(o_f, o_r, z, y_diff, xc, mod3, gdn_norm_w, w_out, ln_g, ln_b, w_router3, b_router, alpha, n_ctx_tiles):
    bsz, lq, _ = y_diff.shape
    nt = lq // ROW_TILE

    def lat_spec(width):
        return pl.BlockSpec((1, ROW_TILE, width), lambda b, i: (b, i + n_ctx_tiles, 0))

    def row_spec(width):
        return pl.BlockSpec((1, ROW_TILE, width), lambda b, i: (b, i, 0))

    def mod_spec(col):
        return pl.BlockSpec((1, 1, D), lambda b, i: (b, 0, col))

    def const_spec(shape):
        return pl.BlockSpec(shape, lambda b, i: tuple(0 for _ in shape))

    return pl.pallas_call(
        functools.partial(_out_kernel, alpha),
        grid=(bsz, nt),
        in_specs=[lat_spec(GH * GDV), lat_spec(GH * GDV), lat_spec(G_Z), row_spec(DH * DDV), lat_spec(D),
                  mod_spec(2), mod_spec(3), mod_spec(4),
                  const_spec((1, GDV)), const_spec((G_Z + DH * DDV, D)),
                  const_spec((1, D)), const_spec((1, D)),
                  const_spec((3, D, LANES)), const_spec((1, LANES))],
        out_specs=[row_spec(D), row_spec(D), row_spec(LANES)],
        out_shape=[jax.ShapeDtypeStruct((bsz, lq, D), F32), jax.ShapeDtypeStruct((bsz, lq, D), F32),
                   jax.ShapeDtypeStruct((bsz, lq, LANES), F32)],
        compiler_params=_cparams(("arbitrary", "arbitrary")),
        name="out",
    )(o_f, o_r, z, y_diff, xc, mod3, mod3, mod3, gdn_norm_w.reshape(1, GDV), w_out,
      ln_g.reshape(1, D), ln_b.reshape(1, D), w_router3, b_router)


def _route_kernel(lg_ref, tri_ref, ids_ref, gate_ref, cnt_ref, run_ref):
    @pl.when(pl.program_id(0) == 0)
    def _():
        run_ref[...] = jnp.zeros_like(run_ref)

    lg = lg_ref[...]
    lane_i = lax.broadcasted_iota(jnp.int32, lg.shape, 1)
    lane = lane_i.astype(F32)
    neg = jnp.float32(-jnp.inf)

    def first_max(mask):
        masked = jnp.where(mask, lg, neg)
        m = jnp.max(masked, axis=-1, keepdims=True)
        idx = jnp.min(jnp.where(jnp.logical_and(mask, masked == m), lane, float(LANES)), axis=-1, keepdims=True)
        return m, idx

    is_group = lane_i < N_GROUPS
    m_g, grp = first_max(is_group)
    p_g = 1.0 / jnp.sum(jnp.where(is_group, jnp.exp(lg - m_g), 0.0), axis=-1, keepdims=True)
    e_lane = lane - float(N_GROUPS)
    lo_e = grp * float(E_PER_GROUP)
    in_group = jnp.logical_and(e_lane >= lo_e, e_lane < lo_e + float(E_PER_GROUP))
    l0, i0 = first_max(in_group)
    l1, i1 = first_max(jnp.logical_and(in_group, lane != i0))
    r = jnp.exp(l1 - l0)
    gate0 = p_g / (1.0 + r)
    gate1 = p_g * r / (1.0 + r)
    e0 = i0 - float(N_GROUPS)
    e1 = i1 - float(N_GROUPS)
    oh0 = (lane == e0).astype(F32)
    oh1 = (lane == e1).astype(F32)
    both = oh0 + oh1
    before = _dot(tri_ref[...], both.astype(BF16)) + run_ref[...]
    rank0 = jnp.sum(oh0 * before, axis=-1, keepdims=True)
    rank1 = jnp.sum(oh1 * before, axis=-1, keepdims=True)
    run_ref[...] = run_ref[...] + jnp.sum(both, axis=0, keepdims=True)
    cnt_ref[...] = run_ref[...]
    ids = jnp.where(lane_i == 0, e0, jnp.where(lane_i == 1, e1,
                                               jnp.where(lane_i == 2, rank0, jnp.where(lane_i == 3, rank1, 0.0))))
    ids_ref[...] = ids.astype(jnp.int32)
    gate_ref[...] = jnp.where(lane_i == 0, gate0, jnp.where(lane_i == 1, gate1, 0.0))


def _route(logits, tri_strict):
    t = logits.shape[0]
    tile = tri_strict.shape[0]
    return pl.pallas_call(
        _route_kernel,
        grid=(t // tile,),
        in_specs=[pl.BlockSpec((tile, LANES), lambda i: (i, 0)),
                  pl.BlockSpec((tile, tile), lambda i: (0, 0))],
        out_specs=[pl.BlockSpec((tile, LANES), lambda i: (i, 0)),
                   pl.BlockSpec((tile, LANES), lambda i: (i, 0)),
                   pl.BlockSpec((1, LANES), lambda i: (0, 0))],
        out_shape=[jax.ShapeDtypeStruct((t, LANES), jnp.int32), jax.ShapeDtypeStruct((t, LANES), F32),
                   jax.ShapeDtypeStruct((1, LANES), F32)],
        scratch_shapes=[pltpu.VMEM((1, LANES), F32)],
        compiler_params=_cparams(("arbitrary",)),
        name="route",
    )(logits, tri_strict)


DMA_TOKENS = 512


def _dispatch_kernel(dest_ref, h_ref, zero_hbm, buf_hbm, sem):
    del zero_hbm

    def start(tok, carry):
        for j in range(2):
            pltpu.make_async_copy(h_ref.at[pl.ds(tok, 1)], buf_hbm.at[pl.ds(dest_ref[2 * tok + j], 1)], sem).start()
        return carry

    lax.fori_loop(0, DMA_TOKENS, start, 0, unroll=8)
    for _ in range(2):
        pltpu.make_async_copy(h_ref, buf_hbm.at[pl.ds(0, DMA_TOKENS)], sem).wait()


def _dispatch(dest_flat, h2, n_rows):
    t, d = h2.shape
    zero = jnp.zeros((n_rows, d), h2.dtype)
    return pl.pallas_call(
        _dispatch_kernel,
        grid=(t // DMA_TOKENS,),
        in_specs=[pl.BlockSpec((2 * DMA_TOKENS,), lambda i: (i,), memory_space=pltpu.SMEM),
                  pl.BlockSpec((DMA_TOKENS, d), lambda i: (i, 0)),
                  pl.BlockSpec(memory_space=pl.ANY)],
        out_specs=pl.BlockSpec(memory_space=pl.ANY),
        out_shape=jax.ShapeDtypeStruct((n_rows, d), h2.dtype),
        scratch_shapes=[pltpu.SemaphoreType.DMA(())],
        input_output_aliases={2: 0},
        compiler_params=_cparams(("arbitrary",)),
        name="dispatch",
    )(dest_flat, h2, zero)


def _combine_kernel(alpha, dest_ref, ys_hbm, gate_ref, x1_ref, g2_ref, lng_ref, lnb_ref, o_ref, r0_ref, r1_ref, sem):
    n = x1_ref.shape[0]

    def start(tok, carry):
        for j, r_ref in enumerate((r0_ref, r1_ref)):
            pltpu.make_async_copy(ys_hbm.at[pl.ds(dest_ref[2 * tok + j], 1)], r_ref.at[pl.ds(tok, 1)], sem).start()
        return carry

    lax.fori_loop(0, n, start, 0, unroll=8)
    for r_ref in (r0_ref, r1_ref):
        pltpu.make_async_copy(ys_hbm.at[pl.ds(0, n)], r_ref, sem).wait()
    gate = gate_ref[...]
    y = gate[:, 0:1] * r0_ref[...] + gate[:, 1:2] * r1_ref[...]
    o_ref[...] = _normalize(alpha * x1_ref[...] + g2_ref[0] * y) * lng_ref[...] + lnb_ref[...]


def _combine(dest_flat, ys, gate, x1, mod3, ln_g, ln_b, alpha, rows_per_batch):
    t, d = x1.shape
    per_batch = rows_per_batch // DMA_TOKENS
    return pl.pallas_call(
        functools.partial(_combine_kernel, alpha),
        grid=(t // DMA_TOKENS,),
        in_specs=[pl.BlockSpec((2 * DMA_TOKENS,), lambda i: (i,), memory_space=pltpu.SMEM),
                  pl.BlockSpec(memory_space=pl.ANY),
                  pl.BlockSpec((DMA_TOKENS, LANES), lambda i: (i, 0)),
                  pl.BlockSpec((DMA_TOKENS, d), lambda i: (i, 0)),
                  pl.BlockSpec((1, 1, d), lambda i: (i // per_batch, 0, 5)),
                  pl.BlockSpec((1, d), lambda i: (0, 0)),
                  pl.BlockSpec((1, d), lambda i: (0, 0))],
        out_specs=pl.BlockSpec((DMA_TOKENS, d), lambda i: (i, 0)),
        out_shape=jax.ShapeDtypeStruct((t, d), F32),
        scratch_shapes=[pltpu.VMEM((DMA_TOKENS, d), F32), pltpu.VMEM((DMA_TOKENS, d), F32),
                        pltpu.SemaphoreType.DMA(())],
        compiler_params=_cparams(("arbitrary",)),
        name="combine",
    )(dest_flat, ys, gate, x1, mod3, ln_g.reshape(1, d), ln_b.reshape(1, d))


def _ffn_kernel(be_ref, x_ref, wg_ref, wu_ref, wd_ref, o_ref, wgb_ref, wub_ref, wdb_ref):
    i = pl.program_id(0)
    changed = jnp.logical_or(i == 0, be_ref[i] != be_ref[jnp.maximum(i - 1, 0)])

    @pl.when(changed)
    def _():
        wgb_ref[...] = wg_ref[0].astype(BF16)
        wub_ref[...] = wu_ref[0].astype(BF16)
        wdb_ref[...] = wd_ref[0].astype(BF16)

    x = x_ref[...].astype(BF16)
    hidden = _silu(_dot(x, wgb_ref[...])) * _dot(x, wub_ref[...])
    o_ref[...] = _dot(hidden.astype(BF16), wdb_ref[...])


def _ffn(block_e, buf, w_gate, w_up, w_down):
    n_rows, d = buf.shape
    n_blocks = n_rows // FFN_BLOCK
    grid_spec = pltpu.PrefetchScalarGridSpec(
        num_scalar_prefetch=1,
        grid=(n_blocks,),
        in_specs=[pl.BlockSpec((FFN_BLOCK, d), lambda i, be: (i, 0)),
                  pl.BlockSpec((1, d, D_EXPERT), lambda i, be: (be[i], 0, 0)),
                  pl.BlockSpec((1, d, D_EXPERT), lambda i, be: (be[i], 0, 0)),
                  pl.BlockSpec((1, D_EXPERT, d), lambda i, be: (be[i], 0, 0))],
        out_specs=pl.BlockSpec((FFN_BLOCK, d), lambda i, be: (i, 0)),
        scratch_shapes=[pltpu.VMEM((d, D_EXPERT), BF16), pltpu.VMEM((d, D_EXPERT), BF16),
                        pltpu.VMEM((D_EXPERT, d), BF16)],
    )
    return pl.pallas_call(
        _ffn_kernel,
        grid_spec=grid_spec,
        out_shape=jax.ShapeDtypeStruct((n_rows, d), F32),
        compiler_params=_cparams(("arbitrary",)),
        name="ffn",
    )(block_e, buf, w_gate, w_up, w_down)


def _rope_tables(seq_len, n_ctx_rows):
    lane = np.arange(LANES)
    within = lane % DDK
    freq = within % 16
    use_col = within >= DDK // 2
    sign = np.where((lane % 32) < 16, -1.0, 1.0).astype(np.float32)
    half = DDK // 2
    inv_freq = ROPE_THETA ** (-jnp.arange(0, half, 2, dtype=F32) / half)
    t = jnp.arange(seq_len)
    pos = jnp.where(jnp.asarray(use_col)[None, :], (t % GRID_W)[:, None], (t // GRID_W)[:, None]).astype(F32)
    ang = pos * inv_freq[jnp.asarray(freq)][None, :]
    cos = jnp.concatenate([jnp.ones((n_ctx_rows, LANES), F32), jnp.cos(ang)], axis=0)
    sin = jnp.concatenate([jnp.zeros((n_ctx_rows, LANES), F32), jnp.sin(ang) * jnp.asarray(sign)[None, :]], axis=0)
    return cos, sin


def _block_tri(n, block, upper):
    i = np.arange(n)
    same = (i[:, None] // block) == (i[None, :] // block)
    tri = (i[None, :] >= i[:, None]) if upper else (i[None, :] <= i[:, None])
    return jnp.asarray((same & tri).astype(np.float32), dtype=BF16)


def _pack_w_in(w_in):
    pad = jnp.zeros((D, LANES - G_GATES), w_in.dtype)
    off = G_QKV + G_Z
    v_start = G_COLS + 2 * D_QK
    w_all = jnp.concatenate([
        w_in[:, :G_QKV], w_in[:, G_QKV:off],
        w_in[:, off:off + G_GATES], pad,
        w_in[:, off + G_GATES:G_COLS], pad,
        w_in[:, G_COLS:v_start],
    ], axis=1).astype(BF16)
    return w_all, w_in[:, v_start:].T.astype(BF16)


def kernel(x, c, ctx, c_ctx, w_ada, b_ada, w_in, conv_w, gdn_a_log, gdn_dt_bias, gdn_norm_w, diff_lambda,
           diff_norm_w, w_out, ln1_g, ln1_b, w_router_group, b_router_group, w_router_expert, b_router_expert,
           w_expert_gate, w_expert_up, w_expert_down, ln2_g, ln2_b):
    depth = w_ada.shape[0]
    assert depth == 1, "single-layer block: the context stream never feeds a later layer"
    bsz, seq, _ = x.shape
    n_ctx = ctx.shape[1]
    assert seq % ROW_TILE == 0 and n_ctx % ROW_TILE == 0 and seq % GRID_W == 0 and bsz < 8
    n_ctx_tiles = n_ctx // ROW_TILE
    alpha = (2.0 * depth) ** 0.25
    lam_init = 0.8 - 0.6 * math.exp(-0.3 * 0)
    i = 0

    cc = jnp.zeros((8, D), F32).at[:bsz].set(c).at[7].set(c_ctx)
    mod = _ada(cc, w_ada[i], b_ada[i])
    mod3 = mod.reshape(8, 1, 6 * D)

    xc = jnp.concatenate([ctx, x], axis=1)
    cos_t, sin_t = _rope_tables(seq, n_ctx)
    qkv, z, gt, dq, dk, dvt = _proj(xc, mod3, cos_t, sin_t, *_pack_w_in(w_in[i]), n_ctx_tiles)

    conv_w8 = jnp.zeros((8, G_QKV), F32).at[:GCONV].set(conv_w[i])
    alog_v = jnp.zeros((1, LANES), F32).at[0, :G_GATES].set(gdn_a_log[i].reshape(-1))
    dtb_v = jnp.zeros((1, LANES), F32).at[0, :G_GATES].set(gdn_dt_bias[i].reshape(-1))
    gq, gk, gv, beta, gc = _prep(qkv, gt, conv_w8, alog_v, dtb_v,
                                 _block_tri(ROW_TILE, CHUNK, False), _block_tri(ROW_TILE, CHUNK, True), n_ctx_tiles)
    uw = _chunk(gk, gv, beta, gc)
    o_f, o_r = _scan(gq, gk, gc, uw, n_ctx_tiles)

    y_diff = _attn(dq, dk, dvt, diff_lambda[i], diff_norm_w[i], lam_init, n_ctx_tiles, tq=ROW_TILE)

    w_router = jnp.zeros((D, LANES), F32).at[:, :N_GROUPS].set(w_router_group[i]) \
        .at[:, N_GROUPS:N_GROUPS + N_EXPERTS].set(w_router_expert[i])
    w_router3 = jnp.stack(_split3(w_router))
    b_router = jnp.zeros((1, LANES), F32).at[0, :N_GROUPS].set(b_router_group[i]) \
        .at[0, N_GROUPS:N_GROUPS + N_EXPERTS].set(b_router_expert[i])
    x1, h2, logits = _out(o_f, o_r, z, y_diff, xc, mod3, gdn_norm_w[i], w_out[i].astype(BF16), ln1_g[i], ln1_b[i],
                          w_router3, b_router, alpha, n_ctx_tiles)

    t = bsz * seq
    tri_strict = jnp.asarray(np.tril(np.ones((ROW_TILE, ROW_TILE), np.float32), -1), dtype=BF16)
    ids, gate, counts = _route(logits.reshape(t, LANES), tri_strict)

    cnt = counts[0, :N_EXPERTS].astype(jnp.int32)
    padded = (cnt + FFN_BLOCK - 1) // FFN_BLOCK * FFN_BLOCK
    p_end = jnp.cumsum(padded)
    p_start = p_end - padded
    n_blocks = (2 * t) // FFN_BLOCK + N_EXPERTS
    block_start = jnp.arange(n_blocks, dtype=jnp.int32) * FFN_BLOCK
    block_e = jnp.minimum(jnp.sum((p_end[None, :] <= block_start[:, None]).astype(jnp.int32), axis=1), N_EXPERTS - 1)
    dest = (p_start[ids[:, 0:2]] + ids[:, 2:4]).reshape(-1).astype(jnp.int32)

    buf = _dispatch(dest, h2.reshape(t, D), n_blocks * FFN_BLOCK)
    ys = _ffn(block_e.astype(jnp.int32), buf, w_expert_gate[i], w_expert_up[i], w_expert_down[i])
    out = _combine(dest, ys, gate, x1.reshape(t, D), mod3, ln2_g[i], ln2_b[i], alpha, seq)
    return out.reshape(bsz, seq, D)
```

```python
import functools
import math

import numpy as np
import jax
import jax.numpy as jnp
from jax import lax
from jax.experimental import pallas as pl
from jax.experimental.pallas import tpu as pltpu

F32 = jnp.float32
BF16 = jnp.bfloat16
HIGHEST = lax.Precision.HIGHEST

D = 1024
GRID_W = 64
ROPE_THETA = 10000.0
GH = 4
GDK = 128
GDV = 128
GCONV = 5
DH = 4
DDK = 64
DDV = 128
G_QK = GH * GDK
G_QKV = 2 * G_QK + GH * GDV
G_Z = GH * GDV
G_GATES = 2 * GH
G_COLS = G_QKV + G_Z + 2 * G_GATES
D_QK = DH * 2 * DDK
N_GROUPS = 4
E_PER_GROUP = 8
N_EXPERTS = N_GROUPS * E_PER_GROUP
D_EXPERT = 512
LN_EPS = 1e-5
NORM_EPS = 1e-6

LANES = 128
ROW_TILE = 256
CHUNK = 64
FFN_BLOCK = 256
KEY_TILE = 256
ATTN_Q_TILE = 512
SCORE_LOOKAHEAD = 2
VMEM_LIMIT = 56 * 1024 * 1024

C_QKV = 0
C_Z = C_QKV + G_QKV
C_GB = C_Z + G_Z
C_GA = C_GB + LANES
C_DQ = C_GA + LANES
C_DK = C_DQ + D_QK
C_END = C_DK + D_QK
Q_SCALE = DDK ** -0.5 * math.log2(math.e)


def _cparams(sem):
    return pltpu.CompilerParams(dimension_semantics=sem, vmem_limit_bytes=VMEM_LIMIT)


def _dot(a, b):
    return jnp.dot(a, b, preferred_element_type=F32)


def _dot_nt(a, b):
    return lax.dot_general(a, b, (((1,), (1,)), ((), ())), preferred_element_type=F32)


def _dot_tn(a, b):
    return lax.dot_general(a, b, (((0,), (0,)), ((), ())), preferred_element_type=F32)


def _normalize(x):
    mu = jnp.mean(x, axis=-1, keepdims=True)
    xc = x - mu
    var = jnp.mean(xc * xc, axis=-1, keepdims=True)
    return xc * lax.rsqrt(var + LN_EPS)


def _silu(x):
    return x * jax.nn.sigmoid(x)


def _split3(x):
    hi = x.astype(BF16)
    r = x - hi.astype(F32)
    mid = r.astype(BF16)
    lo = (r - mid.astype(F32)).astype(BF16)
    return hi, mid, lo


def _ada_kernel(c_ref, w_ref, b_ref, o_ref):
    s = _silu(c_ref[...])
    o_ref[...] = jnp.dot(s, w_ref[...], precision=HIGHEST, preferred_element_type=F32) + b_ref[...]


def _ada(cc, w_ada, b_ada):
    n = w_ada.shape[1]
    bn = 1024
    return pl.pallas_call(
        _ada_kernel,
        grid=(n // bn,),
        in_specs=[pl.BlockSpec((8, D), lambda j: (0, 0)),
                  pl.BlockSpec((D, bn), lambda j: (0, j)),
                  pl.BlockSpec((1, bn), lambda j: (0, j))],
        out_specs=pl.BlockSpec((8, bn), lambda j: (0, j)),
        out_shape=jax.ShapeDtypeStruct((8, n), F32),
        compiler_params=_cparams(("arbitrary",)),
        name="ada",
    )(cc, w_ada, b_ada.reshape(1, n))


def _rope(v, cos, sin, low_half):
    fwd = pltpu.roll(v, LANES - 16, axis=1)
    bwd = pltpu.roll(v, 16, axis=1)
    return v * cos + jnp.where(low_half, fwd, bwd) * sin


def _proj_kernel(n_lat_tiles, x_ref, ctx_ref, sh_ref, sc_ref, cos_ref, sin_ref, w_ref, wvt_ref,
                 qkv_ref, z_ref, gt_ref, dq_ref, dk_ref, dvt_ref):
    rows = jnp.where(pl.program_id(1) >= n_lat_tiles, ctx_ref[0], x_ref[0])
    h = (_normalize(rows) * (1.0 + sc_ref[0]) + sh_ref[0]).astype(BF16)
    qkv_ref[0] = _dot(h, w_ref[:, C_QKV:C_Z])
    z_ref[0] = _dot(h, w_ref[:, C_Z:C_GB])
    gt_ref[0] = _dot(h, w_ref[:, C_GB:C_DQ])
    cos = cos_ref[...]
    sin = sin_ref[...]
    lane = lax.broadcasted_iota(jnp.int32, cos.shape, 1)
    low_half = (lane % 32) < 16
    for j in range(D_QK // LANES):
        q = _dot(h, w_ref[:, C_DQ + j * LANES:C_DQ + (j + 1) * LANES])
        dq_ref[0, :, j * LANES:(j + 1) * LANES] = (_rope(q, cos, sin, low_half) * Q_SCALE).astype(BF16)
        k = _dot(h, w_ref[:, C_DK + j * LANES:C_DK + (j + 1) * LANES])
        dk_ref[0, :, j * LANES:(j + 1) * LANES] = _rope(k, cos, sin, low_half).astype(BF16)
    dvt_ref[0] = _dot_nt(wvt_ref[...], h).astype(BF16)


def _proj(x, ctx, mod3, cos_t, sin_t, w_all, wv_t):
    bsz, seq, _ = x.shape
    lc = seq + ctx.shape[1]
    nt = lc // ROW_TILE
    n_lat_tiles = seq // ROW_TILE
    ctx_row = mod3.shape[0] - 1

    def mod_idx(col):
        return lambda b, i: (jnp.where(i >= n_lat_tiles, ctx_row, b), 0, col)

    def row_spec(width):
        return pl.BlockSpec((1, ROW_TILE, width), lambda b, i: (b, i, 0))

    outs = [(G_QKV, F32), (G_Z, F32), (2 * LANES, F32), (D_QK, BF16), (D_QK, BF16)]
    return pl.pallas_call(
        functools.partial(_proj_kernel, n_lat_tiles),
        grid=(bsz, nt),
        in_specs=[pl.BlockSpec((1, ROW_TILE, D), lambda b, i: (b, jnp.minimum(i, n_lat_tiles - 1), 0)),
                  pl.BlockSpec((1, ROW_TILE, D), lambda b, i: (b, jnp.maximum(i - n_lat_tiles, 0), 0)),
                  pl.BlockSpec((1, 1, D), mod_idx(0)),
                  pl.BlockSpec((1, 1, D), mod_idx(1)),
                  pl.BlockSpec((ROW_TILE, LANES), lambda b, i: (i, 0)),
                  pl.BlockSpec((ROW_TILE, LANES), lambda b, i: (i, 0)),
                  pl.BlockSpec((D, C_END), lambda b, i: (0, 0)),
                  pl.BlockSpec((DH * DDV, D), lambda b, i: (0, 0))],
        out_specs=[row_spec(w) for w, _ in outs]
        + [pl.BlockSpec((1, DH * DDV, ROW_TILE), lambda b, i: (b, 0, i))],
        out_shape=[jax.ShapeDtypeStruct((bsz, lc, w), dt) for w, dt in outs]
        + [jax.ShapeDtypeStruct((bsz, DH * DDV, lc), BF16)],
        compiler_params=_cparams(("arbitrary", "arbitrary")),
        name="proj",
    )(x, ctx, mod3, mod3, cos_t, sin_t, w_all, wv_t)


def _prep_kernel(n_lat_tiles, nt, main_ref, prev_ref, next_ref, gt_ref, cw_ref, alog_ref, dtb_ref,
                 lo_ref, up_ref, q_ref, k_ref, v_ref, beta_ref, gc_ref, ext_ref):
    i = pl.program_id(1)
    halo = prev_ref.shape[1]
    has_prev = jnp.logical_and(i != 0, i != n_lat_tiles)
    has_next = jnp.logical_and(i != n_lat_tiles - 1, i != nt - 1)
    ext_ref[0:halo, :] = prev_ref[0] * has_prev.astype(F32)
    ext_ref[halo:halo + ROW_TILE, :] = main_ref[0]
    ext_ref[halo + ROW_TILE:, :] = next_ref[0] * has_next.astype(F32)
    pad = GCONV // 2
    acc = None
    for j in range(GCONV):
        start = halo - pad + j
        term = ext_ref[start:start + ROW_TILE, :] * cw_ref[j:j + 1, :]
        acc = term if acc is None else acc + term
    qkv = _silu(acc)
    for h in range(GH):
        q = qkv[:, h * GDK:(h + 1) * GDK]
        q_ref[0, :, h * GDK:(h + 1) * GDK] = q * (lax.rsqrt(jnp.sum(q * q, axis=-1, keepdims=True) + NORM_EPS)
                                                * (GDK ** -0.5))
        k = qkv[:, G_QK + h * GDK:G_QK + (h + 1) * GDK]
        k_ref[0, :, h * GDK:(h + 1) * GDK] = k * lax.rsqrt(jnp.sum(k * k, axis=-1, keepdims=True) + NORM_EPS)
    v_ref[0] = qkv[:, 2 * G_QK:]
    gt = gt_ref[0]
    beta_ref[0] = jax.nn.sigmoid(gt[:, :LANES])
    a = gt[:, LANES:] + dtb_ref[...]
    softplus = jnp.maximum(a, 0.0) + jnp.log(1.0 + jnp.exp(-jnp.abs(a)))
    g = -jnp.exp(alog_ref[...]) * softplus
    lo = lo_ref[...]
    up = up_ref[...]
    fwd = None
    bwd = None
    for part in _split3(g):
        f = _dot(lo, part)
        r = _dot(up, part)
        fwd = f if fwd is None else fwd + f
        bwd = r if bwd is None else bwd + r
    lane = lax.broadcasted_iota(jnp.int32, g.shape, 1)
    gc_ref[0] = jnp.where(lane < GH, fwd, bwd)


def _prep(qkv, gt, conv_w8, alog_v, dtb_v, tri_lo, tri_up, n_lat_tiles):
    bsz, lc, _ = qkv.shape
    nt = lc // ROW_TILE
    halo = 8
    per = ROW_TILE // halo
    last = lc // halo - 1

    def row_spec(width):
        return pl.BlockSpec((1, ROW_TILE, width), lambda b, i: (b, i, 0))

    def const_spec(shape):
        return pl.BlockSpec(shape, lambda b, i: tuple(0 for _ in shape))

    outs = [(G_QK, F32), (G_QK, F32), (GH * GDV, F32), (LANES, F32), (LANES, F32)]
    return pl.pallas_call(
        functools.partial(_prep_kernel, n_lat_tiles, nt),
        grid=(bsz, nt),
        in_specs=[row_spec(G_QKV),
                  pl.BlockSpec((1, halo, G_QKV), lambda b, i: (b, jnp.maximum(i * per - 1, 0), 0)),
                  pl.BlockSpec((1, halo, G_QKV), lambda b, i: (b, jnp.minimum((i + 1) * per, last), 0)),
                  row_spec(2 * LANES),
                  const_spec((8, G_QKV)), const_spec((1, LANES)), const_spec((1, LANES)),
                  const_spec((ROW_TILE, ROW_TILE)), const_spec((ROW_TILE, ROW_TILE))],
        out_specs=[row_spec(w) for w, _ in outs],
        out_shape=[jax.ShapeDtypeStruct((bsz, lc, w), dt) for w, dt in outs],
        scratch_shapes=[pltpu.VMEM((ROW_TILE + 2 * halo, G_QKV), F32)],
        compiler_params=_cparams(("arbitrary", "arbitrary")),
        name="prep",
    )(qkv, qkv, qkv, gt, conv_w8, alog_v, dtb_v, tri_lo, tri_up)


def _chunk_masks(n, rev):
    row = lax.broadcasted_iota(jnp.int32, (n, n), 0)
    col = lax.broadcasted_iota(jnp.int32, (n, n), 1)
    same = (row // CHUNK) == (col // CHUNK)
    incl = jnp.logical_and(same, (col >= row) if rev else (col <= row))
    strict = jnp.logical_and(same, (col > row) if rev else (col < row))
    return incl, strict


def _decay(gcol, grow, incl):
    return jnp.where(incl, jnp.exp(jnp.minimum(gcol - grow, 0.0)), 0.0)


def _chunk_kernel(k_ref, v_ref, beta_ref, gc_ref, uw_ref):
    n = k_ref.shape[1]
    gc = gc_ref[0]
    gct = gc.T
    beta = beta_ref[0]
    row = lax.broadcasted_iota(jnp.int32, (n, n), 0)
    col = lax.broadcasted_iota(jnp.int32, (n, n), 1)
    eye = (row == col).astype(F32)
    sizes = [2 ** j for j in range(1, int(math.log2(CHUNK)))]
    same = {r: (row // r) == (col // r) for r in sizes + [CHUNK]}
    a_all, t_all, x_all = [], [], []
    for d in range(2):
        incl, strict = _chunk_masks(n, d == 1)
        for h in range(GH):
            c = d * GH + h
            k = k_ref[0, :, h * GDK:(h + 1) * GDK]
            v = v_ref[0, :, h * GDV:(h + 1) * GDV]
            bcol = beta[:, c:c + 1]
            gcol = gc[:, c:c + 1]
            grow = gct[c:c + 1, :]
            kb = k * bcol
            a = jnp.where(strict, _dot_nt(kb.astype(BF16), k.astype(BF16)) * _decay(gcol, grow, incl), 0.0)
            a_all.append(a.astype(BF16))
            t_all.append(eye - jnp.where(same[2], a, 0.0))
            x_all.append(jnp.concatenate([v * bcol, kb * jnp.exp(gcol)], axis=1).astype(BF16))
    n_prob = len(a_all)
    zero = jnp.zeros((n, n), BF16)
    for r in sizes:
        off_mask = jnp.logical_and(same[2 * r], jnp.logical_not(same[r]))
        t_b = [t.astype(BF16) for t in t_all]
        inner = [_dot(jnp.where(off_mask, a_all[c], zero), t_b[c]).astype(BF16) for c in range(n_prob)]
        t_all = [t_all[c] - _dot(t_b[c], inner[c]) for c in range(n_prob)]
    for c in range(n_prob):
        uw_ref[0, c] = _dot(t_all[c].astype(BF16), x_all[c])


def _chunk(k, v, beta, gc):
    bsz, lc, _ = k.shape
    nt = lc // ROW_TILE

    def row_spec(width):
        return pl.BlockSpec((1, ROW_TILE, width), lambda b, i: (b, i, 0))

    return pl.pallas_call(
        _chunk_kernel,
        grid=(bsz, nt),
        in_specs=[row_spec(G_QK), row_spec(GH * GDV), row_spec(LANES), row_spec(LANES)],
        out_specs=pl.BlockSpec((1, 2 * GH, ROW_TILE, GDV + GDK), lambda b, i: (b, 0, i, 0)),
        out_shape=jax.ShapeDtypeStruct((bsz, 2 * GH, lc, GDV + GDK), F32),
        compiler_params=_cparams(("arbitrary", "arbitrary")),
        name="chunk",
    )(k, v, beta, gc)


def _scan_kernel(qf_ref, kf_ref, gf_ref, uwf_ref, qr_ref, kr_ref, gr_ref, uwr_ref,
                 of_ref, or_ref, s_ref):
    @pl.when(pl.program_id(1) == 0)
    def _():
        s_ref[...] = jnp.zeros_like(s_ref)

    n_chunks = ROW_TILE // CHUNK
    row = lax.broadcasted_iota(jnp.int32, (CHUNK, CHUNK), 0)
    col = lax.broadcasted_iota(jnp.int32, (CHUNK, CHUNK), 1)
    state = [s_ref[c] for c in range(2 * GH)]
    dirs = ((qf_ref, kf_ref, gf_ref, uwf_ref, of_ref), (qr_ref, kr_ref, gr_ref, uwr_ref, or_ref))

    def chunk_rows(step, d):
        ci = n_chunks - 1 - step if d == 1 else step
        return slice(ci * CHUNK, (ci + 1) * CHUNK)

    pre = []
    for step in range(n_chunks):
        per_chain = []
        for d, (q_ref, k_ref, g_ref, uw_ref, o_ref) in enumerate(dirs):
            rev = d == 1
            incl = (col >= row) if rev else (col <= row)
            rows = chunk_rows(step, d)
            gc = g_ref[0, rows, :]
            gct = gc.T
            last = 0 if rev else CHUNK - 1
            for h in range(GH):
                c = d * GH + h
                q = q_ref[0, rows, h * GDK:(h + 1) * GDK]
                k = k_ref[0, rows, h * GDK:(h + 1) * GDK]
                gcol = gc[:, c:c + 1]
                grow = gct[c:c + 1, :]
                g_last = gcol[last:last + 1, :]
                a_qk = (_dot_nt(q.astype(BF16), k.astype(BF16)) * _decay(gcol, grow, incl)).astype(BF16)
                wq = jnp.concatenate([uw_ref[0, h, rows, GDV:], q * jnp.exp(gcol)], axis=0).astype(BF16)
                k_tail_t = (k * jnp.exp(g_last - gcol)).T.astype(BF16)
                per_chain.append((a_qk, wq, k_tail_t, jnp.exp(g_last)))
        pre.append(per_chain)

    for step in range(n_chunks):
        ws = [_dot(pre[step][c][1], state[c].astype(BF16)) for c in range(2 * GH)]
        v_new = []
        for d, (q_ref, k_ref, g_ref, uw_ref, o_ref) in enumerate(dirs):
            rows = chunk_rows(step, d)
            for h in range(GH):
                v_new.append((uw_ref[0, h, rows, :GDV] - ws[d * GH + h][:CHUNK]).astype(BF16))
        for d, (q_ref, k_ref, g_ref, uw_ref, o_ref) in enumerate(dirs):
            rows = chunk_rows(step, d)
            for h in range(GH):
                c = d * GH + h
                a_qk, _, k_tail_t, decay_last = pre[step][c]
                o_ref[0, rows, h * GDV:(h + 1) * GDV] = ws[c][CHUNK:] + _dot(a_qk, v_new[c])
                state[c] = state[c] * decay_last + _dot(k_tail_t, v_new[c])
    for c in range(2 * GH):
        s_ref[c] = state[c]


def _scan(q, k, gc, uw, n_lat_tiles):
    bsz, lc, _ = q.shape
    nt = lc // ROW_TILE
    n_ctx_tiles = nt - n_lat_tiles

    def fwd(b, i):
        return jnp.where(i < n_ctx_tiles, n_lat_tiles + i, i - n_ctx_tiles)

    def bwd(b, i):
        return jnp.where(i < n_ctx_tiles, nt - 1 - i, n_lat_tiles - 1 - (i - n_ctx_tiles))

    def specs(tile):
        return [pl.BlockSpec((1, ROW_TILE, G_QK), lambda b, i: (b, tile(b, i), 0)),
                pl.BlockSpec((1, ROW_TILE, G_QK), lambda b, i: (b, tile(b, i), 0)),
                pl.BlockSpec((1, ROW_TILE, LANES), lambda b, i: (b, tile(b, i), 0))]

    def uw_spec(tile, d):
        return pl.BlockSpec((1, GH, ROW_TILE, GDV + GDK), lambda b, i: (b, d, tile(b, i), 0))

    def out_spec(tile):
        return pl.BlockSpec((1, ROW_TILE, GH * GDV), lambda b, i: (b, tile(b, i), 0))

    return pl.pallas_call(
        _scan_kernel,
        grid=(bsz, nt),
        in_specs=specs(fwd) + [uw_spec(fwd, 0)] + specs(bwd) + [uw_spec(bwd, 1)],
        out_specs=[out_spec(fwd), out_spec(bwd)],
        out_shape=[jax.ShapeDtypeStruct((bsz, lc, GH * GDV), F32)] * 2,
        scratch_shapes=[pltpu.VMEM((2 * GH, GDK, GDV), F32)],
        compiler_params=_cparams(("arbitrary", "arbitrary")),
        name="scan",
    )(q, k, gc, uw, q, k, gc, uw)


def _attn_kernel(lam_init, q_ref, k_ref, vt_ref, lam_ref, nw_ref, o_ref):
    lam = lam_ref[...]
    lam_full = (jnp.exp(jnp.sum(lam[0:1] * lam[1:2], axis=-1, keepdims=True))
                - jnp.exp(jnp.sum(lam[2:3] * lam[3:4], axis=-1, keepdims=True)) + lam_init)
    q = q_ref[0]
    tq = q.shape[0]
    n_keys = k_ref.shape[1]
    lane = lax.broadcasted_iota(jnp.int32, q.shape, 1)
    zero = jnp.zeros_like(q)
    qc = [jnp.where(lane < DDK, q, zero), jnp.where(lane >= DDK, q, zero)]
    m = [jnp.full((1, tq), -jnp.inf, F32) for _ in range(2)]
    l = [jnp.zeros((1, tq), F32) for _ in range(2)]
    acc = [jnp.zeros((DDV, tq), F32) for _ in range(2)]
    n_tiles = n_keys // KEY_TILE
    ones = jnp.ones((16, KEY_TILE), BF16)

    def scores(t):
        k_t = k_ref[0, t * KEY_TILE:(t + 1) * KEY_TILE, :]
        return [_dot_nt(k_t, qc[c]) for c in range(2)]

    pending = [scores(t) for t in range(min(SCORE_LOOKAHEAD, n_tiles))]
    for t in range(n_tiles):
        s_cur = pending.pop(0)
        if t + SCORE_LOOKAHEAD < n_tiles:
            pending.append(scores(t + SCORE_LOOKAHEAD))
        vt_t = jnp.concatenate([vt_ref[0, :, t * KEY_TILE:(t + 1) * KEY_TILE], ones], axis=0)
        for c in range(2):
            s = s_cur[c]
            m_new = jnp.maximum(m[c], jnp.max(s, axis=0, keepdims=True))
            p = jnp.exp2(s - m_new)
            scale = jnp.exp2(m[c] - m_new)
            pv = _dot(vt_t, p.astype(BF16))
            l[c] = scale * l[c] + pv[DDV:DDV + 1]
            acc[c] = scale * acc[c] + pv[:DDV]
            m[c] = m_new
    o = acc[0] / l[0] - lam_full * (acc[1] / l[1])
    y = o * lax.rsqrt(jnp.mean(o * o, axis=0, keepdims=True) + NORM_EPS)
    o_ref[0] = y.T * nw_ref[...] * (1.0 - lam_init)


def _attn(dq, dk, dvt, lam, norm_w, lam_init, lq, tq):
    bsz, lc, _ = dq.shape
    return pl.pallas_call(
        functools.partial(_attn_kernel, lam_init),
        grid=(bsz, DH, lq // tq),
        in_specs=[pl.BlockSpec((1, tq, 2 * DDK), lambda b, h, i: (b, i, h)),
                  pl.BlockSpec((1, lc, 2 * DDK), lambda b, h, i: (b, 0, h)),
                  pl.BlockSpec((1, DDV, lc), lambda b, h, i: (b, h, 0)),
                  pl.BlockSpec((4, DDK), lambda b, h, i: (0, 0)),
                  pl.BlockSpec((1, DDV), lambda b, h, i: (0, 0))],
        out_specs=pl.BlockSpec((1, tq, DDV), lambda b, h, i: (b, i, h)),
        out_shape=jax.ShapeDtypeStruct((bsz, lq, DH * DDV), F32),
        compiler_params=_cparams(("arbitrary", "arbitrary", "arbitrary")),
        name="attn",
    )(dq, dk, dvt, lam, norm_w.reshape(1, DDV))


def _out_kernel(alpha, of_ref, or_ref, z_ref, yd_ref, x_ref, g1_ref, sh2_ref, sc2_ref, gnw_ref,
                wo_ref, lng_ref, lnb_ref, wr_ref, br_ref, x1_ref, h2_ref, lg_ref):
    o = of_ref[0] + or_ref[0]
    z = z_ref[0]
    acc = _dot(yd_ref[0].astype(BF16), wo_ref[G_Z:, :])
    for h in range(GH):
        oh = o[:, h * GDV:(h + 1) * GDV]
        yh = (oh * lax.rsqrt(jnp.mean(oh * oh, axis=-1, keepdims=True) + NORM_EPS) * gnw_ref[...]
              * _silu(z[:, h * GDV:(h + 1) * GDV]))
        acc = acc + _dot(yh.astype(BF16), wo_ref[h * GDV:(h + 1) * GDV, :])
    x1 = _normalize(alpha * x_ref[0] + g1_ref[0] * acc) * lng_ref[...] + lnb_ref[...]
    x1_ref[0] = x1
    h2 = _normalize(x1) * (1.0 + sc2_ref[0]) + sh2_ref[0]
    h2_ref[0] = h2
    lg = br_ref[...]
    for part in _split3(h2):
        for wpart in range(3):
            lg = lg + _dot(part, wr_ref[wpart])
    lg_ref[0] = lg


def _out(o_f, o_r, z, y_diff, xc, mod3, gdn_norm_w, w_out, ln_g, ln_b, w_router3, b_router, alpha):
    bsz, lq, _ = y_diff.shape
    nt = lq // ROW_TILE

    def row_spec(width):
        return pl.BlockSpec((1, ROW_TILE, width), lambda b, i: (b, i, 0))

    def mod_spec(col):
        return pl.BlockSpec((1, 1, D), lambda b, i: (b, 0, col))

    def const_spec(shape):
        return pl.BlockSpec(shape, lambda b, i: tuple(0 for _ in shape))

    return pl.pallas_call(
        functools.partial(_out_kernel, alpha),
        grid=(bsz, nt),
        in_specs=[row_spec(GH * GDV), row_spec(GH * GDV), row_spec(G_Z), row_spec(DH * DDV), row_spec(D),
                  mod_spec(2), mod_spec(3), mod_spec(4),
                  const_spec((1, GDV)), const_spec((G_Z + DH * DDV, D)),
                  const_spec((1, D)), const_spec((1, D)),
                  const_spec((3, D, LANES)), const_spec((1, LANES))],
        out_specs=[row_spec(D), row_spec(D), row_spec(LANES)],
        out_shape=[jax.ShapeDtypeStruct((bsz, lq, D), F32), jax.ShapeDtypeStruct((bsz, lq, D), F32),
                   jax.ShapeDtypeStruct((bsz, lq, LANES), F32)],
        compiler_params=_cparams(("arbitrary", "arbitrary")),
        name="out",
    )(o_f, o_r, z, y_diff, xc, mod3, mod3, mod3, gdn_norm_w.reshape(1, GDV), w_out,
      ln_g.reshape(1, D), ln_b.reshape(1, D), w_router3, b_router)


def _route_kernel(lg_ref, tri_ref, ids_ref, gate_ref, cnt_ref, run_ref):
    @pl.when(pl.program_id(0) == 0)
    def _():
        run_ref[...] = jnp.zeros_like(run_ref)

    lg = lg_ref[...]
    lane_i = lax.broadcasted_iota(jnp.int32, lg.shape, 1)
    lane = lane_i.astype(F32)
    neg = jnp.float32(-jnp.inf)

    def first_max(mask):
        masked = jnp.where(mask, lg, neg)
        m = jnp.max(masked, axis=-1, keepdims=True)
        idx = jnp.min(jnp.where(jnp.logical_and(mask, masked == m), lane, float(LANES)), axis=-1, keepdims=True)
        return m, idx

    is_group = lane_i < N_GROUPS
    m_g, grp = first_max(is_group)
    p_g = 1.0 / jnp.sum(jnp.where(is_group, jnp.exp(lg - m_g), 0.0), axis=-1, keepdims=True)
    e_lane = lane - float(N_GROUPS)
    lo_e = grp * float(E_PER_GROUP)
    in_group = jnp.logical_and(e_lane >= lo_e, e_lane < lo_e + float(E_PER_GROUP))
    l0, i0 = first_max(in_group)
    l1, i1 = first_max(jnp.logical_and(in_group, lane != i0))
    r = jnp.exp(l1 - l0)
    gate0 = p_g / (1.0 + r)
    gate1 = p_g * r / (1.0 + r)
    e0 = i0 - float(N_GROUPS)
    e1 = i1 - float(N_GROUPS)
    oh0 = (lane == e0).astype(F32)
    oh1 = (lane == e1).astype(F32)
    both = oh0 + oh1
    before = _dot(tri_ref[...], both.astype(BF16)) + run_ref[...]
    rank0 = jnp.sum(oh0 * before, axis=-1, keepdims=True)
    rank1 = jnp.sum(oh1 * before, axis=-1, keepdims=True)
    run_ref[...] = run_ref[...] + jnp.sum(both, axis=0, keepdims=True)
    cnt_ref[...] = run_ref[...]
    ids = jnp.where(lane_i == 0, e0, jnp.where(lane_i == 1, e1,
                                               jnp.where(lane_i == 2, rank0, jnp.where(lane_i == 3, rank1, 0.0))))
    ids_ref[...] = ids.T[:8].astype(jnp.int32)
    gate_ref[...] = jnp.where(lane_i == 0, gate0, jnp.where(lane_i == 1, gate1, 0.0))


def _route(logits, tri_strict):
    t = logits.shape[0]
    tile = tri_strict.shape[0]
    return pl.pallas_call(
        _route_kernel,
        grid=(t // tile,),
        in_specs=[pl.BlockSpec((tile, LANES), lambda i: (i, 0)),
                  pl.BlockSpec((tile, tile), lambda i: (0, 0))],
        out_specs=[pl.BlockSpec((8, tile), lambda i: (0, i)),
                   pl.BlockSpec((tile, LANES), lambda i: (i, 0)),
                   pl.BlockSpec((1, LANES), lambda i: (0, 0))],
        out_shape=[jax.ShapeDtypeStruct((8, t), jnp.int32), jax.ShapeDtypeStruct((t, LANES), F32),
                   jax.ShapeDtypeStruct((1, LANES), F32)],
        scratch_shapes=[pltpu.VMEM((1, LANES), F32)],
        compiler_params=_cparams(("arbitrary",)),
        name="route",
    )(logits, tri_strict)


DMA_TOKENS = 512


def _dispatch_kernel(tail_ref, used_ref, dest_ref, h_ref, buf_hbm, zero_ref, sem, zsem):
    @pl.when(pl.program_id(0) == 0)
    def _():
        zero_ref[...] = jnp.zeros_like(zero_ref)
        n_blocks = buf_hbm.shape[0] // FFN_BLOCK

        def block_copy(b):
            return pltpu.make_async_copy(zero_ref, buf_hbm.at[pl.ds(b * FFN_BLOCK, FFN_BLOCK)], zsem)

        for e in range(N_EXPERTS):
            @pl.when(tail_ref[e] >= 0)
            def _():
                block_copy(tail_ref[e]).start()
        lax.fori_loop(used_ref[0], n_blocks, lambda b, carry: (block_copy(b).start(), carry)[1], 0)
        for e in range(N_EXPERTS):
            @pl.when(tail_ref[e] >= 0)
            def _():
                block_copy(tail_ref[e]).wait()
        lax.fori_loop(used_ref[0], n_blocks, lambda b, carry: (block_copy(b).wait(), carry)[1], 0)

    def start(tok, carry):
        for j in range(2):
            pltpu.make_async_copy(h_ref.at[pl.ds(tok, 1)], buf_hbm.at[pl.ds(dest_ref[2 * tok + j], 1)], sem).start()
        return carry

    lax.fori_loop(0, DMA_TOKENS, start, 0, unroll=8)
    for _ in range(2):
        pltpu.make_async_copy(h_ref, buf_hbm.at[pl.ds(0, DMA_TOKENS)], sem).wait()


def _dispatch(tail_block, n_used, dest_flat, h2, n_rows):
    t, d = h2.shape
    grid_spec = pltpu.PrefetchScalarGridSpec(
        num_scalar_prefetch=2,
        grid=(t // DMA_TOKENS,),
        in_specs=[pl.BlockSpec((2 * DMA_TOKENS,), lambda i, tail, used: (i,), memory_space=pltpu.SMEM),
                  pl.BlockSpec((DMA_TOKENS, d), lambda i, tail, used: (i, 0))],
        out_specs=pl.BlockSpec(memory_space=pl.ANY),
        scratch_shapes=[pltpu.VMEM((FFN_BLOCK, d), h2.dtype), pltpu.SemaphoreType.DMA(()),
                        pltpu.SemaphoreType.DMA(())],
    )
    return pl.pallas_call(
        _dispatch_kernel,
        grid_spec=grid_spec,
        out_shape=jax.ShapeDtypeStruct((n_rows, d), h2.dtype),
        compiler_params=_cparams(("arbitrary",)),
        name="dispatch",
    )(tail_block, n_used, dest_flat, h2)


def _combine_kernel(alpha, dest_ref, ys_hbm, gate_ref, x1_ref, g2_ref, lng_ref, lnb_ref, o_ref, r0_ref, r1_ref, sem):
    n = x1_ref.shape[0]

    def start(tok, carry):
        for j, r_ref in enumerate((r0_ref, r1_ref)):
            pltpu.make_async_copy(ys_hbm.at[pl.ds(dest_ref[2 * tok + j], 1)], r_ref.at[pl.ds(tok, 1)], sem).start()
        return carry

    lax.fori_loop(0, n, start, 0, unroll=8)
    for r_ref in (r0_ref, r1_ref):
        pltpu.make_async_copy(ys_hbm.at[pl.ds(0, n)], r_ref, sem).wait()
    gate = gate_ref[...]
    y = gate[:, 0:1] * r0_ref[...] + gate[:, 1:2] * r1_ref[...]
    o_ref[...] = _normalize(alpha * x1_ref[...] + g2_ref[0] * y) * lng_ref[...] + lnb_ref[...]


def _combine(dest_flat, ys, gate, x1, mod3, ln_g, ln_b, alpha, rows_per_batch):
    t, d = x1.shape
    per_batch = rows_per_batch // DMA_TOKENS
    return pl.pallas_call(
        functools.partial(_combine_kernel, alpha),
        grid=(t // DMA_TOKENS,),
        in_specs=[pl.BlockSpec((2 * DMA_TOKENS,), lambda i: (i,), memory_space=pltpu.SMEM),
                  pl.BlockSpec(memory_space=pl.ANY),
                  pl.BlockSpec((DMA_TOKENS, LANES), lambda i: (i, 0)),
                  pl.BlockSpec((DMA_TOKENS, d), lambda i: (i, 0)),
                  pl.BlockSpec((1, 1, d), lambda i: (i // per_batch, 0, 5)),
                  pl.BlockSpec((1, d), lambda i: (0, 0)),
                  pl.BlockSpec((1, d), lambda i: (0, 0))],
        out_specs=pl.BlockSpec((DMA_TOKENS, d), lambda i: (i, 0)),
        out_shape=jax.ShapeDtypeStruct((t, d), F32),
        scratch_shapes=[pltpu.VMEM((DMA_TOKENS, d), F32), pltpu.VMEM((DMA_TOKENS, d), F32),
                        pltpu.SemaphoreType.DMA(())],
        compiler_params=_cparams(("arbitrary",)),
        name="combine",
    )(dest_flat, ys, gate, x1, mod3, ln_g.reshape(1, d), ln_b.reshape(1, d))


def _ffn_kernel(be_ref, used_ref, x_ref, wg_ref, wu_ref, wd_ref, o_ref, wgb_ref, wub_ref, wdb_ref):
    i = pl.program_id(0)

    @pl.when(i >= used_ref[0])
    def _():
        o_ref[...] = jnp.zeros_like(o_ref)

    @pl.when(i < used_ref[0])
    def _():
        changed = jnp.logical_or(i == 0, be_ref[i] != be_ref[jnp.maximum(i - 1, 0)])

        @pl.when(changed)
        def _():
            wgb_ref[...] = wg_ref[0].astype(BF16)
            wub_ref[...] = wu_ref[0].astype(BF16)
            wdb_ref[...] = wd_ref[0].astype(BF16)

        x = x_ref[...].astype(BF16)
        hidden = _silu(_dot(x, wgb_ref[...])) * _dot(x, wub_ref[...])
        o_ref[...] = _dot(hidden.astype(BF16), wdb_ref[...])


def _ffn(block_e, n_used, buf, w_gate, w_up, w_down):
    n_rows, d = buf.shape
    n_blocks = n_rows // FFN_BLOCK

    def blk(i, used):
        return jnp.minimum(i, used[0] - 1)

    grid_spec = pltpu.PrefetchScalarGridSpec(
        num_scalar_prefetch=2,
        grid=(n_blocks,),
        in_specs=[pl.BlockSpec((FFN_BLOCK, d), lambda i, be, used: (blk(i, used), 0)),
                  pl.BlockSpec((1, d, D_EXPERT), lambda i, be, used: (be[blk(i, used)], 0, 0)),
                  pl.BlockSpec((1, d, D_EXPERT), lambda i, be, used: (be[blk(i, used)], 0, 0)),
                  pl.BlockSpec((1, D_EXPERT, d), lambda i, be, used: (be[blk(i, used)], 0, 0))],
        out_specs=pl.BlockSpec((FFN_BLOCK, d), lambda i, be, used: (i, 0)),
        scratch_shapes=[pltpu.VMEM((d, D_EXPERT), BF16), pltpu.VMEM((d, D_EXPERT), BF16),
                        pltpu.VMEM((D_EXPERT, d), BF16)],
    )
    return pl.pallas_call(
        _ffn_kernel,
        grid_spec=grid_spec,
        out_shape=jax.ShapeDtypeStruct((n_rows, d), F32),
        compiler_params=_cparams(("arbitrary",)),
        name="ffn",
    )(block_e, n_used, buf, w_gate, w_up, w_down)


def _rope_tables(seq_len, n_ctx_rows):
    lane = np.arange(LANES)
    within = lane % DDK
    freq = within % 16
    use_col = within >= DDK // 2
    sign = np.where((lane % 32) < 16, -1.0, 1.0).astype(np.float32)
    half = DDK // 2
    inv_freq = ROPE_THETA ** (-jnp.arange(0, half, 2, dtype=F32) / half)
    t = jnp.arange(seq_len)
    pos = jnp.where(jnp.asarray(use_col)[None, :], (t % GRID_W)[:, None], (t // GRID_W)[:, None]).astype(F32)
    ang = pos * inv_freq[jnp.asarray(freq)][None, :]
    cos = jnp.concatenate([jnp.cos(ang), jnp.ones((n_ctx_rows, LANES), F32)], axis=0)
    sin = jnp.concatenate([jnp.sin(ang) * jnp.asarray(sign)[None, :], jnp.zeros((n_ctx_rows, LANES), F32)], axis=0)
    return cos, sin


def _block_tri(n, block, upper):
    i = np.arange(n)
    same = (i[:, None] // block) == (i[None, :] // block)
    tri = (i[None, :] >= i[:, None]) if upper else (i[None, :] <= i[:, None])
    return jnp.asarray((same & tri).astype(np.float32), dtype=BF16)


def _pack_w_in(w_in):
    pad = jnp.zeros((D, LANES - G_GATES), w_in.dtype)
    off = G_QKV + G_Z
    v_start = G_COLS + 2 * D_QK
    w_all = jnp.concatenate([
        w_in[:, :G_QKV], w_in[:, G_QKV:off],
        w_in[:, off:off + G_GATES], pad,
        w_in[:, off + G_GATES:G_COLS], pad,
        w_in[:, G_COLS:v_start],
    ], axis=1).astype(BF16)
    return w_all, w_in[:, v_start:].T.astype(BF16)


def kernel(x, c, ctx, c_ctx, w_ada, b_ada, w_in, conv_w, gdn_a_log, gdn_dt_bias, gdn_norm_w, diff_lambda,
           diff_norm_w, w_out, ln1_g, ln1_b, w_router_group, b_router_group, w_router_expert, b_router_expert,
           w_expert_gate, w_expert_up, w_expert_down, ln2_g, ln2_b):
    depth = w_ada.shape[0]
    assert depth == 1, "single-layer block: the context stream never feeds a later layer"
    bsz, seq, _ = x.shape
    n_ctx = ctx.shape[1]
    assert seq % ROW_TILE == 0 and n_ctx % ROW_TILE == 0 and seq % GRID_W == 0 and bsz < 8
    n_lat_tiles = seq // ROW_TILE
    alpha = (2.0 * depth) ** 0.25
    lam_init = 0.8 - 0.6 * math.exp(-0.3 * 0)
    i = 0

    cc = jnp.zeros((8, D), F32).at[:bsz].set(c).at[7].set(c_ctx)
    mod = _ada(cc, w_ada[i], b_ada[i])
    mod3 = mod.reshape(8, 1, 6 * D)

    cos_t, sin_t = _rope_tables(seq, n_ctx)
    qkv, z, gt, dq, dk, dvt = _proj(x, ctx, mod3, cos_t, sin_t, *_pack_w_in(w_in[i]))

    conv_w8 = jnp.zeros((8, G_QKV), F32).at[:GCONV].set(conv_w[i])
    alog_v = jnp.zeros((1, LANES), F32).at[0, :G_GATES].set(gdn_a_log[i].reshape(-1))
    dtb_v = jnp.zeros((1, LANES), F32).at[0, :G_GATES].set(gdn_dt_bias[i].reshape(-1))
    gq, gk, gv, beta, gc = _prep(qkv, gt, conv_w8, alog_v, dtb_v,
                                 _block_tri(ROW_TILE, CHUNK, False), _block_tri(ROW_TILE, CHUNK, True), n_lat_tiles)
    uw = _chunk(gk, gv, beta, gc)
    o_f, o_r = _scan(gq, gk, gc, uw, n_lat_tiles)

    y_diff = _attn(dq, dk, dvt, diff_lambda[i], diff_norm_w[i], lam_init, seq, tq=ATTN_Q_TILE)

    w_router = jnp.zeros((D, LANES), F32).at[:, :N_GROUPS].set(w_router_group[i]) \
        .at[:, N_GROUPS:N_GROUPS + N_EXPERTS].set(w_router_expert[i])
    w_router3 = jnp.stack(_split3(w_router))
    b_router = jnp.zeros((1, LANES), F32).at[0, :N_GROUPS].set(b_router_group[i]) \
        .at[0, N_GROUPS:N_GROUPS + N_EXPERTS].set(b_router_expert[i])
    x1, h2, logits = _out(o_f, o_r, z, y_diff, x, mod3, gdn_norm_w[i], w_out[i].astype(BF16), ln1_g[i], ln1_b[i],
                          w_router3, b_router, alpha)

    t = bsz * seq
    tri_strict = jnp.asarray(np.tril(np.ones((ROW_TILE, ROW_TILE), np.float32), -1), dtype=BF16)
    ids, gate, counts = _route(logits.reshape(t, LANES), tri_strict)

    cnt = counts[0, :N_EXPERTS].astype(jnp.int32)
    padded = (cnt + FFN_BLOCK - 1) // FFN_BLOCK * FFN_BLOCK
    p_end = jnp.cumsum(padded)
    p_start = p_end - padded
    n_blocks = (2 * t) // FFN_BLOCK + N_EXPERTS
    n_used = p_end[-1:] // FFN_BLOCK
    block_start = jnp.minimum(jnp.arange(n_blocks, dtype=jnp.int32), n_used - 1) * FFN_BLOCK
    block_e = jnp.minimum(jnp.sum((p_end[None, :] <= block_start[:, None]).astype(jnp.int32), axis=1), N_EXPERTS - 1)
    tail_block = jnp.where(cnt > 0, p_end // FFN_BLOCK - 1, -1).astype(jnp.int32)
    dest = (jnp.take(p_start, ids[0:2], axis=0) + ids[2:4]).T.reshape(-1).astype(jnp.int32)

    n_used = n_used.astype(jnp.int32)
    buf = _dispatch(tail_block, n_used, dest, h2.reshape(t, D), n_blocks * FFN_BLOCK)
    ys = _ffn(block_e.astype(jnp.int32), n_used, buf,
              w_expert_gate[i], w_expert_up[i], w_expert_down[i])
    out = _combine(dest, ys, gate, x1.reshape(t, D), mod3, ln2_g[i], ln2_b[i], alpha, seq)
    return out.reshape(bsz, seq, D)
```

```python
import functools
import math

import numpy as np
import jax
import jax.numpy as jnp
from jax import lax
from jax.experimental import pallas as pl
from jax.experimental.pallas import tpu as pltpu

F32 = jnp.float32
BF16 = jnp.bfloat16
HIGHEST = lax.Precision.HIGHEST

D = 1024
GRID_W = 64
ROPE_THETA = 10000.0
GH = 4
GDK = 128
GDV = 128
GCONV = 5
DH = 4
DDK = 64
DDV = 128
G_QK = GH * GDK
G_QKV = 2 * G_QK + GH * GDV
G_Z = GH * GDV
G_GATES = 2 * GH
G_COLS = G_QKV + G_Z + 2 * G_GATES
D_QK = DH * 2 * DDK
N_GROUPS = 4
E_PER_GROUP = 8
N_EXPERTS = N_GROUPS * E_PER_GROUP
D_EXPERT = 512
LN_EPS = 1e-5
NORM_EPS = 1e-6

LANES = 128
ROW_TILE = 256
CHUNK = 64
FFN_BLOCK = 256
FFN_SLABS = 2
KEY_TILE = 256
ATTN_Q_TILE = 512
SCORE_LOOKAHEAD = 2
VMEM_LIMIT = 56 * 1024 * 1024

C_QKV = 0
C_Z = C_QKV + G_QKV
C_GB = C_Z + G_Z
C_GA = C_GB + LANES
C_DQ = C_GA + LANES
C_DK = C_DQ + D_QK
C_END = C_DK + D_QK
Q_SCALE = DDK ** -0.5 * math.log2(math.e)


def _cparams(sem):
    return pltpu.CompilerParams(dimension_semantics=sem, vmem_limit_bytes=VMEM_LIMIT)


def _dot(a, b):
    return jnp.dot(a, b, preferred_element_type=F32)


def _dot_nt(a, b):
    return lax.dot_general(a, b, (((1,), (1,)), ((), ())), preferred_element_type=F32)


def _dot_tn(a, b):
    return lax.dot_general(a, b, (((0,), (0,)), ((), ())), preferred_element_type=F32)


def _normalize(x):
    mu = jnp.mean(x, axis=-1, keepdims=True)
    xc = x - mu
    var = jnp.mean(xc * xc, axis=-1, keepdims=True)
    return xc * lax.rsqrt(var + LN_EPS)


def _silu(x):
    return x * jax.nn.sigmoid(x)


def _split3(x):
    hi = x.astype(BF16)
    r = x - hi.astype(F32)
    mid = r.astype(BF16)
    lo = (r - mid.astype(F32)).astype(BF16)
    return hi, mid, lo


def _ada_kernel(c_ref, w_ref, b_ref, o_ref):
    s = _silu(c_ref[...])
    o_ref[...] = jnp.dot(s, w_ref[...], precision=HIGHEST, preferred_element_type=F32) + b_ref[...]


def _ada(cc, w_ada, b_ada):
    n = w_ada.shape[1]
    bn = 1024
    return pl.pallas_call(
        _ada_kernel,
        grid=(n // bn,),
        in_specs=[pl.BlockSpec((8, D), lambda j: (0, 0)),
                  pl.BlockSpec((D, bn), lambda j: (0, j)),
                  pl.BlockSpec((1, bn), lambda j: (0, j))],
        out_specs=pl.BlockSpec((8, bn), lambda j: (0, j)),
        out_shape=jax.ShapeDtypeStruct((8, n), F32),
        compiler_params=_cparams(("arbitrary",)),
        name="ada",
    )(cc, w_ada, b_ada.reshape(1, n))


def _rope(v, cos, sin, low_half):
    fwd = pltpu.roll(v, LANES - 16, axis=1)
    bwd = pltpu.roll(v, 16, axis=1)
    return v * cos + jnp.where(low_half, fwd, bwd) * sin


def _proj_kernel(n_lat_tiles, x_ref, ctx_ref, sh_ref, sc_ref, cos_ref, sin_ref, w_ref, wvt_ref,
                 qkv_ref, z_ref, gt_ref, dq_ref, dk_ref, dvt_ref):
    rows = jnp.where(pl.program_id(1) >= n_lat_tiles, ctx_ref[0], x_ref[0])
    h = (_normalize(rows) * (1.0 + sc_ref[0]) + sh_ref[0]).astype(BF16)
    qkv_ref[0] = _dot(h, w_ref[:, C_QKV:C_Z])
    z_ref[0] = _dot(h, w_ref[:, C_Z:C_GB])
    gt_ref[0] = _dot(h, w_ref[:, C_GB:C_DQ])
    cos = cos_ref[...]
    sin = sin_ref[...]
    lane = lax.broadcasted_iota(jnp.int32, cos.shape, 1)
    low_half = (lane % 32) < 16
    for j in range(D_QK // LANES):
        q = _dot(h, w_ref[:, C_DQ + j * LANES:C_DQ + (j + 1) * LANES])
        dq_ref[0, :, j * LANES:(j + 1) * LANES] = (_rope(q, cos, sin, low_half) * Q_SCALE).astype(BF16)
        k = _dot(h, w_ref[:, C_DK + j * LANES:C_DK + (j + 1) * LANES])
        dk_ref[0, :, j * LANES:(j + 1) * LANES] = _rope(k, cos, sin, low_half).astype(BF16)
    dvt_ref[0] = _dot_nt(wvt_ref[...], h).astype(BF16)


def _proj(x, ctx, mod3, cos_t, sin_t, w_all, wv_t):
    bsz, seq, _ = x.shape
    lc = seq + ctx.shape[1]
    nt = lc // ROW_TILE
    n_lat_tiles = seq // ROW_TILE
    ctx_row = mod3.shape[0] - 1

    def mod_idx(col):
        return lambda b, i: (jnp.where(i >= n_lat_tiles, ctx_row, b), 0, col)

    def row_spec(width):
        return pl.BlockSpec((1, ROW_TILE, width), lambda b, i: (b, i, 0))

    outs = [(G_QKV, F32), (G_Z, F32), (2 * LANES, F32), (D_QK, BF16), (D_QK, BF16)]
    return pl.pallas_call(
        functools.partial(_proj_kernel, n_lat_tiles),
        grid=(bsz, nt),
        in_specs=[pl.BlockSpec((1, ROW_TILE, D), lambda b, i: (b, jnp.minimum(i, n_lat_tiles - 1), 0)),
                  pl.BlockSpec((1, ROW_TILE, D), lambda b, i: (b, jnp.maximum(i - n_lat_tiles, 0), 0)),
                  pl.BlockSpec((1, 1, D), mod_idx(0)),
                  pl.BlockSpec((1, 1, D), mod_idx(1)),
                  pl.BlockSpec((ROW_TILE, LANES), lambda b, i: (i, 0)),
                  pl.BlockSpec((ROW_TILE, LANES), lambda b, i: (i, 0)),
                  pl.BlockSpec((D, C_END), lambda b, i: (0, 0)),
                  pl.BlockSpec((DH * DDV, D), lambda b, i: (0, 0))],
        out_specs=[row_spec(w) for w, _ in outs]
        + [pl.BlockSpec((1, DH * DDV, ROW_TILE), lambda b, i: (b, 0, i))],
        out_shape=[jax.ShapeDtypeStruct((bsz, lc, w), dt) for w, dt in outs]
        + [jax.ShapeDtypeStruct((bsz, DH * DDV, lc), BF16)],
        compiler_params=_cparams(("arbitrary", "arbitrary")),
        name="proj",
    )(x, ctx, mod3, mod3, cos_t, sin_t, w_all, wv_t)


def _prep_kernel(n_lat_tiles, nt, main_ref, prev_ref, next_ref, gt_ref, cw_ref, alog_ref, dtb_ref,
                 lo_ref, up_ref, q_ref, k_ref, v_ref, beta_ref, gc_ref, ext_ref):
    i = pl.program_id(1)
    halo = prev_ref.shape[1]
    has_prev = jnp.logical_and(i != 0, i != n_lat_tiles)
    has_next = jnp.logical_and(i != n_lat_tiles - 1, i != nt - 1)
    ext_ref[0:halo, :] = prev_ref[0] * has_prev.astype(F32)
    ext_ref[halo:halo + ROW_TILE, :] = main_ref[0]
    ext_ref[halo + ROW_TILE:, :] = next_ref[0] * has_next.astype(F32)
    pad = GCONV // 2
    acc = None
    for j in range(GCONV):
        start = halo - pad + j
        term = ext_ref[start:start + ROW_TILE, :] * cw_ref[j:j + 1, :]
        acc = term if acc is None else acc + term
    qkv = _silu(acc)
    for h in range(GH):
        q = qkv[:, h * GDK:(h + 1) * GDK]
        q_ref[0, :, h * GDK:(h + 1) * GDK] = q * (lax.rsqrt(jnp.sum(q * q, axis=-1, keepdims=True) + NORM_EPS)
                                                * (GDK ** -0.5))
        k = qkv[:, G_QK + h * GDK:G_QK + (h + 1) * GDK]
        k_ref[0, :, h * GDK:(h + 1) * GDK] = k * lax.rsqrt(jnp.sum(k * k, axis=-1, keepdims=True) + NORM_EPS)
    v_ref[0] = qkv[:, 2 * G_QK:]
    gt = gt_ref[0]
    beta_ref[0] = jax.nn.sigmoid(gt[:, :LANES])
    a = gt[:, LANES:] + dtb_ref[...]
    softplus = jnp.maximum(a, 0.0) + jnp.log(1.0 + jnp.exp(-jnp.abs(a)))
    g = -jnp.exp(alog_ref[...]) * softplus
    lo = lo_ref[...]
    up = up_ref[...]
    fwd = None
    bwd = None
    for part in _split3(g):
        f = _dot(lo, part)
        r = _dot(up, part)
        fwd = f if fwd is None else fwd + f
        bwd = r if bwd is None else bwd + r
    lane = lax.broadcasted_iota(jnp.int32, g.shape, 1)
    gc_ref[0] = jnp.where(lane < GH, fwd, bwd)


def _prep(qkv, gt, conv_w8, alog_v, dtb_v, tri_lo, tri_up, n_lat_tiles):
    bsz, lc, _ = qkv.shape
    nt = lc // ROW_TILE
    halo = 8
    per = ROW_TILE // halo
    last = lc // halo - 1

    def row_spec(width):
        return pl.BlockSpec((1, ROW_TILE, width), lambda b, i: (b, i, 0))

    def const_spec(shape):
        return pl.BlockSpec(shape, lambda b, i: tuple(0 for _ in shape))

    outs = [(G_QK, F32), (G_QK, F32), (GH * GDV, F32), (LANES, F32), (LANES, F32)]
    return pl.pallas_call(
        functools.partial(_prep_kernel, n_lat_tiles, nt),
        grid=(bsz, nt),
        in_specs=[row_spec(G_QKV),
                  pl.BlockSpec((1, halo, G_QKV), lambda b, i: (b, jnp.maximum(i * per - 1, 0), 0)),
                  pl.BlockSpec((1, halo, G_QKV), lambda b, i: (b, jnp.minimum((i + 1) * per, last), 0)),
                  row_spec(2 * LANES),
                  const_spec((8, G_QKV)), const_spec((1, LANES)), const_spec((1, LANES)),
                  const_spec((ROW_TILE, ROW_TILE)), const_spec((ROW_TILE, ROW_TILE))],
        out_specs=[row_spec(w) for w, _ in outs],
        out_shape=[jax.ShapeDtypeStruct((bsz, lc, w), dt) for w, dt in outs],
        scratch_shapes=[pltpu.VMEM((ROW_TILE + 2 * halo, G_QKV), F32)],
        compiler_params=_cparams(("arbitrary", "arbitrary")),
        name="prep",
    )(qkv, qkv, qkv, gt, conv_w8, alog_v, dtb_v, tri_lo, tri_up)


def _chunk_masks(n, rev):
    row = lax.broadcasted_iota(jnp.int32, (n, n), 0)
    col = lax.broadcasted_iota(jnp.int32, (n, n), 1)
    same = (row // CHUNK) == (col // CHUNK)
    incl = jnp.logical_and(same, (col >= row) if rev else (col <= row))
    strict = jnp.logical_and(same, (col > row) if rev else (col < row))
    return incl, strict


def _decay(gcol, grow, incl):
    return jnp.where(incl, jnp.exp(jnp.minimum(gcol - grow, 0.0)), 0.0)


def _chunk_kernel(k_ref, v_ref, beta_ref, gc_ref, uw_ref):
    n = k_ref.shape[1]
    gc = gc_ref[0]
    gct = gc.T
    beta = beta_ref[0]
    row = lax.broadcasted_iota(jnp.int32, (n, n), 0)
    col = lax.broadcasted_iota(jnp.int32, (n, n), 1)
    eye = (row == col).astype(F32)
    sizes = [2 ** j for j in range(1, int(math.log2(CHUNK)))]
    same = {r: (row // r) == (col // r) for r in sizes + [CHUNK]}
    a_all, t_all, x_all = [], [], []
    for d in range(2):
        incl, strict = _chunk_masks(n, d == 1)
        for h in range(GH):
            c = d * GH + h
            k = k_ref[0, :, h * GDK:(h + 1) * GDK]
            v = v_ref[0, :, h * GDV:(h + 1) * GDV]
            bcol = beta[:, c:c + 1]
            gcol = gc[:, c:c + 1]
            grow = gct[c:c + 1, :]
            kb = k * bcol
            a = jnp.where(strict, _dot_nt(kb.astype(BF16), k.astype(BF16)) * _decay(gcol, grow, incl), 0.0)
            a_all.append(a.astype(BF16))
            t_all.append(eye - jnp.where(same[2], a, 0.0))
            x_all.append(jnp.concatenate([v * bcol, kb * jnp.exp(gcol)], axis=1).astype(BF16))
    n_prob = len(a_all)
    zero = jnp.zeros((n, n), BF16)
    for r in sizes:
        off_mask = jnp.logical_and(same[2 * r], jnp.logical_not(same[r]))
        t_b = [t.astype(BF16) for t in t_all]
        inner = [_dot(jnp.where(off_mask, a_all[c], zero), t_b[c]).astype(BF16) for c in range(n_prob)]
        t_all = [t_all[c] - _dot(t_b[c], inner[c]) for c in range(n_prob)]
    for c in range(n_prob):
        uw_ref[0, c] = _dot(t_all[c].astype(BF16), x_all[c])


def _chunk(k, v, beta, gc):
    bsz, lc, _ = k.shape
    nt = lc // ROW_TILE

    def row_spec(width):
        return pl.BlockSpec((1, ROW_TILE, width), lambda b, i: (b, i, 0))

    return pl.pallas_call(
        _chunk_kernel,
        grid=(bsz, nt),
        in_specs=[row_spec(G_QK), row_spec(GH * GDV), row_spec(LANES), row_spec(LANES)],
        out_specs=pl.BlockSpec((1, 2 * GH, ROW_TILE, GDV + GDK), lambda b, i: (b, 0, i, 0)),
        out_shape=jax.ShapeDtypeStruct((bsz, 2 * GH, lc, GDV + GDK), F32),
        compiler_params=_cparams(("arbitrary", "arbitrary")),
        name="chunk",
    )(k, v, beta, gc)


def _scan_kernel(qf_ref, kf_ref, gf_ref, uwf_ref, qr_ref, kr_ref, gr_ref, uwr_ref,
                 of_ref, or_ref, s_ref):
    @pl.when(pl.program_id(1) == 0)
    def _():
        s_ref[...] = jnp.zeros_like(s_ref)

    n_chunks = ROW_TILE // CHUNK
    row = lax.broadcasted_iota(jnp.int32, (CHUNK, CHUNK), 0)
    col = lax.broadcasted_iota(jnp.int32, (CHUNK, CHUNK), 1)
    state = [s_ref[c] for c in range(2 * GH)]
    dirs = ((qf_ref, kf_ref, gf_ref, uwf_ref, of_ref), (qr_ref, kr_ref, gr_ref, uwr_ref, or_ref))

    def chunk_rows(step, d):
        ci = n_chunks - 1 - step if d == 1 else step
        return slice(ci * CHUNK, (ci + 1) * CHUNK)

    pre = []
    for step in range(n_chunks):
        per_chain = []
        for d, (q_ref, k_ref, g_ref, uw_ref, o_ref) in enumerate(dirs):
            rev = d == 1
            incl = (col >= row) if rev else (col <= row)
            rows = chunk_rows(step, d)
            gc = g_ref[0, rows, :]
            gct = gc.T
            last = 0 if rev else CHUNK - 1
            for h in range(GH):
                c = d * GH + h
                q = q_ref[0, rows, h * GDK:(h + 1) * GDK]
                k = k_ref[0, rows, h * GDK:(h + 1) * GDK]
                gcol = gc[:, c:c + 1]
                grow = gct[c:c + 1, :]
                g_last = gcol[last:last + 1, :]
                a_qk = (_dot_nt(q.astype(BF16), k.astype(BF16)) * _decay(gcol, grow, incl)).astype(BF16)
                wq = jnp.concatenate([uw_ref[0, h, rows, GDV:], q * jnp.exp(gcol)], axis=0).astype(BF16)
                k_tail_t = (k * jnp.exp(g_last - gcol)).T.astype(BF16)
                per_chain.append((a_qk, wq, k_tail_t, jnp.exp(g_last)))
        pre.append(per_chain)

    for step in range(n_chunks):
        ws = [_dot(pre[step][c][1], state[c].astype(BF16)) for c in range(2 * GH)]
        v_new = []
        for d, (q_ref, k_ref, g_ref, uw_ref, o_ref) in enumerate(dirs):
            rows = chunk_rows(step, d)
            for h in range(GH):
                v_new.append((uw_ref[0, h, rows, :GDV] - ws[d * GH + h][:CHUNK]).astype(BF16))
        for d, (q_ref, k_ref, g_ref, uw_ref, o_ref) in enumerate(dirs):
            rows = chunk_rows(step, d)
            for h in range(GH):
                c = d * GH + h
                a_qk, _, k_tail_t, decay_last = pre[step][c]
                o_ref[0, rows, h * GDV:(h + 1) * GDV] = ws[c][CHUNK:] + _dot(a_qk, v_new[c])
                state[c] = state[c] * decay_last + _dot(k_tail_t, v_new[c])
    for c in range(2 * GH):
        s_ref[c] = state[c]


def _scan(q, k, gc, uw, n_lat_tiles):
    bsz, lc, _ = q.shape
    nt = lc // ROW_TILE
    n_ctx_tiles = nt - n_lat_tiles

    def fwd(b, i):
        return jnp.where(i < n_ctx_tiles, n_lat_tiles + i, i - n_ctx_tiles)

    def bwd(b, i):
        return jnp.where(i < n_ctx_tiles, nt - 1 - i, n_lat_tiles - 1 - (i - n_ctx_tiles))

    def specs(tile):
        return [pl.BlockSpec((1, ROW_TILE, G_QK), lambda b, i: (b, tile(b, i), 0)),
                pl.BlockSpec((1, ROW_TILE, G_QK), lambda b, i: (b, tile(b, i), 0)),
                pl.BlockSpec((1, ROW_TILE, LANES), lambda b, i: (b, tile(b, i), 0))]

    def uw_spec(tile, d):
        return pl.BlockSpec((1, GH, ROW_TILE, GDV + GDK), lambda b, i: (b, d, tile(b, i), 0))

    def out_spec(tile):
        return pl.BlockSpec((1, ROW_TILE, GH * GDV), lambda b, i: (b, tile(b, i), 0))

    return pl.pallas_call(
        _scan_kernel,
        grid=(bsz, nt),
        in_specs=specs(fwd) + [uw_spec(fwd, 0)] + specs(bwd) + [uw_spec(bwd, 1)],
        out_specs=[out_spec(fwd), out_spec(bwd)],
        out_shape=[jax.ShapeDtypeStruct((bsz, lc, GH * GDV), F32)] * 2,
        scratch_shapes=[pltpu.VMEM((2 * GH, GDK, GDV), F32)],
        compiler_params=_cparams(("arbitrary", "arbitrary")),
        name="scan",
    )(q, k, gc, uw, q, k, gc, uw)


def _attn_kernel(lam_init, q_ref, k_ref, vt_ref, lam_ref, nw_ref, o_ref):
    lam = lam_ref[...]
    lam_full = (jnp.exp(jnp.sum(lam[0:1] * lam[1:2], axis=-1, keepdims=True))
                - jnp.exp(jnp.sum(lam[2:3] * lam[3:4], axis=-1, keepdims=True)) + lam_init)
    q = q_ref[0]
    tq = q.shape[0]
    n_keys = k_ref.shape[1]
    lane = lax.broadcasted_iota(jnp.int32, q.shape, 1)
    zero = jnp.zeros_like(q)
    qc = [jnp.where(lane < DDK, q, zero), jnp.where(lane >= DDK, q, zero)]
    m = [jnp.full((1, tq), -jnp.inf, F32) for _ in range(2)]
    l = [jnp.zeros((1, tq), F32) for _ in range(2)]
    acc = [jnp.zeros((DDV, tq), F32) for _ in range(2)]
    n_tiles = n_keys // KEY_TILE
    ones = jnp.ones((16, KEY_TILE), BF16)

    def scores(t):
        k_t = k_ref[0, t * KEY_TILE:(t + 1) * KEY_TILE, :]
        tiles = [_dot_nt(k_t, qc[c]) for c in range(2)]
        return [(s, jnp.max(s, axis=0, keepdims=True)) for s in tiles]

    pending = [scores(t) for t in range(min(SCORE_LOOKAHEAD, n_tiles))]
    for t in range(n_tiles):
        s_cur = pending.pop(0)
        if t + SCORE_LOOKAHEAD < n_tiles:
            pending.append(scores(t + SCORE_LOOKAHEAD))
        vt_t = jnp.concatenate([vt_ref[0, :, t * KEY_TILE:(t + 1) * KEY_TILE], ones], axis=0)
        for c in range(2):
            s, s_max = s_cur[c]
            m_new = jnp.maximum(m[c], s_max)
            p = jnp.exp2(s - m_new)
            scale = jnp.exp2(m[c] - m_new)
            pv = _dot(vt_t, p.astype(BF16))
            l[c] = scale * l[c] + pv[DDV:DDV + 1]
            acc[c] = scale * acc[c] + pv[:DDV]
            m[c] = m_new
    o = acc[0] / l[0] - lam_full * (acc[1] / l[1])
    y = o * lax.rsqrt(jnp.mean(o * o, axis=0, keepdims=True) + NORM_EPS)
    o_ref[0] = (y.T * nw_ref[...] * (1.0 - lam_init)).astype(o_ref.dtype)


def _attn(dq, dk, dvt, lam, norm_w, lam_init, lq, tq):
    bsz, lc, _ = dq.shape
    return pl.pallas_call(
        functools.partial(_attn_kernel, lam_init),
        grid=(bsz, DH, lq // tq),
        in_specs=[pl.BlockSpec((1, tq, 2 * DDK), lambda b, h, i: (b, i, h)),
                  pl.BlockSpec((1, lc, 2 * DDK), lambda b, h, i: (b, 0, h)),
                  pl.BlockSpec((1, DDV, lc), lambda b, h, i: (b, h, 0)),
                  pl.BlockSpec((4, DDK), lambda b, h, i: (0, 0)),
                  pl.BlockSpec((1, DDV), lambda b, h, i: (0, 0))],
        out_specs=pl.BlockSpec((1, tq, DDV), lambda b, h, i: (b, i, h)),
        out_shape=jax.ShapeDtypeStruct((bsz, lq, DH * DDV), BF16),
        compiler_params=_cparams(("arbitrary", "arbitrary", "arbitrary")),
        name="attn",
    )(dq, dk, dvt, lam, norm_w.reshape(1, DDV))


def _out_kernel(alpha, of_ref, or_ref, z_ref, yd_ref, x_ref, g1_ref, sh2_ref, sc2_ref, gnw_ref,
                wo_ref, lng_ref, lnb_ref, wr_ref, br_ref, x1_ref, h2_ref, lg_ref):
    o = of_ref[0] + or_ref[0]
    z = z_ref[0]
    heads = []
    for h in range(GH):
        oh = o[:, h * GDV:(h + 1) * GDV]
        heads.append((oh * lax.rsqrt(jnp.mean(oh * oh, axis=-1, keepdims=True) + NORM_EPS) * gnw_ref[...]
                      * _silu(z[:, h * GDV:(h + 1) * GDV])).astype(BF16))
    mix = jnp.concatenate(heads + [yd_ref[0]], axis=1)
    x1 = _normalize(alpha * x_ref[0] + g1_ref[0] * _dot(mix, wo_ref[...])) * lng_ref[...] + lnb_ref[...]
    x1_ref[0] = x1
    h2 = _normalize(x1) * (1.0 + sc2_ref[0]) + sh2_ref[0]
    h2_ref[0] = h2
    h_hi = h2.astype(BF16)
    h_lo = (h2 - h_hi.astype(F32)).astype(BF16)
    both = _dot(h_hi, wr_ref[...])
    lg_ref[0] = br_ref[...] + both[:, :LANES] + both[:, LANES:] + _dot(h_lo, wr_ref[:, :LANES])


def _out(o_f, o_r, z, y_diff, xc, mod3, gdn_norm_w, w_out, ln_g, ln_b, w_router3, b_router, alpha):
    bsz, lq, _ = y_diff.shape
    nt = lq // ROW_TILE

    def row_spec(width):
        return pl.BlockSpec((1, ROW_TILE, width), lambda b, i: (b, i, 0))

    def mod_spec(col):
        return pl.BlockSpec((1, 1, D), lambda b, i: (b, 0, col))

    def const_spec(shape):
        return pl.BlockSpec(shape, lambda b, i: tuple(0 for _ in shape))

    return pl.pallas_call(
        functools.partial(_out_kernel, alpha),
        grid=(bsz, nt),
        in_specs=[row_spec(GH * GDV), row_spec(GH * GDV), row_spec(G_Z), row_spec(DH * DDV), row_spec(D),
                  mod_spec(2), mod_spec(3), mod_spec(4),
                  const_spec((1, GDV)), const_spec((G_Z + DH * DDV, D)),
                  const_spec((1, D)), const_spec((1, D)),
                  const_spec((D, 2 * LANES)), const_spec((1, LANES))],
        out_specs=[row_spec(D), row_spec(D), row_spec(LANES)],
        out_shape=[jax.ShapeDtypeStruct((bsz, lq, D), F32), jax.ShapeDtypeStruct((bsz, lq, D), F32),
                   jax.ShapeDtypeStruct((bsz, lq, LANES), F32)],
        compiler_params=_cparams(("arbitrary", "arbitrary")),
        name="out",
    )(o_f, o_r, z, y_diff, xc, mod3, mod3, mod3, gdn_norm_w.reshape(1, GDV), w_out,
      ln_g.reshape(1, D), ln_b.reshape(1, D), w_router3, b_router)


def _route_kernel(lg_ref, tri_ref, ids_ref, gate_ref, cnt_ref, run_ref):
    @pl.when(pl.program_id(0) == 0)
    def _():
        run_ref[...] = jnp.zeros_like(run_ref)

    lg = lg_ref[...]
    lane_i = lax.broadcasted_iota(jnp.int32, lg.shape, 1)
    lane = lane_i.astype(F32)
    neg = jnp.float32(-jnp.inf)

    def first_max(mask):
        masked = jnp.where(mask, lg, neg)
        m = jnp.max(masked, axis=-1, keepdims=True)
        idx = jnp.min(jnp.where(jnp.logical_and(mask, masked == m), lane, float(LANES)), axis=-1, keepdims=True)
        return m, idx

    is_group = lane_i < N_GROUPS
    m_g, grp = first_max(is_group)
    p_g = 1.0 / jnp.sum(jnp.where(is_group, jnp.exp(lg - m_g), 0.0), axis=-1, keepdims=True)
    e_lane = lane - float(N_GROUPS)
    lo_e = grp * float(E_PER_GROUP)
    in_group = jnp.logical_and(e_lane >= lo_e, e_lane < lo_e + float(E_PER_GROUP))
    l0, i0 = first_max(in_group)
    l1, i1 = first_max(jnp.logical_and(in_group, lane != i0))
    r = jnp.exp(l1 - l0)
    gate0 = p_g / (1.0 + r)
    gate1 = p_g * r / (1.0 + r)
    e0 = i0 - float(N_GROUPS)
    e1 = i1 - float(N_GROUPS)
    oh0 = (lane == e0).astype(F32)
    oh1 = (lane == e1).astype(F32)
    both = oh0 + oh1
    before = _dot(tri_ref[...], both.astype(BF16)) + run_ref[...]
    rank0 = jnp.sum(oh0 * before, axis=-1, keepdims=True)
    rank1 = jnp.sum(oh1 * before, axis=-1, keepdims=True)
    run_ref[...] = run_ref[...] + jnp.sum(both, axis=0, keepdims=True)
    cnt_ref[...] = run_ref[...]
    ids = jnp.where(lane_i == 0, e0, jnp.where(lane_i == 1, e1,
                                               jnp.where(lane_i == 2, rank0, jnp.where(lane_i == 3, rank1, 0.0))))
    ids_ref[...] = ids.T[:8].astype(jnp.int32)
    gate_ref[...] = jnp.where(lane_i == 0, gate0, jnp.where(lane_i == 1, gate1, 0.0))


def _route(logits, tri_strict):
    t = logits.shape[0]
    tile = tri_strict.shape[0]
    return pl.pallas_call(
        _route_kernel,
        grid=(t // tile,),
        in_specs=[pl.BlockSpec((tile, LANES), lambda i: (i, 0)),
                  pl.BlockSpec((tile, tile), lambda i: (0, 0))],
        out_specs=[pl.BlockSpec((8, tile), lambda i: (0, i)),
                   pl.BlockSpec((tile, LANES), lambda i: (i, 0)),
                   pl.BlockSpec((1, LANES), lambda i: (0, 0))],
        out_shape=[jax.ShapeDtypeStruct((8, t), jnp.int32), jax.ShapeDtypeStruct((t, LANES), F32),
                   jax.ShapeDtypeStruct((1, LANES), F32)],
        scratch_shapes=[pltpu.VMEM((1, LANES), F32)],
        compiler_params=_cparams(("arbitrary",)),
        name="route",
    )(logits, tri_strict)


DMA_TOKENS = 512


def _dispatch_kernel(tail_ref, used_ref, dest_ref, h_ref, buf_hbm, zero_ref, sem, zsem):
    @pl.when(pl.program_id(0) == 0)
    def _():
        zero_ref[...] = jnp.zeros_like(zero_ref)
        n_blocks = buf_hbm.shape[0] // FFN_BLOCK

        def block_copy(b):
            return pltpu.make_async_copy(zero_ref, buf_hbm.at[pl.ds(b * FFN_BLOCK, FFN_BLOCK)], zsem)

        for e in range(N_EXPERTS):
            @pl.when(tail_ref[e] >= 0)
            def _():
                block_copy(tail_ref[e]).start()
        lax.fori_loop(used_ref[0], n_blocks, lambda b, carry: (block_copy(b).start(), carry)[1], 0)
        for e in range(N_EXPERTS):
            @pl.when(tail_ref[e] >= 0)
            def _():
                block_copy(tail_ref[e]).wait()
        lax.fori_loop(used_ref[0], n_blocks, lambda b, carry: (block_copy(b).wait(), carry)[1], 0)

    def start(tok, carry):
        for j in range(2):
            pltpu.make_async_copy(h_ref.at[pl.ds(tok, 1)], buf_hbm.at[pl.ds(dest_ref[2 * tok + j], 1)], sem).start()
        return carry

    lax.fori_loop(0, DMA_TOKENS, start, 0, unroll=8)
    for _ in range(2):
        pltpu.make_async_copy(h_ref, buf_hbm.at[pl.ds(0, DMA_TOKENS)], sem).wait()


def _dispatch(tail_block, n_used, dest_flat, h2, n_rows):
    t, d = h2.shape
    grid_spec = pltpu.PrefetchScalarGridSpec(
        num_scalar_prefetch=2,
        grid=(t // DMA_TOKENS,),
        in_specs=[pl.BlockSpec((2 * DMA_TOKENS,), lambda i, tail, used: (i,), memory_space=pltpu.SMEM),
                  pl.BlockSpec((DMA_TOKENS, d), lambda i, tail, used: (i, 0))],
        out_specs=pl.BlockSpec(memory_space=pl.ANY),
        scratch_shapes=[pltpu.VMEM((FFN_BLOCK, d), h2.dtype), pltpu.SemaphoreType.DMA(()),
                        pltpu.SemaphoreType.DMA(())],
    )
    return pl.pallas_call(
        _dispatch_kernel,
        grid_spec=grid_spec,
        out_shape=jax.ShapeDtypeStruct((n_rows, d), h2.dtype),
        compiler_params=_cparams(("arbitrary",)),
        name="dispatch",
    )(tail_block, n_used, dest_flat, h2)


def _combine_kernel(alpha, dest_ref, ys_hbm, gate_ref, x1_ref, g2_ref, lng_ref, lnb_ref, o_ref, r0_ref, r1_ref, sem):
    n = x1_ref.shape[0]

    def start(tok, carry):
        for j, r_ref in enumerate((r0_ref, r1_ref)):
            pltpu.make_async_copy(ys_hbm.at[pl.ds(dest_ref[2 * tok + j], 1)], r_ref.at[pl.ds(tok, 1)], sem).start()
        return carry

    lax.fori_loop(0, n, start, 0, unroll=8)
    for r_ref in (r0_ref, r1_ref):
        pltpu.make_async_copy(ys_hbm.at[pl.ds(0, n)], r_ref, sem).wait()
    gate = gate_ref[...]
    y = gate[:, 0:1] * r0_ref[...] + gate[:, 1:2] * r1_ref[...]
    o_ref[...] = _normalize(alpha * x1_ref[...] + g2_ref[0] * y) * lng_ref[...] + lnb_ref[...]


def _combine(dest_flat, ys, gate, x1, mod3, ln_g, ln_b, alpha, rows_per_batch):
    t, d = x1.shape
    per_batch = rows_per_batch // DMA_TOKENS
    return pl.pallas_call(
        functools.partial(_combine_kernel, alpha),
        grid=(t // DMA_TOKENS,),
        in_specs=[pl.BlockSpec((2 * DMA_TOKENS,), lambda i: (i,), memory_space=pltpu.SMEM),
                  pl.BlockSpec(memory_space=pl.ANY),
                  pl.BlockSpec((DMA_TOKENS, LANES), lambda i: (i, 0)),
                  pl.BlockSpec((DMA_TOKENS, d), lambda i: (i, 0)),
                  pl.BlockSpec((1, 1, d), lambda i: (i // per_batch, 0, 5)),
                  pl.BlockSpec((1, d), lambda i: (0, 0)),
                  pl.BlockSpec((1, d), lambda i: (0, 0))],
        out_specs=pl.BlockSpec((DMA_TOKENS, d), lambda i: (i, 0)),
        out_shape=jax.ShapeDtypeStruct((t, d), F32),
        scratch_shapes=[pltpu.VMEM((DMA_TOKENS, d), F32), pltpu.VMEM((DMA_TOKENS, d), F32),
                        pltpu.SemaphoreType.DMA(())],
        compiler_params=_cparams(("arbitrary",)),
        name="combine",
    )(dest_flat, ys, gate, x1, mod3, ln_g.reshape(1, d), ln_b.reshape(1, d))


def _ffn_kernel(start_ref, nblk_ref, used_ref, buf_hbm, wg_ref, wu_ref, wd_ref, ys_hbm,
                wgb_ref, wub_ref, wdb_ref, x_buf, o_buf, in_sem, out_sem):
    e = pl.program_id(0)
    nb = nblk_ref[e]
    base = start_ref[e]

    def rows(j):
        return pl.ds((base + j) * FFN_BLOCK, FFN_BLOCK)

    def in_copy(j, slot):
        return pltpu.make_async_copy(buf_hbm.at[rows(j)], x_buf.at[slot], in_sem.at[slot])

    def out_copy(j, slot):
        return pltpu.make_async_copy(o_buf.at[slot], ys_hbm.at[rows(j)], out_sem.at[slot])

    @pl.when(nb > 0)
    def _():
        in_copy(0, 0).start()
        wgb_ref[...] = wg_ref[0].astype(BF16)
        wub_ref[...] = wu_ref[0].astype(BF16)
        wdb_ref[...] = wd_ref[0].astype(BF16)

        def block(j, carry):
            slot = lax.rem(j, 2)
            in_copy(j, slot).wait()

            @pl.when(j + 1 < nb)
            def _():
                in_copy(j + 1, 1 - slot).start()

            @pl.when(j >= 2)
            def _():
                out_copy(j - 2, slot).wait()

            slab = FFN_BLOCK // FFN_SLABS
            xs = [x_buf[slot, s * slab:(s + 1) * slab, :].astype(BF16) for s in range(FFN_SLABS)]
            gates = [_dot(x, wgb_ref[...]) for x in xs]
            ups = [_dot(x, wub_ref[...]) for x in xs]
            hidden = [(_silu(g) * u).astype(BF16) for g, u in zip(gates, ups)]
            for s in range(FFN_SLABS):
                o_buf[slot, s * slab:(s + 1) * slab, :] = _dot(hidden[s], wdb_ref[...])
            out_copy(j, slot).start()
            return carry

        lax.fori_loop(0, nb, block, 0)

        @pl.when(nb >= 2)
        def _():
            out_copy(nb - 2, lax.rem(nb, 2)).wait()
        out_copy(nb - 1, lax.rem(nb - 1, 2)).wait()

    @pl.when(e == pl.num_programs(0) - 1)
    def _():
        n_blocks = ys_hbm.shape[0] // FFN_BLOCK
        o_buf[0] = jnp.zeros(o_buf.shape[1:], o_buf.dtype)

        def clear(b):
            return pltpu.make_async_copy(o_buf.at[0], ys_hbm.at[pl.ds(b * FFN_BLOCK, FFN_BLOCK)], out_sem.at[0])

        lax.fori_loop(used_ref[0], n_blocks, lambda b, carry: (clear(b).start(), carry)[1], 0)
        lax.fori_loop(used_ref[0], n_blocks, lambda b, carry: (clear(b).wait(), carry)[1], 0)


def _ffn(seg_start, seg_blocks, n_used, buf, w_gate, w_up, w_down):
    n_rows, d = buf.shape
    n_experts = w_gate.shape[0]
    grid_spec = pltpu.PrefetchScalarGridSpec(
        num_scalar_prefetch=3,
        grid=(n_experts,),
        in_specs=[pl.BlockSpec(memory_space=pl.ANY),
                  pl.BlockSpec((1, d, D_EXPERT), lambda e, *_: (e, 0, 0)),
                  pl.BlockSpec((1, d, D_EXPERT), lambda e, *_: (e, 0, 0)),
                  pl.BlockSpec((1, D_EXPERT, d), lambda e, *_: (e, 0, 0))],
        out_specs=pl.BlockSpec(memory_space=pl.ANY),
        scratch_shapes=[pltpu.VMEM((d, D_EXPERT), BF16), pltpu.VMEM((d, D_EXPERT), BF16),
                        pltpu.VMEM((D_EXPERT, d), BF16),
                        pltpu.VMEM((2, FFN_BLOCK, d), F32), pltpu.VMEM((2, FFN_BLOCK, d), F32),
                        pltpu.SemaphoreType.DMA((2,)), pltpu.SemaphoreType.DMA((2,))],
    )
    return pl.pallas_call(
        _ffn_kernel,
        grid_spec=grid_spec,
        out_shape=jax.ShapeDtypeStruct((n_rows, d), F32),
        compiler_params=_cparams(("arbitrary",)),
        name="ffn",
    )(seg_start, seg_blocks, n_used, buf, w_gate, w_up, w_down)


def _rope_tables(seq_len, n_ctx_rows):
    lane = np.arange(LANES)
    within = lane % DDK
    freq = within % 16
    use_col = within >= DDK // 2
    sign = np.where((lane % 32) < 16, -1.0, 1.0).astype(np.float32)
    half = DDK // 2
    inv_freq = ROPE_THETA ** (-jnp.arange(0, half, 2, dtype=F32) / half)
    t = jnp.arange(seq_len)
    pos = jnp.where(jnp.asarray(use_col)[None, :], (t % GRID_W)[:, None], (t // GRID_W)[:, None]).astype(F32)
    ang = pos * inv_freq[jnp.asarray(freq)][None, :]
    cos = jnp.concatenate([jnp.cos(ang), jnp.ones((n_ctx_rows, LANES), F32)], axis=0)
    sin = jnp.concatenate([jnp.sin(ang) * jnp.asarray(sign)[None, :], jnp.zeros((n_ctx_rows, LANES), F32)], axis=0)
    return cos, sin


def _block_tri(n, block, upper):
    i = np.arange(n)
    same = (i[:, None] // block) == (i[None, :] // block)
    tri = (i[None, :] >= i[:, None]) if upper else (i[None, :] <= i[:, None])
    return jnp.asarray((same & tri).astype(np.float32), dtype=BF16)


def _pack_w_in(w_in):
    pad = jnp.zeros((D, LANES - G_GATES), w_in.dtype)
    off = G_QKV + G_Z
    v_start = G_COLS + 2 * D_QK
    w_all = jnp.concatenate([
        w_in[:, :G_QKV], w_in[:, G_QKV:off],
        w_in[:, off:off + G_GATES], pad,
        w_in[:, off + G_GATES:G_COLS], pad,
        w_in[:, G_COLS:v_start],
    ], axis=1).astype(BF16)
    return w_all, w_in[:, v_start:].T.astype(BF16)


def kernel(x, c, ctx, c_ctx, w_ada, b_ada, w_in, conv_w, gdn_a_log, gdn_dt_bias, gdn_norm_w, diff_lambda,
           diff_norm_w, w_out, ln1_g, ln1_b, w_router_group, b_router_group, w_router_expert, b_router_expert,
           w_expert_gate, w_expert_up, w_expert_down, ln2_g, ln2_b):
    depth = w_ada.shape[0]
    assert depth == 1, "single-layer block: the context stream never feeds a later layer"
    bsz, seq, _ = x.shape
    n_ctx = ctx.shape[1]
    assert seq % ROW_TILE == 0 and n_ctx % ROW_TILE == 0 and seq % GRID_W == 0 and bsz < 8
    n_lat_tiles = seq // ROW_TILE
    alpha = (2.0 * depth) ** 0.25
    lam_init = 0.8 - 0.6 * math.exp(-0.3 * 0)
    i = 0

    cc = jnp.zeros((8, D), F32).at[:bsz].set(c).at[7].set(c_ctx)
    mod = _ada(cc, w_ada[i], b_ada[i])
    mod3 = mod.reshape(8, 1, 6 * D)

    cos_t, sin_t = _rope_tables(seq, n_ctx)
    qkv, z, gt, dq, dk, dvt = _proj(x, ctx, mod3, cos_t, sin_t, *_pack_w_in(w_in[i]))

    conv_w8 = jnp.zeros((8, G_QKV), F32).at[:GCONV].set(conv_w[i])
    alog_v = jnp.zeros((1, LANES), F32).at[0, :G_GATES].set(gdn_a_log[i].reshape(-1))
    dtb_v = jnp.zeros((1, LANES), F32).at[0, :G_GATES].set(gdn_dt_bias[i].reshape(-1))
    gq, gk, gv, beta, gc = _prep(qkv, gt, conv_w8, alog_v, dtb_v,
                                 _block_tri(ROW_TILE, CHUNK, False), _block_tri(ROW_TILE, CHUNK, True), n_lat_tiles)
    uw = _chunk(gk, gv, beta, gc)
    o_f, o_r = _scan(gq, gk, gc, uw, n_lat_tiles)

    y_diff = _attn(dq, dk, dvt, diff_lambda[i], diff_norm_w[i], lam_init, seq, tq=ATTN_Q_TILE)

    w_router = jnp.zeros((D, LANES), F32).at[:, :N_GROUPS].set(w_router_group[i]) \
        .at[:, N_GROUPS:N_GROUPS + N_EXPERTS].set(w_router_expert[i])
    w_hi = w_router.astype(BF16)
    w_router3 = jnp.concatenate([w_hi, (w_router - w_hi.astype(F32)).astype(BF16)], axis=1)
    b_router = jnp.zeros((1, LANES), F32).at[0, :N_GROUPS].set(b_router_group[i]) \
        .at[0, N_GROUPS:N_GROUPS + N_EXPERTS].set(b_router_expert[i])
    x1, h2, logits = _out(o_f, o_r, z, y_diff, x, mod3, gdn_norm_w[i], w_out[i].astype(BF16), ln1_g[i], ln1_b[i],
                          w_router3, b_router, alpha)

    t = bsz * seq
    tri_strict = jnp.asarray(np.tril(np.ones((ROW_TILE, ROW_TILE), np.float32), -1), dtype=BF16)
    ids, gate, counts = _route(logits.reshape(t, LANES), tri_strict)

    cnt = counts[0, :N_EXPERTS].astype(jnp.int32)
    padded = (cnt + FFN_BLOCK - 1) // FFN_BLOCK * FFN_BLOCK
    p_end = jnp.cumsum(padded)
    p_start = p_end - padded
    n_blocks = (2 * t) // FFN_BLOCK + N_EXPERTS
    n_used = (p_end[-1:] // FFN_BLOCK).astype(jnp.int32)
    tail_block = jnp.where(cnt > 0, p_end // FFN_BLOCK - 1, -1).astype(jnp.int32)
    one_hot = ids[0:2, :, None] == jnp.arange(N_EXPERTS, dtype=jnp.int32)
    row_start = jnp.sum(jnp.where(one_hot, p_start, 0), axis=-1)
    dest = (row_start + ids[2:4]).T.reshape(-1).astype(jnp.int32)

    buf = _dispatch(tail_block, n_used, dest, h2.reshape(t, D), n_blocks * FFN_BLOCK)
    ys = _ffn((p_start // FFN_BLOCK).astype(jnp.int32), (padded // FFN_BLOCK).astype(jnp.int32), n_used, buf,
              w_expert_gate[i], w_expert_up[i], w_expert_down[i])
    out = _combine(dest, ys, gate, x1.reshape(t, D), mod3, ln2_g[i], ln2_b[i], alpha, seq)
    return out.reshape(bsz, seq, D)
```

```python
import functools
import math

import numpy as np
import jax
import jax.numpy as jnp
from jax import lax
from jax.experimental import pallas as pl
from jax.experimental.pallas import tpu as pltpu

F32 = jnp.float32
BF16 = jnp.bfloat16
HIGHEST = lax.Precision.HIGHEST

D = 1024
GRID_W = 64
ROPE_THETA = 10000.0
GH = 4
GDK = 128
GDV = 128
GCONV = 5
DH = 4
DDK = 64
DDV = 128
G_QK = GH * GDK
G_QKV = 2 * G_QK + GH * GDV
G_Z = GH * GDV
G_GATES = 2 * GH
G_COLS = G_QKV + G_Z + 2 * G_GATES
D_QK = DH * 2 * DDK
N_GROUPS = 4
E_PER_GROUP = 8
N_EXPERTS = N_GROUPS * E_PER_GROUP
D_EXPERT = 512
LN_EPS = 1e-5
NORM_EPS = 1e-6

LANES = 128
ROW_TILE = 256
CHUNK = 64
FFN_BLOCK = 256
FFN_SLABS = 2
FFN_IN_SLOTS = 4
FFN_OUT_SLOTS = 3
KEY_TILE = 512
ATTN_Q_TILE = 512
SCORE_LOOKAHEAD = 2
VMEM_LIMIT = 56 * 1024 * 1024

C_QKV = 0
C_Z = C_QKV + G_QKV
C_GB = C_Z + G_Z
C_GA = C_GB + LANES
C_DQ = C_GA + LANES
C_DK = C_DQ + D_QK
C_END = C_DK + D_QK
Q_SCALE = DDK ** -0.5 * math.log2(math.e)


def _cparams(sem):
    return pltpu.CompilerParams(dimension_semantics=sem, vmem_limit_bytes=VMEM_LIMIT)


def _dot(a, b):
    return jnp.dot(a, b, preferred_element_type=F32)


def _dot_nt(a, b):
    return lax.dot_general(a, b, (((1,), (1,)), ((), ())), preferred_element_type=F32)


def _dot_tn(a, b):
    return lax.dot_general(a, b, (((0,), (0,)), ((), ())), preferred_element_type=F32)


def _normalize(x):
    mu = jnp.mean(x, axis=-1, keepdims=True)
    xc = x - mu
    var = jnp.mean(xc * xc, axis=-1, keepdims=True)
    return xc * lax.rsqrt(var + LN_EPS)


def _silu(x):
    return x * jax.nn.sigmoid(x)


def _split3(x):
    hi = x.astype(BF16)
    r = x - hi.astype(F32)
    mid = r.astype(BF16)
    lo = (r - mid.astype(F32)).astype(BF16)
    return hi, mid, lo


def _ada_kernel(c_ref, w_ref, b_ref, o_ref):
    s = _silu(c_ref[...])
    o_ref[...] = jnp.dot(s, w_ref[...], precision=HIGHEST, preferred_element_type=F32) + b_ref[...]


def _ada(cc, w_ada, b_ada):
    n = w_ada.shape[1]
    bn = 1024
    return pl.pallas_call(
        _ada_kernel,
        grid=(n // bn,),
        in_specs=[pl.BlockSpec((8, D), lambda j: (0, 0)),
                  pl.BlockSpec((D, bn), lambda j: (0, j)),
                  pl.BlockSpec((1, bn), lambda j: (0, j))],
        out_specs=pl.BlockSpec((8, bn), lambda j: (0, j)),
        out_shape=jax.ShapeDtypeStruct((8, n), F32),
        compiler_params=_cparams(("arbitrary",)),
        name="ada",
    )(cc, w_ada, b_ada.reshape(1, n))


def _rope(v, cos, sin, low_half):
    fwd = pltpu.roll(v, LANES - 16, axis=1)
    bwd = pltpu.roll(v, 16, axis=1)
    return v * cos + jnp.where(low_half, fwd, bwd) * sin


def _proj_kernel(n_lat_tiles, x_ref, ctx_ref, sh_ref, sc_ref, cos_ref, sin_ref, w_ref, wvt_ref,
                 qkv_ref, z_ref, gt_ref, dq_ref, dk_ref, dvt_ref):
    rows = jnp.where(pl.program_id(1) >= n_lat_tiles, ctx_ref[0], x_ref[0])
    h = (_normalize(rows) * (1.0 + sc_ref[0]) + sh_ref[0]).astype(BF16)
    qkv_ref[0] = _dot(h, w_ref[:, C_QKV:C_Z])
    z_ref[0] = _dot(h, w_ref[:, C_Z:C_GB])
    gt_ref[0] = _dot(h, w_ref[:, C_GB:C_DQ])
    cos = cos_ref[...]
    sin = sin_ref[...]
    lane = lax.broadcasted_iota(jnp.int32, cos.shape, 1)
    low_half = (lane % 32) < 16
    q_all = _dot(h, w_ref[:, C_DQ:C_DK])
    k_all = _dot(h, w_ref[:, C_DK:C_END])
    for j in range(D_QK // LANES):
        slab = slice(j * LANES, (j + 1) * LANES)
        dq_ref[0, :, slab] = (_rope(q_all[:, slab], cos, sin, low_half) * Q_SCALE).astype(BF16)
        dk_ref[0, :, slab] = _rope(k_all[:, slab], cos, sin, low_half).astype(BF16)
    dvt_ref[0] = _dot_nt(wvt_ref[...], h).astype(BF16)


def _proj(x, ctx, mod3, cos_t, sin_t, w_all, wv_t):
    bsz, seq, _ = x.shape
    lc = seq + ctx.shape[1]
    nt = lc // ROW_TILE
    n_lat_tiles = seq // ROW_TILE
    ctx_row = mod3.shape[0] - 1

    def mod_idx(col):
        return lambda b, i: (jnp.where(i >= n_lat_tiles, ctx_row, b), 0, col)

    def row_spec(width):
        return pl.BlockSpec((1, ROW_TILE, width), lambda b, i: (b, i, 0))

    outs = [(G_QKV, F32), (G_Z, F32), (2 * LANES, F32), (D_QK, BF16), (D_QK, BF16)]
    return pl.pallas_call(
        functools.partial(_proj_kernel, n_lat_tiles),
        grid=(bsz, nt),
        in_specs=[pl.BlockSpec((1, ROW_TILE, D), lambda b, i: (b, jnp.minimum(i, n_lat_tiles - 1), 0)),
                  pl.BlockSpec((1, ROW_TILE, D), lambda b, i: (b, jnp.maximum(i - n_lat_tiles, 0), 0)),
                  pl.BlockSpec((1, 1, D), mod_idx(0)),
                  pl.BlockSpec((1, 1, D), mod_idx(1)),
                  pl.BlockSpec((ROW_TILE, LANES), lambda b, i: (i, 0)),
                  pl.BlockSpec((ROW_TILE, LANES), lambda b, i: (i, 0)),
                  pl.BlockSpec((D, C_END), lambda b, i: (0, 0)),
                  pl.BlockSpec((DH * DDV, D), lambda b, i: (0, 0))],
        out_specs=[row_spec(w) for w, _ in outs]
        + [pl.BlockSpec((1, DH * DDV, ROW_TILE), lambda b, i: (b, 0, i))],
        out_shape=[jax.ShapeDtypeStruct((bsz, lc, w), dt) for w, dt in outs]
        + [jax.ShapeDtypeStruct((bsz, DH * DDV, lc), BF16)],
        compiler_params=_cparams(("arbitrary", "arbitrary")),
        name="proj",
    )(x, ctx, mod3, mod3, cos_t, sin_t, w_all, wv_t)


def _prep_kernel(n_lat_tiles, nt, main_ref, prev_ref, next_ref, gt_ref, cw_ref, alog_ref, dtb_ref,
                 lo_ref, up_ref, q_ref, k_ref, v_ref, beta_ref, gc_ref, ext_ref):
    i = pl.program_id(1)
    halo = prev_ref.shape[1]
    has_prev = jnp.logical_and(i != 0, i != n_lat_tiles)
    has_next = jnp.logical_and(i != n_lat_tiles - 1, i != nt - 1)
    ext_ref[0:halo, :] = prev_ref[0] * has_prev.astype(F32)
    ext_ref[halo:halo + ROW_TILE, :] = main_ref[0]
    ext_ref[halo + ROW_TILE:, :] = next_ref[0] * has_next.astype(F32)
    pad = GCONV // 2
    acc = None
    for j in range(GCONV):
        start = halo - pad + j
        term = ext_ref[start:start + ROW_TILE, :] * cw_ref[j:j + 1, :]
        acc = term if acc is None else acc + term
    qkv = _silu(acc)
    for h in range(GH):
        q = qkv[:, h * GDK:(h + 1) * GDK]
        q_ref[0, :, h * GDK:(h + 1) * GDK] = q * (lax.rsqrt(jnp.sum(q * q, axis=-1, keepdims=True) + NORM_EPS)
                                                * (GDK ** -0.5))
        k = qkv[:, G_QK + h * GDK:G_QK + (h + 1) * GDK]
        k_ref[0, :, h * GDK:(h + 1) * GDK] = k * lax.rsqrt(jnp.sum(k * k, axis=-1, keepdims=True) + NORM_EPS)
    v_ref[0] = qkv[:, 2 * G_QK:]
    gt = gt_ref[0]
    beta_ref[0] = jax.nn.sigmoid(gt[:, :LANES])
    a = gt[:, LANES:] + dtb_ref[...]
    softplus = jnp.maximum(a, 0.0) + jnp.log(1.0 + jnp.exp(-jnp.abs(a)))
    g = -jnp.exp(alog_ref[...]) * softplus
    lo = lo_ref[...]
    up = up_ref[...]
    fwd = None
    bwd = None
    for part in _split3(g):
        f = _dot(lo, part)
        r = _dot(up, part)
        fwd = f if fwd is None else fwd + f
        bwd = r if bwd is None else bwd + r
    lane = lax.broadcasted_iota(jnp.int32, g.shape, 1)
    gc_ref[0] = jnp.where(lane < GH, fwd, bwd)


def _prep(qkv, gt, conv_w8, alog_v, dtb_v, tri_lo, tri_up, n_lat_tiles):
    bsz, lc, _ = qkv.shape
    nt = lc // ROW_TILE
    halo = 8
    per = ROW_TILE // halo
    last = lc // halo - 1

    def row_spec(width):
        return pl.BlockSpec((1, ROW_TILE, width), lambda b, i: (b, i, 0))

    def const_spec(shape):
        return pl.BlockSpec(shape, lambda b, i: tuple(0 for _ in shape))

    outs = [(G_QK, F32), (G_QK, F32), (GH * GDV, F32), (LANES, F32), (LANES, F32)]
    return pl.pallas_call(
        functools.partial(_prep_kernel, n_lat_tiles, nt),
        grid=(bsz, nt),
        in_specs=[row_spec(G_QKV),
                  pl.BlockSpec((1, halo, G_QKV), lambda b, i: (b, jnp.maximum(i * per - 1, 0), 0)),
                  pl.BlockSpec((1, halo, G_QKV), lambda b, i: (b, jnp.minimum((i + 1) * per, last), 0)),
                  row_spec(2 * LANES),
                  const_spec((8, G_QKV)), const_spec((1, LANES)), const_spec((1, LANES)),
                  const_spec((ROW_TILE, ROW_TILE)), const_spec((ROW_TILE, ROW_TILE))],
        out_specs=[row_spec(w) for w, _ in outs],
        out_shape=[jax.ShapeDtypeStruct((bsz, lc, w), dt) for w, dt in outs],
        scratch_shapes=[pltpu.VMEM((ROW_TILE + 2 * halo, G_QKV), F32)],
        compiler_params=_cparams(("arbitrary", "arbitrary")),
        name="prep",
    )(qkv, qkv, qkv, gt, conv_w8, alog_v, dtb_v, tri_lo, tri_up)


def _chunk_masks(n, rev):
    row = lax.broadcasted_iota(jnp.int32, (n, n), 0)
    col = lax.broadcasted_iota(jnp.int32, (n, n), 1)
    same = (row // CHUNK) == (col // CHUNK)
    incl = jnp.logical_and(same, (col >= row) if rev else (col <= row))
    strict = jnp.logical_and(same, (col > row) if rev else (col < row))
    return incl, strict


def _decay(gcol, grow, incl):
    return jnp.where(incl, jnp.exp(jnp.minimum(gcol - grow, 0.0)), 0.0)


def _chunk_kernel(k_ref, v_ref, beta_ref, gc_ref, uw_ref):
    n = k_ref.shape[1]
    gc = gc_ref[0]
    gct = gc.T
    beta = beta_ref[0]
    row = lax.broadcasted_iota(jnp.int32, (n, n), 0)
    col = lax.broadcasted_iota(jnp.int32, (n, n), 1)
    eye = (row == col).astype(F32)
    sizes = [2 ** j for j in range(1, int(math.log2(CHUNK)))]
    same = {r: (row // r) == (col // r) for r in sizes + [CHUNK]}
    a_all, t_all, x_all = [], [], []
    for d in range(2):
        incl, strict = _chunk_masks(n, d == 1)
        for h in range(GH):
            c = d * GH + h
            k = k_ref[0, :, h * GDK:(h + 1) * GDK]
            v = v_ref[0, :, h * GDV:(h + 1) * GDV]
            bcol = beta[:, c:c + 1]
            gcol = gc[:, c:c + 1]
            grow = gct[c:c + 1, :]
            kb = k * bcol
            a = jnp.where(strict, _dot_nt(kb.astype(BF16), k.astype(BF16)) * _decay(gcol, grow, incl), 0.0)
            a_all.append(a.astype(BF16))
            t_all.append(eye - jnp.where(same[2], a, 0.0))
            x_all.append(jnp.concatenate([v * bcol, kb * jnp.exp(gcol)], axis=1).astype(BF16))
    n_prob = len(a_all)
    zero = jnp.zeros((n, n), BF16)
    for r in sizes:
        off_mask = jnp.logical_and(same[2 * r], jnp.logical_not(same[r]))
        t_b = [t.astype(BF16) for t in t_all]
        inner = [_dot(jnp.where(off_mask, a_all[c], zero), t_b[c]).astype(BF16) for c in range(n_prob)]
        t_all = [t_all[c] - _dot(t_b[c], inner[c]) for c in range(n_prob)]
    for c in range(n_prob):
        uw_ref[0, c] = _dot(t_all[c].astype(BF16), x_all[c])


def _chunk(k, v, beta, gc):
    bsz, lc, _ = k.shape
    nt = lc // ROW_TILE

    def row_spec(width):
        return pl.BlockSpec((1, ROW_TILE, width), lambda b, i: (b, i, 0))

    return pl.pallas_call(
        _chunk_kernel,
        grid=(bsz, nt),
        in_specs=[row_spec(G_QK), row_spec(GH * GDV), row_spec(LANES), row_spec(LANES)],
        out_specs=pl.BlockSpec((1, 2 * GH, ROW_TILE, GDV + GDK), lambda b, i: (b, 0, i, 0)),
        out_shape=jax.ShapeDtypeStruct((bsz, 2 * GH, lc, GDV + GDK), F32),
        compiler_params=_cparams(("arbitrary", "arbitrary")),
        name="chunk",
    )(k, v, beta, gc)


def _scan_kernel(qf_ref, kf_ref, gf_ref, uwf_ref, qr_ref, kr_ref, gr_ref, uwr_ref,
                 of_ref, or_ref, s_ref):
    @pl.when(pl.program_id(1) == 0)
    def _():
        s_ref[...] = jnp.zeros_like(s_ref)

    n_chunks = ROW_TILE // CHUNK
    row = lax.broadcasted_iota(jnp.int32, (CHUNK, CHUNK), 0)
    col = lax.broadcasted_iota(jnp.int32, (CHUNK, CHUNK), 1)
    state = [s_ref[c] for c in range(2 * GH)]
    dirs = ((qf_ref, kf_ref, gf_ref, uwf_ref, of_ref), (qr_ref, kr_ref, gr_ref, uwr_ref, or_ref))

    def chunk_rows(step, d):
        ci = n_chunks - 1 - step if d == 1 else step
        return slice(ci * CHUNK, (ci + 1) * CHUNK)

    pre = []
    for step in range(n_chunks):
        per_chain = []
        for d, (q_ref, k_ref, g_ref, uw_ref, o_ref) in enumerate(dirs):
            rev = d == 1
            incl = (col >= row) if rev else (col <= row)
            rows = chunk_rows(step, d)
            gc = g_ref[0, rows, :]
            gct = gc.T
            last = 0 if rev else CHUNK - 1
            for h in range(GH):
                c = d * GH + h
                q = q_ref[0, rows, h * GDK:(h + 1) * GDK]
                k = k_ref[0, rows, h * GDK:(h + 1) * GDK]
                gcol = gc[:, c:c + 1]
                grow = gct[c:c + 1, :]
                g_last = gcol[last:last + 1, :]
                a_qk = (_dot_nt(q.astype(BF16), k.astype(BF16)) * _decay(gcol, grow, incl)).astype(BF16)
                wq = jnp.concatenate([uw_ref[0, h, rows, GDV:], q * jnp.exp(gcol)], axis=0).astype(BF16)
                k_tail_t = (k * jnp.exp(g_last - gcol)).T.astype(BF16)
                per_chain.append((a_qk, wq, k_tail_t, jnp.exp(g_last)))
        pre.append(per_chain)

    for step in range(n_chunks):
        ws = [_dot(pre[step][c][1], state[c].astype(BF16)) for c in range(2 * GH)]
        v_new = []
        for d, (q_ref, k_ref, g_ref, uw_ref, o_ref) in enumerate(dirs):
            rows = chunk_rows(step, d)
            for h in range(GH):
                v_new.append((uw_ref[0, h, rows, :GDV] - ws[d * GH + h][:CHUNK]).astype(BF16))
        for d, (q_ref, k_ref, g_ref, uw_ref, o_ref) in enumerate(dirs):
            rows = chunk_rows(step, d)
            for h in range(GH):
                c = d * GH + h
                a_qk, _, k_tail_t, decay_last = pre[step][c]
                o_ref[0, rows, h * GDV:(h + 1) * GDV] = ws[c][CHUNK:] + _dot(a_qk, v_new[c])
                state[c] = state[c] * decay_last + _dot(k_tail_t, v_new[c])
    for c in range(2 * GH):
        s_ref[c] = state[c]


def _scan(q, k, gc, uw, n_lat_tiles):
    bsz, lc, _ = q.shape
    nt = lc // ROW_TILE
    n_ctx_tiles = nt - n_lat_tiles

    def fwd(b, i):
        return jnp.where(i < n_ctx_tiles, n_lat_tiles + i, i - n_ctx_tiles)

    def bwd(b, i):
        return jnp.where(i < n_ctx_tiles, nt - 1 - i, n_lat_tiles - 1 - (i - n_ctx_tiles))

    def specs(tile):
        return [pl.BlockSpec((1, ROW_TILE, G_QK), lambda b, i: (b, tile(b, i), 0)),
                pl.BlockSpec((1, ROW_TILE, G_QK), lambda b, i: (b, tile(b, i), 0)),
                pl.BlockSpec((1, ROW_TILE, LANES), lambda b, i: (b, tile(b, i), 0))]

    def uw_spec(tile, d):
        return pl.BlockSpec((1, GH, ROW_TILE, GDV + GDK), lambda b, i: (b, d, tile(b, i), 0))

    def out_spec(tile):
        return pl.BlockSpec((1, ROW_TILE, GH * GDV), lambda b, i: (b, tile(b, i), 0))

    return pl.pallas_call(
        _scan_kernel,
        grid=(bsz, nt),
        in_specs=specs(fwd) + [uw_spec(fwd, 0)] + specs(bwd) + [uw_spec(bwd, 1)],
        out_specs=[out_spec(fwd), out_spec(bwd)],
        out_shape=[jax.ShapeDtypeStruct((bsz, lc, GH * GDV), F32)] * 2,
        scratch_shapes=[pltpu.VMEM((2 * GH, GDK, GDV), F32)],
        compiler_params=_cparams(("arbitrary", "arbitrary")),
        name="scan",
    )(q, k, gc, uw, q, k, gc, uw)


def _attn_kernel(lam_init, q_ref, k_ref, vt_ref, lam_ref, nw_ref, o_ref):
    lam = lam_ref[...]
    lam_full = (jnp.exp(jnp.sum(lam[0:1] * lam[1:2], axis=-1, keepdims=True))
                - jnp.exp(jnp.sum(lam[2:3] * lam[3:4], axis=-1, keepdims=True)) + lam_init)
    q = q_ref[0]
    tq = q.shape[0]
    n_keys = k_ref.shape[1]
    lane = lax.broadcasted_iota(jnp.int32, q.shape, 1)
    zero = jnp.zeros_like(q)
    qc = [jnp.where(lane < DDK, q, zero), jnp.where(lane >= DDK, q, zero)]
    m = [jnp.full((1, tq), -jnp.inf, F32) for _ in range(2)]
    l = [jnp.zeros((1, tq), F32) for _ in range(2)]
    acc = [jnp.zeros((DDV, tq), F32) for _ in range(2)]
    n_tiles = n_keys // KEY_TILE
    bounds = [t * KEY_TILE for t in range(n_tiles)] + [n_keys]

    def scores(t):
        k_t = k_ref[0, bounds[t]:bounds[t + 1], :]
        tiles = [_dot_nt(k_t, qc[c]) for c in range(2)]
        return [(s, jnp.max(s, axis=0, keepdims=True)) for s in tiles]

    pending = [scores(t) for t in range(min(SCORE_LOOKAHEAD, n_tiles))]
    for t in range(n_tiles):
        s_cur = pending.pop(0)
        if t + SCORE_LOOKAHEAD < n_tiles:
            pending.append(scores(t + SCORE_LOOKAHEAD))
        width = bounds[t + 1] - bounds[t]
        vt_t = jnp.concatenate([vt_ref[0, :, bounds[t]:bounds[t + 1]], jnp.ones((16, width), BF16)], axis=0)
        for c in range(2):
            s, s_max = s_cur[c]
            m_new = jnp.maximum(m[c], s_max)
            p = jnp.exp2(s - m_new)
            scale = jnp.exp2(m[c] - m_new)
            pv = _dot(vt_t, p.astype(BF16))
            l[c] = scale * l[c] + pv[DDV:DDV + 1]
            acc[c] = scale * acc[c] + pv[:DDV]
            m[c] = m_new
    o = acc[0] / l[0] - lam_full * (acc[1] / l[1])
    y = o * lax.rsqrt(jnp.mean(o * o, axis=0, keepdims=True) + NORM_EPS)
    o_ref[0] = (y.T * nw_ref[...] * (1.0 - lam_init)).astype(o_ref.dtype)


def _attn(dq, dk, dvt, lam, norm_w, lam_init, lq, tq):
    bsz, lc, _ = dq.shape
    return pl.pallas_call(
        functools.partial(_attn_kernel, lam_init),
        grid=(bsz, DH, lq // tq),
        in_specs=[pl.BlockSpec((1, tq, 2 * DDK), lambda b, h, i: (b, i, h)),
                  pl.BlockSpec((1, lc, 2 * DDK), lambda b, h, i: (b, 0, h)),
                  pl.BlockSpec((1, DDV, lc), lambda b, h, i: (b, h, 0)),
                  pl.BlockSpec((4, DDK), lambda b, h, i: (0, 0)),
                  pl.BlockSpec((1, DDV), lambda b, h, i: (0, 0))],
        out_specs=pl.BlockSpec((1, tq, DDV), lambda b, h, i: (b, i, h)),
        out_shape=jax.ShapeDtypeStruct((bsz, lq, DH * DDV), BF16),
        compiler_params=_cparams(("arbitrary", "arbitrary", "arbitrary")),
        name="attn",
    )(dq, dk, dvt, lam, norm_w.reshape(1, DDV))


def _out_kernel(alpha, of_ref, or_ref, z_ref, yd_ref, x_ref, g1_ref, sh2_ref, sc2_ref, gnw_ref,
                wo_ref, lng_ref, lnb_ref, wr_ref, br_ref, x1_ref, h2_ref, lg_ref):
    o = of_ref[0] + or_ref[0]
    z = z_ref[0]
    heads = []
    for h in range(GH):
        oh = o[:, h * GDV:(h + 1) * GDV]
        heads.append((oh * lax.rsqrt(jnp.mean(oh * oh, axis=-1, keepdims=True) + NORM_EPS) * gnw_ref[...]
                      * _silu(z[:, h * GDV:(h + 1) * GDV])).astype(BF16))
    mix = jnp.concatenate(heads + [yd_ref[0]], axis=1)
    x1 = _normalize(alpha * x_ref[0] + g1_ref[0] * _dot(mix, wo_ref[...])) * lng_ref[...] + lnb_ref[...]
    x1_ref[0] = x1
    h2 = _normalize(x1) * (1.0 + sc2_ref[0]) + sh2_ref[0]
    h2_ref[0] = h2
    h_hi = h2.astype(BF16)
    h_lo = (h2 - h_hi.astype(F32)).astype(BF16)
    both = _dot(h_hi, wr_ref[...])
    lg_ref[0] = br_ref[...] + both[:, :LANES] + both[:, LANES:] + _dot(h_lo, wr_ref[:, :LANES])


def _out(o_f, o_r, z, y_diff, xc, mod3, gdn_norm_w, w_out, ln_g, ln_b, w_router3, b_router, alpha):
    bsz, lq, _ = y_diff.shape
    nt = lq // ROW_TILE

    def row_spec(width):
        return pl.BlockSpec((1, ROW_TILE, width), lambda b, i: (b, i, 0))

    def mod_spec(col):
        return pl.BlockSpec((1, 1, D), lambda b, i: (b, 0, col))

    def const_spec(shape):
        return pl.BlockSpec(shape, lambda b, i: tuple(0 for _ in shape))

    return pl.pallas_call(
        functools.partial(_out_kernel, alpha),
        grid=(bsz, nt),
        in_specs=[row_spec(GH * GDV), row_spec(GH * GDV), row_spec(G_Z), row_spec(DH * DDV), row_spec(D),
                  mod_spec(2), mod_spec(3), mod_spec(4),
                  const_spec((1, GDV)), const_spec((G_Z + DH * DDV, D)),
                  const_spec((1, D)), const_spec((1, D)),
                  const_spec((D, 2 * LANES)), const_spec((1, LANES))],
        out_specs=[row_spec(D), row_spec(D), row_spec(LANES)],
        out_shape=[jax.ShapeDtypeStruct((bsz, lq, D), F32), jax.ShapeDtypeStruct((bsz, lq, D), F32),
                   jax.ShapeDtypeStruct((bsz, lq, LANES), F32)],
        compiler_params=_cparams(("arbitrary", "arbitrary")),
        name="out",
    )(o_f, o_r, z, y_diff, xc, mod3, mod3, mod3, gdn_norm_w.reshape(1, GDV), w_out,
      ln_g.reshape(1, D), ln_b.reshape(1, D), w_router3, b_router)


def _route_kernel(lg_ref, tri_ref, ids_ref, gate_ref, cnt_ref, run_ref):
    @pl.when(pl.program_id(0) == 0)
    def _():
        run_ref[...] = jnp.zeros_like(run_ref)

    lg = lg_ref[...]
    lane_i = lax.broadcasted_iota(jnp.int32, lg.shape, 1)
    lane = lane_i.astype(F32)
    neg = jnp.float32(-jnp.inf)

    def first_max(mask):
        masked = jnp.where(mask, lg, neg)
        m = jnp.max(masked, axis=-1, keepdims=True)
        idx = jnp.min(jnp.where(jnp.logical_and(mask, masked == m), lane, float(LANES)), axis=-1, keepdims=True)
        return m, idx

    is_group = lane_i < N_GROUPS
    m_g, grp = first_max(is_group)
    p_g = 1.0 / jnp.sum(jnp.where(is_group, jnp.exp(lg - m_g), 0.0), axis=-1, keepdims=True)
    e_lane = lane - float(N_GROUPS)
    lo_e = grp * float(E_PER_GROUP)
    in_group = jnp.logical_and(e_lane >= lo_e, e_lane < lo_e + float(E_PER_GROUP))
    l0, i0 = first_max(in_group)
    l1, i1 = first_max(jnp.logical_and(in_group, lane != i0))
    r = jnp.exp(l1 - l0)
    gate0 = p_g / (1.0 + r)
    gate1 = p_g * r / (1.0 + r)
    e0 = i0 - float(N_GROUPS)
    e1 = i1 - float(N_GROUPS)
    oh0 = (lane == e0).astype(F32)
    oh1 = (lane == e1).astype(F32)
    both = oh0 + oh1
    before = _dot(tri_ref[...], both.astype(BF16)) + run_ref[...]
    rank0 = jnp.sum(oh0 * before, axis=-1, keepdims=True)
    rank1 = jnp.sum(oh1 * before, axis=-1, keepdims=True)
    run_ref[...] = run_ref[...] + jnp.sum(both, axis=0, keepdims=True)
    cnt_ref[...] = run_ref[...]
    ids = jnp.where(lane_i == 0, e0, jnp.where(lane_i == 1, e1,
                                               jnp.where(lane_i == 2, rank0, jnp.where(lane_i == 3, rank1, 0.0))))
    ids_ref[...] = ids.T[:8].astype(jnp.int32)
    gate_ref[...] = jnp.where(lane_i == 0, gate0, jnp.where(lane_i == 1, gate1, 0.0))


def _route(logits, tri_strict):
    t = logits.shape[0]
    tile = tri_strict.shape[0]
    return pl.pallas_call(
        _route_kernel,
        grid=(t // tile,),
        in_specs=[pl.BlockSpec((tile, LANES), lambda i: (i, 0)),
                  pl.BlockSpec((tile, tile), lambda i: (0, 0))],
        out_specs=[pl.BlockSpec((8, tile), lambda i: (0, i)),
                   pl.BlockSpec((tile, LANES), lambda i: (i, 0)),
                   pl.BlockSpec((1, LANES), lambda i: (0, 0))],
        out_shape=[jax.ShapeDtypeStruct((8, t), jnp.int32), jax.ShapeDtypeStruct((t, LANES), F32),
                   jax.ShapeDtypeStruct((1, LANES), F32)],
        scratch_shapes=[pltpu.VMEM((1, LANES), F32)],
        compiler_params=_cparams(("arbitrary",)),
        name="route",
    )(logits, tri_strict)


DMA_TOKENS = 512


def _dispatch_kernel(tail_ref, used_ref, dest_ref, h_ref, buf_hbm, zero_ref, sem, zsem):
    @pl.when(pl.program_id(0) == 0)
    def _():
        zero_ref[...] = jnp.zeros_like(zero_ref)
        n_blocks = buf_hbm.shape[0] // FFN_BLOCK

        def block_copy(b):
            return pltpu.make_async_copy(zero_ref, buf_hbm.at[pl.ds(b * FFN_BLOCK, FFN_BLOCK)], zsem)

        for e in range(N_EXPERTS):
            @pl.when(tail_ref[e] >= 0)
            def _():
                block_copy(tail_ref[e]).start()
        lax.fori_loop(used_ref[0], n_blocks, lambda b, carry: (block_copy(b).start(), carry)[1], 0)
        for e in range(N_EXPERTS):
            @pl.when(tail_ref[e] >= 0)
            def _():
                block_copy(tail_ref[e]).wait()
        lax.fori_loop(used_ref[0], n_blocks, lambda b, carry: (block_copy(b).wait(), carry)[1], 0)

    def start(tok, carry):
        for j in range(2):
            pltpu.make_async_copy(h_ref.at[pl.ds(tok, 1)], buf_hbm.at[pl.ds(dest_ref[2 * tok + j], 1)], sem).start()
        return carry

    lax.fori_loop(0, DMA_TOKENS, start, 0, unroll=8)
    for _ in range(2):
        pltpu.make_async_copy(h_ref, buf_hbm.at[pl.ds(0, DMA_TOKENS)], sem).wait()


def _dispatch(tail_block, n_used, dest_flat, h2, n_rows):
    t, d = h2.shape
    grid_spec = pltpu.PrefetchScalarGridSpec(
        num_scalar_prefetch=2,
        grid=(t // DMA_TOKENS,),
        in_specs=[pl.BlockSpec((2 * DMA_TOKENS,), lambda i, tail, used: (i,), memory_space=pltpu.SMEM),
                  pl.BlockSpec((DMA_TOKENS, d), lambda i, tail, used: (i, 0))],
        out_specs=pl.BlockSpec(memory_space=pl.ANY),
        scratch_shapes=[pltpu.VMEM((FFN_BLOCK, d), h2.dtype), pltpu.SemaphoreType.DMA(()),
                        pltpu.SemaphoreType.DMA(())],
    )
    return pl.pallas_call(
        _dispatch_kernel,
        grid_spec=grid_spec,
        out_shape=jax.ShapeDtypeStruct((n_rows, d), h2.dtype),
        compiler_params=_cparams(("arbitrary",)),
        name="dispatch",
    )(tail_block, n_used, dest_flat, h2)


def _combine_kernel(alpha, dest_ref, ys_hbm, gate_ref, x1_ref, g2_ref, lng_ref, lnb_ref, o_ref, r0_ref, r1_ref, sem):
    n = x1_ref.shape[0]

    def start(tok, carry):
        for j, r_ref in enumerate((r0_ref, r1_ref)):
            pltpu.make_async_copy(ys_hbm.at[pl.ds(dest_ref[2 * tok + j], 1)], r_ref.at[pl.ds(tok, 1)], sem).start()
        return carry

    lax.fori_loop(0, n, start, 0, unroll=8)
    for r_ref in (r0_ref, r1_ref):
        pltpu.make_async_copy(ys_hbm.at[pl.ds(0, n)], r_ref, sem).wait()
    gate = gate_ref[...]
    y = gate[:, 0:1] * r0_ref[...] + gate[:, 1:2] * r1_ref[...]
    o_ref[...] = _normalize(alpha * x1_ref[...] + g2_ref[0] * y) * lng_ref[...] + lnb_ref[...]


def _combine(dest_flat, ys, gate, x1, mod3, ln_g, ln_b, alpha, rows_per_batch):
    t, d = x1.shape
    per_batch = rows_per_batch // DMA_TOKENS
    return pl.pallas_call(
        functools.partial(_combine_kernel, alpha),
        grid=(t // DMA_TOKENS,),
        in_specs=[pl.BlockSpec((2 * DMA_TOKENS,), lambda i: (i,), memory_space=pltpu.SMEM),
                  pl.BlockSpec(memory_space=pl.ANY),
                  pl.BlockSpec((DMA_TOKENS, LANES), lambda i: (i, 0)),
                  pl.BlockSpec((DMA_TOKENS, d), lambda i: (i, 0)),
                  pl.BlockSpec((1, 1, d), lambda i: (i // per_batch, 0, 5)),
                  pl.BlockSpec((1, d), lambda i: (0, 0)),
                  pl.BlockSpec((1, d), lambda i: (0, 0))],
        out_specs=pl.BlockSpec((DMA_TOKENS, d), lambda i: (i, 0)),
        out_shape=jax.ShapeDtypeStruct((t, d), F32),
        scratch_shapes=[pltpu.VMEM((DMA_TOKENS, d), F32), pltpu.VMEM((DMA_TOKENS, d), F32),
                        pltpu.SemaphoreType.DMA(())],
        compiler_params=_cparams(("arbitrary",)),
        name="combine",
    )(dest_flat, ys, gate, x1, mod3, ln_g.reshape(1, d), ln_b.reshape(1, d))


def _ffn_kernel(start_ref, nblk_ref, used_ref, buf_hbm, wg_ref, wu_ref, wd_ref, ys_hbm,
                wgb_ref, wub_ref, wdb_ref, x_buf, o_buf, in_sem, out_sem):
    e = pl.program_id(0)
    nb = nblk_ref[e]
    base = start_ref[e]
    used = used_ref[0]
    n_in = x_buf.shape[0]
    n_out = o_buf.shape[0]
    ahead = n_in - 1

    def rows(g):
        return pl.ds(g * FFN_BLOCK, FFN_BLOCK)

    def in_copy(g):
        slot = lax.rem(g, n_in)
        return pltpu.make_async_copy(buf_hbm.at[rows(g)], x_buf.at[slot], in_sem.at[slot])

    def out_copy(g):
        slot = lax.rem(g, n_out)
        return pltpu.make_async_copy(o_buf.at[slot], ys_hbm.at[rows(g)], out_sem.at[slot])

    @pl.when(e == 0)
    def _():
        for g in range(ahead):
            @pl.when(g < used)
            def _():
                in_copy(g).start()

    @pl.when(nb > 0)
    def _():
        wgb_ref[...] = wg_ref[0].astype(BF16)
        wub_ref[...] = wu_ref[0].astype(BF16)
        wdb_ref[...] = wd_ref[0].astype(BF16)

        def block(j, carry):
            g = base + j
            in_copy(g).wait()

            @pl.when(g + ahead < used)
            def _():
                in_copy(g + ahead).start()

            @pl.when(g >= n_out)
            def _():
                out_copy(g - n_out).wait()

            islot = lax.rem(g, n_in)
            oslot = lax.rem(g, n_out)
            slab = FFN_BLOCK // FFN_SLABS
            xs = [x_buf[islot, s * slab:(s + 1) * slab, :].astype(BF16) for s in range(FFN_SLABS)]
            gates = [_dot(x, wgb_ref[...]) for x in xs]
            ups = [_dot(x, wub_ref[...]) for x in xs]
            hidden = [(_silu(gt) * up).astype(BF16) for gt, up in zip(gates, ups)]
            for s in range(FFN_SLABS):
                o_buf[oslot, s * slab:(s + 1) * slab, :] = _dot(hidden[s], wdb_ref[...])
            out_copy(g).start()
            return carry

        lax.fori_loop(0, nb, block, 0)

    @pl.when(e == pl.num_programs(0) - 1)
    def _():
        for back in range(n_out):
            @pl.when(used - 1 - back >= 0)
            def _():
                out_copy(used - 1 - back).wait()
        n_blocks = ys_hbm.shape[0] // FFN_BLOCK
        o_buf[0] = jnp.zeros(o_buf.shape[1:], o_buf.dtype)

        def clear(b):
            return pltpu.make_async_copy(o_buf.at[0], ys_hbm.at[pl.ds(b * FFN_BLOCK, FFN_BLOCK)], out_sem.at[0])

        lax.fori_loop(used_ref[0], n_blocks, lambda b, carry: (clear(b).start(), carry)[1], 0)
        lax.fori_loop(used_ref[0], n_blocks, lambda b, carry: (clear(b).wait(), carry)[1], 0)


def _ffn(seg_start, seg_blocks, n_used, buf, w_gate, w_up, w_down):
    n_rows, d = buf.shape
    n_experts = w_gate.shape[0]
    grid_spec = pltpu.PrefetchScalarGridSpec(
        num_scalar_prefetch=3,
        grid=(n_experts,),
        in_specs=[pl.BlockSpec(memory_space=pl.ANY),
                  pl.BlockSpec((1, d, D_EXPERT), lambda e, *_: (e, 0, 0)),
                  pl.BlockSpec((1, d, D_EXPERT), lambda e, *_: (e, 0, 0)),
                  pl.BlockSpec((1, D_EXPERT, d), lambda e, *_: (e, 0, 0))],
        out_specs=pl.BlockSpec(memory_space=pl.ANY),
        scratch_shapes=[pltpu.VMEM((d, D_EXPERT), BF16), pltpu.VMEM((d, D_EXPERT), BF16),
                        pltpu.VMEM((D_EXPERT, d), BF16),
                        pltpu.VMEM((FFN_IN_SLOTS, FFN_BLOCK, d), F32), pltpu.VMEM((FFN_OUT_SLOTS, FFN_BLOCK, d), F32),
                        pltpu.SemaphoreType.DMA((FFN_IN_SLOTS,)), pltpu.SemaphoreType.DMA((FFN_OUT_SLOTS,))],
    )
    return pl.pallas_call(
        _ffn_kernel,
        grid_spec=grid_spec,
        out_shape=jax.ShapeDtypeStruct((n_rows, d), F32),
        compiler_params=_cparams(("arbitrary",)),
        name="ffn",
    )(seg_start, seg_blocks, n_used, buf, w_gate, w_up, w_down)


def _rope_tables(seq_len, n_ctx_rows):
    lane = np.arange(LANES)
    within = lane % DDK
    freq = within % 16
    use_col = within >= DDK // 2
    sign = np.where((lane % 32) < 16, -1.0, 1.0).astype(np.float32)
    half = DDK // 2
    inv_freq = (ROPE_THETA ** (-np.arange(0, half, 2, dtype=np.float32) / np.float32(half))).astype(np.float32)
    t = np.arange(seq_len)
    pos = np.where(use_col[None, :], (t % GRID_W)[:, None], (t // GRID_W)[:, None]).astype(np.float32)
    ang = pos * inv_freq[freq][None, :]
    cos = np.concatenate([np.cos(ang), np.ones((n_ctx_rows, LANES), np.float32)], axis=0)
    sin = np.concatenate([np.sin(ang) * sign[None, :], np.zeros((n_ctx_rows, LANES), np.float32)], axis=0)
    return jnp.asarray(cos, F32), jnp.asarray(sin, F32)


def _block_tri(n, block, upper):
    i = np.arange(n)
    same = (i[:, None] // block) == (i[None, :] // block)
    tri = (i[None, :] >= i[:, None]) if upper else (i[None, :] <= i[:, None])
    return jnp.asarray((same & tri).astype(np.float32), dtype=BF16)


def _pack_w_in(w_in):
    pad = jnp.zeros((D, LANES - G_GATES), w_in.dtype)
    off = G_QKV + G_Z
    v_start = G_COLS + 2 * D_QK
    w_all = jnp.concatenate([
        w_in[:, :G_QKV], w_in[:, G_QKV:off],
        w_in[:, off:off + G_GATES], pad,
        w_in[:, off + G_GATES:G_COLS], pad,
        w_in[:, G_COLS:v_start],
    ], axis=1).astype(BF16)
    return w_all, w_in[:, v_start:].T.astype(BF16)


def kernel(x, c, ctx, c_ctx, w_ada, b_ada, w_in, conv_w, gdn_a_log, gdn_dt_bias, gdn_norm_w, diff_lambda,
           diff_norm_w, w_out, ln1_g, ln1_b, w_router_group, b_router_group, w_router_expert, b_router_expert,
           w_expert_gate, w_expert_up, w_expert_down, ln2_g, ln2_b):
    depth = w_ada.shape[0]
    assert depth == 1, "single-layer block: the context stream never feeds a later layer"
    bsz, seq, _ = x.shape
    n_ctx = ctx.shape[1]
    assert seq % ROW_TILE == 0 and n_ctx % ROW_TILE == 0 and seq % GRID_W == 0 and bsz < 8
    n_lat_tiles = seq // ROW_TILE
    alpha = (2.0 * depth) ** 0.25
    lam_init = 0.8 - 0.6 * math.exp(-0.3 * 0)
    i = 0

    cc = jnp.zeros((8, D), F32).at[:bsz].set(c).at[7].set(c_ctx)
    mod = _ada(cc, w_ada[i], b_ada[i])
    mod3 = mod.reshape(8, 1, 6 * D)

    cos_t, sin_t = _rope_tables(seq, n_ctx)
    qkv, z, gt, dq, dk, dvt = _proj(x, ctx, mod3, cos_t, sin_t, *_pack_w_in(w_in[i]))

    conv_w8 = jnp.zeros((8, G_QKV), F32).at[:GCONV].set(conv_w[i])
    alog_v = jnp.zeros((1, LANES), F32).at[0, :G_GATES].set(gdn_a_log[i].reshape(-1))
    dtb_v = jnp.zeros((1, LANES), F32).at[0, :G_GATES].set(gdn_dt_bias[i].reshape(-1))
    gq, gk, gv, beta, gc = _prep(qkv, gt, conv_w8, alog_v, dtb_v,
                                 _block_tri(ROW_TILE, CHUNK, False), _block_tri(ROW_TILE, CHUNK, True), n_lat_tiles)
    uw = _chunk(gk, gv, beta, gc)
    o_f, o_r = _scan(gq, gk, gc, uw, n_lat_tiles)

    y_diff = _attn(dq, dk, dvt, diff_lambda[i], diff_norm_w[i], lam_init, seq, tq=ATTN_Q_TILE)

    w_router = jnp.zeros((D, LANES), F32).at[:, :N_GROUPS].set(w_router_group[i]) \
        .at[:, N_GROUPS:N_GROUPS + N_EXPERTS].set(w_router_expert[i])
    w_hi = w_router.astype(BF16)
    w_router3 = jnp.concatenate([w_hi, (w_router - w_hi.astype(F32)).astype(BF16)], axis=1)
    b_router = jnp.zeros((1, LANES), F32).at[0, :N_GROUPS].set(b_router_group[i]) \
        .at[0, N_GROUPS:N_GROUPS + N_EXPERTS].set(b_router_expert[i])
    x1, h2, logits = _out(o_f, o_r, z, y_diff, x, mod3, gdn_norm_w[i], w_out[i].astype(BF16), ln1_g[i], ln1_b[i],
                          w_router3, b_router, alpha)

    t = bsz * seq
    tri_strict = jnp.asarray(np.tril(np.ones((ROW_TILE, ROW_TILE), np.float32), -1), dtype=BF16)
    ids, gate, counts = _route(logits.reshape(t, LANES), tri_strict)

    cnt = counts[0, :N_EXPERTS].astype(jnp.int32)
    padded = (cnt + FFN_BLOCK - 1) // FFN_BLOCK * FFN_BLOCK
    p_end = jnp.cumsum(padded)
    p_start = p_end - padded
    n_blocks = (2 * t) // FFN_BLOCK + N_EXPERTS
    n_used = (p_end[-1:] // FFN_BLOCK).astype(jnp.int32)
    tail_block = jnp.where(cnt > 0, p_end // FFN_BLOCK - 1, -1).astype(jnp.int32)
    one_hot = ids[0:2, :, None] == jnp.arange(N_EXPERTS, dtype=jnp.int32)
    row_start = jnp.sum(jnp.where(one_hot, p_start, 0), axis=-1)
    dest = (row_start + ids[2:4]).T.reshape(-1).astype(jnp.int32)

    buf = _dispatch(tail_block, n_used, dest, h2.reshape(t, D), n_blocks * FFN_BLOCK)
    ys = _ffn((p_start // FFN_BLOCK).astype(jnp.int32), (padded // FFN_BLOCK).astype(jnp.int32), n_used, buf,
              w_expert_gate[i], w_expert_up[i], w_expert_down[i])
    out = _combine(dest, ys, gate, x1.reshape(t, D), mod3, ln2_g[i], ln2_b[i], alpha, seq)
    return out.reshape(bsz, seq, D)
```

```python
import functools
import math

import numpy as np
import jax
import jax.numpy as jnp
from jax import lax
from jax.experimental import pallas as pl
from jax.experimental.pallas import tpu as pltpu

F32 = jnp.float32
BF16 = jnp.bfloat16
HIGHEST = lax.Precision.HIGHEST

D = 1024
GRID_W = 64
ROPE_THETA = 10000.0
GH = 4
GDK = 128
GDV = 128
GCONV = 5
DH = 4
DDK = 64
DDV = 128
G_QK = GH * GDK
G_QKV = 2 * G_QK + GH * GDV
G_Z = GH * GDV
G_GATES = 2 * GH
G_COLS = G_QKV + G_Z + 2 * G_GATES
D_QK = DH * 2 * DDK
N_GROUPS = 4
E_PER_GROUP = 8
N_EXPERTS = N_GROUPS * E_PER_GROUP
D_EXPERT = 512
LN_EPS = 1e-5
NORM_EPS = 1e-6

LANES = 128
ROW_TILE = 256
CHUNK = 64
SCAN_BATCH = 2
FFN_BLOCK = 256
FFN_SLABS = 2
FFN_IN_SLOTS = 4
FFN_OUT_SLOTS = 3
KEY_TILE = 512
ATTN_Q_TILE = 512
SCORE_LOOKAHEAD = 2
VMEM_LIMIT = 56 * 1024 * 1024

C_QKV = 0
C_Z = C_QKV + G_QKV
C_GB = C_Z + G_Z
C_GA = C_GB + LANES
C_DQ = C_GA + LANES
C_DK = C_DQ + D_QK
C_END = C_DK + D_QK
Q_SCALE = DDK ** -0.5 * math.log2(math.e)


def _cparams(sem):
    return pltpu.CompilerParams(dimension_semantics=sem, vmem_limit_bytes=VMEM_LIMIT)


def _dot(a, b):
    return jnp.dot(a, b, preferred_element_type=F32)


def _dot_nt(a, b):
    return lax.dot_general(a, b, (((1,), (1,)), ((), ())), preferred_element_type=F32)


def _dot_tn(a, b):
    return lax.dot_general(a, b, (((0,), (0,)), ((), ())), preferred_element_type=F32)


def _normalize(x):
    mu = jnp.mean(x, axis=-1, keepdims=True)
    xc = x - mu
    var = jnp.mean(xc * xc, axis=-1, keepdims=True)
    return xc * lax.rsqrt(var + LN_EPS)


def _silu(x):
    return x * jax.nn.sigmoid(x)


def _split3(x):
    hi = x.astype(BF16)
    r = x - hi.astype(F32)
    mid = r.astype(BF16)
    lo = (r - mid.astype(F32)).astype(BF16)
    return hi, mid, lo


def _ada_kernel(c_ref, w_ref, b_ref, o_ref):
    s = _silu(c_ref[...])
    o_ref[...] = jnp.dot(s, w_ref[...], precision=HIGHEST, preferred_element_type=F32) + b_ref[...]


def _ada(cc, w_ada, b_ada):
    n = w_ada.shape[1]
    bn = 1024
    return pl.pallas_call(
        _ada_kernel,
        grid=(n // bn,),
        in_specs=[pl.BlockSpec((8, D), lambda j: (0, 0)),
                  pl.BlockSpec((D, bn), lambda j: (0, j)),
                  pl.BlockSpec((1, bn), lambda j: (0, j))],
        out_specs=pl.BlockSpec((8, bn), lambda j: (0, j)),
        out_shape=jax.ShapeDtypeStruct((8, n), F32),
        compiler_params=_cparams(("arbitrary",)),
        name="ada",
    )(cc, w_ada, b_ada.reshape(1, n))


def _rope(v, cos, sin, low_half):
    fwd = pltpu.roll(v, LANES - 16, axis=1)
    bwd = pltpu.roll(v, 16, axis=1)
    return v * cos + jnp.where(low_half, fwd, bwd) * sin


def _proj_kernel(n_lat_tiles, x_ref, ctx_ref, sh_ref, sc_ref, cos_ref, sin_ref, w_ref, wvt_ref,
                 qkv_ref, z_ref, gt_ref, dq_ref, dk_ref, dvt_ref):
    rows = jnp.where(pl.program_id(1) >= n_lat_tiles, ctx_ref[0], x_ref[0])
    h = (_normalize(rows) * (1.0 + sc_ref[0]) + sh_ref[0]).astype(BF16)
    qkv_ref[0] = _dot(h, w_ref[:, C_QKV:C_Z])
    z_ref[0] = _dot(h, w_ref[:, C_Z:C_GB])
    gt_ref[0] = _dot(h, w_ref[:, C_GB:C_DQ])
    cos = cos_ref[...]
    sin = sin_ref[...]
    lane = lax.broadcasted_iota(jnp.int32, cos.shape, 1)
    low_half = (lane % 32) < 16
    q_all = _dot(h, w_ref[:, C_DQ:C_DK])
    k_all = _dot(h, w_ref[:, C_DK:C_END])
    for j in range(D_QK // LANES):
        slab = slice(j * LANES, (j + 1) * LANES)
        dq_ref[0, :, slab] = (_rope(q_all[:, slab], cos, sin, low_half) * Q_SCALE).astype(BF16)
        dk_ref[0, :, slab] = _rope(k_all[:, slab], cos, sin, low_half).astype(BF16)
    dvt_ref[0] = _dot_nt(wvt_ref[...], h).astype(BF16)


def _proj(x, ctx, mod3, cos_t, sin_t, w_all, wv_t):
    bsz, seq, _ = x.shape
    lc = seq + ctx.shape[1]
    nt = lc // ROW_TILE
    n_lat_tiles = seq // ROW_TILE
    ctx_row = mod3.shape[0] - 1

    def mod_idx(col):
        return lambda b, i: (jnp.where(i >= n_lat_tiles, ctx_row, b), 0, col)

    def row_spec(width):
        return pl.BlockSpec((1, ROW_TILE, width), lambda b, i: (b, i, 0))

    outs = [(G_QKV, F32), (G_Z, F32), (2 * LANES, F32), (D_QK, BF16), (D_QK, BF16)]
    return pl.pallas_call(
        functools.partial(_proj_kernel, n_lat_tiles),
        grid=(bsz, nt),
        in_specs=[pl.BlockSpec((1, ROW_TILE, D), lambda b, i: (b, jnp.minimum(i, n_lat_tiles - 1), 0)),
                  pl.BlockSpec((1, ROW_TILE, D), lambda b, i: (b, jnp.maximum(i - n_lat_tiles, 0), 0)),
                  pl.BlockSpec((1, 1, D), mod_idx(0)),
                  pl.BlockSpec((1, 1, D), mod_idx(1)),
                  pl.BlockSpec((ROW_TILE, LANES), lambda b, i: (i, 0)),
                  pl.BlockSpec((ROW_TILE, LANES), lambda b, i: (i, 0)),
                  pl.BlockSpec((D, C_END), lambda b, i: (0, 0)),
                  pl.BlockSpec((DH * DDV, D), lambda b, i: (0, 0))],
        out_specs=[row_spec(w) for w, _ in outs]
        + [pl.BlockSpec((1, DH * DDV, ROW_TILE), lambda b, i: (b, 0, i))],
        out_shape=[jax.ShapeDtypeStruct((bsz, lc, w), dt) for w, dt in outs]
        + [jax.ShapeDtypeStruct((bsz, DH * DDV, lc), BF16)],
        compiler_params=_cparams(("arbitrary", "arbitrary")),
        name="proj",
    )(x, ctx, mod3, mod3, cos_t, sin_t, w_all, wv_t)


def _prep_kernel(n_lat_tiles, nt, main_ref, prev_ref, next_ref, gt_ref, cw_ref, alog_ref, dtb_ref,
                 lo_ref, up_ref, q_ref, k_ref, v_ref, beta_ref, gc_ref, ext_ref):
    i = pl.program_id(1)
    halo = prev_ref.shape[1]
    has_prev = jnp.logical_and(i != 0, i != n_lat_tiles)
    has_next = jnp.logical_and(i != n_lat_tiles - 1, i != nt - 1)
    ext_ref[0:halo, :] = prev_ref[0] * has_prev.astype(F32)
    ext_ref[halo:halo + ROW_TILE, :] = main_ref[0]
    ext_ref[halo + ROW_TILE:, :] = next_ref[0] * has_next.astype(F32)
    pad = GCONV // 2
    acc = None
    for j in range(GCONV):
        start = halo - pad + j
        term = ext_ref[start:start + ROW_TILE, :] * cw_ref[j:j + 1, :]
        acc = term if acc is None else acc + term
    qkv = _silu(acc)
    for h in range(GH):
        q = qkv[:, h * GDK:(h + 1) * GDK]
        q_ref[0, :, h * GDK:(h + 1) * GDK] = q * (lax.rsqrt(jnp.sum(q * q, axis=-1, keepdims=True) + NORM_EPS)
                                                * (GDK ** -0.5))
        k = qkv[:, G_QK + h * GDK:G_QK + (h + 1) * GDK]
        k_ref[0, :, h * GDK:(h + 1) * GDK] = k * lax.rsqrt(jnp.sum(k * k, axis=-1, keepdims=True) + NORM_EPS)
    v_ref[0] = qkv[:, 2 * G_QK:]
    gt = gt_ref[0]
    beta_ref[0] = jax.nn.sigmoid(gt[:, :LANES])
    a = gt[:, LANES:] + dtb_ref[...]
    softplus = jnp.maximum(a, 0.0) + jnp.log(1.0 + jnp.exp(-jnp.abs(a)))
    g = -jnp.exp(alog_ref[...]) * softplus
    lo = lo_ref[...]
    up = up_ref[...]
    fwd = None
    bwd = None
    for part in _split3(g):
        f = _dot(lo, part)
        r = _dot(up, part)
        fwd = f if fwd is None else fwd + f
        bwd = r if bwd is None else bwd + r
    lane = lax.broadcasted_iota(jnp.int32, g.shape, 1)
    gc_ref[0] = jnp.where(lane < GH, fwd, bwd)


def _prep(qkv, gt, conv_w8, alog_v, dtb_v, tri_lo, tri_up, n_lat_tiles):
    bsz, lc, _ = qkv.shape
    nt = lc // ROW_TILE
    halo = 8
    per = ROW_TILE // halo
    last = lc // halo - 1

    def row_spec(width):
        return pl.BlockSpec((1, ROW_TILE, width), lambda b, i: (b, i, 0))

    def const_spec(shape):
        return pl.BlockSpec(shape, lambda b, i: tuple(0 for _ in shape))

    outs = [(G_QK, F32), (G_QK, F32), (GH * GDV, F32), (LANES, F32), (LANES, F32)]
    return pl.pallas_call(
        functools.partial(_prep_kernel, n_lat_tiles, nt),
        grid=(bsz, nt),
        in_specs=[row_spec(G_QKV),
                  pl.BlockSpec((1, halo, G_QKV), lambda b, i: (b, jnp.maximum(i * per - 1, 0), 0)),
                  pl.BlockSpec((1, halo, G_QKV), lambda b, i: (b, jnp.minimum((i + 1) * per, last), 0)),
                  row_spec(2 * LANES),
                  const_spec((8, G_QKV)), const_spec((1, LANES)), const_spec((1, LANES)),
                  const_spec((ROW_TILE, ROW_TILE)), const_spec((ROW_TILE, ROW_TILE))],
        out_specs=[row_spec(w) for w, _ in outs],
        out_shape=[jax.ShapeDtypeStruct((bsz, lc, w), dt) for w, dt in outs],
        scratch_shapes=[pltpu.VMEM((ROW_TILE + 2 * halo, G_QKV), F32)],
        compiler_params=_cparams(("arbitrary", "arbitrary")),
        name="prep",
    )(qkv, qkv, qkv, gt, conv_w8, alog_v, dtb_v, tri_lo, tri_up)


def _chunk_masks(n, rev):
    row = lax.broadcasted_iota(jnp.int32, (n, n), 0)
    col = lax.broadcasted_iota(jnp.int32, (n, n), 1)
    same = (row // CHUNK) == (col // CHUNK)
    incl = jnp.logical_and(same, (col >= row) if rev else (col <= row))
    strict = jnp.logical_and(same, (col > row) if rev else (col < row))
    return incl, strict


def _decay(gcol, grow, incl):
    return jnp.where(incl, jnp.exp(jnp.minimum(gcol - grow, 0.0)), 0.0)


def _chunk_kernel(k_ref, v_ref, beta_ref, gc_ref, uw_ref):
    n = k_ref.shape[1]
    gc = gc_ref[0]
    gct = gc.T
    beta = beta_ref[0]
    row = lax.broadcasted_iota(jnp.int32, (n, n), 0)
    col = lax.broadcasted_iota(jnp.int32, (n, n), 1)
    eye = (row == col).astype(F32)
    sizes = [2 ** j for j in range(1, int(math.log2(CHUNK)))]
    same = {r: (row // r) == (col // r) for r in sizes + [CHUNK]}
    a_all, t_all, x_all = [], [], []
    for d in range(2):
        incl, strict = _chunk_masks(n, d == 1)
        for h in range(GH):
            c = d * GH + h
            k = k_ref[0, :, h * GDK:(h + 1) * GDK]
            v = v_ref[0, :, h * GDV:(h + 1) * GDV]
            bcol = beta[:, c:c + 1]
            gcol = gc[:, c:c + 1]
            grow = gct[c:c + 1, :]
            kb = k * bcol
            a = jnp.where(strict, _dot_nt(kb.astype(BF16), k.astype(BF16)) * _decay(gcol, grow, incl), 0.0)
            a_all.append(a.astype(BF16))
            t_all.append(eye - jnp.where(same[2], a, 0.0))
            x_all.append(jnp.concatenate([v * bcol, kb * jnp.exp(gcol)], axis=1).astype(BF16))
    n_prob = len(a_all)
    zero = jnp.zeros((n, n), BF16)
    for r in sizes:
        off_mask = jnp.logical_and(same[2 * r], jnp.logical_not(same[r]))
        t_b = [t.astype(BF16) for t in t_all]
        inner = [_dot(jnp.where(off_mask, a_all[c], zero), t_b[c]).astype(BF16) for c in range(n_prob)]
        t_all = [t_all[c] - _dot(t_b[c], inner[c]) for c in range(n_prob)]
    for c in range(n_prob):
        uw_ref[0, c] = _dot(t_all[c].astype(BF16), x_all[c])


def _chunk(k, v, beta, gc):
    bsz, lc, _ = k.shape
    nt = lc // ROW_TILE

    def row_spec(width):
        return pl.BlockSpec((1, ROW_TILE, width), lambda b, i: (b, i, 0))

    return pl.pallas_call(
        _chunk_kernel,
        grid=(bsz, nt),
        in_specs=[row_spec(G_QK), row_spec(GH * GDV), row_spec(LANES), row_spec(LANES)],
        out_specs=pl.BlockSpec((1, 2 * GH, ROW_TILE, GDV + GDK), lambda b, i: (b, 0, i, 0)),
        out_shape=jax.ShapeDtypeStruct((bsz, 2 * GH, lc, GDV + GDK), F32),
        compiler_params=_cparams(("arbitrary", "arbitrary")),
        name="chunk",
    )(k, v, beta, gc)


def _scan_kernel(qf_ref, kf_ref, gf_ref, uwf_ref, qr_ref, kr_ref, gr_ref, uwr_ref,
                 of_ref, or_ref, s_ref):
    @pl.when(pl.program_id(1) == 0)
    def _():
        s_ref[...] = jnp.zeros_like(s_ref)

    n_chunks = ROW_TILE // CHUNK
    n_batch = qf_ref.shape[0]
    row = lax.broadcasted_iota(jnp.int32, (CHUNK, CHUNK), 0)
    col = lax.broadcasted_iota(jnp.int32, (CHUNK, CHUNK), 1)
    dirs = ((qf_ref, kf_ref, gf_ref, uwf_ref, of_ref), (qr_ref, kr_ref, gr_ref, uwr_ref, or_ref))
    chains = [(bb, d, h) for bb in range(n_batch) for d in range(2) for h in range(GH)]
    state = [s_ref[i] for i in range(len(chains))]

    def chunk_rows(step, d):
        ci = n_chunks - 1 - step if d == 1 else step
        return slice(ci * CHUNK, (ci + 1) * CHUNK)

    pre = []
    for step in range(n_chunks):
        gates = {}
        for bb in range(n_batch):
            for d in range(2):
                gc = dirs[d][2][bb, chunk_rows(step, d), :]
                gates[bb, d] = (gc, gc.T)
        per_chain = []
        for bb, d, h in chains:
            q_ref, k_ref, _, uw_ref, _ = dirs[d]
            rev = d == 1
            incl = (col >= row) if rev else (col <= row)
            rows = chunk_rows(step, d)
            gc, gct = gates[bb, d]
            c = d * GH + h
            q = q_ref[bb, rows, h * GDK:(h + 1) * GDK]
            k = k_ref[bb, rows, h * GDK:(h + 1) * GDK]
            gcol = gc[:, c:c + 1]
            grow = gct[c:c + 1, :]
            last = 0 if rev else CHUNK - 1
            g_last = gcol[last:last + 1, :]
            a_qk = (_dot_nt(q.astype(BF16), k.astype(BF16)) * _decay(gcol, grow, incl)).astype(BF16)
            wq = jnp.concatenate([uw_ref[bb, h, rows, GDV:], q * jnp.exp(gcol)], axis=0).astype(BF16)
            k_tail_t = (k * jnp.exp(g_last - gcol)).T.astype(BF16)
            per_chain.append((a_qk, wq, k_tail_t, jnp.exp(g_last)))
        pre.append(per_chain)

    for step in range(n_chunks):
        ws = [_dot(pre[step][i][1], state[i].astype(BF16)) for i in range(len(chains))]
        v_new = [(dirs[d][3][bb, h, chunk_rows(step, d), :GDV] - ws[i][:CHUNK]).astype(BF16)
                 for i, (bb, d, h) in enumerate(chains)]
        for i, (bb, d, h) in enumerate(chains):
            a_qk, _, k_tail_t, decay_last = pre[step][i]
            dirs[d][4][bb, chunk_rows(step, d), h * GDV:(h + 1) * GDV] = ws[i][CHUNK:] + _dot(a_qk, v_new[i])
            state[i] = state[i] * decay_last + _dot(k_tail_t, v_new[i])
    for i in range(len(chains)):
        s_ref[i] = state[i]


def _scan(q, k, gc, uw, n_lat_tiles):
    bsz, lc, _ = q.shape
    nt = lc // ROW_TILE
    n_ctx_tiles = nt - n_lat_tiles

    def fwd(b, i):
        return jnp.where(i < n_ctx_tiles, n_lat_tiles + i, i - n_ctx_tiles)

    def bwd(b, i):
        return jnp.where(i < n_ctx_tiles, nt - 1 - i, n_lat_tiles - 1 - (i - n_ctx_tiles))

    nb = SCAN_BATCH if bsz % SCAN_BATCH == 0 else 1

    def specs(tile):
        return [pl.BlockSpec((nb, ROW_TILE, G_QK), lambda b, i: (b, tile(b, i), 0)),
                pl.BlockSpec((nb, ROW_TILE, G_QK), lambda b, i: (b, tile(b, i), 0)),
                pl.BlockSpec((nb, ROW_TILE, LANES), lambda b, i: (b, tile(b, i), 0))]

    def uw_spec(tile, d):
        return pl.BlockSpec((nb, GH, ROW_TILE, GDV + GDK), lambda b, i: (b, d, tile(b, i), 0))

    def out_spec(tile):
        return pl.BlockSpec((nb, ROW_TILE, GH * GDV), lambda b, i: (b, tile(b, i), 0))

    return pl.pallas_call(
        _scan_kernel,
        grid=(bsz // nb, nt),
        in_specs=specs(fwd) + [uw_spec(fwd, 0)] + specs(bwd) + [uw_spec(bwd, 1)],
        out_specs=[out_spec(fwd), out_spec(bwd)],
        out_shape=[jax.ShapeDtypeStruct((bsz, lc, GH * GDV), F32)] * 2,
        scratch_shapes=[pltpu.VMEM((nb * 2 * GH, GDK, GDV), F32)],
        compiler_params=_cparams(("arbitrary", "arbitrary")),
        name="scan",
    )(q, k, gc, uw, q, k, gc, uw)


def _attn_kernel(lam_init, q_ref, k_ref, vt_ref, lam_ref, nw_ref, o_ref):
    lam = lam_ref[...]
    lam_full = (jnp.exp(jnp.sum(lam[0:1] * lam[1:2], axis=-1, keepdims=True))
                - jnp.exp(jnp.sum(lam[2:3] * lam[3:4], axis=-1, keepdims=True)) + lam_init)
    q = q_ref[0]
    tq = q.shape[0]
    n_keys = k_ref.shape[1]
    lane = lax.broadcasted_iota(jnp.int32, q.shape, 1)
    zero = jnp.zeros_like(q)
    qc = [jnp.where(lane < DDK, q, zero), jnp.where(lane >= DDK, q, zero)]
    m = [jnp.full((1, tq), -jnp.inf, F32) for _ in range(2)]
    l = [jnp.zeros((1, tq), F32) for _ in range(2)]
    acc = [jnp.zeros((DDV, tq), F32) for _ in range(2)]
    n_tiles = n_keys // KEY_TILE
    bounds = [t * KEY_TILE for t in range(n_tiles)] + [n_keys]

    def scores(t):
        k_t = k_ref[0, bounds[t]:bounds[t + 1], :]
        tiles = [_dot_nt(k_t, qc[c]) for c in range(2)]
        return [(s, jnp.max(s, axis=0, keepdims=True)) for s in tiles]

    pending = [scores(t) for t in range(min(SCORE_LOOKAHEAD, n_tiles))]
    for t in range(n_tiles):
        s_cur = pending.pop(0)
        if t + SCORE_LOOKAHEAD < n_tiles:
            pending.append(scores(t + SCORE_LOOKAHEAD))
        width = bounds[t + 1] - bounds[t]
        vt_t = jnp.concatenate([vt_ref[0, :, bounds[t]:bounds[t + 1]], jnp.ones((16, width), BF16)], axis=0)
        for c in range(2):
            s, s_max = s_cur[c]
            m_new = jnp.maximum(m[c], s_max)
            p = jnp.exp2(s - m_new)
            scale = jnp.exp2(m[c] - m_new)
            pv = _dot(vt_t, p.astype(BF16))
            l[c] = scale * l[c] + pv[DDV:DDV + 1]
            acc[c] = scale * acc[c] + pv[:DDV]
            m[c] = m_new
    o = acc[0] / l[0] - lam_full * (acc[1] / l[1])
    y = o * lax.rsqrt(jnp.mean(o * o, axis=0, keepdims=True) + NORM_EPS)
    o_ref[0] = (y.T * nw_ref[...] * (1.0 - lam_init)).astype(o_ref.dtype)


def _attn(dq, dk, dvt, lam, norm_w, lam_init, lq, tq):
    bsz, lc, _ = dq.shape
    return pl.pallas_call(
        functools.partial(_attn_kernel, lam_init),
        grid=(bsz, DH, lq // tq),
        in_specs=[pl.BlockSpec((1, tq, 2 * DDK), lambda b, h, i: (b, i, h)),
                  pl.BlockSpec((1, lc, 2 * DDK), lambda b, h, i: (b, 0, h)),
                  pl.BlockSpec((1, DDV, lc), lambda b, h, i: (b, h, 0)),
                  pl.BlockSpec((4, DDK), lambda b, h, i: (0, 0)),
                  pl.BlockSpec((1, DDV), lambda b, h, i: (0, 0))],
        out_specs=pl.BlockSpec((1, tq, DDV), lambda b, h, i: (b, i, h)),
        out_shape=jax.ShapeDtypeStruct((bsz, lq, DH * DDV), BF16),
        compiler_params=_cparams(("arbitrary", "arbitrary", "arbitrary")),
        name="attn",
    )(dq, dk, dvt, lam, norm_w.reshape(1, DDV))


def _out_kernel(alpha, of_ref, or_ref, z_ref, yd_ref, x_ref, g1_ref, sh2_ref, sc2_ref, gnw_ref,
                wo_ref, lng_ref, lnb_ref, wr_ref, br_ref, x1_ref, h2_ref, lg_ref):
    o = of_ref[0] + or_ref[0]
    z = z_ref[0]
    heads = []
    for h in range(GH):
        oh = o[:, h * GDV:(h + 1) * GDV]
        heads.append((oh * lax.rsqrt(jnp.mean(oh * oh, axis=-1, keepdims=True) + NORM_EPS) * gnw_ref[...]
                      * _silu(z[:, h * GDV:(h + 1) * GDV])).astype(BF16))
    mix = jnp.concatenate(heads + [yd_ref[0]], axis=1)
    x1 = _normalize(alpha * x_ref[0] + g1_ref[0] * _dot(mix, wo_ref[...])) * lng_ref[...] + lnb_ref[...]
    x1_ref[0] = x1
    h2 = _normalize(x1) * (1.0 + sc2_ref[0]) + sh2_ref[0]
    h2_ref[0] = h2
    h_hi = h2.astype(BF16)
    h_lo = (h2 - h_hi.astype(F32)).astype(BF16)
    both = _dot(h_hi, wr_ref[...])
    lg_ref[0] = br_ref[...] + both[:, :LANES] + both[:, LANES:] + _dot(h_lo, wr_ref[:, :LANES])


def _out(o_f, o_r, z, y_diff, xc, mod3, gdn_norm_w, w_out, ln_g, ln_b, w_router3, b_router, alpha):
    bsz, lq, _ = y_diff.shape
    nt = lq // ROW_TILE

    def row_spec(width):
        return pl.BlockSpec((1, ROW_TILE, width), lambda b, i: (b, i, 0))

    def mod_spec(col):
        return pl.BlockSpec((1, 1, D), lambda b, i: (b, 0, col))

    def const_spec(shape):
        return pl.BlockSpec(shape, lambda b, i: tuple(0 for _ in shape))

    return pl.pallas_call(
        functools.partial(_out_kernel, alpha),
        grid=(bsz, nt),
        in_specs=[row_spec(GH * GDV), row_spec(GH * GDV), row_spec(G_Z), row_spec(DH * DDV), row_spec(D),
                  mod_spec(2), mod_spec(3), mod_spec(4),
                  const_spec((1, GDV)), const_spec((G_Z + DH * DDV, D)),
                  const_spec((1, D)), const_spec((1, D)),
                  const_spec((D, 2 * LANES)), const_spec((1, LANES))],
        out_specs=[row_spec(D), row_spec(D), row_spec(LANES)],
        out_shape=[jax.ShapeDtypeStruct((bsz, lq, D), F32), jax.ShapeDtypeStruct((bsz, lq, D), F32),
                   jax.ShapeDtypeStruct((bsz, lq, LANES), F32)],
        compiler_params=_cparams(("arbitrary", "arbitrary")),
        name="out",
    )(o_f, o_r, z, y_diff, xc, mod3, mod3, mod3, gdn_norm_w.reshape(1, GDV), w_out,
      ln_g.reshape(1, D), ln_b.reshape(1, D), w_router3, b_router)


def _route_kernel(lg_ref, tri_ref, ids_ref, gate_ref, cnt_ref, run_ref):
    @pl.when(pl.program_id(0) == 0)
    def _():
        run_ref[...] = jnp.zeros_like(run_ref)

    lg = lg_ref[...]
    lane_i = lax.broadcasted_iota(jnp.int32, lg.shape, 1)
    lane = lane_i.astype(F32)
    neg = jnp.float32(-jnp.inf)

    def first_max(mask):
        masked = jnp.where(mask, lg, neg)
        m = jnp.max(masked, axis=-1, keepdims=True)
        idx = jnp.min(jnp.where(jnp.logical_and(mask, masked == m), lane, float(LANES)), axis=-1, keepdims=True)
        return m, idx

    is_group = lane_i < N_GROUPS
    m_g, grp = first_max(is_group)
    p_g = 1.0 / jnp.sum(jnp.where(is_group, jnp.exp(lg - m_g), 0.0), axis=-1, keepdims=True)
    e_lane = lane - float(N_GROUPS)
    lo_e = grp * float(E_PER_GROUP)
    in_group = jnp.logical_and(e_lane >= lo_e, e_lane < lo_e + float(E_PER_GROUP))
    l0, i0 = first_max(in_group)
    l1, i1 = first_max(jnp.logical_and(in_group, lane != i0))
    r = jnp.exp(l1 - l0)
    gate0 = p_g / (1.0 + r)
    gate1 = p_g * r / (1.0 + r)
    e0 = i0 - float(N_GROUPS)
    e1 = i1 - float(N_GROUPS)
    oh0 = (lane == e0).astype(F32)
    oh1 = (lane == e1).astype(F32)
    both = oh0 + oh1
    before = _dot(tri_ref[...], both.astype(BF16)) + run_ref[...]
    rank0 = jnp.sum(oh0 * before, axis=-1, keepdims=True)
    rank1 = jnp.sum(oh1 * before, axis=-1, keepdims=True)
    run_ref[...] = run_ref[...] + jnp.sum(both, axis=0, keepdims=True)
    cnt_ref[...] = run_ref[...]
    ids = jnp.where(lane_i == 0, e0, jnp.where(lane_i == 1, e1,
                                               jnp.where(lane_i == 2, rank0, jnp.where(lane_i == 3, rank1, 0.0))))
    ids_ref[...] = ids.T[:8].astype(jnp.int32)
    gate_ref[...] = jnp.where(lane_i == 0, gate0, jnp.where(lane_i == 1, gate1, 0.0))


def _route(logits, tri_strict):
    t = logits.shape[0]
    tile = tri_strict.shape[0]
    return pl.pallas_call(
        _route_kernel,
        grid=(t // tile,),
        in_specs=[pl.BlockSpec((tile, LANES), lambda i: (i, 0)),
                  pl.BlockSpec((tile, tile), lambda i: (0, 0))],
        out_specs=[pl.BlockSpec((8, tile), lambda i: (0, i)),
                   pl.BlockSpec((tile, LANES), lambda i: (i, 0)),
                   pl.BlockSpec((1, LANES), lambda i: (0, 0))],
        out_shape=[jax.ShapeDtypeStruct((8, t), jnp.int32), jax.ShapeDtypeStruct((t, LANES), F32),
                   jax.ShapeDtypeStruct((1, LANES), F32)],
        scratch_shapes=[pltpu.VMEM((1, LANES), F32)],
        compiler_params=_cparams(("arbitrary",)),
        name="route",
    )(logits, tri_strict)


DMA_TOKENS = 512


def _dispatch_kernel(tail_ref, used_ref, dest_ref, h_ref, buf_hbm, zero_ref, sem, zsem):
    @pl.when(pl.program_id(0) == 0)
    def _():
        zero_ref[...] = jnp.zeros_like(zero_ref)
        n_blocks = buf_hbm.shape[0] // FFN_BLOCK

        def block_copy(b):
            return pltpu.make_async_copy(zero_ref, buf_hbm.at[pl.ds(b * FFN_BLOCK, FFN_BLOCK)], zsem)

        for e in range(N_EXPERTS):
            @pl.when(tail_ref[e] >= 0)
            def _():
                block_copy(tail_ref[e]).start()
        lax.fori_loop(used_ref[0], n_blocks, lambda b, carry: (block_copy(b).start(), carry)[1], 0)
        for e in range(N_EXPERTS):
            @pl.when(tail_ref[e] >= 0)
            def _():
                block_copy(tail_ref[e]).wait()
        lax.fori_loop(used_ref[0], n_blocks, lambda b, carry: (block_copy(b).wait(), carry)[1], 0)

    def start(grp, carry):
        for u in range(8):
            for j in range(2):
                dst = dest_ref[16 * grp + 2 * u + j]
                pltpu.make_async_copy(h_ref.at[grp, pl.ds(u, 1)], buf_hbm.at[pl.ds(dst, 1)], sem).start()
        return carry

    lax.fori_loop(0, DMA_TOKENS // 8, start, 0)
    for _ in range(2):
        pltpu.make_async_copy(buf_hbm.at[pl.ds(0, DMA_TOKENS)], buf_hbm.at[pl.ds(0, DMA_TOKENS)], sem).wait()


def _dispatch(tail_block, n_used, dest_flat, h2, n_rows):
    t, d = h2.shape
    grid_spec = pltpu.PrefetchScalarGridSpec(
        num_scalar_prefetch=2,
        grid=(t // DMA_TOKENS,),
        in_specs=[pl.BlockSpec((2 * DMA_TOKENS,), lambda i, tail, used: (i,), memory_space=pltpu.SMEM),
                  pl.BlockSpec((DMA_TOKENS // 8, 8, d), lambda i, tail, used: (i, 0, 0))],
        out_specs=pl.BlockSpec(memory_space=pl.ANY),
        scratch_shapes=[pltpu.VMEM((FFN_BLOCK, d), h2.dtype), pltpu.SemaphoreType.DMA(()),
                        pltpu.SemaphoreType.DMA(())],
    )
    return pl.pallas_call(
        _dispatch_kernel,
        grid_spec=grid_spec,
        out_shape=jax.ShapeDtypeStruct((n_rows, d), h2.dtype),
        compiler_params=_cparams(("arbitrary",)),
        name="dispatch",
    )(tail_block, n_used, dest_flat, h2.reshape(t // 8, 8, d))


def _combine_kernel(alpha, dest_ref, ys_hbm, gate_ref, x1_ref, g2_ref, lng_ref, lnb_ref, o_ref, rows_ref, sem):
    i = pl.program_id(0)
    n_tiles = pl.num_programs(0) - 1
    n = x1_ref.shape[0]

    @pl.when(i < n_tiles)
    def _():
        slot = lax.rem(i, 2)

        def start(grp, carry):
            for u in range(8):
                for j in range(2):
                    src = dest_ref[16 * grp + 2 * u + j]
                    pltpu.make_async_copy(ys_hbm.at[pl.ds(src, 1)], rows_ref.at[slot, j, grp, pl.ds(u, 1)],
                                          sem.at[slot]).start()
            return carry

        lax.fori_loop(0, n // 8, start, 0)

    @pl.when(i > 0)
    def _():
        slot = lax.rem(i - 1, 2)
        for j in range(2):
            pltpu.make_async_copy(ys_hbm.at[pl.ds(0, n)], ys_hbm.at[pl.ds(0, n)], sem.at[slot]).wait()
        gate = gate_ref[...]
        r0 = rows_ref[slot, 0].reshape(n, -1)
        r1 = rows_ref[slot, 1].reshape(n, -1)
        y = gate[:, 0:1] * r0 + gate[:, 1:2] * r1
        o_ref[...] = _normalize(alpha * x1_ref[...] + g2_ref[0] * y) * lng_ref[...] + lnb_ref[...]


def _combine(dest_flat, ys, gate, x1, mod3, ln_g, ln_b, alpha, rows_per_batch):
    t, d = x1.shape
    per_batch = rows_per_batch // DMA_TOKENS
    n_tiles = t // DMA_TOKENS

    def done(i):
        return jnp.maximum(i - 1, 0)

    return pl.pallas_call(
        functools.partial(_combine_kernel, alpha),
        grid=(n_tiles + 1,),
        in_specs=[pl.BlockSpec((2 * DMA_TOKENS,), lambda i: (jnp.minimum(i, n_tiles - 1),), memory_space=pltpu.SMEM),
                  pl.BlockSpec(memory_space=pl.ANY),
                  pl.BlockSpec((DMA_TOKENS, LANES), lambda i: (done(i), 0)),
                  pl.BlockSpec((DMA_TOKENS, d), lambda i: (done(i), 0)),
                  pl.BlockSpec((1, 1, d), lambda i: (done(i) // per_batch, 0, 5)),
                  pl.BlockSpec((1, d), lambda i: (0, 0)),
                  pl.BlockSpec((1, d), lambda i: (0, 0))],
        out_specs=pl.BlockSpec((DMA_TOKENS, d), lambda i: (done(i), 0)),
        out_shape=jax.ShapeDtypeStruct((t, d), F32),
        scratch_shapes=[pltpu.VMEM((2, 2, DMA_TOKENS // 8, 8, d), F32), pltpu.SemaphoreType.DMA((2,))],
        compiler_params=_cparams(("arbitrary",)),
        name="combine",
    )(dest_flat, ys, gate, x1, mod3, ln_g.reshape(1, d), ln_b.reshape(1, d))


def _ffn_kernel(start_ref, nblk_ref, used_ref, buf_hbm, wg_ref, wu_ref, wd_ref, ys_hbm,
                wgb_ref, wub_ref, wdb_ref, x_buf, o_buf, in_sem, out_sem):
    e = pl.program_id(0)
    nb = nblk_ref[e]
    base = start_ref[e]
    used = used_ref[0]
    n_in = x_buf.shape[0]
    n_out = o_buf.shape[0]
    ahead = n_in - 1

    def rows(g):
        return pl.ds(g * FFN_BLOCK, FFN_BLOCK)

    def in_copy(g):
        slot = lax.rem(g, n_in)
        return pltpu.make_async_copy(buf_hbm.at[rows(g)], x_buf.at[slot], in_sem.at[slot])

    def out_copy(g):
        slot = lax.rem(g, n_out)
        return pltpu.make_async_copy(o_buf.at[slot], ys_hbm.at[rows(g)], out_sem.at[slot])

    @pl.when(e == 0)
    def _():
        for g in range(ahead):
            @pl.when(g < used)
            def _():
                in_copy(g).start()

    @pl.when(nb > 0)
    def _():
        wgb_ref[...] = wg_ref[0].astype(BF16)
        wub_ref[...] = wu_ref[0].astype(BF16)
        wdb_ref[...] = wd_ref[0].astype(BF16)

        def block(j, carry):
            g = base + j
            in_copy(g).wait()

            @pl.when(g + ahead < used)
            def _():
                in_copy(g + ahead).start()

            @pl.when(g >= n_out)
            def _():
                out_copy(g - n_out).wait()

            islot = lax.rem(g, n_in)
            oslot = lax.rem(g, n_out)
            slab = FFN_BLOCK // FFN_SLABS
            xs = [x_buf[islot, s * slab:(s + 1) * slab, :].astype(BF16) for s in range(FFN_SLABS)]
            gates = [_dot(x, wgb_ref[...]) for x in xs]
            ups = [_dot(x, wub_ref[...]) for x in xs]
            hidden = [(_silu(gt) * up).astype(BF16) for gt, up in zip(gates, ups)]
            for s in range(FFN_SLABS):
                o_buf[oslot, s * slab:(s + 1) * slab, :] = _dot(hidden[s], wdb_ref[...])
            out_copy(g).start()
            return carry

        lax.fori_loop(0, nb, block, 0)

    @pl.when(e == pl.num_programs(0) - 1)
    def _():
        for back in range(n_out):
            @pl.when(used - 1 - back >= 0)
            def _():
                out_copy(used - 1 - back).wait()
        n_blocks = ys_hbm.shape[0] // FFN_BLOCK
        o_buf[0] = jnp.zeros(o_buf.shape[1:], o_buf.dtype)

        def clear(b):
            return pltpu.make_async_copy(o_buf.at[0], ys_hbm.at[pl.ds(b * FFN_BLOCK, FFN_BLOCK)], out_sem.at[0])

        lax.fori_loop(used_ref[0], n_blocks, lambda b, carry: (clear(b).start(), carry)[1], 0)
        lax.fori_loop(used_ref[0], n_blocks, lambda b, carry: (clear(b).wait(), carry)[1], 0)


def _ffn(seg_start, seg_blocks, n_used, buf, w_gate, w_up, w_down):
    n_rows, d = buf.shape
    n_experts = w_gate.shape[0]
    grid_spec = pltpu.PrefetchScalarGridSpec(
        num_scalar_prefetch=3,
        grid=(n_experts,),
        in_specs=[pl.BlockSpec(memory_space=pl.ANY),
                  pl.BlockSpec((1, d, D_EXPERT), lambda e, *_: (e, 0, 0)),
                  pl.BlockSpec((1, d, D_EXPERT), lambda e, *_: (e, 0, 0)),
                  pl.BlockSpec((1, D_EXPERT, d), lambda e, *_: (e, 0, 0))],
        out_specs=pl.BlockSpec(memory_space=pl.ANY),
        scratch_shapes=[pltpu.VMEM((d, D_EXPERT), BF16), pltpu.VMEM((d, D_EXPERT), BF16),
                        pltpu.VMEM((D_EXPERT, d), BF16),
                        pltpu.VMEM((FFN_IN_SLOTS, FFN_BLOCK, d), F32), pltpu.VMEM((FFN_OUT_SLOTS, FFN_BLOCK, d), F32),
                        pltpu.SemaphoreType.DMA((FFN_IN_SLOTS,)), pltpu.SemaphoreType.DMA((FFN_OUT_SLOTS,))],
    )
    return pl.pallas_call(
        _ffn_kernel,
        grid_spec=grid_spec,
        out_shape=jax.ShapeDtypeStruct((n_rows, d), F32),
        compiler_params=_cparams(("arbitrary",)),
        name="ffn",
    )(seg_start, seg_blocks, n_used, buf, w_gate, w_up, w_down)


def _rope_tables(seq_len, n_ctx_rows):
    lane = np.arange(LANES)
    within = lane % DDK
    freq = within % 16
    use_col = within >= DDK // 2
    sign = np.where((lane % 32) < 16, -1.0, 1.0).astype(np.float32)
    half = DDK // 2
    inv_freq = (ROPE_THETA ** (-np.arange(0, half, 2, dtype=np.float32) / np.float32(half))).astype(np.float32)
    t = np.arange(seq_len)
    pos = np.where(use_col[None, :], (t % GRID_W)[:, None], (t // GRID_W)[:, None]).astype(np.float32)
    ang = pos * inv_freq[freq][None, :]
    cos = np.concatenate([np.cos(ang), np.ones((n_ctx_rows, LANES), np.float32)], axis=0)
    sin = np.concatenate([np.sin(ang) * sign[None, :], np.zeros((n_ctx_rows, LANES), np.float32)], axis=0)
    return jnp.asarray(cos, F32), jnp.asarray(sin, F32)


def _block_tri(n, block, upper):
    i = np.arange(n)
    same = (i[:, None] // block) == (i[None, :] // block)
    tri = (i[None, :] >= i[:, None]) if upper else (i[None, :] <= i[:, None])
    return jnp.asarray((same & tri).astype(np.float32), dtype=BF16)


def _pack_w_in(w_in):
    pad = jnp.zeros((D, LANES - G_GATES), w_in.dtype)
    off = G_QKV + G_Z
    v_start = G_COLS + 2 * D_QK
    w_all = jnp.concatenate([
        w_in[:, :G_QKV], w_in[:, G_QKV:off],
        w_in[:, off:off + G_GATES], pad,
        w_in[:, off + G_GATES:G_COLS], pad,
        w_in[:, G_COLS:v_start],
    ], axis=1).astype(BF16)
    return w_all, w_in[:, v_start:].T.astype(BF16)


def kernel(x, c, ctx, c_ctx, w_ada, b_ada, w_in, conv_w, gdn_a_log, gdn_dt_bias, gdn_norm_w, diff_lambda,
           diff_norm_w, w_out, ln1_g, ln1_b, w_router_group, b_router_group, w_router_expert, b_router_expert,
           w_expert_gate, w_expert_up, w_expert_down, ln2_g, ln2_b):
    depth = w_ada.shape[0]
    assert depth == 1, "single-layer block: the context stream never feeds a later layer"
    bsz, seq, _ = x.shape
    n_ctx = ctx.shape[1]
    assert seq % ROW_TILE == 0 and n_ctx % ROW_TILE == 0 and seq % GRID_W == 0 and bsz < 8
    n_lat_tiles = seq // ROW_TILE
    alpha = (2.0 * depth) ** 0.25
    lam_init = 0.8 - 0.6 * math.exp(-0.3 * 0)
    i = 0

    cc = jnp.zeros((8, D), F32).at[:bsz].set(c).at[7].set(c_ctx)
    mod = _ada(cc, w_ada[i], b_ada[i])
    mod3 = mod.reshape(8, 1, 6 * D)

    cos_t, sin_t = _rope_tables(seq, n_ctx)
    qkv, z, gt, dq, dk, dvt = _proj(x, ctx, mod3, cos_t, sin_t, *_pack_w_in(w_in[i]))

    conv_w8 = jnp.zeros((8, G_QKV), F32).at[:GCONV].set(conv_w[i])
    alog_v = jnp.zeros((1, LANES), F32).at[0, :G_GATES].set(gdn_a_log[i].reshape(-1))
    dtb_v = jnp.zeros((1, LANES), F32).at[0, :G_GATES].set(gdn_dt_bias[i].reshape(-1))
    gq, gk, gv, beta, gc = _prep(qkv, gt, conv_w8, alog_v, dtb_v,
                                 _block_tri(ROW_TILE, CHUNK, False), _block_tri(ROW_TILE, CHUNK, True), n_lat_tiles)
    uw = _chunk(gk, gv, beta, gc)
    o_f, o_r = _scan(gq, gk, gc, uw, n_lat_tiles)

    y_diff = _attn(dq, dk, dvt, diff_lambda[i], diff_norm_w[i], lam_init, seq, tq=ATTN_Q_TILE)

    w_router = jnp.zeros((D, LANES), F32).at[:, :N_GROUPS].set(w_router_group[i]) \
        .at[:, N_GROUPS:N_GROUPS + N_EXPERTS].set(w_router_expert[i])
    w_hi = w_router.astype(BF16)
    w_router3 = jnp.concatenate([w_hi, (w_router - w_hi.astype(F32)).astype(BF16)], axis=1)
    b_router = jnp.zeros((1, LANES), F32).at[0, :N_GROUPS].set(b_router_group[i]) \
        .at[0, N_GROUPS:N_GROUPS + N_EXPERTS].set(b_router_expert[i])
    x1, h2, logits = _out(o_f, o_r, z, y_diff, x, mod3, gdn_norm_w[i], w_out[i].astype(BF16), ln1_g[i], ln1_b[i],
                          w_router3, b_router, alpha)

    t = bsz * seq
    tri_strict = jnp.asarray(np.tril(np.ones((ROW_TILE, ROW_TILE), np.float32), -1), dtype=BF16)
    ids, gate, counts = _route(logits.reshape(t, LANES), tri_strict)

    cnt = counts[0, :N_EXPERTS].astype(jnp.int32)
    padded = (cnt + FFN_BLOCK - 1) // FFN_BLOCK * FFN_BLOCK
    p_end = jnp.cumsum(padded)
    p_start = p_end - padded
    n_blocks = (2 * t) // FFN_BLOCK + N_EXPERTS
    n_used = (p_end[-1:] // FFN_BLOCK).astype(jnp.int32)
    tail_block = jnp.where(cnt > 0, p_end // FFN_BLOCK - 1, -1).astype(jnp.int32)
    one_hot = ids[0:2, :, None] == jnp.arange(N_EXPERTS, dtype=jnp.int32)
    row_start = jnp.sum(jnp.where(one_hot, p_start, 0), axis=-1)
    dest = (row_start + ids[2:4]).T.reshape(-1).astype(jnp.int32)

    buf = _dispatch(tail_block, n_used, dest, h2.reshape(t, D), n_blocks * FFN_BLOCK)
    ys = _ffn((p_start // FFN_BLOCK).astype(jnp.int32), (padded // FFN_BLOCK).astype(jnp.int32), n_used, buf,
              w_expert_gate[i], w_expert_up[i], w_expert_down[i])
    out = _combine(dest, ys, gate, x1.reshape(t, D), mod3, ln2_g[i], ln2_b[i], alpha, seq)
    return out.reshape(bsz, seq, D)
```

```python
import functools
import math

import numpy as np
import jax
import jax.numpy as jnp
from jax import lax
from jax.experimental import pallas as pl
from jax.experimental.pallas import tpu as pltpu

F32 = jnp.float32
BF16 = jnp.bfloat16
HIGHEST = lax.Precision.HIGHEST

D = 1024
GRID_W = 64
ROPE_THETA = 10000.0
GH = 4
GDK = 128
GDV = 128
GCONV = 5
DH = 4
DDK = 64
DDV = 128
G_QK = GH * GDK
G_QKV = 2 * G_QK + GH * GDV
G_Z = GH * GDV
G_GATES = 2 * GH
G_COLS = G_QKV + G_Z + 2 * G_GATES
D_QK = DH * 2 * DDK
N_GROUPS = 4
E_PER_GROUP = 8
N_EXPERTS = N_GROUPS * E_PER_GROUP
D_EXPERT = 512
LN_EPS = 1e-5
NORM_EPS = 1e-6

LANES = 128
ROW_TILE = 256
CHUNK = 64
SCAN_BATCH = 2
FFN_BLOCK = 256
FFN_SLABS = 2
FFN_IN_SLOTS = 4
FFN_OUT_SLOTS = 3
KEY_TILE = 512
ATTN_Q_TILE = 512
SCORE_LOOKAHEAD = 2
VMEM_LIMIT = 56 * 1024 * 1024

C_QKV = 0
C_Z = C_QKV + G_QKV
C_GB = C_Z + G_Z
C_GA = C_GB + LANES
C_DQ = C_GA + LANES
C_DK = C_DQ + D_QK
C_END = C_DK + D_QK
Q_SCALE = DDK ** -0.5 * math.log2(math.e)


def _cparams(sem):
    return pltpu.CompilerParams(dimension_semantics=sem, vmem_limit_bytes=VMEM_LIMIT)


def _dot(a, b):
    return jnp.dot(a, b, preferred_element_type=F32)


def _dot_nt(a, b):
    return lax.dot_general(a, b, (((1,), (1,)), ((), ())), preferred_element_type=F32)


def _dot_tn(a, b):
    return lax.dot_general(a, b, (((0,), (0,)), ((), ())), preferred_element_type=F32)


def _normalize(x):
    mu = jnp.mean(x, axis=-1, keepdims=True)
    xc = x - mu
    var = jnp.mean(xc * xc, axis=-1, keepdims=True)
    return xc * lax.rsqrt(var + LN_EPS)


def _silu(x):
    return x * jax.nn.sigmoid(x)


def _split3(x):
    hi = x.astype(BF16)
    r = x - hi.astype(F32)
    mid = r.astype(BF16)
    lo = (r - mid.astype(F32)).astype(BF16)
    return hi, mid, lo


def _ada_kernel(c_ref, w_ref, b_ref, o_ref):
    s = _silu(c_ref[...])
    o_ref[...] = jnp.dot(s, w_ref[...], precision=HIGHEST, preferred_element_type=F32) + b_ref[...]


def _ada(cc, w_ada, b_ada):
    n = w_ada.shape[1]
    bn = 1024
    return pl.pallas_call(
        _ada_kernel,
        grid=(n // bn,),
        in_specs=[pl.BlockSpec((8, D), lambda j: (0, 0)),
                  pl.BlockSpec((D, bn), lambda j: (0, j)),
                  pl.BlockSpec((1, bn), lambda j: (0, j))],
        out_specs=pl.BlockSpec((8, bn), lambda j: (0, j)),
        out_shape=jax.ShapeDtypeStruct((8, n), F32),
        compiler_params=_cparams(("arbitrary",)),
        name="ada",
    )(cc, w_ada, b_ada.reshape(1, n))


def _rope(v, cos, sin, low_half):
    fwd = pltpu.roll(v, LANES - 16, axis=1)
    bwd = pltpu.roll(v, 16, axis=1)
    return v * cos + jnp.where(low_half, fwd, bwd) * sin


def _proj_kernel(n_lat_tiles, x_ref, ctx_ref, sh_ref, sc_ref, cos_ref, sin_ref, w_ref, wvt_ref,
                 qkv_ref, z_ref, gt_ref, dq_ref, dk_ref, dvt_ref):
    rows = jnp.where(pl.program_id(1) >= n_lat_tiles, ctx_ref[0], x_ref[0])
    h = (_normalize(rows) * (1.0 + sc_ref[0]) + sh_ref[0]).astype(BF16)
    qkv_ref[0] = _dot(h, w_ref[:, C_QKV:C_Z])
    z_ref[0] = _dot(h, w_ref[:, C_Z:C_GB])
    gt_ref[0] = _dot(h, w_ref[:, C_GB:C_DQ])
    cos = cos_ref[...]
    sin = sin_ref[...]
    lane = lax.broadcasted_iota(jnp.int32, cos.shape, 1)
    low_half = (lane % 32) < 16
    q_all = _dot(h, w_ref[:, C_DQ:C_DK])
    k_all = _dot(h, w_ref[:, C_DK:C_END])
    for j in range(D_QK // LANES):
        slab = slice(j * LANES, (j + 1) * LANES)
        dq_ref[0, :, slab] = (_rope(q_all[:, slab], cos, sin, low_half) * Q_SCALE).astype(BF16)
        dk_ref[0, :, slab] = _rope(k_all[:, slab], cos, sin, low_half).astype(BF16)
    dvt_ref[0] = _dot_nt(wvt_ref[...], h).astype(BF16)


def _proj(x, ctx, mod3, cos_t, sin_t, w_all, wv_t):
    bsz, seq, _ = x.shape
    lc = seq + ctx.shape[1]
    nt = lc // ROW_TILE
    n_lat_tiles = seq // ROW_TILE
    ctx_row = mod3.shape[0] - 1

    def mod_idx(col):
        return lambda b, i: (jnp.where(i >= n_lat_tiles, ctx_row, b), 0, col)

    def row_spec(width):
        return pl.BlockSpec((1, ROW_TILE, width), lambda b, i: (b, i, 0))

    outs = [(G_QKV, F32), (G_Z, F32), (2 * LANES, F32), (D_QK, BF16), (D_QK, BF16)]
    return pl.pallas_call(
        functools.partial(_proj_kernel, n_lat_tiles),
        grid=(bsz, nt),
        in_specs=[pl.BlockSpec((1, ROW_TILE, D), lambda b, i: (b, jnp.minimum(i, n_lat_tiles - 1), 0)),
                  pl.BlockSpec((1, ROW_TILE, D), lambda b, i: (b, jnp.maximum(i - n_lat_tiles, 0), 0)),
                  pl.BlockSpec((1, 1, D), mod_idx(0)),
                  pl.BlockSpec((1, 1, D), mod_idx(1)),
                  pl.BlockSpec((ROW_TILE, LANES), lambda b, i: (i, 0)),
                  pl.BlockSpec((ROW_TILE, LANES), lambda b, i: (i, 0)),
                  pl.BlockSpec((D, C_END), lambda b, i: (0, 0)),
                  pl.BlockSpec((DH * DDV, D), lambda b, i: (0, 0))],
        out_specs=[row_spec(w) for w, _ in outs]
        + [pl.BlockSpec((1, DH * DDV, ROW_TILE), lambda b, i: (b, 0, i))],
        out_shape=[jax.ShapeDtypeStruct((bsz, lc, w), dt) for w, dt in outs]
        + [jax.ShapeDtypeStruct((bsz, DH * DDV, lc), BF16)],
        compiler_params=_cparams(("arbitrary", "arbitrary")),
        name="proj",
    )(x, ctx, mod3, mod3, cos_t, sin_t, w_all, wv_t)


def _prep_kernel(n_lat_tiles, nt, main_ref, prev_ref, next_ref, gt_ref, cw_ref, alog_ref, dtb_ref,
                 lo_ref, up_ref, q_ref, k_ref, v_ref, beta_ref, gc_ref, ext_ref):
    i = pl.program_id(1)
    halo = prev_ref.shape[1]
    has_prev = jnp.logical_and(i != 0, i != n_lat_tiles)
    has_next = jnp.logical_and(i != n_lat_tiles - 1, i != nt - 1)
    ext_ref[0:halo, :] = prev_ref[0] * has_prev.astype(F32)
    ext_ref[halo:halo + ROW_TILE, :] = main_ref[0]
    ext_ref[halo + ROW_TILE:, :] = next_ref[0] * has_next.astype(F32)
    pad = GCONV // 2
    acc = None
    for j in range(GCONV):
        start = halo - pad + j
        term = ext_ref[start:start + ROW_TILE, :] * cw_ref[j:j + 1, :]
        acc = term if acc is None else acc + term
    qkv = _silu(acc)
    for h in range(GH):
        q = qkv[:, h * GDK:(h + 1) * GDK]
        q_ref[0, :, h * GDK:(h + 1) * GDK] = q * (lax.rsqrt(jnp.sum(q * q, axis=-1, keepdims=True) + NORM_EPS)
                                                * (GDK ** -0.5))
        k = qkv[:, G_QK + h * GDK:G_QK + (h + 1) * GDK]
        k_ref[0, :, h * GDK:(h + 1) * GDK] = k * lax.rsqrt(jnp.sum(k * k, axis=-1, keepdims=True) + NORM_EPS)
    v_ref[0] = qkv[:, 2 * G_QK:]
    gt = gt_ref[0]
    beta_ref[0] = jax.nn.sigmoid(gt[:, :LANES])
    a = gt[:, LANES:] + dtb_ref[...]
    softplus = jnp.maximum(a, 0.0) + jnp.log(1.0 + jnp.exp(-jnp.abs(a)))
    g = -jnp.exp(alog_ref[...]) * softplus
    lo = lo_ref[...]
    up = up_ref[...]
    fwd = None
    bwd = None
    for part in _split3(g):
        f = _dot(lo, part)
        r = _dot(up, part)
        fwd = f if fwd is None else fwd + f
        bwd = r if bwd is None else bwd + r
    lane = lax.broadcasted_iota(jnp.int32, g.shape, 1)
    gc_ref[0] = jnp.where(lane < GH, fwd, bwd)


def _prep(qkv, gt, conv_w8, alog_v, dtb_v, tri_lo, tri_up, n_lat_tiles):
    bsz, lc, _ = qkv.shape
    nt = lc // ROW_TILE
    halo = 8
    per = ROW_TILE // halo
    last = lc // halo - 1

    def row_spec(width):
        return pl.BlockSpec((1, ROW_TILE, width), lambda b, i: (b, i, 0))

    def const_spec(shape):
        return pl.BlockSpec(shape, lambda b, i: tuple(0 for _ in shape))

    outs = [(G_QK, F32), (G_QK, F32), (GH * GDV, F32), (LANES, F32), (LANES, F32)]
    return pl.pallas_call(
        functools.partial(_prep_kernel, n_lat_tiles, nt),
        grid=(bsz, nt),
        in_specs=[row_spec(G_QKV),
                  pl.BlockSpec((1, halo, G_QKV), lambda b, i: (b, jnp.maximum(i * per - 1, 0), 0)),
                  pl.BlockSpec((1, halo, G_QKV), lambda b, i: (b, jnp.minimum((i + 1) * per, last), 0)),
                  row_spec(2 * LANES),
                  const_spec((8, G_QKV)), const_spec((1, LANES)), const_spec((1, LANES)),
                  const_spec((ROW_TILE, ROW_TILE)), const_spec((ROW_TILE, ROW_TILE))],
        out_specs=[row_spec(w) for w, _ in outs],
        out_shape=[jax.ShapeDtypeStruct((bsz, lc, w), dt) for w, dt in outs],
        scratch_shapes=[pltpu.VMEM((ROW_TILE + 2 * halo, G_QKV), F32)],
        compiler_params=_cparams(("arbitrary", "arbitrary")),
        name="prep",
    )(qkv, qkv, qkv, gt, conv_w8, alog_v, dtb_v, tri_lo, tri_up)


def _chunk_masks(n, rev):
    row = lax.broadcasted_iota(jnp.int32, (n, n), 0)
    col = lax.broadcasted_iota(jnp.int32, (n, n), 1)
    same = (row // CHUNK) == (col // CHUNK)
    incl = jnp.logical_and(same, (col >= row) if rev else (col <= row))
    strict = jnp.logical_and(same, (col > row) if rev else (col < row))
    return incl, strict


def _decay(gcol, grow, incl):
    return jnp.where(incl, jnp.exp(jnp.minimum(gcol - grow, 0.0)), 0.0)


def _chunk_kernel(k_ref, v_ref, beta_ref, gc_ref, uw_ref):
    n = k_ref.shape[1]
    gc = gc_ref[0]
    gct = gc.T
    beta = beta_ref[0]
    row = lax.broadcasted_iota(jnp.int32, (n, n), 0)
    col = lax.broadcasted_iota(jnp.int32, (n, n), 1)
    eye = (row == col).astype(F32)
    sizes = [2 ** j for j in range(1, int(math.log2(CHUNK)))]
    same = {r: (row // r) == (col // r) for r in sizes + [CHUNK]}
    a_all, t_all, x_all = [], [], []
    for d in range(2):
        incl, strict = _chunk_masks(n, d == 1)
        for h in range(GH):
            c = d * GH + h
            k = k_ref[0, :, h * GDK:(h + 1) * GDK]
            v = v_ref[0, :, h * GDV:(h + 1) * GDV]
            bcol = beta[:, c:c + 1]
            gcol = gc[:, c:c + 1]
            grow = gct[c:c + 1, :]
            kb = k * bcol
            a = jnp.where(strict, _dot_nt(kb.astype(BF16), k.astype(BF16)) * _decay(gcol, grow, incl), 0.0)
            a_all.append(a.astype(BF16))
            t_all.append(eye - jnp.where(same[2], a, 0.0))
            x_all.append(jnp.concatenate([v * bcol, kb * jnp.exp(gcol)], axis=1).astype(BF16))
    n_prob = len(a_all)
    zero = jnp.zeros((n, n), BF16)
    for r in sizes:
        off_mask = jnp.logical_and(same[2 * r], jnp.logical_not(same[r]))
        t_b = [t.astype(BF16) for t in t_all]
        inner = [_dot(jnp.where(off_mask, a_all[c], zero), t_b[c]).astype(BF16) for c in range(n_prob)]
        t_all = [t_all[c] - _dot(t_b[c], inner[c]) for c in range(n_prob)]
    for c in range(n_prob):
        uw_ref[0, c] = _dot(t_all[c].astype(BF16), x_all[c])


def _chunk(k, v, beta, gc):
    bsz, lc, _ = k.shape
    nt = lc // ROW_TILE

    def row_spec(width):
        return pl.BlockSpec((1, ROW_TILE, width), lambda b, i: (b, i, 0))

    return pl.pallas_call(
        _chunk_kernel,
        grid=(bsz, nt),
        in_specs=[row_spec(G_QK), row_spec(GH * GDV), row_spec(LANES), row_spec(LANES)],
        out_specs=pl.BlockSpec((1, 2 * GH, ROW_TILE, GDV + GDK), lambda b, i: (b, 0, i, 0)),
        out_shape=jax.ShapeDtypeStruct((bsz, 2 * GH, lc, GDV + GDK), F32),
        compiler_params=_cparams(("arbitrary", "arbitrary")),
        name="chunk",
    )(k, v, beta, gc)


def _scan_kernel(qf_ref, kf_ref, gf_ref, uwf_ref, qr_ref, kr_ref, gr_ref, uwr_ref,
                 of_ref, or_ref, s_ref):
    @pl.when(pl.program_id(1) == 0)
    def _():
        s_ref[...] = jnp.zeros_like(s_ref)

    n_chunks = ROW_TILE // CHUNK
    n_batch = qf_ref.shape[0]
    row = lax.broadcasted_iota(jnp.int32, (CHUNK, CHUNK), 0)
    col = lax.broadcasted_iota(jnp.int32, (CHUNK, CHUNK), 1)
    dirs = ((qf_ref, kf_ref, gf_ref, uwf_ref, of_ref), (qr_ref, kr_ref, gr_ref, uwr_ref, or_ref))
    chains = [(bb, d, h) for bb in range(n_batch) for d in range(2) for h in range(GH)]
    state = [s_ref[i] for i in range(len(chains))]

    def chunk_rows(step, d):
        ci = n_chunks - 1 - step if d == 1 else step
        return slice(ci * CHUNK, (ci + 1) * CHUNK)

    pre = []
    for step in range(n_chunks):
        gates = {}
        for bb in range(n_batch):
            for d in range(2):
                gc = dirs[d][2][bb, chunk_rows(step, d), :]
                gates[bb, d] = (gc, gc.T)
        per_chain = []
        for bb, d, h in chains:
            q_ref, k_ref, _, uw_ref, _ = dirs[d]
            rev = d == 1
            incl = (col >= row) if rev else (col <= row)
            rows = chunk_rows(step, d)
            gc, gct = gates[bb, d]
            c = d * GH + h
            q = q_ref[bb, rows, h * GDK:(h + 1) * GDK]
            k = k_ref[bb, rows, h * GDK:(h + 1) * GDK]
            gcol = gc[:, c:c + 1]
            grow = gct[c:c + 1, :]
            last = 0 if rev else CHUNK - 1
            g_last = gcol[last:last + 1, :]
            a_qk = (_dot_nt(q.astype(BF16), k.astype(BF16)) * _decay(gcol, grow, incl)).astype(BF16)
            wq = jnp.concatenate([uw_ref[bb, h, rows, GDV:], q * jnp.exp(gcol)], axis=0).astype(BF16)
            k_tail_t = (k * jnp.exp(g_last - gcol)).T.astype(BF16)
            per_chain.append((a_qk, wq, k_tail_t, jnp.exp(g_last)))
        pre.append(per_chain)

    for step in range(n_chunks):
        ws = [_dot(pre[step][i][1], state[i].astype(BF16)) for i in range(len(chains))]
        v_new = [(dirs[d][3][bb, h, chunk_rows(step, d), :GDV] - ws[i][:CHUNK]).astype(BF16)
                 for i, (bb, d, h) in enumerate(chains)]
        for i, (bb, d, h) in enumerate(chains):
            a_qk, _, k_tail_t, decay_last = pre[step][i]
            dirs[d][4][bb, chunk_rows(step, d), h * GDV:(h + 1) * GDV] = ws[i][CHUNK:] + _dot(a_qk, v_new[i])
            state[i] = state[i] * decay_last + _dot(k_tail_t, v_new[i])
    for i in range(len(chains)):
        s_ref[i] = state[i]


def _scan(q, k, gc, uw, n_lat_tiles):
    bsz, lc, _ = q.shape
    nt = lc // ROW_TILE
    n_ctx_tiles = nt - n_lat_tiles

    def fwd(b, i):
        return jnp.where(i < n_ctx_tiles, n_lat_tiles + i, i - n_ctx_tiles)

    def bwd(b, i):
        return jnp.where(i < n_ctx_tiles, nt - 1 - i, n_lat_tiles - 1 - (i - n_ctx_tiles))

    nb = SCAN_BATCH if bsz % SCAN_BATCH == 0 else 1

    def specs(tile):
        return [pl.BlockSpec((nb, ROW_TILE, G_QK), lambda b, i: (b, tile(b, i), 0)),
                pl.BlockSpec((nb, ROW_TILE, G_QK), lambda b, i: (b, tile(b, i), 0)),
                pl.BlockSpec((nb, ROW_TILE, LANES), lambda b, i: (b, tile(b, i), 0))]

    def uw_spec(tile, d):
        return pl.BlockSpec((nb, GH, ROW_TILE, GDV + GDK), lambda b, i: (b, d, tile(b, i), 0))

    def out_spec(tile):
        return pl.BlockSpec((nb, ROW_TILE, GH * GDV), lambda b, i: (b, tile(b, i), 0))

    return pl.pallas_call(
        _scan_kernel,
        grid=(bsz // nb, nt),
        in_specs=specs(fwd) + [uw_spec(fwd, 0)] + specs(bwd) + [uw_spec(bwd, 1)],
        out_specs=[out_spec(fwd), out_spec(bwd)],
        out_shape=[jax.ShapeDtypeStruct((bsz, lc, GH * GDV), F32)] * 2,
        scratch_shapes=[pltpu.VMEM((nb * 2 * GH, GDK, GDV), F32)],
        compiler_params=_cparams(("arbitrary", "arbitrary")),
        name="scan",
    )(q, k, gc, uw, q, k, gc, uw)


def _attn_kernel(lam_init, q_ref, k_ref, vt_ref, lam_ref, nw_ref, o_ref):
    lam = lam_ref[...]
    lam_full = (jnp.exp(jnp.sum(lam[0:1] * lam[1:2], axis=-1, keepdims=True))
                - jnp.exp(jnp.sum(lam[2:3] * lam[3:4], axis=-1, keepdims=True)) + lam_init)
    q = q_ref[0]
    tq = q.shape[0]
    n_keys = k_ref.shape[1]
    lane = lax.broadcasted_iota(jnp.int32, q.shape, 1)
    zero = jnp.zeros_like(q)
    qc = [jnp.where(lane < DDK, q, zero), jnp.where(lane >= DDK, q, zero)]
    m = [jnp.full((1, tq), -jnp.inf, F32) for _ in range(2)]
    l = [jnp.zeros((1, tq), F32) for _ in range(2)]
    acc = [jnp.zeros((DDV, tq), F32) for _ in range(2)]
    n_tiles = n_keys // KEY_TILE
    bounds = [t * KEY_TILE for t in range(n_tiles)] + [n_keys]

    def scores(t):
        k_t = k_ref[0, bounds[t]:bounds[t + 1], :]
        tiles = [_dot_nt(k_t, qc[c]) for c in range(2)]
        return [(s, jnp.max(s, axis=0, keepdims=True)) for s in tiles]

    pending = [scores(t) for t in range(min(SCORE_LOOKAHEAD, n_tiles))]
    for t in range(n_tiles):
        s_cur = pending.pop(0)
        if t + SCORE_LOOKAHEAD < n_tiles:
            pending.append(scores(t + SCORE_LOOKAHEAD))
        width = bounds[t + 1] - bounds[t]
        vt_t = jnp.concatenate([vt_ref[0, :, bounds[t]:bounds[t + 1]], jnp.ones((16, width), BF16)], axis=0)
        for c in range(2):
            s, s_max = s_cur[c]
            m_new = jnp.maximum(m[c], s_max)
            p = jnp.exp2(s - m_new)
            scale = jnp.exp2(m[c] - m_new)
            pv = _dot(vt_t, p.astype(BF16))
            l[c] = scale * l[c] + pv[DDV:DDV + 1]
            acc[c] = scale * acc[c] + pv[:DDV]
            m[c] = m_new
    o = acc[0] / l[0] - lam_full * (acc[1] / l[1])
    y = o * lax.rsqrt(jnp.mean(o * o, axis=0, keepdims=True) + NORM_EPS)
    o_ref[0] = (y.T * nw_ref[...] * (1.0 - lam_init)).astype(o_ref.dtype)


def _attn(dq, dk, dvt, lam, norm_w, lam_init, lq, tq):
    bsz, lc, _ = dq.shape
    return pl.pallas_call(
        functools.partial(_attn_kernel, lam_init),
        grid=(bsz, DH, lq // tq),
        in_specs=[pl.BlockSpec((1, tq, 2 * DDK), lambda b, h, i: (b, i, h)),
                  pl.BlockSpec((1, lc, 2 * DDK), lambda b, h, i: (b, 0, h)),
                  pl.BlockSpec((1, DDV, lc), lambda b, h, i: (b, h, 0)),
                  pl.BlockSpec((4, DDK), lambda b, h, i: (0, 0)),
                  pl.BlockSpec((1, DDV), lambda b, h, i: (0, 0))],
        out_specs=pl.BlockSpec((1, tq, DDV), lambda b, h, i: (b, i, h)),
        out_shape=jax.ShapeDtypeStruct((bsz, lq, DH * DDV), BF16),
        compiler_params=_cparams(("arbitrary", "arbitrary", "arbitrary")),
        name="attn",
    )(dq, dk, dvt, lam, norm_w.reshape(1, DDV))


def _out_kernel(alpha, of_ref, or_ref, z_ref, yd_ref, x_ref, g1_ref, sh2_ref, sc2_ref, gnw_ref,
                wo_ref, lng_ref, lnb_ref, wr_ref, br_ref, x1_ref, h2_ref, lg_ref):
    o = of_ref[0] + or_ref[0]
    z = z_ref[0]
    heads = []
    for h in range(GH):
        oh = o[:, h * GDV:(h + 1) * GDV]
        heads.append((oh * lax.rsqrt(jnp.mean(oh * oh, axis=-1, keepdims=True) + NORM_EPS) * gnw_ref[...]
                      * _silu(z[:, h * GDV:(h + 1) * GDV])).astype(BF16))
    mix = jnp.concatenate(heads + [yd_ref[0]], axis=1)
    x1 = _normalize(alpha * x_ref[0] + g1_ref[0] * _dot(mix, wo_ref[...])) * lng_ref[...] + lnb_ref[...]
    x1_ref[0] = x1
    h2 = _normalize(x1) * (1.0 + sc2_ref[0]) + sh2_ref[0]
    h2_ref[0] = h2
    h_hi = h2.astype(BF16)
    h_lo = (h2 - h_hi.astype(F32)).astype(BF16)
    both = _dot(h_hi, wr_ref[...])
    lg_ref[0] = br_ref[...] + both[:, :LANES] + both[:, LANES:] + _dot(h_lo, wr_ref[:, :LANES])


def _out(o_f, o_r, z, y_diff, xc, mod3, gdn_norm_w, w_out, ln_g, ln_b, w_router3, b_router, alpha):
    bsz, lq, _ = y_diff.shape
    nt = lq // ROW_TILE

    def row_spec(width):
        return pl.BlockSpec((1, ROW_TILE, width), lambda b, i: (b, i, 0))

    def mod_spec(col):
        return pl.BlockSpec((1, 1, D), lambda b, i: (b, 0, col))

    def const_spec(shape):
        return pl.BlockSpec(shape, lambda b, i: tuple(0 for _ in shape))

    return pl.pallas_call(
        functools.partial(_out_kernel, alpha),
        grid=(bsz, nt),
        in_specs=[row_spec(GH * GDV), row_spec(GH * GDV), row_spec(G_Z), row_spec(DH * DDV), row_spec(D),
                  mod_spec(2), mod_spec(3), mod_spec(4),
                  const_spec((1, GDV)), const_spec((G_Z + DH * DDV, D)),
                  const_spec((1, D)), const_spec((1, D)),
                  const_spec((D, 2 * LANES)), const_spec((1, LANES))],
        out_specs=[row_spec(D), row_spec(D), row_spec(LANES)],
        out_shape=[jax.ShapeDtypeStruct((bsz, lq, D), F32), jax.ShapeDtypeStruct((bsz, lq, D), F32),
                   jax.ShapeDtypeStruct((bsz, lq, LANES), F32)],
        compiler_params=_cparams(("arbitrary", "arbitrary")),
        name="out",
    )(o_f, o_r, z, y_diff, xc, mod3, mod3, mod3, gdn_norm_w.reshape(1, GDV), w_out,
      ln_g.reshape(1, D), ln_b.reshape(1, D), w_router3, b_router)


def _route_kernel(lg_ref, tri_ref, ids_ref, gate_ref, cnt_ref, run_ref):
    @pl.when(pl.program_id(0) == 0)
    def _():
        run_ref[...] = jnp.zeros_like(run_ref)

    lgt = lg_ref[...].T[:ROUTE_ROWS]
    n = lgt.shape[1]
    row_i = lax.broadcasted_iota(jnp.int32, lgt.shape, 0)
    row = row_i.astype(F32)
    neg = jnp.float32(-jnp.inf)

    def first_max(mask):
        masked = jnp.where(mask, lgt, neg)
        m = jnp.max(masked, axis=0, keepdims=True)
        idx = jnp.min(jnp.where(jnp.logical_and(mask, masked == m), row, float(LANES)), axis=0, keepdims=True)
        return m, idx

    is_group = row_i < N_GROUPS
    m_g, grp = first_max(is_group)
    p_g = 1.0 / jnp.sum(jnp.where(is_group, jnp.exp(lgt - m_g), 0.0), axis=0, keepdims=True)
    e_row = row - float(N_GROUPS)
    lo_e = grp * float(E_PER_GROUP)
    in_group = jnp.logical_and(e_row >= lo_e, e_row < lo_e + float(E_PER_GROUP))
    l0, i0 = first_max(in_group)
    l1, i1 = first_max(jnp.logical_and(in_group, row != i0))
    r = jnp.exp(l1 - l0)
    gate0 = p_g / (1.0 + r)
    gate1 = p_g * r / (1.0 + r)
    e0 = i0 - float(N_GROUPS)
    e1 = i1 - float(N_GROUPS)
    oh0 = (e_row == e0).astype(F32)
    oh1 = (e_row == e1).astype(F32)
    both = oh0 + oh1
    before = _dot(both.astype(BF16), tri_ref[...]) + run_ref[...]
    rank0 = jnp.sum(oh0 * before, axis=0, keepdims=True)
    rank1 = jnp.sum(oh1 * before, axis=0, keepdims=True)
    run_ref[...] = run_ref[...] + jnp.sum(both, axis=1, keepdims=True)
    cnt_ref[...] = jnp.broadcast_to(run_ref[...], cnt_ref.shape)
    ids_ref[...] = jnp.concatenate([e0, e1, rank0, rank1, jnp.zeros((4, n), F32)], axis=0).astype(jnp.int32)
    gate_ref[...] = jnp.concatenate([gate0, gate1, jnp.zeros((LANES - 2, n), F32)], axis=0).T


def _route(logits, tri_strict):
    t = logits.shape[0]
    tile = tri_strict.shape[0]
    return pl.pallas_call(
        _route_kernel,
        grid=(t // tile,),
        in_specs=[pl.BlockSpec((tile, LANES), lambda i: (i, 0)),
                  pl.BlockSpec((tile, tile), lambda i: (0, 0))],
        out_specs=[pl.BlockSpec((8, tile), lambda i: (0, i)),
                   pl.BlockSpec((tile, LANES), lambda i: (i, 0)),
                   pl.BlockSpec((ROUTE_ROWS, LANES), lambda i: (0, 0))],
        out_shape=[jax.ShapeDtypeStruct((8, t), jnp.int32), jax.ShapeDtypeStruct((t, LANES), F32),
                   jax.ShapeDtypeStruct((ROUTE_ROWS, LANES), F32)],
        scratch_shapes=[pltpu.VMEM((ROUTE_ROWS, 1), F32)],
        compiler_params=_cparams(("arbitrary",)),
        name="route",
    )(logits, tri_strict)


ROUTE_ROWS = 40
DMA_TOKENS = 512
COMBINE_LAG = 2


def _dispatch_kernel(tail_ref, used_ref, dest_ref, h_ref, buf_hbm, zero_ref, sem, zsem):
    @pl.when(pl.program_id(0) == 0)
    def _():
        zero_ref[...] = jnp.zeros_like(zero_ref)
        n_blocks = buf_hbm.shape[0] // FFN_BLOCK

        def block_copy(b):
            return pltpu.make_async_copy(zero_ref, buf_hbm.at[pl.ds(b * FFN_BLOCK, FFN_BLOCK)], zsem)

        for e in range(N_EXPERTS):
            @pl.when(tail_ref[e] >= 0)
            def _():
                block_copy(tail_ref[e]).start()
        lax.fori_loop(used_ref[0], n_blocks, lambda b, carry: (block_copy(b).start(), carry)[1], 0)
        for e in range(N_EXPERTS):
            @pl.when(tail_ref[e] >= 0)
            def _():
                block_copy(tail_ref[e]).wait()
        lax.fori_loop(used_ref[0], n_blocks, lambda b, carry: (block_copy(b).wait(), carry)[1], 0)

    def start(grp, carry):
        for u in range(8):
            for j in range(2):
                dst = dest_ref[16 * grp + 2 * u + j]
                pltpu.make_async_copy(h_ref.at[grp, pl.ds(u, 1)], buf_hbm.at[pl.ds(dst, 1)], sem).start()
        return carry

    lax.fori_loop(0, DMA_TOKENS // 8, start, 0)
    for _ in range(2):
        pltpu.make_async_copy(buf_hbm.at[pl.ds(0, DMA_TOKENS)], buf_hbm.at[pl.ds(0, DMA_TOKENS)], sem).wait()


def _dispatch(tail_block, n_used, dest_flat, h2, n_rows):
    t, d = h2.shape
    grid_spec = pltpu.PrefetchScalarGridSpec(
        num_scalar_prefetch=2,
        grid=(t // DMA_TOKENS,),
        in_specs=[pl.BlockSpec((2 * DMA_TOKENS,), lambda i, tail, used: (i,), memory_space=pltpu.SMEM),
                  pl.BlockSpec((DMA_TOKENS // 8, 8, d), lambda i, tail, used: (i, 0, 0))],
        out_specs=pl.BlockSpec(memory_space=pl.ANY),
        scratch_shapes=[pltpu.VMEM((FFN_BLOCK, d), h2.dtype), pltpu.SemaphoreType.DMA(()),
                        pltpu.SemaphoreType.DMA(())],
    )
    return pl.pallas_call(
        _dispatch_kernel,
        grid_spec=grid_spec,
        out_shape=jax.ShapeDtypeStruct((n_rows, d), h2.dtype),
        compiler_params=_cparams(("arbitrary",)),
        name="dispatch",
    )(tail_block, n_used, dest_flat, h2.reshape(t // 8, 8, d))


def _combine_kernel(alpha, dest_ref, ys_hbm, gate_ref, x1_ref, g2_ref, lng_ref, lnb_ref, o_ref, rows_ref, sem):
    i = pl.program_id(0)
    n_tiles = pl.num_programs(0) - COMBINE_LAG
    n = x1_ref.shape[0]
    n_slots = rows_ref.shape[0]
    slot_in = lax.rem(i, n_slots)
    slot_out = lax.rem(i + n_slots - COMBINE_LAG, n_slots)

    def start_group(grp):
        for u in range(8):
            for j in range(2):
                src = dest_ref[16 * grp + 2 * u + j]
                pltpu.make_async_copy(ys_hbm.at[pl.ds(src, 1)], rows_ref.at[slot_in, j, grp, pl.ds(u, 1)],
                                      sem.at[slot_in]).start()

    def finish():
        gate = gate_ref[...]
        r0 = rows_ref[slot_out, 0].reshape(n, -1)
        r1 = rows_ref[slot_out, 1].reshape(n, -1)
        y = gate[:, 0:1] * r0 + gate[:, 1:2] * r1
        o_ref[...] = _normalize(alpha * x1_ref[...] + g2_ref[0] * y) * lng_ref[...] + lnb_ref[...]

    @pl.when(i >= COMBINE_LAG)
    def _():
        for j in range(2):
            pltpu.make_async_copy(ys_hbm.at[pl.ds(0, n)], ys_hbm.at[pl.ds(0, n)], sem.at[slot_out]).wait()

    @pl.when(i < COMBINE_LAG)
    def _():
        lax.fori_loop(0, n // 8, lambda grp, carry: (start_group(grp), carry)[1], 0)

    @pl.when(jnp.logical_and(i >= COMBINE_LAG, i < n_tiles))
    def _():
        for grp in range(n // 8):
            start_group(grp)
        finish()

    @pl.when(i >= n_tiles)
    def _():
        finish()


def _combine(dest_flat, ys, gate, x1, mod3, ln_g, ln_b, alpha, rows_per_batch):
    t, d = x1.shape
    per_batch = rows_per_batch // DMA_TOKENS
    n_tiles = t // DMA_TOKENS

    def done(i):
        return jnp.maximum(i - COMBINE_LAG, 0)

    return pl.pallas_call(
        functools.partial(_combine_kernel, alpha),
        grid=(n_tiles + COMBINE_LAG,),
        in_specs=[pl.BlockSpec((2 * DMA_TOKENS,), lambda i: (jnp.minimum(i, n_tiles - 1),), memory_space=pltpu.SMEM),
                  pl.BlockSpec(memory_space=pl.ANY),
                  pl.BlockSpec((DMA_TOKENS, LANES), lambda i: (done(i), 0)),
                  pl.BlockSpec((DMA_TOKENS, d), lambda i: (done(i), 0)),
                  pl.BlockSpec((1, 1, d), lambda i: (done(i) // per_batch, 0, 5)),
                  pl.BlockSpec((1, d), lambda i: (0, 0)),
                  pl.BlockSpec((1, d), lambda i: (0, 0))],
        out_specs=pl.BlockSpec((DMA_TOKENS, d), lambda i: (done(i), 0)),
        out_shape=jax.ShapeDtypeStruct((t, d), F32),
        scratch_shapes=[pltpu.VMEM((COMBINE_LAG + 1, 2, DMA_TOKENS // 8, 8, d), F32),
                        pltpu.SemaphoreType.DMA((COMBINE_LAG + 1,))],
        compiler_params=_cparams(("arbitrary",)),
        name="combine",
    )(dest_flat, ys, gate, x1, mod3, ln_g.reshape(1, d), ln_b.reshape(1, d))


def _ffn_kernel(start_ref, nblk_ref, used_ref, buf_hbm, wg_ref, wu_ref, wd_ref, ys_hbm,
                wgb_ref, wub_ref, wdb_ref, x_buf, o_buf, in_sem, out_sem):
    e = pl.program_id(0)
    nb = nblk_ref[e]
    base = start_ref[e]
    used = used_ref[0]
    n_in = x_buf.shape[0]
    n_out = o_buf.shape[0]
    ahead = n_in - 1

    def rows(g):
        return pl.ds(g * FFN_BLOCK, FFN_BLOCK)

    def in_copy(g):
        slot = lax.rem(g, n_in)
        return pltpu.make_async_copy(buf_hbm.at[rows(g)], x_buf.at[slot], in_sem.at[slot])

    def out_copy(g):
        slot = lax.rem(g, n_out)
        return pltpu.make_async_copy(o_buf.at[slot], ys_hbm.at[rows(g)], out_sem.at[slot])

    @pl.when(e == 0)
    def _():
        for g in range(ahead):
            @pl.when(g < used)
            def _():
                in_copy(g).start()

    @pl.when(nb > 0)
    def _():
        wgb_ref[...] = wg_ref[0].astype(BF16)
        wub_ref[...] = wu_ref[0].astype(BF16)
        wdb_ref[...] = wd_ref[0].astype(BF16)

        def block(j, carry):
            g = base + j
            in_copy(g).wait()

            @pl.when(g + ahead < used)
            def _():
                in_copy(g + ahead).start()

            @pl.when(g >= n_out)
            def _():
                out_copy(g - n_out).wait()

            islot = lax.rem(g, n_in)
            oslot = lax.rem(g, n_out)
            slab = FFN_BLOCK // FFN_SLABS
            xs = [x_buf[islot, s * slab:(s + 1) * slab, :].astype(BF16) for s in range(FFN_SLABS)]
            gates = [_dot(x, wgb_ref[...]) for x in xs]
            ups = [_dot(x, wub_ref[...]) for x in xs]
            hidden = [(_silu(gt) * up).astype(BF16) for gt, up in zip(gates, ups)]
            for s in range(FFN_SLABS):
                o_buf[oslot, s * slab:(s + 1) * slab, :] = _dot(hidden[s], wdb_ref[...])
            out_copy(g).start()
            return carry

        lax.fori_loop(0, nb, block, 0)

    @pl.when(e == pl.num_programs(0) - 1)
    def _():
        for back in range(n_out):
            @pl.when(used - 1 - back >= 0)
            def _():
                out_copy(used - 1 - back).wait()
        n_blocks = ys_hbm.shape[0] // FFN_BLOCK
        o_buf[0] = jnp.zeros(o_buf.shape[1:], o_buf.dtype)

        def clear(b):
            return pltpu.make_async_copy(o_buf.at[0], ys_hbm.at[pl.ds(b * FFN_BLOCK, FFN_BLOCK)], out_sem.at[0])

        lax.fori_loop(used_ref[0], n_blocks, lambda b, carry: (clear(b).start(), carry)[1], 0)
        lax.fori_loop(used_ref[0], n_blocks, lambda b, carry: (clear(b).wait(), carry)[1], 0)


def _ffn(seg_start, seg_blocks, n_used, buf, w_gate, w_up, w_down):
    n_rows, d = buf.shape
    n_experts = w_gate.shape[0]
    grid_spec = pltpu.PrefetchScalarGridSpec(
        num_scalar_prefetch=3,
        grid=(n_experts,),
        in_specs=[pl.BlockSpec(memory_space=pl.ANY),
                  pl.BlockSpec((1, d, D_EXPERT), lambda e, *_: (e, 0, 0)),
                  pl.BlockSpec((1, d, D_EXPERT), lambda e, *_: (e, 0, 0)),
                  pl.BlockSpec((1, D_EXPERT, d), lambda e, *_: (e, 0, 0))],
        out_specs=pl.BlockSpec(memory_space=pl.ANY),
        scratch_shapes=[pltpu.VMEM((d, D_EXPERT), BF16), pltpu.VMEM((d, D_EXPERT), BF16),
                        pltpu.VMEM((D_EXPERT, d), BF16),
                        pltpu.VMEM((FFN_IN_SLOTS, FFN_BLOCK, d), F32), pltpu.VMEM((FFN_OUT_SLOTS, FFN_BLOCK, d), F32),
                        pltpu.SemaphoreType.DMA((FFN_IN_SLOTS,)), pltpu.SemaphoreType.DMA((FFN_OUT_SLOTS,))],
    )
    return pl.pallas_call(
        _ffn_kernel,
        grid_spec=grid_spec,
        out_shape=jax.ShapeDtypeStruct((n_rows, d), F32),
        compiler_params=_cparams(("arbitrary",)),
        name="ffn",
    )(seg_start, seg_blocks, n_used, buf, w_gate, w_up, w_down)


def _rope_tables(seq_len, n_ctx_rows):
    lane = np.arange(LANES)
    within = lane % DDK
    freq = within % 16
    use_col = within >= DDK // 2
    sign = np.where((lane % 32) < 16, -1.0, 1.0).astype(np.float32)
    half = DDK // 2
    inv_freq = (ROPE_THETA ** (-np.arange(0, half, 2, dtype=np.float32) / np.float32(half))).astype(np.float32)
    t = np.arange(seq_len)
    pos = np.where(use_col[None, :], (t % GRID_W)[:, None], (t // GRID_W)[:, None]).astype(np.float32)
    ang = pos * inv_freq[freq][None, :]
    cos = np.concatenate([np.cos(ang), np.ones((n_ctx_rows, LANES), np.float32)], axis=0)
    sin = np.concatenate([np.sin(ang) * sign[None, :], np.zeros((n_ctx_rows, LANES), np.float32)], axis=0)
    return jnp.asarray(cos, F32), jnp.asarray(sin, F32)


def _block_tri(n, block, upper):
    i = np.arange(n)
    same = (i[:, None] // block) == (i[None, :] // block)
    tri = (i[None, :] >= i[:, None]) if upper else (i[None, :] <= i[:, None])
    return jnp.asarray((same & tri).astype(np.float32), dtype=BF16)


def _pack_w_in(w_in):
    pad = jnp.zeros((D, LANES - G_GATES), w_in.dtype)
    off = G_QKV + G_Z
    v_start = G_COLS + 2 * D_QK
    w_all = jnp.concatenate([
        w_in[:, :G_QKV], w_in[:, G_QKV:off],
        w_in[:, off:off + G_GATES], pad,
        w_in[:, off + G_GATES:G_COLS], pad,
        w_in[:, G_COLS:v_start],
    ], axis=1).astype(BF16)
    return w_all, w_in[:, v_start:].T.astype(BF16)


def kernel(x, c, ctx, c_ctx, w_ada, b_ada, w_in, conv_w, gdn_a_log, gdn_dt_bias, gdn_norm_w, diff_lambda,
           diff_norm_w, w_out, ln1_g, ln1_b, w_router_group, b_router_group, w_router_expert, b_router_expert,
           w_expert_gate, w_expert_up, w_expert_down, ln2_g, ln2_b):
    depth = w_ada.shape[0]
    assert depth == 1, "single-layer block: the context stream never feeds a later layer"
    bsz, seq, _ = x.shape
    n_ctx = ctx.shape[1]
    assert seq % ROW_TILE == 0 and n_ctx % ROW_TILE == 0 and seq % GRID_W == 0 and bsz < 8
    n_lat_tiles = seq // ROW_TILE
    alpha = (2.0 * depth) ** 0.25
    lam_init = 0.8 - 0.6 * math.exp(-0.3 * 0)
    i = 0

    cc = jnp.zeros((8, D), F32).at[:bsz].set(c).at[7].set(c_ctx)
    mod = _ada(cc, w_ada[i], b_ada[i])
    mod3 = mod.reshape(8, 1, 6 * D)

    cos_t, sin_t = _rope_tables(seq, n_ctx)
    qkv, z, gt, dq, dk, dvt = _proj(x, ctx, mod3, cos_t, sin_t, *_pack_w_in(w_in[i]))

    conv_w8 = jnp.zeros((8, G_QKV), F32).at[:GCONV].set(conv_w[i])
    alog_v = jnp.zeros((1, LANES), F32).at[0, :G_GATES].set(gdn_a_log[i].reshape(-1))
    dtb_v = jnp.zeros((1, LANES), F32).at[0, :G_GATES].set(gdn_dt_bias[i].reshape(-1))
    gq, gk, gv, beta, gc = _prep(qkv, gt, conv_w8, alog_v, dtb_v,
                                 _block_tri(ROW_TILE, CHUNK, False), _block_tri(ROW_TILE, CHUNK, True), n_lat_tiles)
    uw = _chunk(gk, gv, beta, gc)
    o_f, o_r = _scan(gq, gk, gc, uw, n_lat_tiles)

    y_diff = _attn(dq, dk, dvt, diff_lambda[i], diff_norm_w[i], lam_init, seq, tq=ATTN_Q_TILE)

    w_router = jnp.zeros((D, LANES), F32).at[:, :N_GROUPS].set(w_router_group[i]) \
        .at[:, N_GROUPS:N_GROUPS + N_EXPERTS].set(w_router_expert[i])
    w_hi = w_router.astype(BF16)
    w_router3 = jnp.concatenate([w_hi, (w_router - w_hi.astype(F32)).astype(BF16)], axis=1)
    b_router = jnp.zeros((1, LANES), F32).at[0, :N_GROUPS].set(b_router_group[i]) \
        .at[0, N_GROUPS:N_GROUPS + N_EXPERTS].set(b_router_expert[i])
    x1, h2, logits = _out(o_f, o_r, z, y_diff, x, mod3, gdn_norm_w[i], w_out[i].astype(BF16), ln1_g[i], ln1_b[i],
                          w_router3, b_router, alpha)

    t = bsz * seq
    tri_strict = jnp.asarray(np.triu(np.ones((ROW_TILE, ROW_TILE), np.float32), 1), dtype=BF16)
    ids, gate, counts = _route(logits.reshape(t, LANES), tri_strict)

    cnt = counts[N_GROUPS:N_GROUPS + N_EXPERTS, 0].astype(jnp.int32)
    padded = (cnt + FFN_BLOCK - 1) // FFN_BLOCK * FFN_BLOCK
    p_end = jnp.cumsum(padded)
    p_start = p_end - padded
    n_blocks = (2 * t) // FFN_BLOCK + N_EXPERTS
    n_used = (p_end[-1:] // FFN_BLOCK).astype(jnp.int32)
    tail_block = jnp.where(cnt > 0, p_end // FFN_BLOCK - 1, -1).astype(jnp.int32)
    one_hot = ids[0:2, :, None] == jnp.arange(N_EXPERTS, dtype=jnp.int32)
    row_start = jnp.sum(jnp.where(one_hot, p_start, 0), axis=-1)
    dest = (row_start + ids[2:4]).T.reshape(-1).astype(jnp.int32)

    buf = _dispatch(tail_block, n_used, dest, h2.reshape(t, D), n_blocks * FFN_BLOCK)
    ys = _ffn((p_start // FFN_BLOCK).astype(jnp.int32), (padded // FFN_BLOCK).astype(jnp.int32), n_used, buf,
              w_expert_gate[i], w_expert_up[i], w_expert_down[i])
    out = _combine(dest, ys, gate, x1.reshape(t, D), mod3, ln2_g[i], ln2_b[i], alpha, seq)
    return out.reshape(bsz, seq, D)
```

```python
import functools
import math

import numpy as np
import jax
import jax.numpy as jnp
from jax import lax
from jax.experimental import pallas as pl
from jax.experimental.pallas import tpu as pltpu

F32 = jnp.float32
BF16 = jnp.bfloat16
HIGHEST = lax.Precision.HIGHEST

D = 1024
GRID_W = 64
ROPE_THETA = 10000.0
GH = 4
GDK = 128
GDV = 128
GCONV = 5
DH = 4
DDK = 64
DDV = 128
G_QK = GH * GDK
G_QKV = 2 * G_QK + GH * GDV
G_Z = GH * GDV
G_GATES = 2 * GH
G_COLS = G_QKV + G_Z + 2 * G_GATES
D_QK = DH * 2 * DDK
N_GROUPS = 4
E_PER_GROUP = 8
N_EXPERTS = N_GROUPS * E_PER_GROUP
D_EXPERT = 512
LN_EPS = 1e-5
NORM_EPS = 1e-6

LANES = 128
ROW_TILE = 256
CHUNK = 64
OUT_TILE = 512
SCAN_BATCH = 2
FFN_BLOCK = 256
FFN_SLABS = 2
FFN_IN_SLOTS = 4
FFN_OUT_SLOTS = 3
KEY_TILE = 512
ATTN_Q_TILE = 1024
SCORE_LOOKAHEAD = 2
VMEM_LIMIT = 56 * 1024 * 1024

C_QKV = 0
C_Z = C_QKV + G_QKV
C_GB = C_Z + G_Z
C_GA = C_GB + LANES
C_DQ = C_GA + LANES
C_DK = C_DQ + D_QK
C_END = C_DK + D_QK
Q_SCALE = DDK ** -0.5 * math.log2(math.e)


def _cparams(sem):
    return pltpu.CompilerParams(dimension_semantics=sem, vmem_limit_bytes=VMEM_LIMIT)


def _dot(a, b):
    return jnp.dot(a, b, preferred_element_type=F32)


def _dot_nt(a, b):
    return lax.dot_general(a, b, (((1,), (1,)), ((), ())), preferred_element_type=F32)


def _dot_tn(a, b):
    return lax.dot_general(a, b, (((0,), (0,)), ((), ())), preferred_element_type=F32)


def _normalize(x):
    mu = jnp.mean(x, axis=-1, keepdims=True)
    xc = x - mu
    var = jnp.mean(xc * xc, axis=-1, keepdims=True)
    return xc * lax.rsqrt(var + LN_EPS)


def _silu(x):
    return x * jax.nn.sigmoid(x)


def _split3(x):
    hi = x.astype(BF16)
    r = x - hi.astype(F32)
    mid = r.astype(BF16)
    lo = (r - mid.astype(F32)).astype(BF16)
    return hi, mid, lo


def _ada_kernel(c_ref, w_ref, b_ref, o_ref):
    s = _silu(c_ref[...])
    o_ref[...] = jnp.dot(s, w_ref[...], precision=HIGHEST, preferred_element_type=F32) + b_ref[...]


def _ada(cc, w_ada, b_ada):
    n = w_ada.shape[1]
    bn = 1024
    return pl.pallas_call(
        _ada_kernel,
        grid=(n // bn,),
        in_specs=[pl.BlockSpec((8, D), lambda j: (0, 0)),
                  pl.BlockSpec((D, bn), lambda j: (0, j)),
                  pl.BlockSpec((1, bn), lambda j: (0, j))],
        out_specs=pl.BlockSpec((8, bn), lambda j: (0, j)),
        out_shape=jax.ShapeDtypeStruct((8, n), F32),
        compiler_params=_cparams(("arbitrary",)),
        name="ada",
    )(cc, w_ada, b_ada.reshape(1, n))


def _rope(v, cos, sin, low_half):
    fwd = pltpu.roll(v, LANES - 16, axis=1)
    bwd = pltpu.roll(v, 16, axis=1)
    return v * cos + jnp.where(low_half, fwd, bwd) * sin


def _proj_kernel(n_lat_tiles, x_ref, ctx_ref, sh_ref, sc_ref, cos_ref, sin_ref, w_ref, wvt_ref,
                 qkv_ref, z_ref, gt_ref, dq_ref, dk_ref, dvt_ref):
    rows = jnp.where(pl.program_id(1) >= n_lat_tiles, ctx_ref[0], x_ref[0])
    h = (_normalize(rows) * (1.0 + sc_ref[0]) + sh_ref[0]).astype(BF16)
    qkv_ref[0] = _dot(h, w_ref[:, C_QKV:C_Z])
    z_ref[0] = _dot(h, w_ref[:, C_Z:C_GB])
    gt_ref[0] = _dot(h, w_ref[:, C_GB:C_DQ])
    cos = cos_ref[...]
    sin = sin_ref[...]
    lane = lax.broadcasted_iota(jnp.int32, cos.shape, 1)
    low_half = (lane % 32) < 16
    q_all = _dot(h, w_ref[:, C_DQ:C_DK])
    k_all = _dot(h, w_ref[:, C_DK:C_END])
    for j in range(D_QK // LANES):
        slab = slice(j * LANES, (j + 1) * LANES)
        dq_ref[0, :, slab] = (_rope(q_all[:, slab], cos, sin, low_half) * Q_SCALE).astype(BF16)
        dk_ref[0, :, slab] = _rope(k_all[:, slab], cos, sin, low_half).astype(BF16)
    dvt_ref[0] = _dot_nt(wvt_ref[...], h).astype(BF16)


def _proj(x, ctx, mod3, cos_t, sin_t, w_all, wv_t):
    bsz, seq, _ = x.shape
    lc = seq + ctx.shape[1]
    nt = lc // ROW_TILE
    n_lat_tiles = seq // ROW_TILE
    ctx_row = mod3.shape[0] - 1

    def mod_idx(col):
        return lambda b, i: (jnp.where(i >= n_lat_tiles, ctx_row, b), 0, col)

    def row_spec(width):
        return pl.BlockSpec((1, ROW_TILE, width), lambda b, i: (b, i, 0))

    outs = [(G_QKV, F32), (G_Z, F32), (2 * LANES, F32), (D_QK, BF16), (D_QK, BF16)]
    return pl.pallas_call(
        functools.partial(_proj_kernel, n_lat_tiles),
        grid=(bsz, nt),
        in_specs=[pl.BlockSpec((1, ROW_TILE, D), lambda b, i: (b, jnp.minimum(i, n_lat_tiles - 1), 0)),
                  pl.BlockSpec((1, ROW_TILE, D), lambda b, i: (b, jnp.maximum(i - n_lat_tiles, 0), 0)),
                  pl.BlockSpec((1, 1, D), mod_idx(0)),
                  pl.BlockSpec((1, 1, D), mod_idx(1)),
                  pl.BlockSpec((ROW_TILE, LANES), lambda b, i: (i, 0)),
                  pl.BlockSpec((ROW_TILE, LANES), lambda b, i: (i, 0)),
                  pl.BlockSpec((D, C_END), lambda b, i: (0, 0)),
                  pl.BlockSpec((DH * DDV, D), lambda b, i: (0, 0))],
        out_specs=[row_spec(w) for w, _ in outs]
        + [pl.BlockSpec((1, DH * DDV, ROW_TILE), lambda b, i: (b, 0, i))],
        out_shape=[jax.ShapeDtypeStruct((bsz, lc, w), dt) for w, dt in outs]
        + [jax.ShapeDtypeStruct((bsz, DH * DDV, lc), BF16)],
        compiler_params=_cparams(("arbitrary", "arbitrary")),
        name="proj",
    )(x, ctx, mod3, mod3, cos_t, sin_t, w_all, wv_t)


def _prep_kernel(n_lat_tiles, nt, main_ref, prev_ref, next_ref, gt_ref, cw_ref, alog_ref, dtb_ref,
                 lo_ref, up_ref, q_ref, k_ref, v_ref, beta_ref, gc_ref, ext_ref):
    i = pl.program_id(1)
    halo = prev_ref.shape[1]
    has_prev = jnp.logical_and(i != 0, i != n_lat_tiles)
    has_next = jnp.logical_and(i != n_lat_tiles - 1, i != nt - 1)
    ext_ref[0:halo, :] = prev_ref[0] * has_prev.astype(F32)
    ext_ref[halo:halo + ROW_TILE, :] = main_ref[0]
    ext_ref[halo + ROW_TILE:, :] = next_ref[0] * has_next.astype(F32)
    pad = GCONV // 2
    acc = None
    for j in range(GCONV):
        start = halo - pad + j
        term = ext_ref[start:start + ROW_TILE, :] * cw_ref[j:j + 1, :]
        acc = term if acc is None else acc + term
    qkv = _silu(acc)
    for h in range(GH):
        q = qkv[:, h * GDK:(h + 1) * GDK]
        q_ref[0, :, h * GDK:(h + 1) * GDK] = q * (lax.rsqrt(jnp.sum(q * q, axis=-1, keepdims=True) + NORM_EPS)
                                                * (GDK ** -0.5))
        k = qkv[:, G_QK + h * GDK:G_QK + (h + 1) * GDK]
        k_ref[0, :, h * GDK:(h + 1) * GDK] = k * lax.rsqrt(jnp.sum(k * k, axis=-1, keepdims=True) + NORM_EPS)
    v_ref[0] = qkv[:, 2 * G_QK:]
    gt = gt_ref[0]
    beta_ref[0] = jax.nn.sigmoid(gt[:, :LANES])
    a = gt[:, LANES:] + dtb_ref[...]
    softplus = jnp.maximum(a, 0.0) + jnp.log(1.0 + jnp.exp(-jnp.abs(a)))
    g = -jnp.exp(alog_ref[...]) * softplus
    lo = lo_ref[...]
    up = up_ref[...]
    fwd = None
    bwd = None
    for part in _split3(g):
        f = _dot(lo, part)
        r = _dot(up, part)
        fwd = f if fwd is None else fwd + f
        bwd = r if bwd is None else bwd + r
    lane = lax.broadcasted_iota(jnp.int32, g.shape, 1)
    gc_ref[0] = jnp.where(lane < GH, fwd, bwd)


def _prep(qkv, gt, conv_w8, alog_v, dtb_v, tri_lo, tri_up, n_lat_tiles):
    bsz, lc, _ = qkv.shape
    nt = lc // ROW_TILE
    halo = 8
    per = ROW_TILE // halo
    last = lc // halo - 1

    def row_spec(width):
        return pl.BlockSpec((1, ROW_TILE, width), lambda b, i: (b, i, 0))

    def const_spec(shape):
        return pl.BlockSpec(shape, lambda b, i: tuple(0 for _ in shape))

    outs = [(G_QK, F32), (G_QK, F32), (GH * GDV, F32), (LANES, F32), (LANES, F32)]
    return pl.pallas_call(
        functools.partial(_prep_kernel, n_lat_tiles, nt),
        grid=(bsz, nt),
        in_specs=[row_spec(G_QKV),
                  pl.BlockSpec((1, halo, G_QKV), lambda b, i: (b, jnp.maximum(i * per - 1, 0), 0)),
                  pl.BlockSpec((1, halo, G_QKV), lambda b, i: (b, jnp.minimum((i + 1) * per, last), 0)),
                  row_spec(2 * LANES),
                  const_spec((8, G_QKV)), const_spec((1, LANES)), const_spec((1, LANES)),
                  const_spec((ROW_TILE, ROW_TILE)), const_spec((ROW_TILE, ROW_TILE))],
        out_specs=[row_spec(w) for w, _ in outs],
        out_shape=[jax.ShapeDtypeStruct((bsz, lc, w), dt) for w, dt in outs],
        scratch_shapes=[pltpu.VMEM((ROW_TILE + 2 * halo, G_QKV), F32)],
        compiler_params=_cparams(("arbitrary", "arbitrary")),
        name="prep",
    )(qkv, qkv, qkv, gt, conv_w8, alog_v, dtb_v, tri_lo, tri_up)


def _chunk_masks(n, rev):
    row = lax.broadcasted_iota(jnp.int32, (n, n), 0)
    col = lax.broadcasted_iota(jnp.int32, (n, n), 1)
    same = (row // CHUNK) == (col // CHUNK)
    incl = jnp.logical_and(same, (col >= row) if rev else (col <= row))
    strict = jnp.logical_and(same, (col > row) if rev else (col < row))
    return incl, strict


def _decay(gcol, grow, incl):
    return jnp.where(incl, jnp.exp(jnp.minimum(gcol - grow, 0.0)), 0.0)


def _chunk_kernel(k_ref, v_ref, beta_ref, gc_ref, uw_ref):
    n = k_ref.shape[1]
    gc = gc_ref[0]
    gct = gc.T
    beta = beta_ref[0]
    row = lax.broadcasted_iota(jnp.int32, (n, n), 0)
    col = lax.broadcasted_iota(jnp.int32, (n, n), 1)
    eye = (row == col).astype(F32)
    sizes = [2 ** j for j in range(1, int(math.log2(CHUNK)))]
    same = {r: (row // r) == (col // r) for r in sizes + [CHUNK]}
    a_all, t_all, x_all = [], [], []
    for d in range(2):
        incl, strict = _chunk_masks(n, d == 1)
        for h in range(GH):
            c = d * GH + h
            k = k_ref[0, :, h * GDK:(h + 1) * GDK]
            v = v_ref[0, :, h * GDV:(h + 1) * GDV]
            bcol = beta[:, c:c + 1]
            gcol = gc[:, c:c + 1]
            grow = gct[c:c + 1, :]
            kb = k * bcol
            a = jnp.where(strict, _dot_nt(kb.astype(BF16), k.astype(BF16)) * _decay(gcol, grow, incl), 0.0)
            a_all.append(a.astype(BF16))
            t_all.append(eye - jnp.where(same[2], a, 0.0))
            x_all.append(jnp.concatenate([v * bcol, kb * jnp.exp(gcol)], axis=1).astype(BF16))
    n_prob = len(a_all)
    zero = jnp.zeros((n, n), BF16)
    for r in sizes:
        off_mask = jnp.logical_and(same[2 * r], jnp.logical_not(same[r]))
        t_b = [t.astype(BF16) for t in t_all]
        inner = [_dot(jnp.where(off_mask, a_all[c], zero), t_b[c]).astype(BF16) for c in range(n_prob)]
        t_all = [t_all[c] - _dot(t_b[c], inner[c]) for c in range(n_prob)]
    for c in range(n_prob):
        uw_ref[0, c] = _dot(t_all[c].astype(BF16), x_all[c])


def _chunk(k, v, beta, gc):
    bsz, lc, _ = k.shape
    nt = lc // ROW_TILE

    def row_spec(width):
        return pl.BlockSpec((1, ROW_TILE, width), lambda b, i: (b, i, 0))

    return pl.pallas_call(
        _chunk_kernel,
        grid=(bsz, nt),
        in_specs=[row_spec(G_QK), row_spec(GH * GDV), row_spec(LANES), row_spec(LANES)],
        out_specs=pl.BlockSpec((1, 2 * GH, ROW_TILE, GDV + GDK), lambda b, i: (b, 0, i, 0)),
        out_shape=jax.ShapeDtypeStruct((bsz, 2 * GH, lc, GDV + GDK), F32),
        compiler_params=_cparams(("arbitrary", "arbitrary")),
        name="chunk",
    )(k, v, beta, gc)


def _scan_kernel(qf_ref, kf_ref, gf_ref, uwf_ref, qr_ref, kr_ref, gr_ref, uwr_ref,
                 of_ref, or_ref, s_ref):
    @pl.when(pl.program_id(1) == 0)
    def _():
        s_ref[...] = jnp.zeros_like(s_ref)

    n_chunks = ROW_TILE // CHUNK
    n_batch = qf_ref.shape[0]
    row = lax.broadcasted_iota(jnp.int32, (CHUNK, CHUNK), 0)
    col = lax.broadcasted_iota(jnp.int32, (CHUNK, CHUNK), 1)
    dirs = ((qf_ref, kf_ref, gf_ref, uwf_ref, of_ref), (qr_ref, kr_ref, gr_ref, uwr_ref, or_ref))
    chains = [(bb, d, h) for bb in range(n_batch) for d in range(2) for h in range(GH)]
    state = [s_ref[i] for i in range(len(chains))]

    def chunk_rows(step, d):
        ci = n_chunks - 1 - step if d == 1 else step
        return slice(ci * CHUNK, (ci + 1) * CHUNK)

    pre = []
    for step in range(n_chunks):
        gates = {}
        for bb in range(n_batch):
            for d in range(2):
                gc = dirs[d][2][bb, chunk_rows(step, d), :]
                gates[bb, d] = (gc, gc.T)
        per_chain = []
        for bb, d, h in chains:
            q_ref, k_ref, _, uw_ref, _ = dirs[d]
            rev = d == 1
            incl = (col >= row) if rev else (col <= row)
            rows = chunk_rows(step, d)
            gc, gct = gates[bb, d]
            c = d * GH + h
            q = q_ref[bb, rows, h * GDK:(h + 1) * GDK]
            k = k_ref[bb, rows, h * GDK:(h + 1) * GDK]
            gcol = gc[:, c:c + 1]
            grow = gct[c:c + 1, :]
            last = 0 if rev else CHUNK - 1
            g_last = gcol[last:last + 1, :]
            a_qk = (_dot_nt(q.astype(BF16), k.astype(BF16)) * _decay(gcol, grow, incl)).astype(BF16)
            wq = jnp.concatenate([uw_ref[bb, h, rows, GDV:], q * jnp.exp(gcol)], axis=0).astype(BF16)
            k_tail_t = (k * jnp.exp(g_last - gcol)).T.astype(BF16)
            per_chain.append((a_qk, wq, k_tail_t, jnp.exp(g_last)))
        pre.append(per_chain)

    for step in range(n_chunks):
        ws = [_dot(pre[step][i][1], state[i].astype(BF16)) for i in range(len(chains))]
        v_new = [(dirs[d][3][bb, h, chunk_rows(step, d), :GDV] - ws[i][:CHUNK]).astype(BF16)
                 for i, (bb, d, h) in enumerate(chains)]
        for i, (bb, d, h) in enumerate(chains):
            a_qk, _, k_tail_t, decay_last = pre[step][i]
            dirs[d][4][bb, chunk_rows(step, d), h * GDV:(h + 1) * GDV] = ws[i][CHUNK:] + _dot(a_qk, v_new[i])
            state[i] = state[i] * decay_last + _dot(k_tail_t, v_new[i])
    for i in range(len(chains)):
        s_ref[i] = state[i]


def _scan(q, k, gc, uw, n_lat_tiles):
    bsz, lc, _ = q.shape
    nt = lc // ROW_TILE
    n_ctx_tiles = nt - n_lat_tiles

    def fwd(b, i):
        return jnp.where(i < n_ctx_tiles, n_lat_tiles + i, i - n_ctx_tiles)

    def bwd(b, i):
        return jnp.where(i < n_ctx_tiles, nt - 1 - i, n_lat_tiles - 1 - (i - n_ctx_tiles))

    nb = SCAN_BATCH if bsz % SCAN_BATCH == 0 else 1

    def specs(tile):
        return [pl.BlockSpec((nb, ROW_TILE, G_QK), lambda b, i: (b, tile(b, i), 0)),
                pl.BlockSpec((nb, ROW_TILE, G_QK), lambda b, i: (b, tile(b, i), 0)),
                pl.BlockSpec((nb, ROW_TILE, LANES), lambda b, i: (b, tile(b, i), 0))]

    def uw_spec(tile, d):
        return pl.BlockSpec((nb, GH, ROW_TILE, GDV + GDK), lambda b, i: (b, d, tile(b, i), 0))

    def out_spec(tile):
        return pl.BlockSpec((nb, ROW_TILE, GH * GDV), lambda b, i: (b, tile(b, i), 0))

    return pl.pallas_call(
        _scan_kernel,
        grid=(bsz // nb, nt),
        in_specs=specs(fwd) + [uw_spec(fwd, 0)] + specs(bwd) + [uw_spec(bwd, 1)],
        out_specs=[out_spec(fwd), out_spec(bwd)],
        out_shape=[jax.ShapeDtypeStruct((bsz, lc, GH * GDV), F32)] * 2,
        scratch_shapes=[pltpu.VMEM((nb * 2 * GH, GDK, GDV), F32)],
        compiler_params=_cparams(("arbitrary", "arbitrary")),
        name="scan",
    )(q, k, gc, uw, q, k, gc, uw)


def _attn_kernel(lam_init, q_ref, k_ref, vt_ref, lam_ref, nw_ref, o_ref):
    lam = lam_ref[...]
    lam_full = (jnp.exp(jnp.sum(lam[0:1] * lam[1:2], axis=-1, keepdims=True))
                - jnp.exp(jnp.sum(lam[2:3] * lam[3:4], axis=-1, keepdims=True)) + lam_init)
    q = q_ref[0]
    tq = q.shape[0]
    n_keys = k_ref.shape[1]
    lane = lax.broadcasted_iota(jnp.int32, q.shape, 1)
    zero = jnp.zeros_like(q)
    qc = [jnp.where(lane < DDK, q, zero), jnp.where(lane >= DDK, q, zero)]
    m = [jnp.full((1, tq), -jnp.inf, F32) for _ in range(2)]
    l = [jnp.zeros((1, tq), F32) for _ in range(2)]
    acc = [jnp.zeros((DDV, tq), F32) for _ in range(2)]
    n_tiles = n_keys // KEY_TILE
    bounds = [t * KEY_TILE for t in range(n_tiles)] + [n_keys]

    def scores(t):
        k_t = k_ref[0, bounds[t]:bounds[t + 1], :]
        tiles = [_dot_nt(k_t, qc[c]) for c in range(2)]
        return [(s, jnp.max(s, axis=0, keepdims=True)) for s in tiles]

    pending = [scores(t) for t in range(min(SCORE_LOOKAHEAD, n_tiles))]
    for t in range(n_tiles):
        s_cur = pending.pop(0)
        if t + SCORE_LOOKAHEAD < n_tiles:
            pending.append(scores(t + SCORE_LOOKAHEAD))
        width = bounds[t + 1] - bounds[t]
        vt_t = jnp.concatenate([vt_ref[0, :, bounds[t]:bounds[t + 1]], jnp.ones((16, width), BF16)], axis=0)
        for c in range(2):
            s, s_max = s_cur[c]
            m_new = jnp.maximum(m[c], s_max)
            p = jnp.exp2(s - m_new)
            scale = jnp.exp2(m[c] - m_new)
            pv = _dot(vt_t, p.astype(BF16))
            l[c] = scale * l[c] + pv[DDV:DDV + 1]
            acc[c] = scale * acc[c] + pv[:DDV]
            m[c] = m_new
    o = acc[0] / l[0] - lam_full * (acc[1] / l[1])
    y = o * lax.rsqrt(jnp.mean(o * o, axis=0, keepdims=True) + NORM_EPS)
    o_ref[0] = (y.T * nw_ref[...] * (1.0 - lam_init)).astype(o_ref.dtype)


def _attn(dq, dk, dvt, lam, norm_w, lam_init, lq, tq):
    bsz, lc, _ = dq.shape
    return pl.pallas_call(
        functools.partial(_attn_kernel, lam_init),
        grid=(bsz, DH, lq // tq),
        in_specs=[pl.BlockSpec((1, tq, 2 * DDK), lambda b, h, i: (b, i, h)),
                  pl.BlockSpec((1, lc, 2 * DDK), lambda b, h, i: (b, 0, h)),
                  pl.BlockSpec((1, DDV, lc), lambda b, h, i: (b, h, 0)),
                  pl.BlockSpec((4, DDK), lambda b, h, i: (0, 0)),
                  pl.BlockSpec((1, DDV), lambda b, h, i: (0, 0))],
        out_specs=pl.BlockSpec((1, tq, DDV), lambda b, h, i: (b, i, h)),
        out_shape=jax.ShapeDtypeStruct((bsz, lq, DH * DDV), BF16),
        compiler_params=_cparams(("arbitrary", "arbitrary", "arbitrary")),
        name="attn",
    )(dq, dk, dvt, lam, norm_w.reshape(1, DDV))


def _out_kernel(alpha, of_ref, or_ref, z_ref, yd_ref, x_ref, g1_ref, sh2_ref, sc2_ref, gnw_ref,
                wo_ref, lng_ref, lnb_ref, wr_ref, br_ref, x1_ref, h2_ref, lg_ref):
    o = of_ref[0] + or_ref[0]
    z = z_ref[0]
    heads = []
    for h in range(GH):
        oh = o[:, h * GDV:(h + 1) * GDV]
        heads.append((oh * lax.rsqrt(jnp.mean(oh * oh, axis=-1, keepdims=True) + NORM_EPS) * gnw_ref[...]
                      * _silu(z[:, h * GDV:(h + 1) * GDV])).astype(BF16))
    mix = jnp.concatenate(heads + [yd_ref[0]], axis=1)
    x1 = _normalize(alpha * x_ref[0] + g1_ref[0] * _dot(mix, wo_ref[...])) * lng_ref[...] + lnb_ref[...]
    x1_ref[0] = x1
    h2 = _normalize(x1) * (1.0 + sc2_ref[0]) + sh2_ref[0]
    h2_ref[0] = h2
    h_hi = h2.astype(BF16)
    h_lo = (h2 - h_hi.astype(F32)).astype(BF16)
    both = _dot(h_hi, wr_ref[...])
    lg_ref[0] = br_ref[...] + both[:, :LANES] + both[:, LANES:] + _dot(h_lo, wr_ref[:, :LANES])


def _out(o_f, o_r, z, y_diff, xc, mod3, gdn_norm_w, w_out, ln_g, ln_b, w_router3, b_router, alpha):
    bsz, lq, _ = y_diff.shape
    tile = OUT_TILE if lq % OUT_TILE == 0 else ROW_TILE
    nt = lq // tile

    def row_spec(width):
        return pl.BlockSpec((1, tile, width), lambda b, i: (b, i, 0))

    def mod_spec(col):
        return pl.BlockSpec((1, 1, D), lambda b, i: (b, 0, col))

    def const_spec(shape):
        return pl.BlockSpec(shape, lambda b, i: tuple(0 for _ in shape))

    return pl.pallas_call(
        functools.partial(_out_kernel, alpha),
        grid=(bsz, nt),
        in_specs=[row_spec(GH * GDV), row_spec(GH * GDV), row_spec(G_Z), row_spec(DH * DDV), row_spec(D),
                  mod_spec(2), mod_spec(3), mod_spec(4),
                  const_spec((1, GDV)), const_spec((G_Z + DH * DDV, D)),
                  const_spec((1, D)), const_spec((1, D)),
                  const_spec((D, 2 * LANES)), const_spec((1, LANES))],
        out_specs=[row_spec(D), row_spec(D), row_spec(LANES)],
        out_shape=[jax.ShapeDtypeStruct((bsz, lq, D), F32), jax.ShapeDtypeStruct((bsz, lq, D), F32),
                   jax.ShapeDtypeStruct((bsz, lq, LANES), F32)],
        compiler_params=_cparams(("arbitrary", "arbitrary")),
        name="out",
    )(o_f, o_r, z, y_diff, xc, mod3, mod3, mod3, gdn_norm_w.reshape(1, GDV), w_out,
      ln_g.reshape(1, D), ln_b.reshape(1, D), w_router3, b_router)


def _route_kernel(lg_ref, tri_ref, ids_ref, gate_ref, cnt_ref, run_ref):
    @pl.when(pl.program_id(0) == 0)
    def _():
        run_ref[...] = jnp.zeros_like(run_ref)

    lgt = lg_ref[...].T[:ROUTE_ROWS]
    n = lgt.shape[1]
    row_i = lax.broadcasted_iota(jnp.int32, lgt.shape, 0)
    row = row_i.astype(F32)
    neg = jnp.float32(-jnp.inf)

    def first_max(mask):
        masked = jnp.where(mask, lgt, neg)
        m = jnp.max(masked, axis=0, keepdims=True)
        idx = jnp.min(jnp.where(jnp.logical_and(mask, masked == m), row, float(LANES)), axis=0, keepdims=True)
        return m, idx

    is_group = row_i < N_GROUPS
    m_g, grp = first_max(is_group)
    p_g = 1.0 / jnp.sum(jnp.where(is_group, jnp.exp(lgt - m_g), 0.0), axis=0, keepdims=True)
    e_row = row - float(N_GROUPS)
    lo_e = grp * float(E_PER_GROUP)
    in_group = jnp.logical_and(e_row >= lo_e, e_row < lo_e + float(E_PER_GROUP))
    l0, i0 = first_max(in_group)
    l1, i1 = first_max(jnp.logical_and(in_group, row != i0))
    r = jnp.exp(l1 - l0)
    gate0 = p_g / (1.0 + r)
    gate1 = p_g * r / (1.0 + r)
    e0 = i0 - float(N_GROUPS)
    e1 = i1 - float(N_GROUPS)
    oh0 = (e_row == e0).astype(F32)
    oh1 = (e_row == e1).astype(F32)
    both = oh0 + oh1
    before = _dot(both.astype(BF16), tri_ref[...]) + run_ref[...]
    rank0 = jnp.sum(oh0 * before, axis=0, keepdims=True)
    rank1 = jnp.sum(oh1 * before, axis=0, keepdims=True)
    run_ref[...] = run_ref[...] + jnp.sum(both, axis=1, keepdims=True)
    cnt_ref[...] = jnp.broadcast_to(run_ref[...], cnt_ref.shape)
    ids_ref[...] = jnp.concatenate([e0, e1, rank0, rank1, jnp.zeros((4, n), F32)], axis=0).astype(jnp.int32)
    gate_ref[...] = jnp.concatenate([gate0, gate1, jnp.zeros((LANES - 2, n), F32)], axis=0).T


def _route(logits, tri_strict):
    t = logits.shape[0]
    tile = tri_strict.shape[0]
    return pl.pallas_call(
        _route_kernel,
        grid=(t // tile,),
        in_specs=[pl.BlockSpec((tile, LANES), lambda i: (i, 0)),
                  pl.BlockSpec((tile, tile), lambda i: (0, 0))],
        out_specs=[pl.BlockSpec((8, tile), lambda i: (0, i)),
                   pl.BlockSpec((tile, LANES), lambda i: (i, 0)),
                   pl.BlockSpec((ROUTE_ROWS, LANES), lambda i: (0, 0))],
        out_shape=[jax.ShapeDtypeStruct((8, t), jnp.int32), jax.ShapeDtypeStruct((t, LANES), F32),
                   jax.ShapeDtypeStruct((ROUTE_ROWS, LANES), F32)],
        scratch_shapes=[pltpu.VMEM((ROUTE_ROWS, 1), F32)],
        compiler_params=_cparams(("arbitrary",)),
        name="route",
    )(logits, tri_strict)


ROUTE_TILE = 1024
ROUTE_ROWS = 40
DMA_TOKENS = 512
COMBINE_LAG = 2


def _dispatch_kernel(tail_ref, used_ref, dest_ref, h_hbm, buf_hbm, zero_ref, src_ref, load_sem, row_sem, zsem):
    i = pl.program_id(0)
    n_tiles = pl.num_programs(0)
    n_slots = src_ref.shape[0]
    groups = DMA_TOKENS // 8

    def load(tile):
        slot = lax.rem(tile, n_slots)
        return pltpu.make_async_copy(h_hbm.at[pl.ds(tile * groups, groups)], src_ref.at[slot], load_sem.at[slot])

    def wait_rows(tile):
        slot = lax.rem(tile, n_slots)
        for _ in range(2):
            pltpu.make_async_copy(buf_hbm.at[pl.ds(0, DMA_TOKENS)], buf_hbm.at[pl.ds(0, DMA_TOKENS)],
                                  row_sem.at[slot]).wait()

    @pl.when(i == 0)
    def _():
        load(0).start()
        zero_ref[...] = jnp.zeros_like(zero_ref)
        n_blocks = buf_hbm.shape[0] // FFN_BLOCK

        def block_copy(b):
            return pltpu.make_async_copy(zero_ref, buf_hbm.at[pl.ds(b * FFN_BLOCK, FFN_BLOCK)], zsem)

        for e in range(N_EXPERTS):
            @pl.when(tail_ref[e] >= 0)
            def _():
                block_copy(tail_ref[e]).start()
        lax.fori_loop(used_ref[0], n_blocks, lambda b, carry: (block_copy(b).start(), carry)[1], 0)
        for e in range(N_EXPERTS):
            @pl.when(tail_ref[e] >= 0)
            def _():
                block_copy(tail_ref[e]).wait()
        lax.fori_loop(used_ref[0], n_blocks, lambda b, carry: (block_copy(b).wait(), carry)[1], 0)

    @pl.when(i + 1 < n_tiles)
    def _():
        load(i + 1).start()

    load(i).wait()
    slot = lax.rem(i, n_slots)

    def start(grp, carry):
        for u in range(8):
            for j in range(2):
                dst = dest_ref[16 * grp + 2 * u + j]
                pltpu.make_async_copy(src_ref.at[slot, grp, pl.ds(u, 1)], buf_hbm.at[pl.ds(dst, 1)],
                                      row_sem.at[slot]).start()
        return carry

    lax.fori_loop(0, groups, start, 0)

    @pl.when(i > 0)
    def _():
        wait_rows(i - 1)

    @pl.when(i == n_tiles - 1)
    def _():
        wait_rows(i)


def _dispatch(tail_block, n_used, dest_flat, h2, n_rows):
    t, d = h2.shape
    grid_spec = pltpu.PrefetchScalarGridSpec(
        num_scalar_prefetch=2,
        grid=(t // DMA_TOKENS,),
        in_specs=[pl.BlockSpec((2 * DMA_TOKENS,), lambda i, tail, used: (i,), memory_space=pltpu.SMEM),
                  pl.BlockSpec(memory_space=pl.ANY)],
        out_specs=pl.BlockSpec(memory_space=pl.ANY),
        scratch_shapes=[pltpu.VMEM((FFN_BLOCK, d), h2.dtype),
                        pltpu.VMEM((3, DMA_TOKENS // 8, 8, d), h2.dtype),
                        pltpu.SemaphoreType.DMA((3,)), pltpu.SemaphoreType.DMA((3,)),
                        pltpu.SemaphoreType.DMA(())],
    )
    return pl.pallas_call(
        _dispatch_kernel,
        grid_spec=grid_spec,
        out_shape=jax.ShapeDtypeStruct((n_rows, d), h2.dtype),
        compiler_params=_cparams(("arbitrary",)),
        name="dispatch",
    )(tail_block, n_used, dest_flat, h2.reshape(t // 8, 8, d))


def _combine_kernel(alpha, dest_ref, ys_hbm, gate_ref, x1_ref, g2_ref, lng_ref, lnb_ref, o_ref, rows_ref, sem):
    i = pl.program_id(0)
    n_tiles = pl.num_programs(0) - COMBINE_LAG
    n = x1_ref.shape[0]
    n_slots = rows_ref.shape[0]
    slot_in = lax.rem(i, n_slots)
    slot_out = lax.rem(i + n_slots - COMBINE_LAG, n_slots)

    def start_group(grp):
        for u in range(8):
            for j in range(2):
                src = dest_ref[16 * grp + 2 * u + j]
                pltpu.make_async_copy(ys_hbm.at[pl.ds(src, 1)], rows_ref.at[slot_in, j, grp, pl.ds(u, 1)],
                                      sem.at[slot_in]).start()

    def finish():
        gate = gate_ref[...]
        r0 = rows_ref[slot_out, 0].reshape(n, -1)
        r1 = rows_ref[slot_out, 1].reshape(n, -1)
        y = gate[:, 0:1] * r0 + gate[:, 1:2] * r1
        o_ref[...] = _normalize(alpha * x1_ref[...] + g2_ref[0] * y) * lng_ref[...] + lnb_ref[...]

    @pl.when(i >= COMBINE_LAG)
    def _():
        for j in range(2):
            pltpu.make_async_copy(ys_hbm.at[pl.ds(0, n)], ys_hbm.at[pl.ds(0, n)], sem.at[slot_out]).wait()

    @pl.when(i < COMBINE_LAG)
    def _():
        lax.fori_loop(0, n // 8, lambda grp, carry: (start_group(grp), carry)[1], 0)

    @pl.when(jnp.logical_and(i >= COMBINE_LAG, i < n_tiles))
    def _():
        for grp in range(n // 8):
            start_group(grp)
        finish()

    @pl.when(i >= n_tiles)
    def _():
        finish()


def _combine(dest_flat, ys, gate, x1, mod3, ln_g, ln_b, alpha, rows_per_batch):
    t, d = x1.shape
    per_batch = rows_per_batch // DMA_TOKENS
    n_tiles = t // DMA_TOKENS

    def done(i):
        return jnp.maximum(i - COMBINE_LAG, 0)

    return pl.pallas_call(
        functools.partial(_combine_kernel, alpha),
        grid=(n_tiles + COMBINE_LAG,),
        in_specs=[pl.BlockSpec((2 * DMA_TOKENS,), lambda i: (jnp.minimum(i, n_tiles - 1),), memory_space=pltpu.SMEM),
                  pl.BlockSpec(memory_space=pl.ANY),
                  pl.BlockSpec((DMA_TOKENS, LANES), lambda i: (done(i), 0)),
                  pl.BlockSpec((DMA_TOKENS, d), lambda i: (done(i), 0)),
                  pl.BlockSpec((1, 1, d), lambda i: (done(i) // per_batch, 0, 5)),
                  pl.BlockSpec((1, d), lambda i: (0, 0)),
                  pl.BlockSpec((1, d), lambda i: (0, 0))],
        out_specs=pl.BlockSpec((DMA_TOKENS, d), lambda i: (done(i), 0)),
        out_shape=jax.ShapeDtypeStruct((t, d), F32),
        scratch_shapes=[pltpu.VMEM((COMBINE_LAG + 1, 2, DMA_TOKENS // 8, 8, d), F32),
                        pltpu.SemaphoreType.DMA((COMBINE_LAG + 1,))],
        compiler_params=_cparams(("arbitrary",)),
        name="combine",
    )(dest_flat, ys, gate, x1, mod3, ln_g.reshape(1, d), ln_b.reshape(1, d))


def _ffn_kernel(start_ref, nblk_ref, used_ref, buf_hbm, wg_ref, wu_ref, wd_ref, ys_hbm,
                wgb_ref, wub_ref, wdb_ref, x_buf, o_buf, in_sem, out_sem):
    e = pl.program_id(0)
    nb = nblk_ref[e]
    base = start_ref[e]
    used = used_ref[0]
    n_in = x_buf.shape[0]
    n_out = o_buf.shape[0]
    ahead = n_in - 1

    def rows(g):
        return pl.ds(g * FFN_BLOCK, FFN_BLOCK)

    def in_copy(g):
        slot = lax.rem(g, n_in)
        return pltpu.make_async_copy(buf_hbm.at[rows(g)], x_buf.at[slot], in_sem.at[slot])

    def out_copy(g):
        slot = lax.rem(g, n_out)
        return pltpu.make_async_copy(o_buf.at[slot], ys_hbm.at[rows(g)], out_sem.at[slot])

    @pl.when(e == 0)
    def _():
        for g in range(ahead):
            @pl.when(g < used)
            def _():
                in_copy(g).start()

    @pl.when(nb > 0)
    def _():
        wgb_ref[...] = wg_ref[0].astype(BF16)
        wub_ref[...] = wu_ref[0].astype(BF16)
        wdb_ref[...] = wd_ref[0].astype(BF16)

        def block(j, carry):
            g = base + j
            in_copy(g).wait()

            @pl.when(g + ahead < used)
            def _():
                in_copy(g + ahead).start()

            @pl.when(g >= n_out)
            def _():
                out_copy(g - n_out).wait()

            islot = lax.rem(g, n_in)
            oslot = lax.rem(g, n_out)
            slab = FFN_BLOCK // FFN_SLABS
            xs = [x_buf[islot, s * slab:(s + 1) * slab, :].astype(BF16) for s in range(FFN_SLABS)]
            gates = [_dot(x, wgb_ref[...]) for x in xs]
            ups = [_dot(x, wub_ref[...]) for x in xs]
            hidden = [(_silu(gt) * up).astype(BF16) for gt, up in zip(gates, ups)]
            for s in range(FFN_SLABS):
                o_buf[oslot, s * slab:(s + 1) * slab, :] = _dot(hidden[s], wdb_ref[...])
            out_copy(g).start()
            return carry

        lax.fori_loop(0, nb, block, 0)

    @pl.when(e == pl.num_programs(0) - 1)
    def _():
        for back in range(n_out):
            @pl.when(used - 1 - back >= 0)
            def _():
                out_copy(used - 1 - back).wait()
        n_blocks = ys_hbm.shape[0] // FFN_BLOCK
        o_buf[0] = jnp.zeros(o_buf.shape[1:], o_buf.dtype)

        def clear(b):
            return pltpu.make_async_copy(o_buf.at[0], ys_hbm.at[pl.ds(b * FFN_BLOCK, FFN_BLOCK)], out_sem.at[0])

        lax.fori_loop(used_ref[0], n_blocks, lambda b, carry: (clear(b).start(), carry)[1], 0)
        lax.fori_loop(used_ref[0], n_blocks, lambda b, carry: (clear(b).wait(), carry)[1], 0)


def _ffn(seg_start, seg_blocks, n_used, buf, w_gate, w_up, w_down):
    n_rows, d = buf.shape
    n_experts = w_gate.shape[0]
    grid_spec = pltpu.PrefetchScalarGridSpec(
        num_scalar_prefetch=3,
        grid=(n_experts,),
        in_specs=[pl.BlockSpec(memory_space=pl.ANY),
                  pl.BlockSpec((1, d, D_EXPERT), lambda e, *_: (e, 0, 0)),
                  pl.BlockSpec((1, d, D_EXPERT), lambda e, *_: (e, 0, 0)),
                  pl.BlockSpec((1, D_EXPERT, d), lambda e, *_: (e, 0, 0))],
        out_specs=pl.BlockSpec(memory_space=pl.ANY),
        scratch_shapes=[pltpu.VMEM((d, D_EXPERT), BF16), pltpu.VMEM((d, D_EXPERT), BF16),
                        pltpu.VMEM((D_EXPERT, d), BF16),
                        pltpu.VMEM((FFN_IN_SLOTS, FFN_BLOCK, d), F32), pltpu.VMEM((FFN_OUT_SLOTS, FFN_BLOCK, d), F32),
                        pltpu.SemaphoreType.DMA((FFN_IN_SLOTS,)), pltpu.SemaphoreType.DMA((FFN_OUT_SLOTS,))],
    )
    return pl.pallas_call(
        _ffn_kernel,
        grid_spec=grid_spec,
        out_shape=jax.ShapeDtypeStruct((n_rows, d), F32),
        compiler_params=_cparams(("arbitrary",)),
        name="ffn",
    )(seg_start, seg_blocks, n_used, buf, w_gate, w_up, w_down)


def _rope_tables(seq_len, n_ctx_rows):
    lane = np.arange(LANES)
    within = lane % DDK
    freq = within % 16
    use_col = within >= DDK // 2
    sign = np.where((lane % 32) < 16, -1.0, 1.0).astype(np.float32)
    half = DDK // 2
    inv_freq = (ROPE_THETA ** (-np.arange(0, half, 2, dtype=np.float32) / np.float32(half))).astype(np.float32)
    t = np.arange(seq_len)
    pos = np.where(use_col[None, :], (t % GRID_W)[:, None], (t // GRID_W)[:, None]).astype(np.float32)
    ang = pos * inv_freq[freq][None, :]
    cos = np.concatenate([np.cos(ang), np.ones((n_ctx_rows, LANES), np.float32)], axis=0)
    sin = np.concatenate([np.sin(ang) * sign[None, :], np.zeros((n_ctx_rows, LANES), np.float32)], axis=0)
    return jnp.asarray(cos, F32), jnp.asarray(sin, F32)


def _block_tri(n, block, upper):
    i = np.arange(n)
    same = (i[:, None] // block) == (i[None, :] // block)
    tri = (i[None, :] >= i[:, None]) if upper else (i[None, :] <= i[:, None])
    return jnp.asarray((same & tri).astype(np.float32), dtype=BF16)


def _pack_w_in(w_in):
    pad = jnp.zeros((D, LANES - G_GATES), w_in.dtype)
    off = G_QKV + G_Z
    v_start = G_COLS + 2 * D_QK
    w_all = jnp.concatenate([
        w_in[:, :G_QKV], w_in[:, G_QKV:off],
        w_in[:, off:off + G_GATES], pad,
        w_in[:, off + G_GATES:G_COLS], pad,
        w_in[:, G_COLS:v_start],
    ], axis=1).astype(BF16)
    return w_all, w_in[:, v_start:].T.astype(BF16)


def kernel(x, c, ctx, c_ctx, w_ada, b_ada, w_in, conv_w, gdn_a_log, gdn_dt_bias, gdn_norm_w, diff_lambda,
           diff_norm_w, w_out, ln1_g, ln1_b, w_router_group, b_router_group, w_router_expert, b_router_expert,
           w_expert_gate, w_expert_up, w_expert_down, ln2_g, ln2_b):
    depth = w_ada.shape[0]
    assert depth == 1, "single-layer block: the context stream never feeds a later layer"
    bsz, seq, _ = x.shape
    n_ctx = ctx.shape[1]
    assert seq % ROW_TILE == 0 and n_ctx % ROW_TILE == 0 and seq % GRID_W == 0 and bsz < 8
    n_lat_tiles = seq // ROW_TILE
    alpha = (2.0 * depth) ** 0.25
    lam_init = 0.8 - 0.6 * math.exp(-0.3 * 0)
    i = 0

    cc = jnp.zeros((8, D), F32).at[:bsz].set(c).at[7].set(c_ctx)
    mod = _ada(cc, w_ada[i], b_ada[i])
    mod3 = mod.reshape(8, 1, 6 * D)

    cos_t, sin_t = _rope_tables(seq, n_ctx)
    qkv, z, gt, dq, dk, dvt = _proj(x, ctx, mod3, cos_t, sin_t, *_pack_w_in(w_in[i]))

    conv_w8 = jnp.zeros((8, G_QKV), F32).at[:GCONV].set(conv_w[i])
    alog_v = jnp.zeros((1, LANES), F32).at[0, :G_GATES].set(gdn_a_log[i].reshape(-1))
    dtb_v = jnp.zeros((1, LANES), F32).at[0, :G_GATES].set(gdn_dt_bias[i].reshape(-1))
    gq, gk, gv, beta, gc = _prep(qkv, gt, conv_w8, alog_v, dtb_v,
                                 _block_tri(ROW_TILE, CHUNK, False), _block_tri(ROW_TILE, CHUNK, True), n_lat_tiles)
    uw = _chunk(gk, gv, beta, gc)
    o_f, o_r = _scan(gq, gk, gc, uw, n_lat_tiles)

    y_diff = _attn(dq, dk, dvt, diff_lambda[i], diff_norm_w[i], lam_init, seq, tq=min(ATTN_Q_TILE, seq))

    w_router = jnp.zeros((D, LANES), F32).at[:, :N_GROUPS].set(w_router_group[i]) \
        .at[:, N_GROUPS:N_GROUPS + N_EXPERTS].set(w_router_expert[i])
    w_hi = w_router.astype(BF16)
    w_router3 = jnp.concatenate([w_hi, (w_router - w_hi.astype(F32)).astype(BF16)], axis=1)
    b_router = jnp.zeros((1, LANES), F32).at[0, :N_GROUPS].set(b_router_group[i]) \
        .at[0, N_GROUPS:N_GROUPS + N_EXPERTS].set(b_router_expert[i])
    x1, h2, logits = _out(o_f, o_r, z, y_diff, x, mod3, gdn_norm_w[i], w_out[i].astype(BF16), ln1_g[i], ln1_b[i],
                          w_router3, b_router, alpha)

    t = bsz * seq
    tri_strict = jnp.asarray(np.triu(np.ones((ROUTE_TILE, ROUTE_TILE), np.float32), 1), dtype=BF16)
    ids, gate, counts = _route(logits.reshape(t, LANES), tri_strict)

    cnt = counts[N_GROUPS:N_GROUPS + N_EXPERTS, 0].astype(jnp.int32)
    padded = (cnt + FFN_BLOCK - 1) // FFN_BLOCK * FFN_BLOCK
    p_end = jnp.cumsum(padded)
    p_start = p_end - padded
    n_blocks = (2 * t) // FFN_BLOCK + N_EXPERTS
    n_used = (p_end[-1:] // FFN_BLOCK).astype(jnp.int32)
    tail_block = jnp.where(cnt > 0, p_end // FFN_BLOCK - 1, -1).astype(jnp.int32)
    one_hot = ids[0:2, :, None] == jnp.arange(N_EXPERTS, dtype=jnp.int32)
    row_start = jnp.sum(jnp.where(one_hot, p_start, 0), axis=-1)
    dest = (row_start + ids[2:4]).T.reshape(-1).astype(jnp.int32)

    buf = _dispatch(tail_block, n_used, dest, h2.reshape(t, D), n_blocks * FFN_BLOCK)
    ys = _ffn((p_start // FFN_BLOCK).astype(jnp.int32), (padded // FFN_BLOCK).astype(jnp.int32), n_used, buf,
              w_expert_gate[i], w_expert_up[i], w_expert_down[i])
    out = _combine(dest, ys, gate, x1.reshape(t, D), mod3, ln2_g[i], ln2_b[i], alpha, seq)
    return out.reshape(bsz, seq, D)
```

```python
import functools
import math

import numpy as np
import jax
import jax.numpy as jnp
from jax import lax
from jax.experimental import pallas as pl
from jax.experimental.pallas import tpu as pltpu

F32 = jnp.float32
BF16 = jnp.bfloat16

D = 1024
GRID_W = 64
ROPE_THETA = 10000.0
GH = 4
GDK = 128
GDV = 128
GCONV = 5
DH = 4
DDK = 64
DDV = 128
G_QK = GH * GDK
G_QKV = 2 * G_QK + GH * GDV
G_Z = GH * GDV
G_GATES = 2 * GH
G_COLS = G_QKV + G_Z + 2 * G_GATES
D_QK = DH * 2 * DDK
N_GROUPS = 4
E_PER_GROUP = 8
N_EXPERTS = N_GROUPS * E_PER_GROUP
D_EXPERT = 512
LN_EPS = 1e-5
NORM_EPS = 1e-6

LANES = 128
ROW_TILE = 256
CHUNK = 64
OUT_TILE = 512
SCAN_BATCH = 2
FFN_BLOCK = 256
FFN_SLABS = 2
FFN_IN_SLOTS = 4
FFN_OUT_SLOTS = 3
KEY_TILE = 512
ATTN_Q_TILE = 1024
SCORE_LOOKAHEAD = 2
VMEM_LIMIT = 56 * 1024 * 1024

Q_SCALE = DDK ** -0.5 * math.log2(math.e)


def _cparams(sem):
    return pltpu.CompilerParams(dimension_semantics=sem, vmem_limit_bytes=VMEM_LIMIT)


def _dot(a, b):
    return jnp.dot(a, b, preferred_element_type=F32)


def _dot_nt(a, b):
    return lax.dot_general(a, b, (((1,), (1,)), ((), ())), preferred_element_type=F32)


def _dot_tn(a, b):
    return lax.dot_general(a, b, (((0,), (0,)), ((), ())), preferred_element_type=F32)


def _normalize(x):
    mu = jnp.mean(x, axis=-1, keepdims=True)
    xc = x - mu
    var = jnp.mean(xc * xc, axis=-1, keepdims=True)
    return xc * lax.rsqrt(var + LN_EPS)


def _silu(x):
    return x * jax.nn.sigmoid(x)


def _split3(x):
    hi = x.astype(BF16)
    r = x - hi.astype(F32)
    mid = r.astype(BF16)
    lo = (r - mid.astype(F32)).astype(BF16)
    return hi, mid, lo


def _ada_kernel(c_ref, w_ref, b_ref, o_ref):
    s = _silu(c_ref[...])
    s_hi = s.astype(BF16)
    s_lo = (s - s_hi.astype(F32)).astype(BF16)
    w = w_ref[...]
    w_hi = w.astype(BF16)
    w_lo = (w - w_hi.astype(F32)).astype(BF16)
    rows = s.shape[0]
    both = _dot(jnp.concatenate([s_hi, s_lo], axis=0), w_hi)
    o_ref[...] = both[:rows] + both[rows:] + _dot(s_hi, w_lo) + b_ref[...]


def _ada(cc, w_ada, b_ada):
    n = w_ada.shape[1]
    bn = 1024
    return pl.pallas_call(
        _ada_kernel,
        grid=(n // bn,),
        in_specs=[pl.BlockSpec((8, D), lambda j: (0, 0)),
                  pl.BlockSpec((D, bn), lambda j: (0, j)),
                  pl.BlockSpec((1, bn), lambda j: (0, j))],
        out_specs=pl.BlockSpec((8, bn), lambda j: (0, j)),
        out_shape=jax.ShapeDtypeStruct((8, n), F32),
        compiler_params=_cparams(("arbitrary",)),
        name="ada",
    )(cc, w_ada, b_ada.reshape(1, n))


def _rope(v, cos, sin, low_half):
    fwd = pltpu.roll(v, LANES - 16, axis=1)
    bwd = pltpu.roll(v, 16, axis=1)
    return v * cos + jnp.where(low_half, fwd, bwd) * sin


def _proj_kernel(n_lat_tiles, x_ref, ctx_ref, sh_ref, sc_ref, cos_ref, sin_ref, wg_ref, wgate_ref, wqk_ref, wvt_ref,
                 qkv_ref, z_ref, gt_ref, dq_ref, dk_ref, dvt_ref):
    rows = jnp.where(pl.program_id(1) >= n_lat_tiles, ctx_ref[0], x_ref[0])
    h = (_normalize(rows) * (1.0 + sc_ref[0]) + sh_ref[0]).astype(BF16)
    qkv_ref[0] = _dot(h, wg_ref[:, :G_QKV])
    z_ref[0] = _dot(h, wg_ref[:, G_QKV:]).astype(z_ref.dtype)
    gt_ref[0] = _dot(h, wgate_ref[...])
    cos = cos_ref[...]
    sin = sin_ref[...]
    lane = lax.broadcasted_iota(jnp.int32, cos.shape, 1)
    low_half = (lane % 32) < 16
    q_all = _dot(h, wqk_ref[:, :D_QK])
    k_all = _dot(h, wqk_ref[:, D_QK:])
    for j in range(D_QK // LANES):
        slab = slice(j * LANES, (j + 1) * LANES)
        dq_ref[0, :, slab] = (_rope(q_all[:, slab], cos, sin, low_half) * Q_SCALE).astype(BF16)
        dk_ref[0, :, slab] = _rope(k_all[:, slab], cos, sin, low_half).astype(BF16)
    dvt_ref[0] = _dot_nt(wvt_ref[...], h).astype(BF16)


def _proj(x, ctx, mod3, cos_t, sin_t, w_gdn, w_gate, w_qk, wv_t):
    bsz, seq, _ = x.shape
    lc = seq + ctx.shape[1]
    nt = lc // ROW_TILE
    n_lat_tiles = seq // ROW_TILE
    ctx_row = mod3.shape[0] - 1

    def mod_idx(col):
        return lambda b, i: (jnp.where(i >= n_lat_tiles, ctx_row, b), 0, col)

    def row_spec(width):
        return pl.BlockSpec((1, ROW_TILE, width), lambda b, i: (b, i, 0))

    outs = [(G_QKV, F32), (G_Z, BF16), (2 * LANES, F32), (D_QK, BF16), (D_QK, BF16)]
    return pl.pallas_call(
        functools.partial(_proj_kernel, n_lat_tiles),
        grid=(bsz, nt),
        in_specs=[pl.BlockSpec((1, ROW_TILE, D), lambda b, i: (b, jnp.minimum(i, n_lat_tiles - 1), 0)),
                  pl.BlockSpec((1, ROW_TILE, D), lambda b, i: (b, jnp.maximum(i - n_lat_tiles, 0), 0)),
                  pl.BlockSpec((1, 1, D), mod_idx(0)),
                  pl.BlockSpec((1, 1, D), mod_idx(1)),
                  pl.BlockSpec((ROW_TILE, LANES), lambda b, i: (i, 0)),
                  pl.BlockSpec((ROW_TILE, LANES), lambda b, i: (i, 0)),
                  pl.BlockSpec(w_gdn.shape, lambda b, i: (0, 0)),
                  pl.BlockSpec(w_gate.shape, lambda b, i: (0, 0)),
                  pl.BlockSpec(w_qk.shape, lambda b, i: (0, 0)),
                  pl.BlockSpec(wv_t.shape, lambda b, i: (0, 0))],
        out_specs=[row_spec(w) for w, _ in outs]
        + [pl.BlockSpec((1, DH * DDV, ROW_TILE), lambda b, i: (b, 0, i))],
        out_shape=[jax.ShapeDtypeStruct((bsz, lc, w), dt) for w, dt in outs]
        + [jax.ShapeDtypeStruct((bsz, DH * DDV, lc), BF16)],
        compiler_params=_cparams(("arbitrary", "arbitrary")),
        name="proj",
    )(x, ctx, mod3, mod3, cos_t, sin_t, w_gdn, w_gate, w_qk, wv_t)


def _prep_kernel(n_lat_tiles, nt, main_ref, prev_ref, next_ref, gt_ref, cw_ref, alog_ref, dtb_ref,
                 lo_ref, up_ref, q_ref, k_ref, v_ref, beta_ref, gc_ref, ext_ref):
    i = pl.program_id(1)
    halo = prev_ref.shape[1]
    has_prev = jnp.logical_and(i != 0, i != n_lat_tiles)
    has_next = jnp.logical_and(i != n_lat_tiles - 1, i != nt - 1)
    ext_ref[0:halo, :] = prev_ref[0] * has_prev.astype(F32)
    ext_ref[halo:halo + ROW_TILE, :] = main_ref[0]
    ext_ref[halo + ROW_TILE:, :] = next_ref[0] * has_next.astype(F32)
    pad = GCONV // 2
    acc = None
    for j in range(GCONV):
        start = halo - pad + j
        term = ext_ref[start:start + ROW_TILE, :] * cw_ref[j:j + 1, :]
        acc = term if acc is None else acc + term
    qkv = _silu(acc)
    for h in range(GH):
        q = qkv[:, h * GDK:(h + 1) * GDK]
        q_ref[0, :, h * GDK:(h + 1) * GDK] = (q * (lax.rsqrt(jnp.sum(q * q, axis=-1, keepdims=True) + NORM_EPS)
                                                 * (GDK ** -0.5))).astype(q_ref.dtype)
        k = qkv[:, G_QK + h * GDK:G_QK + (h + 1) * GDK]
        k_ref[0, :, h * GDK:(h + 1) * GDK] = (
            k * lax.rsqrt(jnp.sum(k * k, axis=-1, keepdims=True) + NORM_EPS)).astype(k_ref.dtype)
    v_ref[0] = qkv[:, 2 * G_QK:].astype(v_ref.dtype)
    gt = gt_ref[0]
    beta_ref[0] = jax.nn.sigmoid(gt[:, :LANES])
    a = gt[:, LANES:] + dtb_ref[...]
    softplus = jnp.maximum(a, 0.0) + jnp.log(1.0 + jnp.exp(-jnp.abs(a)))
    g = -jnp.exp(alog_ref[...]) * softplus
    lo = lo_ref[...]
    up = up_ref[...]
    fwd = None
    bwd = None
    for part in _split3(g):
        f = _dot(lo, part)
        r = _dot(up, part)
        fwd = f if fwd is None else fwd + f
        bwd = r if bwd is None else bwd + r
    lane = lax.broadcasted_iota(jnp.int32, g.shape, 1)
    gc_ref[0] = jnp.where(lane < GH, fwd, bwd)


def _prep(qkv, gt, conv_w8, alog_v, dtb_v, tri_lo, tri_up, n_lat_tiles):
    bsz, lc, _ = qkv.shape
    nt = lc // ROW_TILE
    halo = 8
    per = ROW_TILE // halo
    last = lc // halo - 1

    def row_spec(width):
        return pl.BlockSpec((1, ROW_TILE, width), lambda b, i: (b, i, 0))

    def const_spec(shape):
        return pl.BlockSpec(shape, lambda b, i: tuple(0 for _ in shape))

    outs = [(G_QK, BF16), (G_QK, BF16), (GH * GDV, BF16), (LANES, F32), (LANES, F32)]
    return pl.pallas_call(
        functools.partial(_prep_kernel, n_lat_tiles, nt),
        grid=(bsz, nt),
        in_specs=[row_spec(G_QKV),
                  pl.BlockSpec((1, halo, G_QKV), lambda b, i: (b, jnp.maximum(i * per - 1, 0), 0)),
                  pl.BlockSpec((1, halo, G_QKV), lambda b, i: (b, jnp.minimum((i + 1) * per, last), 0)),
                  row_spec(2 * LANES),
                  const_spec((8, G_QKV)), const_spec((1, LANES)), const_spec((1, LANES)),
                  const_spec((ROW_TILE, ROW_TILE)), const_spec((ROW_TILE, ROW_TILE))],
        out_specs=[row_spec(w) for w, _ in outs],
        out_shape=[jax.ShapeDtypeStruct((bsz, lc, w), dt) for w, dt in outs],
        scratch_shapes=[pltpu.VMEM((ROW_TILE + 2 * halo, G_QKV), F32)],
        compiler_params=_cparams(("arbitrary", "arbitrary")),
        name="prep",
    )(qkv, qkv, qkv, gt, conv_w8, alog_v, dtb_v, tri_lo, tri_up)


def _chunk_masks(n, rev):
    row = lax.broadcasted_iota(jnp.int32, (n, n), 0)
    col = lax.broadcasted_iota(jnp.int32, (n, n), 1)
    same = (row // CHUNK) == (col // CHUNK)
    incl = jnp.logical_and(same, (col >= row) if rev else (col <= row))
    strict = jnp.logical_and(same, (col > row) if rev else (col < row))
    return incl, strict


def _decay(gcol, grow, incl):
    return jnp.where(incl, jnp.exp(jnp.minimum(gcol - grow, 0.0)), 0.0)


def _chunk_kernel(k_ref, v_ref, beta_ref, gc_ref, uw_ref):
    n = k_ref.shape[1]
    gc = gc_ref[0]
    gct = gc.T
    beta = beta_ref[0]
    row = lax.broadcasted_iota(jnp.int32, (n, n), 0)
    col = lax.broadcasted_iota(jnp.int32, (n, n), 1)
    eye = (row == col).astype(F32)
    sizes = [2 ** j for j in range(1, int(math.log2(CHUNK)))]
    same = {r: (row // r) == (col // r) for r in sizes + [CHUNK]}
    a_all, t_all, x_all = [], [], []
    for d in range(2):
        incl, strict = _chunk_masks(n, d == 1)
        for h in range(GH):
            c = d * GH + h
            k_b = k_ref[0, :, h * GDK:(h + 1) * GDK]
            k = k_b.astype(F32)
            v = v_ref[0, :, h * GDV:(h + 1) * GDV].astype(F32)
            bcol = beta[:, c:c + 1]
            gcol = gc[:, c:c + 1]
            grow = gct[c:c + 1, :]
            kb = k * bcol
            a = jnp.where(strict, _dot_nt(kb.astype(BF16), k_b) * _decay(gcol, grow, incl), 0.0)
            a_all.append(a.astype(BF16))
            t_all.append(eye - jnp.where(same[2], a, 0.0))
            x_all.append(jnp.concatenate([v * bcol, kb * jnp.exp(gcol)], axis=1).astype(BF16))
    n_prob = len(a_all)
    zero = jnp.zeros((n, n), BF16)
    for r in sizes:
        off_mask = jnp.logical_and(same[2 * r], jnp.logical_not(same[r]))
        t_b = [t.astype(BF16) for t in t_all]
        inner = [_dot(jnp.where(off_mask, a_all[c], zero), t_b[c]).astype(BF16) for c in range(n_prob)]
        t_all = [t_all[c] - _dot(t_b[c], inner[c]) for c in range(n_prob)]
    for c in range(n_prob):
        uw_ref[0, c] = _dot(t_all[c].astype(BF16), x_all[c]).astype(uw_ref.dtype)


def _chunk(k, v, beta, gc):
    bsz, lc, _ = k.shape
    nt = lc // ROW_TILE

    def row_spec(width):
        return pl.BlockSpec((1, ROW_TILE, width), lambda b, i: (b, i, 0))

    return pl.pallas_call(
        _chunk_kernel,
        grid=(bsz, nt),
        in_specs=[row_spec(G_QK), row_spec(GH * GDV), row_spec(LANES), row_spec(LANES)],
        out_specs=pl.BlockSpec((1, 2 * GH, ROW_TILE, GDV + GDK), lambda b, i: (b, 0, i, 0)),
        out_shape=jax.ShapeDtypeStruct((bsz, 2 * GH, lc, GDV + GDK), BF16),
        compiler_params=_cparams(("arbitrary", "arbitrary")),
        name="chunk",
    )(k, v, beta, gc)


def _scan_kernel(qf_ref, kf_ref, gf_ref, uwf_ref, qr_ref, kr_ref, gr_ref, uwr_ref,
                 of_ref, or_ref, s_ref):
    @pl.when(pl.program_id(1) == 0)
    def _():
        s_ref[...] = jnp.zeros_like(s_ref)

    n_chunks = ROW_TILE // CHUNK
    n_batch = qf_ref.shape[0]
    row = lax.broadcasted_iota(jnp.int32, (CHUNK, CHUNK), 0)
    col = lax.broadcasted_iota(jnp.int32, (CHUNK, CHUNK), 1)
    dirs = ((qf_ref, kf_ref, gf_ref, uwf_ref, of_ref), (qr_ref, kr_ref, gr_ref, uwr_ref, or_ref))
    chains = [(bb, d, h) for bb in range(n_batch) for d in range(2) for h in range(GH)]
    state = [s_ref[i] for i in range(len(chains))]

    def chunk_rows(step, d):
        ci = n_chunks - 1 - step if d == 1 else step
        return slice(ci * CHUNK, (ci + 1) * CHUNK)

    pre = []
    for step in range(n_chunks):
        gates = {}
        for bb in range(n_batch):
            for d in range(2):
                gc = dirs[d][2][bb, chunk_rows(step, d), :]
                gates[bb, d] = (gc, gc.T)
        per_chain = []
        for bb, d, h in chains:
            q_ref, k_ref, _, uw_ref, _ = dirs[d]
            rev = d == 1
            incl = (col >= row) if rev else (col <= row)
            rows = chunk_rows(step, d)
            gc, gct = gates[bb, d]
            c = d * GH + h
            q_b = q_ref[bb, rows, h * GDK:(h + 1) * GDK]
            k_b = k_ref[bb, rows, h * GDK:(h + 1) * GDK]
            q = q_b.astype(F32)
            k = k_b.astype(F32)
            gcol = gc[:, c:c + 1]
            grow = gct[c:c + 1, :]
            last = 0 if rev else CHUNK - 1
            g_last = gcol[last:last + 1, :]
            a_qk = (_dot_nt(q_b, k_b) * _decay(gcol, grow, incl)).astype(BF16)
            wq = jnp.concatenate([uw_ref[bb, h, rows, GDV:], (q * jnp.exp(gcol)).astype(BF16)], axis=0)
            k_tail_t = (k * jnp.exp(g_last - gcol)).T.astype(BF16)
            per_chain.append((a_qk, wq, k_tail_t, jnp.exp(g_last)))
        pre.append(per_chain)

    for step in range(n_chunks):
        ws = [_dot(pre[step][i][1], state[i].astype(BF16)) for i in range(len(chains))]
        v_new = [(dirs[d][3][bb, h, chunk_rows(step, d), :GDV].astype(F32) - ws[i][:CHUNK]).astype(BF16)
                 for i, (bb, d, h) in enumerate(chains)]
        for i, (bb, d, h) in enumerate(chains):
            a_qk, _, k_tail_t, decay_last = pre[step][i]
            dirs[d][4][bb, chunk_rows(step, d), h * GDV:(h + 1) * GDV] = (
                ws[i][CHUNK:] + _dot(a_qk, v_new[i])).astype(BF16)
            state[i] = state[i] * decay_last + _dot(k_tail_t, v_new[i])
    for i in range(len(chains)):
        s_ref[i] = state[i]


def _scan(q, k, gc, uw, n_lat_tiles):
    bsz, lc, _ = q.shape
    nt = lc // ROW_TILE
    n_ctx_tiles = nt - n_lat_tiles

    def fwd(b, i):
        return jnp.where(i < n_ctx_tiles, n_lat_tiles + i, i - n_ctx_tiles)

    def bwd(b, i):
        return jnp.where(i < n_ctx_tiles, nt - 1 - i, n_lat_tiles - 1 - (i - n_ctx_tiles))

    nb = SCAN_BATCH if bsz % SCAN_BATCH == 0 else 1

    def specs(tile):
        return [pl.BlockSpec((nb, ROW_TILE, G_QK), lambda b, i: (b, tile(b, i), 0)),
                pl.BlockSpec((nb, ROW_TILE, G_QK), lambda b, i: (b, tile(b, i), 0)),
                pl.BlockSpec((nb, ROW_TILE, LANES), lambda b, i: (b, tile(b, i), 0))]

    def uw_spec(tile, d):
        return pl.BlockSpec((nb, GH, ROW_TILE, GDV + GDK), lambda b, i: (b, d, tile(b, i), 0))

    def out_spec(tile):
        return pl.BlockSpec((nb, ROW_TILE, GH * GDV), lambda b, i: (b, tile(b, i), 0))

    return pl.pallas_call(
        _scan_kernel,
        grid=(bsz // nb, nt),
        in_specs=specs(fwd) + [uw_spec(fwd, 0)] + specs(bwd) + [uw_spec(bwd, 1)],
        out_specs=[out_spec(fwd), out_spec(bwd)],
        out_shape=[jax.ShapeDtypeStruct((bsz, lc, GH * GDV), BF16)] * 2,
        scratch_shapes=[pltpu.VMEM((nb * 2 * GH, GDK, GDV), F32)],
        compiler_params=_cparams(("arbitrary", "arbitrary")),
        name="scan",
    )(q, k, gc, uw, q, k, gc, uw)


def _attn_kernel(lam_init, q_ref, k_ref, vt_ref, lam_ref, nw_ref, o_ref):
    lam = lam_ref[...]
    lam_full = (jnp.exp(jnp.sum(lam[0:1] * lam[1:2], axis=-1, keepdims=True))
                - jnp.exp(jnp.sum(lam[2:3] * lam[3:4], axis=-1, keepdims=True)) + lam_init)
    q = q_ref[0]
    tq = q.shape[0]
    n_keys = k_ref.shape[1]
    lane = lax.broadcasted_iota(jnp.int32, q.shape, 1)
    zero = jnp.zeros_like(q)
    qc = [jnp.where(lane < DDK, q, zero), jnp.where(lane >= DDK, q, zero)]
    m = [jnp.full((1, tq), -jnp.inf, F32) for _ in range(2)]
    l = [jnp.zeros((1, tq), F32) for _ in range(2)]
    acc = [jnp.zeros((DDV, tq), F32) for _ in range(2)]
    n_tiles = n_keys // KEY_TILE
    bounds = [t * KEY_TILE for t in range(n_tiles)] + [n_keys]

    def scores(t):
        k_t = k_ref[0, bounds[t]:bounds[t + 1], :]
        tiles = [_dot_nt(k_t, qc[c]) for c in range(2)]
        return [(s, jnp.max(s, axis=0, keepdims=True)) for s in tiles]

    pending = [scores(t) for t in range(min(SCORE_LOOKAHEAD, n_tiles))]
    for t in range(n_tiles):
        s_cur = pending.pop(0)
        if t + SCORE_LOOKAHEAD < n_tiles:
            pending.append(scores(t + SCORE_LOOKAHEAD))
        width = bounds[t + 1] - bounds[t]
        vt_t = jnp.concatenate([vt_ref[0, :, bounds[t]:bounds[t + 1]], jnp.ones((16, width), BF16)], axis=0)
        for c in range(2):
            s, s_max = s_cur[c]
            m_new = jnp.maximum(m[c], s_max)
            p = jnp.exp2(s - m_new)
            scale = jnp.exp2(m[c] - m_new)
            pv = _dot(vt_t, p.astype(BF16))
            l[c] = scale * l[c] + pv[DDV:DDV + 1]
            acc[c] = scale * acc[c] + pv[:DDV]
            m[c] = m_new
    o = acc[0] / l[0] - lam_full * (acc[1] / l[1])
    y = o * lax.rsqrt(jnp.mean(o * o, axis=0, keepdims=True) + NORM_EPS)
    o_ref[0] = (y.T * nw_ref[...] * (1.0 - lam_init)).astype(o_ref.dtype)


def _attn(dq, dk, dvt, lam, norm_w, lam_init, lq, tq):
    bsz, lc, _ = dq.shape
    return pl.pallas_call(
        functools.partial(_attn_kernel, lam_init),
        grid=(bsz, DH, lq // tq),
        in_specs=[pl.BlockSpec((1, tq, 2 * DDK), lambda b, h, i: (b, i, h)),
                  pl.BlockSpec((1, lc, 2 * DDK), lambda b, h, i: (b, 0, h)),
                  pl.BlockSpec((1, DDV, lc), lambda b, h, i: (b, h, 0)),
                  pl.BlockSpec((4, DDK), lambda b, h, i: (0, 0)),
                  pl.BlockSpec((1, DDV), lambda b, h, i: (0, 0))],
        out_specs=pl.BlockSpec((1, tq, DDV), lambda b, h, i: (b, i, h)),
        out_shape=jax.ShapeDtypeStruct((bsz, lq, DH * DDV), BF16),
        compiler_params=_cparams(("arbitrary", "arbitrary", "arbitrary")),
        name="attn",
    )(dq, dk, dvt, lam, norm_w.reshape(1, DDV))


def _out_kernel(alpha, of_ref, or_ref, z_ref, yd_ref, x_ref, g1_ref, sh2_ref, sc2_ref, gnw_ref,
                wo_ref, lng_ref, lnb_ref, wr_ref, br_ref, x1_ref, h2_ref, lg_ref):
    o = of_ref[0].astype(F32) + or_ref[0].astype(F32)
    z = z_ref[0].astype(F32)
    heads = []
    for h in range(GH):
        oh = o[:, h * GDV:(h + 1) * GDV]
        heads.append((oh * lax.rsqrt(jnp.mean(oh * oh, axis=-1, keepdims=True) + NORM_EPS) * gnw_ref[...]
                      * _silu(z[:, h * GDV:(h + 1) * GDV])).astype(BF16))
    mix = jnp.concatenate(heads + [yd_ref[0]], axis=1)
    x1 = _normalize(alpha * x_ref[0] + g1_ref[0] * _dot(mix, wo_ref[...])) * lng_ref[...] + lnb_ref[...]
    x1_ref[0] = x1
    h2 = _normalize(x1) * (1.0 + sc2_ref[0]) + sh2_ref[0]
    h2_ref[0] = h2
    h_hi = h2.astype(BF16)
    h_lo = (h2 - h_hi.astype(F32)).astype(BF16)
    both = _dot(h_hi, wr_ref[...])
    lg_ref[0] = br_ref[...] + both[:, :LANES] + both[:, LANES:] + _dot(h_lo, wr_ref[:, :LANES])


def _out(o_f, o_r, z, y_diff, xc, mod3, gdn_norm_w, w_out, ln_g, ln_b, w_router3, b_router, alpha):
    bsz, lq, _ = y_diff.shape
    tile = OUT_TILE if lq % OUT_TILE == 0 else ROW_TILE
    nt = lq // tile

    def row_spec(width):
        return pl.BlockSpec((1, tile, width), lambda b, i: (b, i, 0))

    def mod_spec(col):
        return pl.BlockSpec((1, 1, D), lambda b, i: (b, 0, col))

    def const_spec(shape):
        return pl.BlockSpec(shape, lambda b, i: tuple(0 for _ in shape))

    return pl.pallas_call(
        functools.partial(_out_kernel, alpha),
        grid=(bsz, nt),
        in_specs=[row_spec(GH * GDV), row_spec(GH * GDV), row_spec(G_Z), row_spec(DH * DDV), row_spec(D),
                  mod_spec(2), mod_spec(3), mod_spec(4),
                  const_spec((1, GDV)), const_spec((G_Z + DH * DDV, D)),
                  const_spec((1, D)), const_spec((1, D)),
                  const_spec((D, 2 * LANES)), const_spec((1, LANES))],
        out_specs=[row_spec(D), row_spec(D), row_spec(LANES)],
        out_shape=[jax.ShapeDtypeStruct((bsz, lq, D), F32), jax.ShapeDtypeStruct((bsz, lq, D), F32),
                   jax.ShapeDtypeStruct((bsz, lq, LANES), F32)],
        compiler_params=_cparams(("arbitrary", "arbitrary")),
        name="out",
    )(o_f, o_r, z, y_diff, xc, mod3, mod3, mod3, gdn_norm_w.reshape(1, GDV), w_out,
      ln_g.reshape(1, D), ln_b.reshape(1, D), w_router3, b_router)


def _route_kernel(lg_ref, tri_ref, ids_ref, gate_ref, cnt_ref, run_ref):
    @pl.when(pl.program_id(0) == 0)
    def _():
        run_ref[...] = jnp.zeros_like(run_ref)

    lgt = lg_ref[...].T[:ROUTE_ROWS]
    n = lgt.shape[1]
    row_i = lax.broadcasted_iota(jnp.int32, lgt.shape, 0)
    row = row_i.astype(F32)
    neg = jnp.float32(-jnp.inf)

    def first_max(mask):
        masked = jnp.where(mask, lgt, neg)
        m = jnp.max(masked, axis=0, keepdims=True)
        idx = jnp.min(jnp.where(jnp.logical_and(mask, masked == m), row, float(LANES)), axis=0, keepdims=True)
        return m, idx

    is_group = row_i < N_GROUPS
    m_g, grp = first_max(is_group)
    p_g = 1.0 / jnp.sum(jnp.where(is_group, jnp.exp(lgt - m_g), 0.0), axis=0, keepdims=True)
    e_row = row - float(N_GROUPS)
    lo_e = grp * float(E_PER_GROUP)
    in_group = jnp.logical_and(e_row >= lo_e, e_row < lo_e + float(E_PER_GROUP))
    l0, i0 = first_max(in_group)
    l1, i1 = first_max(jnp.logical_and(in_group, row != i0))
    r = jnp.exp(l1 - l0)
    gate0 = p_g / (1.0 + r)
    gate1 = p_g * r / (1.0 + r)
    e0 = i0 - float(N_GROUPS)
    e1 = i1 - float(N_GROUPS)
    oh0 = (e_row == e0).astype(F32)
    oh1 = (e_row == e1).astype(F32)
    both = oh0 + oh1
    before = _dot(both.astype(BF16), tri_ref[...]) + run_ref[...]
    rank0 = jnp.sum(oh0 * before, axis=0, keepdims=True)
    rank1 = jnp.sum(oh1 * before, axis=0, keepdims=True)
    run_ref[...] = run_ref[...] + jnp.sum(both, axis=1, keepdims=True)
    cnt_ref[...] = jnp.broadcast_to(run_ref[...], cnt_ref.shape)
    ids_ref[...] = jnp.concatenate([e0, e1, rank0, rank1, jnp.zeros((4, n), F32)], axis=0).astype(jnp.int32)
    gate_ref[...] = jnp.concatenate([gate0, gate1, jnp.zeros((LANES - 2, n), F32)], axis=0).T


def _route(logits, tri_strict):
    t = logits.shape[0]
    tile = tri_strict.shape[0]
    return pl.pallas_call(
        _route_kernel,
        grid=(t // tile,),
        in_specs=[pl.BlockSpec((tile, LANES), lambda i: (i, 0)),
                  pl.BlockSpec((tile, tile), lambda i: (0, 0))],
        out_specs=[pl.BlockSpec((8, tile), lambda i: (0, i)),
                   pl.BlockSpec((tile, LANES), lambda i: (i, 0)),
                   pl.BlockSpec((ROUTE_ROWS, LANES), lambda i: (0, 0))],
        out_shape=[jax.ShapeDtypeStruct((8, t), jnp.int32), jax.ShapeDtypeStruct((t, LANES), F32),
                   jax.ShapeDtypeStruct((ROUTE_ROWS, LANES), F32)],
        scratch_shapes=[pltpu.VMEM((ROUTE_ROWS, 1), F32)],
        compiler_params=_cparams(("arbitrary",)),
        name="route",
    )(logits, tri_strict)


ROUTE_TILE = 1024
ROUTE_ROWS = 40
DMA_TOKENS = 512
COMBINE_LAG = 2


def _dispatch_kernel(tail_ref, used_ref, dest_ref, h_hbm, buf_hbm, zero_ref, src_ref, load_sem, row_sem, zsem):
    i = pl.program_id(0)
    n_tiles = pl.num_programs(0)
    n_slots = src_ref.shape[0]
    groups = DMA_TOKENS // 8

    def load(tile):
        slot = lax.rem(tile, n_slots)
        return pltpu.make_async_copy(h_hbm.at[pl.ds(tile * groups, groups)], src_ref.at[slot], load_sem.at[slot])

    def wait_rows(tile):
        slot = lax.rem(tile, n_slots)
        for _ in range(2):
            pltpu.make_async_copy(buf_hbm.at[pl.ds(0, DMA_TOKENS)], buf_hbm.at[pl.ds(0, DMA_TOKENS)],
                                  row_sem.at[slot]).wait()

    @pl.when(i == 0)
    def _():
        load(0).start()
        zero_ref[...] = jnp.zeros_like(zero_ref)
        n_blocks = buf_hbm.shape[0] // FFN_BLOCK

        def block_copy(b):
            return pltpu.make_async_copy(zero_ref, buf_hbm.at[pl.ds(b * FFN_BLOCK, FFN_BLOCK)], zsem)

        for e in range(N_EXPERTS):
            @pl.when(tail_ref[e] >= 0)
            def _():
                block_copy(tail_ref[e]).start()
        lax.fori_loop(used_ref[0], n_blocks, lambda b, carry: (block_copy(b).start(), carry)[1], 0)
        for e in range(N_EXPERTS):
            @pl.when(tail_ref[e] >= 0)
            def _():
                block_copy(tail_ref[e]).wait()
        lax.fori_loop(used_ref[0], n_blocks, lambda b, carry: (block_copy(b).wait(), carry)[1], 0)

    @pl.when(i + 1 < n_tiles)
    def _():
        load(i + 1).start()

    load(i).wait()
    slot = lax.rem(i, n_slots)

    def start(grp, carry):
        for u in range(8):
            for j in range(2):
                dst = dest_ref[j * DMA_TOKENS + 8 * grp + u]
                pltpu.make_async_copy(src_ref.at[slot, grp, pl.ds(u, 1)], buf_hbm.at[pl.ds(dst, 1)],
                                      row_sem.at[slot]).start()
        return carry

    lax.fori_loop(0, groups, start, 0)

    @pl.when(i > 0)
    def _():
        wait_rows(i - 1)

    @pl.when(i == n_tiles - 1)
    def _():
        wait_rows(i)


def _dispatch(tail_block, n_used, dest_flat, h2, n_rows):
    t, d = h2.shape
    grid_spec = pltpu.PrefetchScalarGridSpec(
        num_scalar_prefetch=2,
        grid=(t // DMA_TOKENS,),
        in_specs=[pl.BlockSpec((2 * DMA_TOKENS,), lambda i, tail, used: (i,), memory_space=pltpu.SMEM),
                  pl.BlockSpec(memory_space=pl.ANY)],
        out_specs=pl.BlockSpec(memory_space=pl.ANY),
        scratch_shapes=[pltpu.VMEM((FFN_BLOCK, d), h2.dtype),
                        pltpu.VMEM((3, DMA_TOKENS // 8, 8, d), h2.dtype),
                        pltpu.SemaphoreType.DMA((3,)), pltpu.SemaphoreType.DMA((3,)),
                        pltpu.SemaphoreType.DMA(())],
    )
    return pl.pallas_call(
        _dispatch_kernel,
        grid_spec=grid_spec,
        out_shape=jax.ShapeDtypeStruct((n_rows, d), h2.dtype),
        compiler_params=_cparams(("arbitrary",)),
        name="dispatch",
    )(tail_block, n_used, dest_flat, h2.reshape(t // 8, 8, d))


def _combine_kernel(alpha, dest_ref, ys_hbm, gate_ref, x1_ref, g2_ref, lng_ref, lnb_ref, o_ref, rows_ref, sem):
    i = pl.program_id(0)
    n_tiles = pl.num_programs(0) - COMBINE_LAG
    n = x1_ref.shape[0]
    n_slots = rows_ref.shape[0]
    slot_in = lax.rem(i, n_slots)
    slot_out = lax.rem(i + n_slots - COMBINE_LAG, n_slots)

    def start_group(grp):
        for u in range(8):
            for j in range(2):
                src = dest_ref[j * DMA_TOKENS + 8 * grp + u]
                pltpu.make_async_copy(ys_hbm.at[pl.ds(src, 1)], rows_ref.at[slot_in, j, grp, pl.ds(u, 1)],
                                      sem.at[slot_in]).start()

    def finish():
        gate = gate_ref[...]
        r0 = rows_ref[slot_out, 0].reshape(n, -1)
        r1 = rows_ref[slot_out, 1].reshape(n, -1)
        y = gate[:, 0:1] * r0 + gate[:, 1:2] * r1
        o_ref[...] = _normalize(alpha * x1_ref[...] + g2_ref[0] * y) * lng_ref[...] + lnb_ref[...]

    @pl.when(i >= COMBINE_LAG)
    def _():
        for j in range(2):
            pltpu.make_async_copy(ys_hbm.at[pl.ds(0, n)], ys_hbm.at[pl.ds(0, n)], sem.at[slot_out]).wait()

    @pl.when(i < COMBINE_LAG)
    def _():
        lax.fori_loop(0, n // 8, lambda grp, carry: (start_group(grp), carry)[1], 0)

    @pl.when(jnp.logical_and(i >= COMBINE_LAG, i < n_tiles))
    def _():
        for grp in range(n // 8):
            start_group(grp)
        finish()

    @pl.when(i >= n_tiles)
    def _():
        finish()


def _combine(dest_flat, ys, gate, x1, mod3, ln_g, ln_b, alpha, rows_per_batch):
    t, d = x1.shape
    per_batch = rows_per_batch // DMA_TOKENS
    n_tiles = t // DMA_TOKENS

    def done(i):
        return jnp.maximum(i - COMBINE_LAG, 0)

    return pl.pallas_call(
        functools.partial(_combine_kernel, alpha),
        grid=(n_tiles + COMBINE_LAG,),
        in_specs=[pl.BlockSpec((2 * DMA_TOKENS,), lambda i: (jnp.minimum(i, n_tiles - 1),), memory_space=pltpu.SMEM),
                  pl.BlockSpec(memory_space=pl.ANY),
                  pl.BlockSpec((DMA_TOKENS, LANES), lambda i: (done(i), 0)),
                  pl.BlockSpec((DMA_TOKENS, d), lambda i: (done(i), 0)),
                  pl.BlockSpec((1, 1, d), lambda i: (done(i) // per_batch, 0, 5)),
                  pl.BlockSpec((1, d), lambda i: (0, 0)),
                  pl.BlockSpec((1, d), lambda i: (0, 0))],
        out_specs=pl.BlockSpec((DMA_TOKENS, d), lambda i: (done(i), 0)),
        out_shape=jax.ShapeDtypeStruct((t, d), F32),
        scratch_shapes=[pltpu.VMEM((COMBINE_LAG + 1, 2, DMA_TOKENS // 8, 8, d), F32),
                        pltpu.SemaphoreType.DMA((COMBINE_LAG + 1,))],
        compiler_params=_cparams(("arbitrary",)),
        name="combine",
    )(dest_flat, ys, gate, x1, mod3, ln_g.reshape(1, d), ln_b.reshape(1, d))


def _ffn_kernel(start_ref, nblk_ref, used_ref, buf_hbm, wg_ref, wu_ref, wd_ref, ys_hbm,
                wgb_ref, wub_ref, wdb_ref, x_buf, o_buf, in_sem, out_sem):
    e = pl.program_id(0)
    nb = nblk_ref[e]
    base = start_ref[e]
    used = used_ref[0]
    n_in = x_buf.shape[0]
    n_out = o_buf.shape[0]
    ahead = n_in - 1

    def rows(g):
        return pl.ds(g * FFN_BLOCK, FFN_BLOCK)

    def in_copy(g):
        slot = lax.rem(g, n_in)
        return pltpu.make_async_copy(buf_hbm.at[rows(g)], x_buf.at[slot], in_sem.at[slot])

    def out_copy(g):
        slot = lax.rem(g, n_out)
        return pltpu.make_async_copy(o_buf.at[slot], ys_hbm.at[rows(g)], out_sem.at[slot])

    @pl.when(e == 0)
    def _():
        for g in range(ahead):
            @pl.when(g < used)
            def _():
                in_copy(g).start()

    @pl.when(nb > 0)
    def _():
        wgb_ref[...] = wg_ref[0].astype(BF16)
        wub_ref[...] = wu_ref[0].astype(BF16)
        wdb_ref[...] = wd_ref[0].astype(BF16)

        def block(j, carry):
            g = base + j
            in_copy(g).wait()

            @pl.when(g + ahead < used)
            def _():
                in_copy(g + ahead).start()

            @pl.when(g >= n_out)
            def _():
                out_copy(g - n_out).wait()

            islot = lax.rem(g, n_in)
            oslot = lax.rem(g, n_out)
            slab = FFN_BLOCK // FFN_SLABS
            xs = [x_buf[islot, s * slab:(s + 1) * slab, :].astype(BF16) for s in range(FFN_SLABS)]
            gates = [_dot(x, wgb_ref[...]) for x in xs]
            ups = [_dot(x, wub_ref[...]) for x in xs]
            hidden = [(_silu(gt) * up).astype(BF16) for gt, up in zip(gates, ups)]
            for s in range(FFN_SLABS):
                o_buf[oslot, s * slab:(s + 1) * slab, :] = _dot(hidden[s], wdb_ref[...])
            out_copy(g).start()
            return carry

        lax.fori_loop(0, nb, block, 0)

    @pl.when(e == pl.num_programs(0) - 1)
    def _():
        for back in range(n_out):
            @pl.when(used - 1 - back >= 0)
            def _():
                out_copy(used - 1 - back).wait()
        n_blocks = ys_hbm.shape[0] // FFN_BLOCK
        o_buf[0] = jnp.zeros(o_buf.shape[1:], o_buf.dtype)

        def clear(b):
            return pltpu.make_async_copy(o_buf.at[0], ys_hbm.at[pl.ds(b * FFN_BLOCK, FFN_BLOCK)], out_sem.at[0])

        lax.fori_loop(used_ref[0], n_blocks, lambda b, carry: (clear(b).start(), carry)[1], 0)
        lax.fori_loop(used_ref[0], n_blocks, lambda b, carry: (clear(b).wait(), carry)[1], 0)


def _ffn(seg_start, seg_blocks, n_used, buf, w_gate, w_up, w_down):
    n_rows, d = buf.shape
    n_experts = w_gate.shape[0]
    grid_spec = pltpu.PrefetchScalarGridSpec(
        num_scalar_prefetch=3,
        grid=(n_experts,),
        in_specs=[pl.BlockSpec(memory_space=pl.ANY),
                  pl.BlockSpec((1, d, D_EXPERT), lambda e, *_: (e, 0, 0)),
                  pl.BlockSpec((1, d, D_EXPERT), lambda e, *_: (e, 0, 0)),
                  pl.BlockSpec((1, D_EXPERT, d), lambda e, *_: (e, 0, 0))],
        out_specs=pl.BlockSpec(memory_space=pl.ANY),
        scratch_shapes=[pltpu.VMEM((d, D_EXPERT), BF16), pltpu.VMEM((d, D_EXPERT), BF16),
                        pltpu.VMEM((D_EXPERT, d), BF16),
                        pltpu.VMEM((FFN_IN_SLOTS, FFN_BLOCK, d), F32), pltpu.VMEM((FFN_OUT_SLOTS, FFN_BLOCK, d), F32),
                        pltpu.SemaphoreType.DMA((FFN_IN_SLOTS,)), pltpu.SemaphoreType.DMA((FFN_OUT_SLOTS,))],
    )
    return pl.pallas_call(
        _ffn_kernel,
        grid_spec=grid_spec,
        out_shape=jax.ShapeDtypeStruct((n_rows, d), F32),
        compiler_params=_cparams(("arbitrary",)),
        name="ffn",
    )(seg_start, seg_blocks, n_used, buf, w_gate, w_up, w_down)


def _rope_tables(seq_len, n_ctx_rows):
    lane = np.arange(LANES)
    within = lane % DDK
    freq = within % 16
    use_col = within >= DDK // 2
    sign = np.where((lane % 32) < 16, -1.0, 1.0).astype(np.float32)
    half = DDK // 2
    inv_freq = (ROPE_THETA ** (-np.arange(0, half, 2, dtype=np.float32) / np.float32(half))).astype(np.float32)
    t = np.arange(seq_len)
    pos = np.where(use_col[None, :], (t % GRID_W)[:, None], (t // GRID_W)[:, None]).astype(np.float32)
    ang = pos * inv_freq[freq][None, :]
    cos = np.concatenate([np.cos(ang), np.ones((n_ctx_rows, LANES), np.float32)], axis=0)
    sin = np.concatenate([np.sin(ang) * sign[None, :], np.zeros((n_ctx_rows, LANES), np.float32)], axis=0)
    return jnp.asarray(cos, F32), jnp.asarray(sin, F32)


def _block_tri(n, block, upper):
    i = np.arange(n)
    same = (i[:, None] // block) == (i[None, :] // block)
    tri = (i[None, :] >= i[:, None]) if upper else (i[None, :] <= i[:, None])
    return jnp.asarray((same & tri).astype(np.float32), dtype=BF16)


def _pack_w_in(w_in):
    off = G_QKV + G_Z
    v_start = G_COLS + 2 * D_QK
    pad = jnp.zeros((D, LANES - G_GATES), w_in.dtype)
    w_gate = jnp.concatenate([w_in[:, off:off + G_GATES], pad, w_in[:, off + G_GATES:G_COLS], pad], axis=1)
    return (w_in[:, :off].astype(BF16), w_gate.astype(BF16), w_in[:, G_COLS:v_start].astype(BF16),
            w_in[:, v_start:].T.astype(BF16))


def kernel(x, c, ctx, c_ctx, w_ada, b_ada, w_in, conv_w, gdn_a_log, gdn_dt_bias, gdn_norm_w, diff_lambda,
           diff_norm_w, w_out, ln1_g, ln1_b, w_router_group, b_router_group, w_router_expert, b_router_expert,
           w_expert_gate, w_expert_up, w_expert_down, ln2_g, ln2_b):
    depth = w_ada.shape[0]
    assert depth == 1, "single-layer block: the context stream never feeds a later layer"
    bsz, seq, _ = x.shape
    n_ctx = ctx.shape[1]
    assert seq % ROW_TILE == 0 and n_ctx % ROW_TILE == 0 and seq % GRID_W == 0 and bsz < 8
    n_lat_tiles = seq // ROW_TILE
    alpha = (2.0 * depth) ** 0.25
    lam_init = 0.8 - 0.6 * math.exp(-0.3 * 0)
    i = 0

    cc = jnp.zeros((8, D), F32).at[:bsz].set(c).at[7].set(c_ctx)
    mod = _ada(cc, w_ada[i], b_ada[i])
    mod3 = mod.reshape(8, 1, 6 * D)

    cos_t, sin_t = _rope_tables(seq, n_ctx)
    qkv, z, gt, dq, dk, dvt = _proj(x, ctx, mod3, cos_t, sin_t, *_pack_w_in(w_in[i]))

    conv_w8 = jnp.zeros((8, G_QKV), F32).at[:GCONV].set(conv_w[i])
    alog_v = jnp.zeros((1, LANES), F32).at[0, :G_GATES].set(gdn_a_log[i].reshape(-1))
    dtb_v = jnp.zeros((1, LANES), F32).at[0, :G_GATES].set(gdn_dt_bias[i].reshape(-1))
    gq, gk, gv, beta, gc = _prep(qkv, gt, conv_w8, alog_v, dtb_v,
                                 _block_tri(ROW_TILE, CHUNK, False), _block_tri(ROW_TILE, CHUNK, True), n_lat_tiles)
    uw = _chunk(gk, gv, beta, gc)
    o_f, o_r = _scan(gq, gk, gc, uw, n_lat_tiles)

    y_diff = _attn(dq, dk, dvt, diff_lambda[i], diff_norm_w[i], lam_init, seq, tq=min(ATTN_Q_TILE, seq))

    w_router = jnp.zeros((D, LANES), F32).at[:, :N_GROUPS].set(w_router_group[i]) \
        .at[:, N_GROUPS:N_GROUPS + N_EXPERTS].set(w_router_expert[i])
    w_hi = w_router.astype(BF16)
    w_router3 = jnp.concatenate([w_hi, (w_router - w_hi.astype(F32)).astype(BF16)], axis=1)
    b_router = jnp.zeros((1, LANES), F32).at[0, :N_GROUPS].set(b_router_group[i]) \
        .at[0, N_GROUPS:N_GROUPS + N_EXPERTS].set(b_router_expert[i])
    x1, h2, logits = _out(o_f, o_r, z, y_diff, x, mod3, gdn_norm_w[i], w_out[i].astype(BF16), ln1_g[i], ln1_b[i],
                          w_router3, b_router, alpha)

    t = bsz * seq
    tri_strict = jnp.asarray(np.triu(np.ones((ROUTE_TILE, ROUTE_TILE), np.float32), 1), dtype=BF16)
    ids, gate, counts = _route(logits.reshape(t, LANES), tri_strict)

    cnt = counts[N_GROUPS:N_GROUPS + N_EXPERTS, 0].astype(jnp.int32)
    padded = (cnt + FFN_BLOCK - 1) // FFN_BLOCK * FFN_BLOCK
    p_end = jnp.cumsum(padded)
    p_start = p_end - padded
    n_blocks = (2 * t) // FFN_BLOCK + N_EXPERTS
    n_used = (p_end[-1:] // FFN_BLOCK).astype(jnp.int32)
    tail_block = jnp.where(cnt > 0, p_end // FFN_BLOCK - 1, -1).astype(jnp.int32)
    one_hot = ids[0:2, :, None] == jnp.arange(N_EXPERTS, dtype=jnp.int32)
    row_start = jnp.sum(jnp.where(one_hot, p_start, 0), axis=-1)
    dest = jnp.swapaxes((row_start + ids[2:4]).reshape(2, t // DMA_TOKENS, DMA_TOKENS), 0, 1)
    dest = dest.reshape(-1).astype(jnp.int32)

    buf = _dispatch(tail_block, n_used, dest, h2.reshape(t, D), n_blocks * FFN_BLOCK)
    ys = _ffn((p_start // FFN_BLOCK).astype(jnp.int32), (padded // FFN_BLOCK).astype(jnp.int32), n_used, buf,
              w_expert_gate[i], w_expert_up[i], w_expert_down[i])
    out = _combine(dest, ys, gate, x1.reshape(t, D), mod3, ln2_g[i], ln2_b[i], alpha, seq)
    return out.reshape(bsz, seq, D)
```

```python
import functools
import math

import numpy as np
import jax
import jax.numpy as jnp
from jax import lax
from jax.experimental import pallas as pl
from jax.experimental.pallas import tpu as pltpu

F32 = jnp.float32
BF16 = jnp.bfloat16

D = 1024
GRID_W = 64
ROPE_THETA = 10000.0
GH = 4
GDK = 128
GDV = 128
GCONV = 5
DH = 4
DDK = 64
DDV = 128
G_QK = GH * GDK
G_QKV = 2 * G_QK + GH * GDV
G_Z = GH * GDV
G_GATES = 2 * GH
G_COLS = G_QKV + G_Z + 2 * G_GATES
D_QK = DH * 2 * DDK
N_GROUPS = 4
E_PER_GROUP = 8
N_EXPERTS = N_GROUPS * E_PER_GROUP
D_EXPERT = 512
LN_EPS = 1e-5
NORM_EPS = 1e-6

LANES = 128
ROW_TILE = 256
CHUNK = 64
OUT_TILE = 512
CHUNK_BATCH = 2
SCAN_BATCH = 2
FFN_BLOCK = 256
FFN_SLABS = 2
FFN_IN_SLOTS = 4
FFN_OUT_SLOTS = 3
KEY_TILE = 512
ATTN_Q_TILE = 1024
SCORE_LOOKAHEAD = 2
VMEM_LIMIT = 56 * 1024 * 1024

Q_SCALE = DDK ** -0.5 * math.log2(math.e)


def _cparams(sem):
    return pltpu.CompilerParams(dimension_semantics=sem, vmem_limit_bytes=VMEM_LIMIT)


def _dot(a, b):
    return jnp.dot(a, b, preferred_element_type=F32)


def _dot_nt(a, b):
    return lax.dot_general(a, b, (((1,), (1,)), ((), ())), preferred_element_type=F32)


def _dot_tn(a, b):
    return lax.dot_general(a, b, (((0,), (0,)), ((), ())), preferred_element_type=F32)


def _normalize(x):
    mu = jnp.mean(x, axis=-1, keepdims=True)
    xc = x - mu
    var = jnp.mean(xc * xc, axis=-1, keepdims=True)
    return xc * lax.rsqrt(var + LN_EPS)


def _silu(x):
    return x * jax.nn.sigmoid(x)


def _split3(x):
    hi = x.astype(BF16)
    r = x - hi.astype(F32)
    mid = r.astype(BF16)
    lo = (r - mid.astype(F32)).astype(BF16)
    return hi, mid, lo


def _ada_kernel(c_ref, w_ref, b_ref, o_ref):
    s = _silu(c_ref[...])
    s_hi = s.astype(BF16)
    s_lo = (s - s_hi.astype(F32)).astype(BF16)
    w = w_ref[...]
    w_hi = w.astype(BF16)
    w_lo = (w - w_hi.astype(F32)).astype(BF16)
    rows = s.shape[0]
    both = _dot(jnp.concatenate([s_hi, s_lo], axis=0), w_hi)
    o_ref[...] = both[:rows] + both[rows:] + _dot(s_hi, w_lo) + b_ref[...]


def _ada(cc, w_ada, b_ada):
    n = w_ada.shape[1]
    bn = 1024
    return pl.pallas_call(
        _ada_kernel,
        grid=(n // bn,),
        in_specs=[pl.BlockSpec((8, D), lambda j: (0, 0)),
                  pl.BlockSpec((D, bn), lambda j: (0, j)),
                  pl.BlockSpec((1, bn), lambda j: (0, j))],
        out_specs=pl.BlockSpec((8, bn), lambda j: (0, j)),
        out_shape=jax.ShapeDtypeStruct((8, n), F32),
        compiler_params=_cparams(("arbitrary",)),
        name="ada",
    )(cc, w_ada, b_ada.reshape(1, n))


def _rope(v, cos, sin, low_half):
    fwd = pltpu.roll(v, LANES - 16, axis=1)
    bwd = pltpu.roll(v, 16, axis=1)
    return v * cos + jnp.where(low_half, fwd, bwd) * sin


def _proj_kernel(n_lat_tiles, x_ref, ctx_ref, sh_ref, sc_ref, cos_ref, sin_ref, wg_ref, wgate_ref, wqk_ref, wvt_ref,
                 qkv_ref, z_ref, gt_ref, dq_ref, dk_ref, dvt_ref):
    rows = jnp.where(pl.program_id(1) >= n_lat_tiles, ctx_ref[0], x_ref[0])
    h = (_normalize(rows) * (1.0 + sc_ref[0]) + sh_ref[0]).astype(BF16)
    q_all = _dot(h, wqk_ref[:, :D_QK])
    k_all = _dot(h, wqk_ref[:, D_QK:])
    qkv_ref[0] = _dot(h, wg_ref[:, :G_QKV])
    z_ref[0] = _dot(h, wg_ref[:, G_QKV:]).astype(z_ref.dtype)
    gt_ref[0] = _dot(h, wgate_ref[...])
    cos = cos_ref[...]
    sin = sin_ref[...]
    lane = lax.broadcasted_iota(jnp.int32, cos.shape, 1)
    low_half = (lane % 32) < 16
    for j in range(D_QK // LANES):
        slab = slice(j * LANES, (j + 1) * LANES)
        dq_ref[0, :, slab] = (_rope(q_all[:, slab], cos, sin, low_half) * Q_SCALE).astype(BF16)
        dk_ref[0, :, slab] = _rope(k_all[:, slab], cos, sin, low_half).astype(BF16)
    dvt_ref[0] = _dot_nt(wvt_ref[...], h).astype(BF16)


def _proj(x, ctx, mod3, cos_t, sin_t, w_gdn, w_gate, w_qk, wv_t):
    bsz, seq, _ = x.shape
    lc = seq + ctx.shape[1]
    nt = lc // ROW_TILE
    n_lat_tiles = seq // ROW_TILE
    ctx_row = mod3.shape[0] - 1

    def mod_idx(col):
        return lambda b, i: (jnp.where(i >= n_lat_tiles, ctx_row, b), 0, col)

    def row_spec(width):
        return pl.BlockSpec((1, ROW_TILE, width), lambda b, i: (b, i, 0))

    outs = [(G_QKV, F32), (G_Z, BF16), (2 * LANES, F32), (D_QK, BF16), (D_QK, BF16)]
    return pl.pallas_call(
        functools.partial(_proj_kernel, n_lat_tiles),
        grid=(bsz, nt),
        in_specs=[pl.BlockSpec((1, ROW_TILE, D), lambda b, i: (b, jnp.minimum(i, n_lat_tiles - 1), 0)),
                  pl.BlockSpec((1, ROW_TILE, D), lambda b, i: (b, jnp.maximum(i - n_lat_tiles, 0), 0)),
                  pl.BlockSpec((1, 1, D), mod_idx(0)),
                  pl.BlockSpec((1, 1, D), mod_idx(1)),
                  pl.BlockSpec((ROW_TILE, LANES), lambda b, i: (i, 0)),
                  pl.BlockSpec((ROW_TILE, LANES), lambda b, i: (i, 0)),
                  pl.BlockSpec(w_gdn.shape, lambda b, i: (0, 0)),
                  pl.BlockSpec(w_gate.shape, lambda b, i: (0, 0)),
                  pl.BlockSpec(w_qk.shape, lambda b, i: (0, 0)),
                  pl.BlockSpec(wv_t.shape, lambda b, i: (0, 0))],
        out_specs=[row_spec(w) for w, _ in outs]
        + [pl.BlockSpec((1, DH * DDV, ROW_TILE), lambda b, i: (b, 0, i))],
        out_shape=[jax.ShapeDtypeStruct((bsz, lc, w), dt) for w, dt in outs]
        + [jax.ShapeDtypeStruct((bsz, DH * DDV, lc), BF16)],
        compiler_params=_cparams(("arbitrary", "arbitrary")),
        name="proj",
    )(x, ctx, mod3, mod3, cos_t, sin_t, w_gdn, w_gate, w_qk, wv_t)


def _prep_kernel(n_lat_tiles, nt, main_ref, prev_ref, next_ref, gt_ref, cw_ref, alog_ref, dtb_ref,
                 lo_ref, up_ref, q_ref, k_ref, v_ref, beta_ref, gc_ref, ext_ref):
    i = pl.program_id(1)
    halo = prev_ref.shape[1]
    has_prev = jnp.logical_and(i != 0, i != n_lat_tiles)
    has_next = jnp.logical_and(i != n_lat_tiles - 1, i != nt - 1)
    ext_ref[0:halo, :] = prev_ref[0] * has_prev.astype(F32)
    ext_ref[halo:halo + ROW_TILE, :] = main_ref[0]
    ext_ref[halo + ROW_TILE:, :] = next_ref[0] * has_next.astype(F32)
    pad = GCONV // 2
    acc = None
    for j in range(GCONV):
        start = halo - pad + j
        term = ext_ref[start:start + ROW_TILE, :] * cw_ref[j:j + 1, :]
        acc = term if acc is None else acc + term
    qkv = _silu(acc)
    for h in range(GH):
        q = qkv[:, h * GDK:(h + 1) * GDK]
        q_ref[0, :, h * GDK:(h + 1) * GDK] = (q * (lax.rsqrt(jnp.sum(q * q, axis=-1, keepdims=True) + NORM_EPS)
                                                 * (GDK ** -0.5))).astype(q_ref.dtype)
        k = qkv[:, G_QK + h * GDK:G_QK + (h + 1) * GDK]
        k_ref[0, :, h * GDK:(h + 1) * GDK] = (
            k * lax.rsqrt(jnp.sum(k * k, axis=-1, keepdims=True) + NORM_EPS)).astype(k_ref.dtype)
    v_ref[0] = qkv[:, 2 * G_QK:].astype(v_ref.dtype)
    gt = gt_ref[0]
    beta_ref[0] = jax.nn.sigmoid(gt[:, :LANES])
    a = gt[:, LANES:] + dtb_ref[...]
    softplus = jnp.maximum(a, 0.0) + jnp.log(1.0 + jnp.exp(-jnp.abs(a)))
    g = -jnp.exp(alog_ref[...]) * softplus
    lo = lo_ref[...]
    up = up_ref[...]
    fwd = None
    bwd = None
    for part in _split3(g):
        f = _dot(lo, part)
        r = _dot(up, part)
        fwd = f if fwd is None else fwd + f
        bwd = r if bwd is None else bwd + r
    lane = lax.broadcasted_iota(jnp.int32, g.shape, 1)
    gc_ref[0] = jnp.where(lane < GH, fwd, bwd)


def _prep(qkv, gt, conv_w8, alog_v, dtb_v, tri_lo, tri_up, n_lat_tiles):
    bsz, lc, _ = qkv.shape
    nt = lc // ROW_TILE
    halo = 8
    per = ROW_TILE // halo
    last = lc // halo - 1

    def row_spec(width):
        return pl.BlockSpec((1, ROW_TILE, width), lambda b, i: (b, i, 0))

    def const_spec(shape):
        return pl.BlockSpec(shape, lambda b, i: tuple(0 for _ in shape))

    outs = [(G_QK, BF16), (G_QK, BF16), (GH * GDV, BF16), (LANES, F32), (LANES, F32)]
    return pl.pallas_call(
        functools.partial(_prep_kernel, n_lat_tiles, nt),
        grid=(bsz, nt),
        in_specs=[row_spec(G_QKV),
                  pl.BlockSpec((1, halo, G_QKV), lambda b, i: (b, jnp.maximum(i * per - 1, 0), 0)),
                  pl.BlockSpec((1, halo, G_QKV), lambda b, i: (b, jnp.minimum((i + 1) * per, last), 0)),
                  row_spec(2 * LANES),
                  const_spec((8, G_QKV)), const_spec((1, LANES)), const_spec((1, LANES)),
                  const_spec((ROW_TILE, ROW_TILE)), const_spec((ROW_TILE, ROW_TILE))],
        out_specs=[row_spec(w) for w, _ in outs],
        out_shape=[jax.ShapeDtypeStruct((bsz, lc, w), dt) for w, dt in outs],
        scratch_shapes=[pltpu.VMEM((ROW_TILE + 2 * halo, G_QKV), F32)],
        compiler_params=_cparams(("arbitrary", "arbitrary")),
        name="prep",
    )(qkv, qkv, qkv, gt, conv_w8, alog_v, dtb_v, tri_lo, tri_up)


def _chunk_masks(n, rev):
    row = lax.broadcasted_iota(jnp.int32, (n, n), 0)
    col = lax.broadcasted_iota(jnp.int32, (n, n), 1)
    same = (row // CHUNK) == (col // CHUNK)
    incl = jnp.logical_and(same, (col >= row) if rev else (col <= row))
    strict = jnp.logical_and(same, (col > row) if rev else (col < row))
    return incl, strict


def _decay(gcol, grow, incl):
    return jnp.where(incl, jnp.exp(jnp.minimum(gcol - grow, 0.0)), 0.0)


def _chunk_kernel(k_ref, v_ref, beta_ref, gc_ref, uw_ref):
    n = k_ref.shape[1]
    row = lax.broadcasted_iota(jnp.int32, (n, n), 0)
    col = lax.broadcasted_iota(jnp.int32, (n, n), 1)
    eye = (row == col).astype(F32)
    sizes = [2 ** j for j in range(1, int(math.log2(CHUNK)))]
    same = {r: (row // r) == (col // r) for r in sizes + [CHUNK]}
    a_all, t_all, x_all, where = [], [], [], []
    masks = [_chunk_masks(n, rev) for rev in (False, True)]
    for bb in range(k_ref.shape[0]):
        gc = gc_ref[bb]
        gct = gc.T
        beta = beta_ref[bb]
        for d in range(2):
            incl, strict = masks[d]
            for h in range(GH):
                c = d * GH + h
                k_b = k_ref[bb, :, h * GDK:(h + 1) * GDK]
                k = k_b.astype(F32)
                v = v_ref[bb, :, h * GDV:(h + 1) * GDV].astype(F32)
                bcol = beta[:, c:c + 1]
                gcol = gc[:, c:c + 1]
                grow = gct[c:c + 1, :]
                kb = k * bcol
                a = jnp.where(strict, _dot_nt(kb.astype(BF16), k_b) * _decay(gcol, grow, incl), 0.0)
                a_all.append(a.astype(BF16))
                t_all.append(eye - jnp.where(same[2], a, 0.0))
                x_all.append(jnp.concatenate([v * bcol, kb * jnp.exp(gcol)], axis=1).astype(BF16))
                where.append((bb, c))
    n_prob = len(a_all)
    zero = jnp.zeros((n, n), BF16)
    for r in sizes:
        off_mask = jnp.logical_and(same[2 * r], jnp.logical_not(same[r]))
        t_b = [t.astype(BF16) for t in t_all]
        inner = [_dot(jnp.where(off_mask, a_all[c], zero), t_b[c]).astype(BF16) for c in range(n_prob)]
        t_all = [t_all[c] - _dot(t_b[c], inner[c]) for c in range(n_prob)]
    for p in range(n_prob):
        bb, c = where[p]
        uw_ref[bb, c] = _dot(t_all[p].astype(BF16), x_all[p]).astype(uw_ref.dtype)


def _chunk(k, v, beta, gc):
    bsz, lc, _ = k.shape
    nt = lc // ROW_TILE
    nb = CHUNK_BATCH if bsz % CHUNK_BATCH == 0 else 1

    def row_spec(width):
        return pl.BlockSpec((nb, ROW_TILE, width), lambda b, i: (b, i, 0))

    return pl.pallas_call(
        _chunk_kernel,
        grid=(bsz // nb, nt),
        in_specs=[row_spec(G_QK), row_spec(GH * GDV), row_spec(LANES), row_spec(LANES)],
        out_specs=pl.BlockSpec((nb, 2 * GH, ROW_TILE, GDV + GDK), lambda b, i: (b, 0, i, 0)),
        out_shape=jax.ShapeDtypeStruct((bsz, 2 * GH, lc, GDV + GDK), BF16),
        compiler_params=_cparams(("arbitrary", "arbitrary")),
        name="chunk",
    )(k, v, beta, gc)


def _scan_kernel(qf_ref, kf_ref, gf_ref, uwf_ref, qr_ref, kr_ref, gr_ref, uwr_ref,
                 of_ref, or_ref, s_ref):
    @pl.when(pl.program_id(1) == 0)
    def _():
        s_ref[...] = jnp.zeros_like(s_ref)

    n_chunks = ROW_TILE // CHUNK
    n_batch = qf_ref.shape[0]
    row = lax.broadcasted_iota(jnp.int32, (CHUNK, CHUNK), 0)
    col = lax.broadcasted_iota(jnp.int32, (CHUNK, CHUNK), 1)
    dirs = ((qf_ref, kf_ref, gf_ref, uwf_ref, of_ref), (qr_ref, kr_ref, gr_ref, uwr_ref, or_ref))
    chains = [(bb, d, h) for bb in range(n_batch) for d in range(2) for h in range(GH)]
    state = [s_ref[i] for i in range(len(chains))]

    def chunk_rows(step, d):
        ci = n_chunks - 1 - step if d == 1 else step
        return slice(ci * CHUNK, (ci + 1) * CHUNK)

    pre = []
    for step in range(n_chunks):
        gates = {}
        for bb in range(n_batch):
            for d in range(2):
                gc = dirs[d][2][bb, chunk_rows(step, d), :]
                gates[bb, d] = (gc, gc.T)
        per_chain = []
        for bb, d, h in chains:
            q_ref, k_ref, _, uw_ref, _ = dirs[d]
            rev = d == 1
            incl = (col >= row) if rev else (col <= row)
            rows = chunk_rows(step, d)
            gc, gct = gates[bb, d]
            c = d * GH + h
            q_b = q_ref[bb, rows, h * GDK:(h + 1) * GDK]
            k_b = k_ref[bb, rows, h * GDK:(h + 1) * GDK]
            q = q_b.astype(F32)
            k = k_b.astype(F32)
            gcol = gc[:, c:c + 1]
            grow = gct[c:c + 1, :]
            last = 0 if rev else CHUNK - 1
            g_last = gcol[last:last + 1, :]
            a_qk = (_dot_nt(q_b, k_b) * _decay(gcol, grow, incl)).astype(BF16)
            wq = jnp.concatenate([uw_ref[bb, h, rows, GDV:], (q * jnp.exp(gcol)).astype(BF16)], axis=0)
            k_tail_t = (k * jnp.exp(g_last - gcol)).T.astype(BF16)
            per_chain.append((a_qk, wq, k_tail_t, jnp.exp(g_last)))
        pre.append(per_chain)

    for step in range(n_chunks):
        ws = [_dot(pre[step][i][1], state[i].astype(BF16)) for i in range(len(chains))]
        v_new = [(dirs[d][3][bb, h, chunk_rows(step, d), :GDV].astype(F32) - ws[i][:CHUNK]).astype(BF16)
                 for i, (bb, d, h) in enumerate(chains)]
        for i, (bb, d, h) in enumerate(chains):
            a_qk, _, k_tail_t, decay_last = pre[step][i]
            dirs[d][4][bb, chunk_rows(step, d), h * GDV:(h + 1) * GDV] = (
                ws[i][CHUNK:] + _dot(a_qk, v_new[i])).astype(BF16)
            state[i] = state[i] * decay_last + _dot(k_tail_t, v_new[i])
    for i in range(len(chains)):
        s_ref[i] = state[i]


def _scan(q, k, gc, uw, n_lat_tiles):
    bsz, lc, _ = q.shape
    nt = lc // ROW_TILE
    n_ctx_tiles = nt - n_lat_tiles

    def fwd(b, i):
        return jnp.where(i < n_ctx_tiles, n_lat_tiles + i, i - n_ctx_tiles)

    def bwd(b, i):
        return jnp.where(i < n_ctx_tiles, nt - 1 - i, n_lat_tiles - 1 - (i - n_ctx_tiles))

    nb = SCAN_BATCH if bsz % SCAN_BATCH == 0 else 1

    def specs(tile):
        return [pl.BlockSpec((nb, ROW_TILE, G_QK), lambda b, i: (b, tile(b, i), 0)),
                pl.BlockSpec((nb, ROW_TILE, G_QK), lambda b, i: (b, tile(b, i), 0)),
                pl.BlockSpec((nb, ROW_TILE, LANES), lambda b, i: (b, tile(b, i), 0))]

    def uw_spec(tile, d):
        return pl.BlockSpec((nb, GH, ROW_TILE, GDV + GDK), lambda b, i: (b, d, tile(b, i), 0))

    def out_spec(tile):
        return pl.BlockSpec((nb, ROW_TILE, GH * GDV), lambda b, i: (b, tile(b, i), 0))

    return pl.pallas_call(
        _scan_kernel,
        grid=(bsz // nb, nt),
        in_specs=specs(fwd) + [uw_spec(fwd, 0)] + specs(bwd) + [uw_spec(bwd, 1)],
        out_specs=[out_spec(fwd), out_spec(bwd)],
        out_shape=[jax.ShapeDtypeStruct((bsz, lc, GH * GDV), BF16)] * 2,
        scratch_shapes=[pltpu.VMEM((nb * 2 * GH, GDK, GDV), F32)],
        compiler_params=_cparams(("arbitrary", "arbitrary")),
        name="scan",
    )(q, k, gc, uw, q, k, gc, uw)


def _attn_kernel(lam_init, q_ref, k_ref, vt_ref, lam_ref, nw_ref, o_ref):
    lam = lam_ref[...]
    lam_full = (jnp.exp(jnp.sum(lam[0:1] * lam[1:2], axis=-1, keepdims=True))
                - jnp.exp(jnp.sum(lam[2:3] * lam[3:4], axis=-1, keepdims=True)) + lam_init)
    q = q_ref[0]
    tq = q.shape[0]
    n_keys = k_ref.shape[1]
    lane = lax.broadcasted_iota(jnp.int32, q.shape, 1)
    zero = jnp.zeros_like(q)
    qc = [jnp.where(lane < DDK, q, zero), jnp.where(lane >= DDK, q, zero)]
    m = [jnp.full((1, tq), -jnp.inf, F32) for _ in range(2)]
    l = [jnp.zeros((1, tq), F32) for _ in range(2)]
    acc = [jnp.zeros((DDV, tq), F32) for _ in range(2)]
    n_tiles = n_keys // KEY_TILE
    bounds = [t * KEY_TILE for t in range(n_tiles)] + [n_keys]

    def scores(t):
        k_t = k_ref[0, bounds[t]:bounds[t + 1], :]
        tiles = [_dot_nt(k_t, qc[c]) for c in range(2)]
        return [(s, jnp.max(s, axis=0, keepdims=True)) for s in tiles]

    pending = [scores(t) for t in range(min(SCORE_LOOKAHEAD, n_tiles))]
    for t in range(n_tiles):
        s_cur = pending.pop(0)
        if t + SCORE_LOOKAHEAD < n_tiles:
            pending.append(scores(t + SCORE_LOOKAHEAD))
        width = bounds[t + 1] - bounds[t]
        vt_t = jnp.concatenate([vt_ref[0, :, bounds[t]:bounds[t + 1]], jnp.ones((16, width), BF16)], axis=0)
        for c in range(2):
            s, s_max = s_cur[c]
            m_new = jnp.maximum(m[c], s_max)
            p = jnp.exp2(s - m_new)
            scale = jnp.exp2(m[c] - m_new)
            pv = _dot(vt_t, p.astype(BF16))
            l[c] = scale * l[c] + pv[DDV:DDV + 1]
            acc[c] = scale * acc[c] + pv[:DDV]
            m[c] = m_new
    o = acc[0] / l[0] - lam_full * (acc[1] / l[1])
    y = o * lax.rsqrt(jnp.mean(o * o, axis=0, keepdims=True) + NORM_EPS)
    o_ref[0] = (y.T * nw_ref[...] * (1.0 - lam_init)).astype(o_ref.dtype)


def _attn(dq, dk, dvt, lam, norm_w, lam_init, lq, tq):
    bsz, lc, _ = dq.shape
    return pl.pallas_call(
        functools.partial(_attn_kernel, lam_init),
        grid=(bsz, DH, lq // tq),
        in_specs=[pl.BlockSpec((1, tq, 2 * DDK), lambda b, h, i: (b, i, h)),
                  pl.BlockSpec((1, lc, 2 * DDK), lambda b, h, i: (b, 0, h)),
                  pl.BlockSpec((1, DDV, lc), lambda b, h, i: (b, h, 0)),
                  pl.BlockSpec((4, DDK), lambda b, h, i: (0, 0)),
                  pl.BlockSpec((1, DDV), lambda b, h, i: (0, 0))],
        out_specs=pl.BlockSpec((1, tq, DDV), lambda b, h, i: (b, i, h)),
        out_shape=jax.ShapeDtypeStruct((bsz, lq, DH * DDV), BF16),
        compiler_params=_cparams(("arbitrary", "arbitrary", "arbitrary")),
        name="attn",
    )(dq, dk, dvt, lam, norm_w.reshape(1, DDV))


def _out_kernel(alpha, of_ref, or_ref, z_ref, yd_ref, x_ref, g1_ref, sh2_ref, sc2_ref, gnw_ref,
                wo_ref, lng_ref, lnb_ref, wr_ref, br_ref, x1_ref, h2_ref, lg_ref):
    o = of_ref[0].astype(F32) + or_ref[0].astype(F32)
    z = z_ref[0].astype(F32)
    heads = []
    for h in range(GH):
        oh = o[:, h * GDV:(h + 1) * GDV]
        heads.append((oh * lax.rsqrt(jnp.mean(oh * oh, axis=-1, keepdims=True) + NORM_EPS) * gnw_ref[...]
                      * _silu(z[:, h * GDV:(h + 1) * GDV])).astype(BF16))
    mix = jnp.concatenate(heads + [yd_ref[0]], axis=1)
    x1 = _normalize(alpha * x_ref[0] + g1_ref[0] * _dot(mix, wo_ref[...])) * lng_ref[...] + lnb_ref[...]
    x1_ref[0] = x1
    h2 = _normalize(x1) * (1.0 + sc2_ref[0]) + sh2_ref[0]
    h2_ref[0] = h2
    h_hi = h2.astype(BF16)
    h_lo = (h2 - h_hi.astype(F32)).astype(BF16)
    both = _dot(h_hi, wr_ref[...])
    lg_ref[0] = br_ref[...] + both[:, :LANES] + both[:, LANES:] + _dot(h_lo, wr_ref[:, :LANES])


def _out(o_f, o_r, z, y_diff, xc, mod3, gdn_norm_w, w_out, ln_g, ln_b, w_router3, b_router, alpha):
    bsz, lq, _ = y_diff.shape
    tile = OUT_TILE if lq % OUT_TILE == 0 else ROW_TILE
    nt = lq // tile

    def row_spec(width):
        return pl.BlockSpec((1, tile, width), lambda b, i: (b, i, 0))

    def mod_spec(col):
        return pl.BlockSpec((1, 1, D), lambda b, i: (b, 0, col))

    def const_spec(shape):
        return pl.BlockSpec(shape, lambda b, i: tuple(0 for _ in shape))

    return pl.pallas_call(
        functools.partial(_out_kernel, alpha),
        grid=(bsz, nt),
        in_specs=[row_spec(GH * GDV), row_spec(GH * GDV), row_spec(G_Z), row_spec(DH * DDV), row_spec(D),
                  mod_spec(2), mod_spec(3), mod_spec(4),
                  const_spec((1, GDV)), const_spec((G_Z + DH * DDV, D)),
                  const_spec((1, D)), const_spec((1, D)),
                  const_spec((D, 2 * LANES)), const_spec((1, LANES))],
        out_specs=[row_spec(D), row_spec(D), row_spec(LANES)],
        out_shape=[jax.ShapeDtypeStruct((bsz, lq, D), F32), jax.ShapeDtypeStruct((bsz, lq, D), F32),
                   jax.ShapeDtypeStruct((bsz, lq, LANES), F32)],
        compiler_params=_cparams(("arbitrary", "arbitrary")),
        name="out",
    )(o_f, o_r, z, y_diff, xc, mod3, mod3, mod3, gdn_norm_w.reshape(1, GDV), w_out,
      ln_g.reshape(1, D), ln_b.reshape(1, D), w_router3, b_router)


def _route_kernel(lg_ref, tri_ref, ids_ref, gate_ref, cnt_ref, run_ref):
    @pl.when(pl.program_id(0) == 0)
    def _():
        run_ref[...] = jnp.zeros_like(run_ref)

    lgt = lg_ref[...].T[:ROUTE_ROWS]
    n = lgt.shape[1]
    row_i = lax.broadcasted_iota(jnp.int32, lgt.shape, 0)
    row = row_i.astype(F32)
    neg = jnp.float32(-jnp.inf)

    def first_max(mask):
        masked = jnp.where(mask, lgt, neg)
        m = jnp.max(masked, axis=0, keepdims=True)
        idx = jnp.min(jnp.where(jnp.logical_and(mask, masked == m), row, float(LANES)), axis=0, keepdims=True)
        return m, idx

    is_group = row_i < N_GROUPS
    m_g, grp = first_max(is_group)
    p_g = 1.0 / jnp.sum(jnp.where(is_group, jnp.exp(lgt - m_g), 0.0), axis=0, keepdims=True)
    e_row = row - float(N_GROUPS)
    lo_e = grp * float(E_PER_GROUP)
    in_group = jnp.logical_and(e_row >= lo_e, e_row < lo_e + float(E_PER_GROUP))
    l0, i0 = first_max(in_group)
    l1, i1 = first_max(jnp.logical_and(in_group, row != i0))
    r = jnp.exp(l1 - l0)
    gate0 = p_g / (1.0 + r)
    gate1 = p_g * r / (1.0 + r)
    e0 = i0 - float(N_GROUPS)
    e1 = i1 - float(N_GROUPS)
    oh0 = (e_row == e0).astype(F32)
    oh1 = (e_row == e1).astype(F32)
    both = oh0 + oh1
    before = _dot(both.astype(BF16), tri_ref[...]) + run_ref[...]
    rank0 = jnp.sum(oh0 * before, axis=0, keepdims=True)
    rank1 = jnp.sum(oh1 * before, axis=0, keepdims=True)
    run_ref[...] = run_ref[...] + jnp.sum(both, axis=1, keepdims=True)
    cnt_ref[...] = jnp.broadcast_to(run_ref[...], cnt_ref.shape)
    ids_ref[...] = jnp.concatenate([e0, e1, rank0, rank1, jnp.zeros((4, n), F32)], axis=0).astype(jnp.int32)
    gate_ref[...] = jnp.concatenate([gate0, gate1, jnp.zeros((LANES - 2, n), F32)], axis=0).T


def _route(logits, tri_strict):
    t = logits.shape[0]
    tile = tri_strict.shape[0]
    return pl.pallas_call(
        _route_kernel,
        grid=(t // tile,),
        in_specs=[pl.BlockSpec((tile, LANES), lambda i: (i, 0)),
                  pl.BlockSpec((tile, tile), lambda i: (0, 0))],
        out_specs=[pl.BlockSpec((8, tile), lambda i: (0, i)),
                   pl.BlockSpec((tile, LANES), lambda i: (i, 0)),
                   pl.BlockSpec((ROUTE_ROWS, LANES), lambda i: (0, 0))],
        out_shape=[jax.ShapeDtypeStruct((8, t), jnp.int32), jax.ShapeDtypeStruct((t, LANES), F32),
                   jax.ShapeDtypeStruct((ROUTE_ROWS, LANES), F32)],
        scratch_shapes=[pltpu.VMEM((ROUTE_ROWS, 1), F32)],
        compiler_params=_cparams(("arbitrary",)),
        name="route",
    )(logits, tri_strict)


ROUTE_TILE = 1024
ROUTE_ROWS = 40
DMA_TOKENS = 512
COMBINE_LAG = 2


def _dispatch_kernel(tail_ref, used_ref, dest_ref, h_hbm, buf_hbm, zero_ref, src_ref, load_sem, row_sem, zsem):
    i = pl.program_id(0)
    n_tiles = pl.num_programs(0)
    n_slots = src_ref.shape[0]
    groups = DMA_TOKENS // 8

    def load(tile):
        slot = lax.rem(tile, n_slots)
        return pltpu.make_async_copy(h_hbm.at[pl.ds(tile * groups, groups)], src_ref.at[slot], load_sem.at[slot])

    def wait_rows(tile):
        slot = lax.rem(tile, n_slots)
        for _ in range(2):
            pltpu.make_async_copy(buf_hbm.at[pl.ds(0, DMA_TOKENS)], buf_hbm.at[pl.ds(0, DMA_TOKENS)],
                                  row_sem.at[slot]).wait()

    @pl.when(i == 0)
    def _():
        load(0).start()
        zero_ref[...] = jnp.zeros_like(zero_ref)
        n_blocks = buf_hbm.shape[0] // FFN_BLOCK

        def block_copy(b):
            return pltpu.make_async_copy(zero_ref, buf_hbm.at[pl.ds(b * FFN_BLOCK, FFN_BLOCK)], zsem)

        for e in range(N_EXPERTS):
            @pl.when(tail_ref[e] >= 0)
            def _():
                block_copy(tail_ref[e]).start()
        lax.fori_loop(used_ref[0], n_blocks, lambda b, carry: (block_copy(b).start(), carry)[1], 0)
        for e in range(N_EXPERTS):
            @pl.when(tail_ref[e] >= 0)
            def _():
                block_copy(tail_ref[e]).wait()
        lax.fori_loop(used_ref[0], n_blocks, lambda b, carry: (block_copy(b).wait(), carry)[1], 0)

    @pl.when(i + 1 < n_tiles)
    def _():
        load(i + 1).start()

    load(i).wait()
    slot = lax.rem(i, n_slots)

    def start(grp, carry):
        for u in range(8):
            for j in range(2):
                dst = dest_ref[j * DMA_TOKENS + 8 * grp + u]
                pltpu.make_async_copy(src_ref.at[slot, grp, pl.ds(u, 1)], buf_hbm.at[pl.ds(dst, 1)],
                                      row_sem.at[slot]).start()
        return carry

    lax.fori_loop(0, groups, start, 0)

    @pl.when(i > 0)
    def _():
        wait_rows(i - 1)

    @pl.when(i == n_tiles - 1)
    def _():
        wait_rows(i)


def _dispatch(tail_block, n_used, dest_flat, h2, n_rows):
    t, d = h2.shape
    grid_spec = pltpu.PrefetchScalarGridSpec(
        num_scalar_prefetch=2,
        grid=(t // DMA_TOKENS,),
        in_specs=[pl.BlockSpec((2 * DMA_TOKENS,), lambda i, tail, used: (i,), memory_space=pltpu.SMEM),
                  pl.BlockSpec(memory_space=pl.ANY)],
        out_specs=pl.BlockSpec(memory_space=pl.ANY),
        scratch_shapes=[pltpu.VMEM((FFN_BLOCK, d), h2.dtype),
                        pltpu.VMEM((3, DMA_TOKENS // 8, 8, d), h2.dtype),
                        pltpu.SemaphoreType.DMA((3,)), pltpu.SemaphoreType.DMA((3,)),
                        pltpu.SemaphoreType.DMA(())],
    )
    return pl.pallas_call(
        _dispatch_kernel,
        grid_spec=grid_spec,
        out_shape=jax.ShapeDtypeStruct((n_rows, d), h2.dtype),
        compiler_params=_cparams(("arbitrary",)),
        name="dispatch",
    )(tail_block, n_used, dest_flat, h2.reshape(t // 8, 8, d))


def _combine_kernel(alpha, dest_ref, ys_hbm, gate_ref, x1_ref, g2_ref, lng_ref, lnb_ref, o_ref, rows_ref, sem):
    i = pl.program_id(0)
    n_tiles = pl.num_programs(0) - COMBINE_LAG
    n = x1_ref.shape[0]
    n_slots = rows_ref.shape[0]
    slot_in = lax.rem(i, n_slots)
    slot_out = lax.rem(i + n_slots - COMBINE_LAG, n_slots)

    def start_group(grp):
        for u in range(8):
            for j in range(2):
                src = dest_ref[j * DMA_TOKENS + 8 * grp + u]
                pltpu.make_async_copy(ys_hbm.at[pl.ds(src, 1)], rows_ref.at[slot_in, j, grp, pl.ds(u, 1)],
                                      sem.at[slot_in]).start()

    def finish():
        gate = gate_ref[...]
        r0 = rows_ref[slot_out, 0].reshape(n, -1)
        r1 = rows_ref[slot_out, 1].reshape(n, -1)
        y = gate[:, 0:1] * r0 + gate[:, 1:2] * r1
        o_ref[...] = _normalize(alpha * x1_ref[...] + g2_ref[0] * y) * lng_ref[...] + lnb_ref[...]

    @pl.when(i >= COMBINE_LAG)
    def _():
        for j in range(2):
            pltpu.make_async_copy(ys_hbm.at[pl.ds(0, n)], ys_hbm.at[pl.ds(0, n)], sem.at[slot_out]).wait()

    @pl.when(i < COMBINE_LAG)
    def _():
        lax.fori_loop(0, n // 8, lambda grp, carry: (start_group(grp), carry)[1], 0)

    @pl.when(jnp.logical_and(i >= COMBINE_LAG, i < n_tiles))
    def _():
        for grp in range(n // 8):
            start_group(grp)
        finish()

    @pl.when(i >= n_tiles)
    def _():
        finish()


def _combine(dest_flat, ys, gate, x1, mod3, ln_g, ln_b, alpha, rows_per_batch):
    t, d = x1.shape
    per_batch = rows_per_batch // DMA_TOKENS
    n_tiles = t // DMA_TOKENS

    def done(i):
        return jnp.maximum(i - COMBINE_LAG, 0)

    return pl.pallas_call(
        functools.partial(_combine_kernel, alpha),
        grid=(n_tiles + COMBINE_LAG,),
        in_specs=[pl.BlockSpec((2 * DMA_TOKENS,), lambda i: (jnp.minimum(i, n_tiles - 1),), memory_space=pltpu.SMEM),
                  pl.BlockSpec(memory_space=pl.ANY),
                  pl.BlockSpec((DMA_TOKENS, LANES), lambda i: (done(i), 0)),
                  pl.BlockSpec((DMA_TOKENS, d), lambda i: (done(i), 0)),
                  pl.BlockSpec((1, 1, d), lambda i: (done(i) // per_batch, 0, 5)),
                  pl.BlockSpec((1, d), lambda i: (0, 0)),
                  pl.BlockSpec((1, d), lambda i: (0, 0))],
        out_specs=pl.BlockSpec((DMA_TOKENS, d), lambda i: (done(i), 0)),
        out_shape=jax.ShapeDtypeStruct((t, d), F32),
        scratch_shapes=[pltpu.VMEM((COMBINE_LAG + 1, 2, DMA_TOKENS // 8, 8, d), F32),
                        pltpu.SemaphoreType.DMA((COMBINE_LAG + 1,))],
        compiler_params=_cparams(("arbitrary",)),
        name="combine",
    )(dest_flat, ys, gate, x1, mod3, ln_g.reshape(1, d), ln_b.reshape(1, d))


def _ffn_kernel(start_ref, nblk_ref, used_ref, buf_hbm, wg_ref, wu_ref, wd_ref, ys_hbm,
                wgb_ref, wub_ref, wdb_ref, x_buf, o_buf, in_sem, out_sem):
    e = pl.program_id(0)
    nb = nblk_ref[e]
    base = start_ref[e]
    used = used_ref[0]
    n_in = x_buf.shape[0]
    n_out = o_buf.shape[0]
    ahead = n_in - 1

    def rows(g):
        return pl.ds(g * FFN_BLOCK, FFN_BLOCK)

    def in_copy(g):
        slot = lax.rem(g, n_in)
        return pltpu.make_async_copy(buf_hbm.at[rows(g)], x_buf.at[slot], in_sem.at[slot])

    def out_copy(g):
        slot = lax.rem(g, n_out)
        return pltpu.make_async_copy(o_buf.at[slot], ys_hbm.at[rows(g)], out_sem.at[slot])

    @pl.when(e == 0)
    def _():
        for g in range(ahead):
            @pl.when(g < used)
            def _():
                in_copy(g).start()

    @pl.when(nb > 0)
    def _():
        wgb_ref[...] = wg_ref[0].astype(BF16)
        wub_ref[...] = wu_ref[0].astype(BF16)
        wdb_ref[...] = wd_ref[0].astype(BF16)

        def block(j, carry):
            g = base + j
            in_copy(g).wait()

            @pl.when(g + ahead < used)
            def _():
                in_copy(g + ahead).start()

            @pl.when(g >= n_out)
            def _():
                out_copy(g - n_out).wait()

            islot = lax.rem(g, n_in)
            oslot = lax.rem(g, n_out)
            slab = FFN_BLOCK // FFN_SLABS
            xs = [x_buf[islot, s * slab:(s + 1) * slab, :].astype(BF16) for s in range(FFN_SLABS)]
            gates = [_dot(x, wgb_ref[...]) for x in xs]
            ups = [_dot(x, wub_ref[...]) for x in xs]
            hidden = [(_silu(gt) * up).astype(BF16) for gt, up in zip(gates, ups)]
            for s in range(FFN_SLABS):
                o_buf[oslot, s * slab:(s + 1) * slab, :] = _dot(hidden[s], wdb_ref[...])
            out_copy(g).start()
            return carry

        lax.fori_loop(0, nb, block, 0)

    @pl.when(e == pl.num_programs(0) - 1)
    def _():
        for back in range(n_out):
            @pl.when(used - 1 - back >= 0)
            def _():
                out_copy(used - 1 - back).wait()
        n_blocks = ys_hbm.shape[0] // FFN_BLOCK
        o_buf[0] = jnp.zeros(o_buf.shape[1:], o_buf.dtype)

        def clear(b):
            return pltpu.make_async_copy(o_buf.at[0], ys_hbm.at[pl.ds(b * FFN_BLOCK, FFN_BLOCK)], out_sem.at[0])

        lax.fori_loop(used_ref[0], n_blocks, lambda b, carry: (clear(b).start(), carry)[1], 0)
        lax.fori_loop(used_ref[0], n_blocks, lambda b, carry: (clear(b).wait(), carry)[1], 0)


def _ffn(seg_start, seg_blocks, n_used, buf, w_gate, w_up, w_down):
    n_rows, d = buf.shape
    n_experts = w_gate.shape[0]
    grid_spec = pltpu.PrefetchScalarGridSpec(
        num_scalar_prefetch=3,
        grid=(n_experts,),
        in_specs=[pl.BlockSpec(memory_space=pl.ANY),
                  pl.BlockSpec((1, d, D_EXPERT), lambda e, *_: (e, 0, 0)),
                  pl.BlockSpec((1, d, D_EXPERT), lambda e, *_: (e, 0, 0)),
                  pl.BlockSpec((1, D_EXPERT, d), lambda e, *_: (e, 0, 0))],
        out_specs=pl.BlockSpec(memory_space=pl.ANY),
        scratch_shapes=[pltpu.VMEM((d, D_EXPERT), BF16), pltpu.VMEM((d, D_EXPERT), BF16),
                        pltpu.VMEM((D_EXPERT, d), BF16),
                        pltpu.VMEM((FFN_IN_SLOTS, FFN_BLOCK, d), F32), pltpu.VMEM((FFN_OUT_SLOTS, FFN_BLOCK, d), F32),
                        pltpu.SemaphoreType.DMA((FFN_IN_SLOTS,)), pltpu.SemaphoreType.DMA((FFN_OUT_SLOTS,))],
    )
    return pl.pallas_call(
        _ffn_kernel,
        grid_spec=grid_spec,
        out_shape=jax.ShapeDtypeStruct((n_rows, d), F32),
        compiler_params=_cparams(("arbitrary",)),
        name="ffn",
    )(seg_start, seg_blocks, n_used, buf, w_gate, w_up, w_down)


def _rope_tables(seq_len, n_ctx_rows):
    lane = np.arange(LANES)
    within = lane % DDK
    freq = within % 16
    use_col = within >= DDK // 2
    sign = np.where((lane % 32) < 16, -1.0, 1.0).astype(np.float32)
    half = DDK // 2
    inv_freq = (ROPE_THETA ** (-np.arange(0, half, 2, dtype=np.float32) / np.float32(half))).astype(np.float32)
    t = np.arange(seq_len)
    pos = np.where(use_col[None, :], (t % GRID_W)[:, None], (t // GRID_W)[:, None]).astype(np.float32)
    ang = pos * inv_freq[freq][None, :]
    cos = np.concatenate([np.cos(ang), np.ones((n_ctx_rows, LANES), np.float32)], axis=0)
    sin = np.concatenate([np.sin(ang) * sign[None, :], np.zeros((n_ctx_rows, LANES), np.float32)], axis=0)
    return jnp.asarray(cos, F32), jnp.asarray(sin, F32)


def _block_tri(n, block, upper):
    i = np.arange(n)
    same = (i[:, None] // block) == (i[None, :] // block)
    tri = (i[None, :] >= i[:, None]) if upper else (i[None, :] <= i[:, None])
    return jnp.asarray((same & tri).astype(np.float32), dtype=BF16)


def _pack_w_in(w_in):
    off = G_QKV + G_Z
    v_start = G_COLS + 2 * D_QK
    pad = jnp.zeros((D, LANES - G_GATES), w_in.dtype)
    w_gate = jnp.concatenate([w_in[:, off:off + G_GATES], pad, w_in[:, off + G_GATES:G_COLS], pad], axis=1)
    return (w_in[:, :off].astype(BF16), w_gate.astype(BF16), w_in[:, G_COLS:v_start].astype(BF16),
            w_in[:, v_start:].T.astype(BF16))


def kernel(x, c, ctx, c_ctx, w_ada, b_ada, w_in, conv_w, gdn_a_log, gdn_dt_bias, gdn_norm_w, diff_lambda,
           diff_norm_w, w_out, ln1_g, ln1_b, w_router_group, b_router_group, w_router_expert, b_router_expert,
           w_expert_gate, w_expert_up, w_expert_down, ln2_g, ln2_b):
    depth = w_ada.shape[0]
    assert depth == 1, "single-layer block: the context stream never feeds a later layer"
    bsz, seq, _ = x.shape
    n_ctx = ctx.shape[1]
    assert seq % ROW_TILE == 0 and n_ctx % ROW_TILE == 0 and seq % GRID_W == 0 and bsz < 8
    n_lat_tiles = seq // ROW_TILE
    alpha = (2.0 * depth) ** 0.25
    lam_init = 0.8 - 0.6 * math.exp(-0.3 * 0)
    i = 0

    cc = jnp.zeros((8, D), F32).at[:bsz].set(c).at[7].set(c_ctx)
    mod = _ada(cc, w_ada[i], b_ada[i])
    mod3 = mod.reshape(8, 1, 6 * D)

    cos_t, sin_t = _rope_tables(seq, n_ctx)
    qkv, z, gt, dq, dk, dvt = _proj(x, ctx, mod3, cos_t, sin_t, *_pack_w_in(w_in[i]))

    conv_w8 = jnp.zeros((8, G_QKV), F32).at[:GCONV].set(conv_w[i])
    alog_v = jnp.zeros((1, LANES), F32).at[0, :G_GATES].set(gdn_a_log[i].reshape(-1))
    dtb_v = jnp.zeros((1, LANES), F32).at[0, :G_GATES].set(gdn_dt_bias[i].reshape(-1))
    gq, gk, gv, beta, gc = _prep(qkv, gt, conv_w8, alog_v, dtb_v,
                                 _block_tri(ROW_TILE, CHUNK, False), _block_tri(ROW_TILE, CHUNK, True), n_lat_tiles)
    uw = _chunk(gk, gv, beta, gc)
    o_f, o_r = _scan(gq, gk, gc, uw, n_lat_tiles)

    y_diff = _attn(dq, dk, dvt, diff_lambda[i], diff_norm_w[i], lam_init, seq, tq=min(ATTN_Q_TILE, seq))

    w_router = jnp.zeros((D, LANES), F32).at[:, :N_GROUPS].set(w_router_group[i]) \
        .at[:, N_GROUPS:N_GROUPS + N_EXPERTS].set(w_router_expert[i])
    w_hi = w_router.astype(BF16)
    w_router3 = jnp.concatenate([w_hi, (w_router - w_hi.astype(F32)).astype(BF16)], axis=1)
    b_router = jnp.zeros((1, LANES), F32).at[0, :N_GROUPS].set(b_router_group[i]) \
        .at[0, N_GROUPS:N_GROUPS + N_EXPERTS].set(b_router_expert[i])
    x1, h2, logits = _out(o_f, o_r, z, y_diff, x, mod3, gdn_norm_w[i], w_out[i].astype(BF16), ln1_g[i], ln1_b[i],
                          w_router3, b_router, alpha)

    t = bsz * seq
    tri_strict = jnp.asarray(np.triu(np.ones((ROUTE_TILE, ROUTE_TILE), np.float32), 1), dtype=BF16)
    ids, gate, counts = _route(logits.reshape(t, LANES), tri_strict)

    cnt = counts[N_GROUPS:N_GROUPS + N_EXPERTS, 0].astype(jnp.int32)
    padded = (cnt + FFN_BLOCK - 1) // FFN_BLOCK * FFN_BLOCK
    p_end = jnp.cumsum(padded)
    p_start = p_end - padded
    n_blocks = (2 * t) // FFN_BLOCK + N_EXPERTS
    n_used = (p_end[-1:] // FFN_BLOCK).astype(jnp.int32)
    tail_block = jnp.where(cnt > 0, p_end // FFN_BLOCK - 1, -1).astype(jnp.int32)
    one_hot = ids[0:2, :, None] == jnp.arange(N_EXPERTS, dtype=jnp.int32)
    row_start = jnp.sum(jnp.where(one_hot, p_start, 0), axis=-1)
    dest = jnp.swapaxes((row_start + ids[2:4]).reshape(2, t // DMA_TOKENS, DMA_TOKENS), 0, 1)
    dest = dest.reshape(-1).astype(jnp.int32)

    buf = _dispatch(tail_block, n_used, dest, h2.reshape(t, D), n_blocks * FFN_BLOCK)
    ys = _ffn((p_start // FFN_BLOCK).astype(jnp.int32), (padded // FFN_BLOCK).astype(jnp.int32), n_used, buf,
              w_expert_gate[i], w_expert_up[i], w_expert_down[i])
    out = _combine(dest, ys, gate, x1.reshape(t, D), mod3, ln2_g[i], ln2_b[i], alpha, seq)
    return out.reshape(bsz, seq, D)
```

```python
import functools
import math

import numpy as np
import jax
import jax.numpy as jnp
from jax import lax
from jax.experimental import pallas as pl
from jax.experimental.pallas import tpu as pltpu

F32 = jnp.float32
BF16 = jnp.bfloat16

D = 1024
GRID_W = 64
ROPE_THETA = 10000.0
GH = 4
GDK = 128
GDV = 128
GCONV = 5
DH = 4
DDK = 64
DDV = 128
G_QK = GH * GDK
G_QKV = 2 * G_QK + GH * GDV
G_Z = GH * GDV
G_GATES = 2 * GH
G_COLS = G_QKV + G_Z + 2 * G_GATES
D_QK = DH * 2 * DDK
N_GROUPS = 4
E_PER_GROUP = 8
N_EXPERTS = N_GROUPS * E_PER_GROUP
D_EXPERT = 512
LN_EPS = 1e-5
NORM_EPS = 1e-6

LANES = 128
ROW_TILE = 256
CHUNK = 64
OUT_TILE = 512
CHUNK_BATCH = 2
SCAN_BATCH = 2
FFN_BLOCK = 256
FFN_SLABS = 2
FFN_IN_SLOTS = 6
FFN_OUT_SLOTS = 4
KEY_TILE = 512
ATTN_Q_TILE = 1024
SCORE_LOOKAHEAD = 2
VMEM_LIMIT = 56 * 1024 * 1024

Q_SCALE = DDK ** -0.5 * math.log2(math.e)


def _cparams(sem):
    return pltpu.CompilerParams(dimension_semantics=sem, vmem_limit_bytes=VMEM_LIMIT)


def _dot(a, b):
    return jnp.dot(a, b, preferred_element_type=F32)


def _dot_nt(a, b):
    return lax.dot_general(a, b, (((1,), (1,)), ((), ())), preferred_element_type=F32)


def _normalize(x):
    mu = jnp.mean(x, axis=-1, keepdims=True)
    xc = x - mu
    var = jnp.mean(xc * xc, axis=-1, keepdims=True)
    return xc * lax.rsqrt(var + LN_EPS)


def _silu(x):
    return x * jax.nn.sigmoid(x)


def _split3(x):
    hi = x.astype(BF16)
    r = x - hi.astype(F32)
    mid = r.astype(BF16)
    lo = (r - mid.astype(F32)).astype(BF16)
    return hi, mid, lo


def _ada_kernel(c_ref, w_ref, b_ref, o_ref):
    s = _silu(c_ref[...])
    s_hi = s.astype(BF16)
    s_lo = (s - s_hi.astype(F32)).astype(BF16)
    w = w_ref[...]
    w_hi = w.astype(BF16)
    w_lo = (w - w_hi.astype(F32)).astype(BF16)
    rows = s.shape[0]
    both = _dot(jnp.concatenate([s_hi, s_lo], axis=0), w_hi)
    o_ref[...] = both[:rows] + both[rows:] + _dot(s_hi, w_lo) + b_ref[...]


def _ada(cc, w_ada, b_ada):
    n = w_ada.shape[1]
    bn = 1024
    return pl.pallas_call(
        _ada_kernel,
        grid=(n // bn,),
        in_specs=[pl.BlockSpec((8, D), lambda j: (0, 0)),
                  pl.BlockSpec((D, bn), lambda j: (0, j)),
                  pl.BlockSpec((1, bn), lambda j: (0, j))],
        out_specs=pl.BlockSpec((8, bn), lambda j: (0, j)),
        out_shape=jax.ShapeDtypeStruct((8, n), F32),
        compiler_params=_cparams(("arbitrary",)),
        name="ada",
    )(cc, w_ada, b_ada.reshape(1, n))


def _rope(v, cos, sin, low_half):
    fwd = pltpu.roll(v, LANES - 16, axis=1)
    bwd = pltpu.roll(v, 16, axis=1)
    return v * cos + jnp.where(low_half, fwd, bwd) * sin


def _proj_kernel(n_lat_tiles, x_ref, ctx_ref, sh_ref, sc_ref, cos_ref, sin_ref, wg_ref, wgate_ref, wqk_ref, wvt_ref,
                 qkv_ref, z_ref, gt_ref, dq_ref, dk_ref, dvt_ref):
    rows = jnp.where(pl.program_id(1) >= n_lat_tiles, ctx_ref[0], x_ref[0])
    h = (_normalize(rows) * (1.0 + sc_ref[0]) + sh_ref[0]).astype(BF16)
    q_all = _dot(h, wqk_ref[:, :D_QK])
    k_all = _dot(h, wqk_ref[:, D_QK:])
    qkv_ref[0] = _dot(h, wg_ref[:, :G_QKV])
    z_ref[0] = _dot(h, wg_ref[:, G_QKV:]).astype(z_ref.dtype)
    gt_ref[0] = _dot(h, wgate_ref[...])
    cos = cos_ref[...]
    sin = sin_ref[...]
    lane = lax.broadcasted_iota(jnp.int32, cos.shape, 1)
    low_half = (lane % 32) < 16
    for j in range(D_QK // LANES):
        slab = slice(j * LANES, (j + 1) * LANES)
        dq_ref[0, :, slab] = (_rope(q_all[:, slab], cos, sin, low_half) * Q_SCALE).astype(BF16)
        dk_ref[0, :, slab] = _rope(k_all[:, slab], cos, sin, low_half).astype(BF16)
    dvt_ref[0] = _dot_nt(wvt_ref[...], h).astype(BF16)


def _proj(x, ctx, mod3, cos_t, sin_t, w_gdn, w_gate, w_qk, wv_t):
    bsz, seq, _ = x.shape
    lc = seq + ctx.shape[1]
    nt = lc // ROW_TILE
    n_lat_tiles = seq // ROW_TILE
    ctx_row = mod3.shape[0] - 1

    def mod_idx(col):
        return lambda b, i: (jnp.where(i >= n_lat_tiles, ctx_row, b), 0, col)

    def row_spec(width):
        return pl.BlockSpec((1, ROW_TILE, width), lambda b, i: (b, i, 0))

    outs = [(G_QKV, F32), (G_Z, BF16), (2 * LANES, F32), (D_QK, BF16), (D_QK, BF16)]
    return pl.pallas_call(
        functools.partial(_proj_kernel, n_lat_tiles),
        grid=(bsz, nt),
        in_specs=[pl.BlockSpec((1, ROW_TILE, D), lambda b, i: (b, jnp.minimum(i, n_lat_tiles - 1), 0)),
                  pl.BlockSpec((1, ROW_TILE, D), lambda b, i: (b, jnp.maximum(i - n_lat_tiles, 0), 0)),
                  pl.BlockSpec((1, 1, D), mod_idx(0)),
                  pl.BlockSpec((1, 1, D), mod_idx(1)),
                  pl.BlockSpec((ROW_TILE, LANES), lambda b, i: (i, 0)),
                  pl.BlockSpec((ROW_TILE, LANES), lambda b, i: (i, 0)),
                  pl.BlockSpec(w_gdn.shape, lambda b, i: (0, 0)),
                  pl.BlockSpec(w_gate.shape, lambda b, i: (0, 0)),
                  pl.BlockSpec(w_qk.shape, lambda b, i: (0, 0)),
                  pl.BlockSpec(wv_t.shape, lambda b, i: (0, 0))],
        out_specs=[row_spec(w) for w, _ in outs]
        + [pl.BlockSpec((1, DH * DDV, ROW_TILE), lambda b, i: (b, 0, i))],
        out_shape=[jax.ShapeDtypeStruct((bsz, lc, w), dt) for w, dt in outs]
        + [jax.ShapeDtypeStruct((bsz, DH * DDV, lc), BF16)],
        compiler_params=_cparams(("arbitrary", "arbitrary")),
        name="proj",
    )(x, ctx, mod3, mod3, cos_t, sin_t, w_gdn, w_gate, w_qk, wv_t)


def _prep_kernel(n_lat_tiles, nt, main_ref, prev_ref, next_ref, gt_ref, cw_ref, alog_ref, dtb_ref,
                 lo_ref, up_ref, q_ref, k_ref, v_ref, beta_ref, gc_ref, ext_ref):
    i = pl.program_id(1)
    halo = prev_ref.shape[1]
    has_prev = jnp.logical_and(i != 0, i != n_lat_tiles)
    has_next = jnp.logical_and(i != n_lat_tiles - 1, i != nt - 1)
    ext_ref[0:halo, :] = prev_ref[0] * has_prev.astype(F32)
    ext_ref[halo:halo + ROW_TILE, :] = main_ref[0]
    ext_ref[halo + ROW_TILE:, :] = next_ref[0] * has_next.astype(F32)
    pad = GCONV // 2
    acc = None
    ext = ext_ref[...]
    n_ext = ext.shape[0]
    for j in range(GCONV):
        shifted = ext if j == pad else pltpu.roll(ext, (pad - j) % n_ext, axis=0)
        term = shifted[halo:halo + ROW_TILE, :] * cw_ref[j:j + 1, :]
        acc = term if acc is None else acc + term
    qkv = _silu(acc)
    for h in range(GH):
        q = qkv[:, h * GDK:(h + 1) * GDK]
        q_ref[0, :, h * GDK:(h + 1) * GDK] = (q * (lax.rsqrt(jnp.sum(q * q, axis=-1, keepdims=True) + NORM_EPS)
                                                 * (GDK ** -0.5))).astype(q_ref.dtype)
        k = qkv[:, G_QK + h * GDK:G_QK + (h + 1) * GDK]
        k_ref[0, :, h * GDK:(h + 1) * GDK] = (
            k * lax.rsqrt(jnp.sum(k * k, axis=-1, keepdims=True) + NORM_EPS)).astype(k_ref.dtype)
    v_ref[0] = qkv[:, 2 * G_QK:].astype(v_ref.dtype)
    gt = gt_ref[0]
    beta_ref[0] = jax.nn.sigmoid(gt[:, :LANES])
    a = gt[:, LANES:] + dtb_ref[...]
    softplus = jnp.maximum(a, 0.0) + jnp.log(1.0 + jnp.exp(-jnp.abs(a)))
    g = -jnp.exp(alog_ref[...]) * softplus
    lo = lo_ref[...]
    up = up_ref[...]
    fwd = None
    bwd = None
    for part in _split3(g):
        f = _dot(lo, part)
        r = _dot(up, part)
        fwd = f if fwd is None else fwd + f
        bwd = r if bwd is None else bwd + r
    lane = lax.broadcasted_iota(jnp.int32, g.shape, 1)
    gc_ref[0] = jnp.where(lane < GH, fwd, bwd)


def _prep(qkv, gt, conv_w8, alog_v, dtb_v, tri_lo, tri_up, n_lat_tiles):
    bsz, lc, _ = qkv.shape
    nt = lc // ROW_TILE
    halo = 8
    per = ROW_TILE // halo
    last = lc // halo - 1

    def row_spec(width):
        return pl.BlockSpec((1, ROW_TILE, width), lambda b, i: (b, i, 0))

    def const_spec(shape):
        return pl.BlockSpec(shape, lambda b, i: tuple(0 for _ in shape))

    outs = [(G_QK, BF16), (G_QK, BF16), (GH * GDV, BF16), (LANES, F32), (LANES, F32)]
    return pl.pallas_call(
        functools.partial(_prep_kernel, n_lat_tiles, nt),
        grid=(bsz, nt),
        in_specs=[row_spec(G_QKV),
                  pl.BlockSpec((1, halo, G_QKV), lambda b, i: (b, jnp.maximum(i * per - 1, 0), 0)),
                  pl.BlockSpec((1, halo, G_QKV), lambda b, i: (b, jnp.minimum((i + 1) * per, last), 0)),
                  row_spec(2 * LANES),
                  const_spec((8, G_QKV)), const_spec((1, LANES)), const_spec((1, LANES)),
                  const_spec((ROW_TILE, ROW_TILE)), const_spec((ROW_TILE, ROW_TILE))],
        out_specs=[row_spec(w) for w, _ in outs],
        out_shape=[jax.ShapeDtypeStruct((bsz, lc, w), dt) for w, dt in outs],
        scratch_shapes=[pltpu.VMEM((ROW_TILE + 2 * halo, G_QKV), F32)],
        compiler_params=_cparams(("arbitrary", "arbitrary")),
        name="prep",
    )(qkv, qkv, qkv, gt, conv_w8, alog_v, dtb_v, tri_lo, tri_up)


def _chunk_masks(n, rev):
    row = lax.broadcasted_iota(jnp.int32, (n, n), 0)
    col = lax.broadcasted_iota(jnp.int32, (n, n), 1)
    same = (row // CHUNK) == (col // CHUNK)
    incl = jnp.logical_and(same, (col >= row) if rev else (col <= row))
    strict = jnp.logical_and(same, (col > row) if rev else (col < row))
    return incl, strict


def _decay(gcol, grow, incl):
    return jnp.where(incl, jnp.exp(jnp.minimum(gcol - grow, 0.0)), 0.0)


def _chunk_kernel(k_ref, v_ref, beta_ref, gc_ref, uw_ref):
    n = k_ref.shape[1]
    row = lax.broadcasted_iota(jnp.int32, (n, n), 0)
    col = lax.broadcasted_iota(jnp.int32, (n, n), 1)
    eye = (row == col).astype(F32)
    sizes = [2 ** j for j in range(1, int(math.log2(CHUNK)))]
    same = {r: (row // r) == (col // r) for r in sizes + [CHUNK]}
    a_all, t_all, x_all, where = [], [], [], []
    masks = [_chunk_masks(n, rev) for rev in (False, True)]
    for bb in range(k_ref.shape[0]):
        gc = gc_ref[bb]
        gct = gc.T
        beta = beta_ref[bb]
        for d in range(2):
            incl, strict = masks[d]
            for h in range(GH):
                c = d * GH + h
                k_b = k_ref[bb, :, h * GDK:(h + 1) * GDK]
                k = k_b.astype(F32)
                v = v_ref[bb, :, h * GDV:(h + 1) * GDV].astype(F32)
                bcol = beta[:, c:c + 1]
                gcol = gc[:, c:c + 1]
                grow = gct[c:c + 1, :]
                kb = k * bcol
                a = jnp.where(strict, _dot_nt(kb.astype(BF16), k_b) * _decay(gcol, grow, incl), 0.0)
                a_all.append(a.astype(BF16))
                t_all.append(eye - jnp.where(same[2], a, 0.0))
                x_all.append(jnp.concatenate([v * bcol, kb * jnp.exp(gcol)], axis=1).astype(BF16))
                where.append((bb, c))
    n_prob = len(a_all)
    zero = jnp.zeros((n, n), BF16)
    for r in sizes:
        off_mask = jnp.logical_and(same[2 * r], jnp.logical_not(same[r]))
        t_b = [t.astype(BF16) for t in t_all]
        inner = [_dot(jnp.where(off_mask, a_all[c], zero), t_b[c]).astype(BF16) for c in range(n_prob)]
        t_all = [t_all[c] - _dot(t_b[c], inner[c]) for c in range(n_prob)]
    for p in range(n_prob):
        bb, c = where[p]
        uw_ref[bb, c] = _dot(t_all[p].astype(BF16), x_all[p]).astype(uw_ref.dtype)


def _chunk(k, v, beta, gc):
    bsz, lc, _ = k.shape
    nt = lc // ROW_TILE
    nb = CHUNK_BATCH if bsz % CHUNK_BATCH == 0 else 1

    def row_spec(width):
        return pl.BlockSpec((nb, ROW_TILE, width), lambda b, i: (b, i, 0))

    return pl.pallas_call(
        _chunk_kernel,
        grid=(bsz // nb, nt),
        in_specs=[row_spec(G_QK), row_spec(GH * GDV), row_spec(LANES), row_spec(LANES)],
        out_specs=pl.BlockSpec((nb, 2 * GH, ROW_TILE, GDV + GDK), lambda b, i: (b, 0, i, 0)),
        out_shape=jax.ShapeDtypeStruct((bsz, 2 * GH, lc, GDV + GDK), BF16),
        compiler_params=_cparams(("arbitrary", "arbitrary")),
        name="chunk",
    )(k, v, beta, gc)


def _scan_kernel(qf_ref, kf_ref, gf_ref, uwf_ref, qr_ref, kr_ref, gr_ref, uwr_ref,
                 of_ref, or_ref, s_ref):
    @pl.when(pl.program_id(1) == 0)
    def _():
        s_ref[...] = jnp.zeros_like(s_ref)

    n_chunks = ROW_TILE // CHUNK
    n_batch = qf_ref.shape[0]
    row = lax.broadcasted_iota(jnp.int32, (CHUNK, CHUNK), 0)
    col = lax.broadcasted_iota(jnp.int32, (CHUNK, CHUNK), 1)
    dirs = ((qf_ref, kf_ref, gf_ref, uwf_ref, of_ref), (qr_ref, kr_ref, gr_ref, uwr_ref, or_ref))
    chains = [(bb, d, h) for bb in range(n_batch) for d in range(2) for h in range(GH)]
    state = [s_ref[i] for i in range(len(chains))]

    def chunk_rows(step, d):
        ci = n_chunks - 1 - step if d == 1 else step
        return slice(ci * CHUNK, (ci + 1) * CHUNK)

    pre = []
    for step in range(n_chunks):
        gates = {}
        for bb in range(n_batch):
            for d in range(2):
                gc = dirs[d][2][bb, chunk_rows(step, d), :]
                gates[bb, d] = (gc, gc.T)
        per_chain = []
        for bb, d, h in chains:
            q_ref, k_ref, _, uw_ref, _ = dirs[d]
            rev = d == 1
            incl = (col >= row) if rev else (col <= row)
            rows = chunk_rows(step, d)
            gc, gct = gates[bb, d]
            c = d * GH + h
            q_b = q_ref[bb, rows, h * GDK:(h + 1) * GDK]
            k_b = k_ref[bb, rows, h * GDK:(h + 1) * GDK]
            q = q_b.astype(F32)
            k = k_b.astype(F32)
            gcol = gc[:, c:c + 1]
            grow = gct[c:c + 1, :]
            last = 0 if rev else CHUNK - 1
            g_last = gcol[last:last + 1, :]
            a_qk = (_dot_nt(q_b, k_b) * _decay(gcol, grow, incl)).astype(BF16)
            wq = jnp.concatenate([uw_ref[bb, h, rows, GDV:], (q * jnp.exp(gcol)).astype(BF16)], axis=0)
            k_tail_t = (k * jnp.exp(g_last - gcol)).T.astype(BF16)
            per_chain.append((a_qk, wq, k_tail_t, jnp.exp(g_last)))
        pre.append(per_chain)

    for step in range(n_chunks):
        ws = [_dot(pre[step][i][1], state[i].astype(BF16)) for i in range(len(chains))]
        v_new = [(dirs[d][3][bb, h, chunk_rows(step, d), :GDV].astype(F32) - ws[i][:CHUNK]).astype(BF16)
                 for i, (bb, d, h) in enumerate(chains)]
        for i, (bb, d, h) in enumerate(chains):
            a_qk, _, k_tail_t, decay_last = pre[step][i]
            dirs[d][4][bb, chunk_rows(step, d), h * GDV:(h + 1) * GDV] = (
                ws[i][CHUNK:] + _dot(a_qk, v_new[i])).astype(BF16)
            state[i] = state[i] * decay_last + _dot(k_tail_t, v_new[i])
    for i in range(len(chains)):
        s_ref[i] = state[i]


def _scan(q, k, gc, uw, n_lat_tiles):
    bsz, lc, _ = q.shape
    nt = lc // ROW_TILE
    n_ctx_tiles = nt - n_lat_tiles

    def fwd(b, i):
        return jnp.where(i < n_ctx_tiles, n_lat_tiles + i, i - n_ctx_tiles)

    def bwd(b, i):
        return jnp.where(i < n_ctx_tiles, nt - 1 - i, n_lat_tiles - 1 - (i - n_ctx_tiles))

    nb = SCAN_BATCH if bsz % SCAN_BATCH == 0 else 1

    def specs(tile):
        return [pl.BlockSpec((nb, ROW_TILE, G_QK), lambda b, i: (b, tile(b, i), 0)),
                pl.BlockSpec((nb, ROW_TILE, G_QK), lambda b, i: (b, tile(b, i), 0)),
                pl.BlockSpec((nb, ROW_TILE, LANES), lambda b, i: (b, tile(b, i), 0))]

    def uw_spec(tile, d):
        return pl.BlockSpec((nb, GH, ROW_TILE, GDV + GDK), lambda b, i: (b, d, tile(b, i), 0))

    def out_spec(tile):
        return pl.BlockSpec((nb, ROW_TILE, GH * GDV), lambda b, i: (b, tile(b, i), 0))

    return pl.pallas_call(
        _scan_kernel,
        grid=(bsz // nb, nt),
        in_specs=specs(fwd) + [uw_spec(fwd, 0)] + specs(bwd) + [uw_spec(bwd, 1)],
        out_specs=[out_spec(fwd), out_spec(bwd)],
        out_shape=[jax.ShapeDtypeStruct((bsz, lc, GH * GDV), BF16)] * 2,
        scratch_shapes=[pltpu.VMEM((nb * 2 * GH, GDK, GDV), F32)],
        compiler_params=_cparams(("arbitrary", "arbitrary")),
        name="scan",
    )(q, k, gc, uw, q, k, gc, uw)


def _attn_kernel(lam_init, q_ref, k_ref, vt_ref, lam_ref, nw_ref, o_ref):
    lam = lam_ref[...]
    lam_full = (jnp.exp(jnp.sum(lam[0:1] * lam[1:2], axis=-1, keepdims=True))
                - jnp.exp(jnp.sum(lam[2:3] * lam[3:4], axis=-1, keepdims=True)) + lam_init)
    q = q_ref[0]
    tq = q.shape[0]
    n_keys = k_ref.shape[1]
    lane = lax.broadcasted_iota(jnp.int32, q.shape, 1)
    zero = jnp.zeros_like(q)
    qc = [jnp.where(lane < DDK, q, zero), jnp.where(lane >= DDK, q, zero)]
    m = [jnp.full((1, tq), -jnp.inf, F32) for _ in range(2)]
    l = [jnp.zeros((1, tq), F32) for _ in range(2)]
    acc = [jnp.zeros((DDV, tq), F32) for _ in range(2)]
    n_tiles = n_keys // KEY_TILE
    bounds = [t * KEY_TILE for t in range(n_tiles)] + [n_keys]

    def scores(t):
        k_t = k_ref[0, bounds[t]:bounds[t + 1], :]
        tiles = [_dot_nt(k_t, qc[c]) for c in range(2)]
        return [(s, jnp.max(s, axis=0, keepdims=True)) for s in tiles]

    pending = [scores(t) for t in range(min(SCORE_LOOKAHEAD, n_tiles))]
    for t in range(n_tiles):
        s_cur = pending.pop(0)
        if t + SCORE_LOOKAHEAD < n_tiles:
            pending.append(scores(t + SCORE_LOOKAHEAD))
        width = bounds[t + 1] - bounds[t]
        vt_t = jnp.concatenate([vt_ref[0, :, bounds[t]:bounds[t + 1]], jnp.ones((16, width), BF16)], axis=0)
        for c in range(2):
            s, s_max = s_cur[c]
            m_new = jnp.maximum(m[c], s_max)
            p = jnp.exp2(s - m_new)
            scale = jnp.exp2(m[c] - m_new)
            pv = _dot(vt_t, p.astype(BF16))
            l[c] = scale * l[c] + pv[DDV:DDV + 1]
            acc[c] = scale * acc[c] + pv[:DDV]
            m[c] = m_new
    o = acc[0] / l[0] - lam_full * (acc[1] / l[1])
    y = o * lax.rsqrt(jnp.mean(o * o, axis=0, keepdims=True) + NORM_EPS)
    o_ref[0] = (y.T * nw_ref[...] * (1.0 - lam_init)).astype(o_ref.dtype)


def _attn(dq, dk, dvt, lam, norm_w, lam_init, lq, tq):
    bsz, lc, _ = dq.shape
    return pl.pallas_call(
        functools.partial(_attn_kernel, lam_init),
        grid=(bsz, DH, lq // tq),
        in_specs=[pl.BlockSpec((1, tq, 2 * DDK), lambda b, h, i: (b, i, h)),
                  pl.BlockSpec((1, lc, 2 * DDK), lambda b, h, i: (b, 0, h)),
                  pl.BlockSpec((1, DDV, lc), lambda b, h, i: (b, h, 0)),
                  pl.BlockSpec((4, DDK), lambda b, h, i: (0, 0)),
                  pl.BlockSpec((1, DDV), lambda b, h, i: (0, 0))],
        out_specs=pl.BlockSpec((1, tq, DDV), lambda b, h, i: (b, i, h)),
        out_shape=jax.ShapeDtypeStruct((bsz, lq, DH * DDV), BF16),
        compiler_params=_cparams(("arbitrary", "arbitrary", "arbitrary")),
        name="attn",
    )(dq, dk, dvt, lam, norm_w.reshape(1, DDV))


def _out_kernel(alpha, of_ref, or_ref, z_ref, yd_ref, x_ref, g1_ref, sh2_ref, sc2_ref, gnw_ref,
                wo_ref, lng_ref, lnb_ref, wr_ref, br_ref, x1_ref, h2_ref, lg_ref):
    o = of_ref[0].astype(F32) + or_ref[0].astype(F32)
    z = z_ref[0].astype(F32)
    heads = []
    for h in range(GH):
        oh = o[:, h * GDV:(h + 1) * GDV]
        heads.append((oh * lax.rsqrt(jnp.mean(oh * oh, axis=-1, keepdims=True) + NORM_EPS) * gnw_ref[...]
                      * _silu(z[:, h * GDV:(h + 1) * GDV])).astype(BF16))
    mix = jnp.concatenate(heads + [yd_ref[0]], axis=1)
    x1 = _normalize(alpha * x_ref[0] + g1_ref[0] * _dot(mix, wo_ref[...])) * lng_ref[...] + lnb_ref[...]
    x1_ref[0] = x1
    h2 = _normalize(x1) * (1.0 + sc2_ref[0]) + sh2_ref[0]
    h2_ref[0] = h2
    h_hi = h2.astype(BF16)
    h_lo = (h2 - h_hi.astype(F32)).astype(BF16)
    both = _dot(h_hi, wr_ref[...])
    lg_ref[0] = br_ref[...] + both[:, :LANES] + both[:, LANES:] + _dot(h_lo, wr_ref[:, :LANES])


def _out(o_f, o_r, z, y_diff, xc, mod3, gdn_norm_w, w_out, ln_g, ln_b, w_router3, b_router, alpha):
    bsz, lq, _ = y_diff.shape
    tile = OUT_TILE if lq % OUT_TILE == 0 else ROW_TILE
    nt = lq // tile

    def row_spec(width):
        return pl.BlockSpec((1, tile, width), lambda b, i: (b, i, 0))

    def mod_spec(col):
        return pl.BlockSpec((1, 1, D), lambda b, i: (b, 0, col))

    def const_spec(shape):
        return pl.BlockSpec(shape, lambda b, i: tuple(0 for _ in shape))

    return pl.pallas_call(
        functools.partial(_out_kernel, alpha),
        grid=(bsz, nt),
        in_specs=[row_spec(GH * GDV), row_spec(GH * GDV), row_spec(G_Z), row_spec(DH * DDV), row_spec(D),
                  mod_spec(2), mod_spec(3), mod_spec(4),
                  const_spec((1, GDV)), const_spec((G_Z + DH * DDV, D)),
                  const_spec((1, D)), const_spec((1, D)),
                  const_spec((D, 2 * LANES)), const_spec((1, LANES))],
        out_specs=[row_spec(D), row_spec(D), row_spec(LANES)],
        out_shape=[jax.ShapeDtypeStruct((bsz, lq, D), F32), jax.ShapeDtypeStruct((bsz, lq, D), F32),
                   jax.ShapeDtypeStruct((bsz, lq, LANES), F32)],
        compiler_params=_cparams(("arbitrary", "arbitrary")),
        name="out",
    )(o_f, o_r, z, y_diff, xc, mod3, mod3, mod3, gdn_norm_w.reshape(1, GDV), w_out,
      ln_g.reshape(1, D), ln_b.reshape(1, D), w_router3, b_router)


def _route_kernel(lg_ref, tri_ref, ids_ref, gate_ref, cnt_ref, run_ref):
    @pl.when(pl.program_id(0) == 0)
    def _():
        run_ref[...] = jnp.zeros_like(run_ref)

    lgt = lg_ref[...].T[:ROUTE_ROWS]
    n = lgt.shape[1]
    row_i = lax.broadcasted_iota(jnp.int32, lgt.shape, 0)
    row = row_i.astype(F32)
    neg = jnp.float32(-jnp.inf)

    def first_max(mask):
        masked = jnp.where(mask, lgt, neg)
        m = jnp.max(masked, axis=0, keepdims=True)
        idx = jnp.min(jnp.where(jnp.logical_and(mask, masked == m), row, float(LANES)), axis=0, keepdims=True)
        return m, idx

    is_group = row_i < N_GROUPS
    m_g, grp = first_max(is_group)
    p_g = 1.0 / jnp.sum(jnp.where(is_group, jnp.exp(lgt - m_g), 0.0), axis=0, keepdims=True)
    e_row = row - float(N_GROUPS)
    lo_e = grp * float(E_PER_GROUP)
    in_group = jnp.logical_and(e_row >= lo_e, e_row < lo_e + float(E_PER_GROUP))
    l0, i0 = first_max(in_group)
    l1, i1 = first_max(jnp.logical_and(in_group, row != i0))
    r = jnp.exp(l1 - l0)
    gate0 = p_g / (1.0 + r)
    gate1 = p_g * r / (1.0 + r)
    e0 = i0 - float(N_GROUPS)
    e1 = i1 - float(N_GROUPS)
    oh0 = (e_row == e0).astype(F32)
    oh1 = (e_row == e1).astype(F32)
    both = oh0 + oh1
    before = _dot(both.astype(BF16), tri_ref[...]) + run_ref[...]
    rank0 = jnp.sum(oh0 * before, axis=0, keepdims=True)
    rank1 = jnp.sum(oh1 * before, axis=0, keepdims=True)
    run_ref[...] = run_ref[...] + jnp.sum(both, axis=1, keepdims=True)
    cnt_ref[...] = jnp.broadcast_to(run_ref[...], cnt_ref.shape)
    ids_ref[...] = jnp.concatenate([e0, e1, rank0, rank1, jnp.zeros((4, n), F32)], axis=0).astype(jnp.int32)
    gate_ref[...] = jnp.concatenate([gate0, gate1, jnp.zeros((LANES - 2, n), F32)], axis=0).T


def _route(logits, tri_strict):
    t = logits.shape[0]
    tile = tri_strict.shape[0]
    return pl.pallas_call(
        _route_kernel,
        grid=(t // tile,),
        in_specs=[pl.BlockSpec((tile, LANES), lambda i: (i, 0)),
                  pl.BlockSpec((tile, tile), lambda i: (0, 0))],
        out_specs=[pl.BlockSpec((8, tile), lambda i: (0, i)),
                   pl.BlockSpec((tile, LANES), lambda i: (i, 0)),
                   pl.BlockSpec((ROUTE_ROWS, LANES), lambda i: (0, 0))],
        out_shape=[jax.ShapeDtypeStruct((8, t), jnp.int32), jax.ShapeDtypeStruct((t, LANES), F32),
                   jax.ShapeDtypeStruct((ROUTE_ROWS, LANES), F32)],
        scratch_shapes=[pltpu.VMEM((ROUTE_ROWS, 1), F32)],
        compiler_params=_cparams(("arbitrary",)),
        name="route",
    )(logits, tri_strict)


ROUTE_TILE = 1024
ROUTE_ROWS = 40
DMA_TOKENS = 512
COMBINE_LAG = 2


def _dispatch_kernel(tail_ref, used_ref, dest_ref, h_hbm, buf_hbm, zero_ref, src_ref, load_sem, row_sem, zsem):
    i = pl.program_id(0)
    n_tiles = pl.num_programs(0)
    n_slots = src_ref.shape[0]
    groups = DMA_TOKENS // 8

    def load(tile):
        slot = lax.rem(tile, n_slots)
        return pltpu.make_async_copy(h_hbm.at[pl.ds(tile * groups, groups)], src_ref.at[slot], load_sem.at[slot])

    def wait_rows(tile):
        slot = lax.rem(tile, n_slots)
        for _ in range(2):
            pltpu.make_async_copy(buf_hbm.at[pl.ds(0, DMA_TOKENS)], buf_hbm.at[pl.ds(0, DMA_TOKENS)],
                                  row_sem.at[slot]).wait()

    @pl.when(i == 0)
    def _():
        load(0).start()
        zero_ref[...] = jnp.zeros_like(zero_ref)
        n_blocks = buf_hbm.shape[0] // FFN_BLOCK

        def block_copy(b):
            return pltpu.make_async_copy(zero_ref, buf_hbm.at[pl.ds(b * FFN_BLOCK, FFN_BLOCK)], zsem)

        for e in range(N_EXPERTS):
            @pl.when(tail_ref[e] >= 0)
            def _():
                block_copy(tail_ref[e]).start()
        lax.fori_loop(used_ref[0], n_blocks, lambda b, carry: (block_copy(b).start(), carry)[1], 0)
        for e in range(N_EXPERTS):
            @pl.when(tail_ref[e] >= 0)
            def _():
                block_copy(tail_ref[e]).wait()
        lax.fori_loop(used_ref[0], n_blocks, lambda b, carry: (block_copy(b).wait(), carry)[1], 0)

    @pl.when(i + 1 < n_tiles)
    def _():
        load(i + 1).start()

    load(i).wait()
    slot = lax.rem(i, n_slots)

    def start(grp, carry):
        for u in range(8):
            for j in range(2):
                dst = dest_ref[j * DMA_TOKENS + 8 * grp + u]
                pltpu.make_async_copy(src_ref.at[slot, grp, pl.ds(u, 1)], buf_hbm.at[pl.ds(dst, 1)],
                                      row_sem.at[slot]).start()
        return carry

    lax.fori_loop(0, groups, start, 0)

    @pl.when(i > 0)
    def _():
        wait_rows(i - 1)

    @pl.when(i == n_tiles - 1)
    def _():
        wait_rows(i)


def _dispatch(tail_block, n_used, dest_flat, h2, n_rows):
    t, d = h2.shape
    grid_spec = pltpu.PrefetchScalarGridSpec(
        num_scalar_prefetch=2,
        grid=(t // DMA_TOKENS,),
        in_specs=[pl.BlockSpec((2 * DMA_TOKENS,), lambda i, tail, used: (i,), memory_space=pltpu.SMEM),
                  pl.BlockSpec(memory_space=pl.ANY)],
        out_specs=pl.BlockSpec(memory_space=pl.ANY),
        scratch_shapes=[pltpu.VMEM((FFN_BLOCK, d), h2.dtype),
                        pltpu.VMEM((3, DMA_TOKENS // 8, 8, d), h2.dtype),
                        pltpu.SemaphoreType.DMA((3,)), pltpu.SemaphoreType.DMA((3,)),
                        pltpu.SemaphoreType.DMA(())],
    )
    return pl.pallas_call(
        _dispatch_kernel,
        grid_spec=grid_spec,
        out_shape=jax.ShapeDtypeStruct((n_rows, d), h2.dtype),
        compiler_params=_cparams(("arbitrary",)),
        name="dispatch",
    )(tail_block, n_used, dest_flat, h2.reshape(t // 8, 8, d))


def _combine_kernel(alpha, dest_ref, ys_hbm, gate_ref, x1_ref, g2_ref, lng_ref, lnb_ref, o_ref, rows_ref, sem):
    i = pl.program_id(0)
    n_tiles = pl.num_programs(0) - COMBINE_LAG
    n = x1_ref.shape[0]
    n_slots = rows_ref.shape[0]
    slot_in = lax.rem(i, n_slots)
    slot_out = lax.rem(i + n_slots - COMBINE_LAG, n_slots)

    def start_group(grp):
        for u in range(8):
            for j in range(2):
                src = dest_ref[j * DMA_TOKENS + 8 * grp + u]
                pltpu.make_async_copy(ys_hbm.at[pl.ds(src, 1)], rows_ref.at[slot_in, j, grp, pl.ds(u, 1)],
                                      sem.at[slot_in]).start()

    def finish():
        gate = gate_ref[...]
        r0 = rows_ref[slot_out, 0].reshape(n, -1)
        r1 = rows_ref[slot_out, 1].reshape(n, -1)
        y = gate[:, 0:1] * r0 + gate[:, 1:2] * r1
        o_ref[...] = _normalize(alpha * x1_ref[...] + g2_ref[0] * y) * lng_ref[...] + lnb_ref[...]

    @pl.when(i >= COMBINE_LAG)
    def _():
        for j in range(2):
            pltpu.make_async_copy(ys_hbm.at[pl.ds(0, n)], ys_hbm.at[pl.ds(0, n)], sem.at[slot_out]).wait()

    @pl.when(i < COMBINE_LAG)
    def _():
        lax.fori_loop(0, n // 8, lambda grp, carry: (start_group(grp), carry)[1], 0)

    @pl.when(jnp.logical_and(i >= COMBINE_LAG, i < n_tiles))
    def _():
        for grp in range(n // 8):
            start_group(grp)
        finish()

    @pl.when(i >= n_tiles)
    def _():
        finish()


def _combine(dest_flat, ys, gate, x1, mod3, ln_g, ln_b, alpha, rows_per_batch):
    t, d = x1.shape
    per_batch = rows_per_batch // DMA_TOKENS
    n_tiles = t // DMA_TOKENS

    def done(i):
        return jnp.maximum(i - COMBINE_LAG, 0)

    return pl.pallas_call(
        functools.partial(_combine_kernel, alpha),
        grid=(n_tiles + COMBINE_LAG,),
        in_specs=[pl.BlockSpec((2 * DMA_TOKENS,), lambda i: (jnp.minimum(i, n_tiles - 1),), memory_space=pltpu.SMEM),
                  pl.BlockSpec(memory_space=pl.ANY),
                  pl.BlockSpec((DMA_TOKENS, LANES), lambda i: (done(i), 0)),
                  pl.BlockSpec((DMA_TOKENS, d), lambda i: (done(i), 0)),
                  pl.BlockSpec((1, 1, d), lambda i: (done(i) // per_batch, 0, 5)),
                  pl.BlockSpec((1, d), lambda i: (0, 0)),
                  pl.BlockSpec((1, d), lambda i: (0, 0))],
        out_specs=pl.BlockSpec((DMA_TOKENS, d), lambda i: (done(i), 0)),
        out_shape=jax.ShapeDtypeStruct((t, d), F32),
        scratch_shapes=[pltpu.VMEM((COMBINE_LAG + 1, 2, DMA_TOKENS // 8, 8, d), F32),
                        pltpu.SemaphoreType.DMA((COMBINE_LAG + 1,))],
        compiler_params=_cparams(("arbitrary",)),
        name="combine",
    )(dest_flat, ys, gate, x1, mod3, ln_g.reshape(1, d), ln_b.reshape(1, d))


def _ffn_kernel(start_ref, nblk_ref, used_ref, buf_hbm, wg_ref, wu_ref, wd_ref, ys_hbm,
                wgb_ref, wub_ref, wdb_ref, x_buf, o_buf, in_sem, out_sem):
    e = pl.program_id(0)
    nb = nblk_ref[e]
    base = start_ref[e]
    used = used_ref[0]
    n_in = x_buf.shape[0]
    n_out = o_buf.shape[0]
    ahead = n_in - 1

    def rows(g):
        return pl.ds(g * FFN_BLOCK, FFN_BLOCK)

    def in_copy(g):
        slot = lax.rem(g, n_in)
        return pltpu.make_async_copy(buf_hbm.at[rows(g)], x_buf.at[slot], in_sem.at[slot])

    def out_copy(g):
        slot = lax.rem(g, n_out)
        return pltpu.make_async_copy(o_buf.at[slot], ys_hbm.at[rows(g)], out_sem.at[slot])

    @pl.when(e == 0)
    def _():
        for g in range(ahead):
            @pl.when(g < used)
            def _():
                in_copy(g).start()

    @pl.when(nb > 0)
    def _():
        wgb_ref[...] = wg_ref[0].astype(BF16)
        wub_ref[...] = wu_ref[0].astype(BF16)
        wdb_ref[...] = wd_ref[0].astype(BF16)

        def block(j, carry):
            g = base + j
            in_copy(g).wait()

            @pl.when(g + ahead < used)
            def _():
                in_copy(g + ahead).start()

            @pl.when(g >= n_out)
            def _():
                out_copy(g - n_out).wait()

            islot = lax.rem(g, n_in)
            oslot = lax.rem(g, n_out)
            slab = FFN_BLOCK // FFN_SLABS
            xs = [x_buf[islot, s * slab:(s + 1) * slab, :].astype(BF16) for s in range(FFN_SLABS)]
            gates = [_dot(x, wgb_ref[...]) for x in xs]
            ups = [_dot(x, wub_ref[...]) for x in xs]
            hidden = [(_silu(gt) * up).astype(BF16) for gt, up in zip(gates, ups)]
            for s in range(FFN_SLABS):
                o_buf[oslot, s * slab:(s + 1) * slab, :] = _dot(hidden[s], wdb_ref[...])
            out_copy(g).start()
            return carry

        lax.fori_loop(0, nb, block, 0)

    @pl.when(e == pl.num_programs(0) - 1)
    def _():
        for back in range(n_out):
            @pl.when(used - 1 - back >= 0)
            def _():
                out_copy(used - 1 - back).wait()
        n_blocks = ys_hbm.shape[0] // FFN_BLOCK
        o_buf[0] = jnp.zeros(o_buf.shape[1:], o_buf.dtype)

        def clear(b):
            return pltpu.make_async_copy(o_buf.at[0], ys_hbm.at[pl.ds(b * FFN_BLOCK, FFN_BLOCK)], out_sem.at[0])

        lax.fori_loop(used_ref[0], n_blocks, lambda b, carry: (clear(b).start(), carry)[1], 0)
        lax.fori_loop(used_ref[0], n_blocks, lambda b, carry: (clear(b).wait(), carry)[1], 0)


def _ffn(seg_start, seg_blocks, n_used, buf, w_gate, w_up, w_down):
    n_rows, d = buf.shape
    n_experts = w_gate.shape[0]
    grid_spec = pltpu.PrefetchScalarGridSpec(
        num_scalar_prefetch=3,
        grid=(n_experts,),
        in_specs=[pl.BlockSpec(memory_space=pl.ANY),
                  pl.BlockSpec((1, d, D_EXPERT), lambda e, *_: (e, 0, 0)),
                  pl.BlockSpec((1, d, D_EXPERT), lambda e, *_: (e, 0, 0)),
                  pl.BlockSpec((1, D_EXPERT, d), lambda e, *_: (e, 0, 0))],
        out_specs=pl.BlockSpec(memory_space=pl.ANY),
        scratch_shapes=[pltpu.VMEM((d, D_EXPERT), BF16), pltpu.VMEM((d, D_EXPERT), BF16),
                        pltpu.VMEM((D_EXPERT, d), BF16),
                        pltpu.VMEM((FFN_IN_SLOTS, FFN_BLOCK, d), F32), pltpu.VMEM((FFN_OUT_SLOTS, FFN_BLOCK, d), F32),
                        pltpu.SemaphoreType.DMA((FFN_IN_SLOTS,)), pltpu.SemaphoreType.DMA((FFN_OUT_SLOTS,))],
    )
    return pl.pallas_call(
        _ffn_kernel,
        grid_spec=grid_spec,
        out_shape=jax.ShapeDtypeStruct((n_rows, d), F32),
        compiler_params=_cparams(("arbitrary",)),
        name="ffn",
    )(seg_start, seg_blocks, n_used, buf, w_gate, w_up, w_down)


def _rope_tables(seq_len, n_ctx_rows):
    lane = np.arange(LANES)
    within = lane % DDK
    freq = within % 16
    use_col = within >= DDK // 2
    sign = np.where((lane % 32) < 16, -1.0, 1.0).astype(np.float32)
    half = DDK // 2
    inv_freq = (ROPE_THETA ** (-np.arange(0, half, 2, dtype=np.float32) / np.float32(half))).astype(np.float32)
    t = np.arange(seq_len)
    pos = np.where(use_col[None, :], (t % GRID_W)[:, None], (t // GRID_W)[:, None]).astype(np.float32)
    ang = pos * inv_freq[freq][None, :]
    cos = np.concatenate([np.cos(ang), np.ones((n_ctx_rows, LANES), np.float32)], axis=0)
    sin = np.concatenate([np.sin(ang) * sign[None, :], np.zeros((n_ctx_rows, LANES), np.float32)], axis=0)
    return jnp.asarray(cos, F32), jnp.asarray(sin, F32)


def _block_tri(n, block, upper):
    i = np.arange(n)
    same = (i[:, None] // block) == (i[None, :] // block)
    tri = (i[None, :] >= i[:, None]) if upper else (i[None, :] <= i[:, None])
    return jnp.asarray((same & tri).astype(np.float32), dtype=BF16)


def _pack_w_in(w_in):
    off = G_QKV + G_Z
    v_start = G_COLS + 2 * D_QK
    pad = jnp.zeros((D, LANES - G_GATES), w_in.dtype)
    w_gate = jnp.concatenate([w_in[:, off:off + G_GATES], pad, w_in[:, off + G_GATES:G_COLS], pad], axis=1)
    return (w_in[:, :off].astype(BF16), w_gate.astype(BF16), w_in[:, G_COLS:v_start].astype(BF16),
            w_in[:, v_start:].T.astype(BF16))


def kernel(x, c, ctx, c_ctx, w_ada, b_ada, w_in, conv_w, gdn_a_log, gdn_dt_bias, gdn_norm_w, diff_lambda,
           diff_norm_w, w_out, ln1_g, ln1_b, w_router_group, b_router_group, w_router_expert, b_router_expert,
           w_expert_gate, w_expert_up, w_expert_down, ln2_g, ln2_b):
    depth = w_ada.shape[0]
    assert depth == 1, "single-layer block: the context stream never feeds a later layer"
    bsz, seq, _ = x.shape
    n_ctx = ctx.shape[1]
    assert seq % ROW_TILE == 0 and n_ctx % ROW_TILE == 0 and seq % GRID_W == 0 and bsz < 8
    n_lat_tiles = seq // ROW_TILE
    alpha = (2.0 * depth) ** 0.25
    lam_init = 0.8 - 0.6 * math.exp(-0.3 * 0)
    i = 0

    cc = jnp.zeros((8, D), F32).at[:bsz].set(c).at[7].set(c_ctx)
    mod = _ada(cc, w_ada[i], b_ada[i])
    mod3 = mod.reshape(8, 1, 6 * D)

    cos_t, sin_t = _rope_tables(seq, n_ctx)
    qkv, z, gt, dq, dk, dvt = _proj(x, ctx, mod3, cos_t, sin_t, *_pack_w_in(w_in[i]))

    conv_w8 = jnp.zeros((8, G_QKV), F32).at[:GCONV].set(conv_w[i])
    alog_v = jnp.zeros((1, LANES), F32).at[0, :G_GATES].set(gdn_a_log[i].reshape(-1))
    dtb_v = jnp.zeros((1, LANES), F32).at[0, :G_GATES].set(gdn_dt_bias[i].reshape(-1))
    gq, gk, gv, beta, gc = _prep(qkv, gt, conv_w8, alog_v, dtb_v,
                                 _block_tri(ROW_TILE, CHUNK, False), _block_tri(ROW_TILE, CHUNK, True), n_lat_tiles)
    uw = _chunk(gk, gv, beta, gc)
    o_f, o_r = _scan(gq, gk, gc, uw, n_lat_tiles)

    y_diff = _attn(dq, dk, dvt, diff_lambda[i], diff_norm_w[i], lam_init, seq, tq=min(ATTN_Q_TILE, seq))

    w_router = jnp.zeros((D, LANES), F32).at[:, :N_GROUPS].set(w_router_group[i]) \
        .at[:, N_GROUPS:N_GROUPS + N_EXPERTS].set(w_router_expert[i])
    w_hi = w_router.astype(BF16)
    w_router3 = jnp.concatenate([w_hi, (w_router - w_hi.astype(F32)).astype(BF16)], axis=1)
    b_router = jnp.zeros((1, LANES), F32).at[0, :N_GROUPS].set(b_router_group[i]) \
        .at[0, N_GROUPS:N_GROUPS + N_EXPERTS].set(b_router_expert[i])
    x1, h2, logits = _out(o_f, o_r, z, y_diff, x, mod3, gdn_norm_w[i], w_out[i].astype(BF16), ln1_g[i], ln1_b[i],
                          w_router3, b_router, alpha)

    t = bsz * seq
    tri_strict = jnp.asarray(np.triu(np.ones((ROUTE_TILE, ROUTE_TILE), np.float32), 1), dtype=BF16)
    ids, gate, counts = _route(logits.reshape(t, LANES), tri_strict)

    cnt = counts[N_GROUPS:N_GROUPS + N_EXPERTS, 0].astype(jnp.int32)
    padded = (cnt + FFN_BLOCK - 1) // FFN_BLOCK * FFN_BLOCK
    p_end = jnp.cumsum(padded)
    p_start = p_end - padded
    n_blocks = (2 * t) // FFN_BLOCK + N_EXPERTS
    n_used = (p_end[-1:] // FFN_BLOCK).astype(jnp.int32)
    tail_block = jnp.where(cnt > 0, p_end // FFN_BLOCK - 1, -1).astype(jnp.int32)
    one_hot = ids[0:2, :, None] == jnp.arange(N_EXPERTS, dtype=jnp.int32)
    row_start = jnp.sum(jnp.where(one_hot, p_start, 0), axis=-1)
    dest = jnp.swapaxes((row_start + ids[2:4]).reshape(2, t // DMA_TOKENS, DMA_TOKENS), 0, 1)
    dest = dest.reshape(-1).astype(jnp.int32)

    buf = _dispatch(tail_block, n_used, dest, h2.reshape(t, D), n_blocks * FFN_BLOCK)
    ys = _ffn((p_start // FFN_BLOCK).astype(jnp.int32), (padded // FFN_BLOCK).astype(jnp.int32), n_used, buf,
              w_expert_gate[i], w_expert_up[i], w_expert_down[i])
    out = _combine(dest, ys, gate, x1.reshape(t, D), mod3, ln2_g[i], ln2_b[i], alpha, seq)
    return out.reshape(bsz, seq, D)
```

```python
import functools
import math

import numpy as np
import jax
import jax.numpy as jnp
from jax import lax
from jax.experimental import pallas as pl
from jax.experimental.pallas import tpu as pltpu

F32 = jnp.float32
BF16 = jnp.bfloat16

D = 1024
GRID_W = 64
ROPE_THETA = 10000.0
GH = 4
GDK = 128
GDV = 128
GCONV = 5
DH = 4
DDK = 64
DDV = 128
G_QK = GH * GDK
G_QKV = 2 * G_QK + GH * GDV
G_Z = GH * GDV
G_GATES = 2 * GH
G_COLS = G_QKV + G_Z + 2 * G_GATES
D_QK = DH * 2 * DDK
N_GROUPS = 4
E_PER_GROUP = 8
N_EXPERTS = N_GROUPS * E_PER_GROUP
D_EXPERT = 512
LN_EPS = 1e-5
NORM_EPS = 1e-6

LANES = 128
ROW_TILE = 256
CHUNK = 64
OUT_TILE = 512
CHUNK_BATCH = 2
SCAN_BATCH = 2
FFN_BLOCK = 256
FFN_SLABS = 2
FFN_IN_SLOTS = 6
FFN_OUT_SLOTS = 4
KEY_TILE = 512
ATTN_Q_TILE = 1024
SCORE_LOOKAHEAD = 2
VMEM_LIMIT = 56 * 1024 * 1024

Q_SCALE = DDK ** -0.5 * math.log2(math.e)


def _cparams(sem):
    return pltpu.CompilerParams(dimension_semantics=sem, vmem_limit_bytes=VMEM_LIMIT)


def _dot(a, b):
    return jnp.dot(a, b, preferred_element_type=F32)


def _dot_nt(a, b):
    return lax.dot_general(a, b, (((1,), (1,)), ((), ())), preferred_element_type=F32)


def _normalize(x):
    mu = jnp.mean(x, axis=-1, keepdims=True)
    xc = x - mu
    var = jnp.mean(xc * xc, axis=-1, keepdims=True)
    return xc * lax.rsqrt(var + LN_EPS)


def _silu(x):
    return x * jax.nn.sigmoid(x)


def _split3(x):
    hi = x.astype(BF16)
    r = x - hi.astype(F32)
    mid = r.astype(BF16)
    lo = (r - mid.astype(F32)).astype(BF16)
    return hi, mid, lo


def _ada_kernel(c_ref, w_ref, b_ref, o_ref):
    s = _silu(c_ref[...])
    s_hi = s.astype(BF16)
    s_lo = (s - s_hi.astype(F32)).astype(BF16)
    w = w_ref[...]
    w_hi = w.astype(BF16)
    w_lo = (w - w_hi.astype(F32)).astype(BF16)
    rows = s.shape[0]
    both = _dot(jnp.concatenate([s_hi, s_lo], axis=0), w_hi)
    o_ref[...] = both[:rows] + both[rows:] + _dot(s_hi, w_lo) + b_ref[...]


def _ada(cc, w_ada, b_ada):
    n = w_ada.shape[1]
    bn = 1024
    return pl.pallas_call(
        _ada_kernel,
        grid=(n // bn,),
        in_specs=[pl.BlockSpec((8, D), lambda j: (0, 0)),
                  pl.BlockSpec((D, bn), lambda j: (0, j)),
                  pl.BlockSpec((1, bn), lambda j: (0, j))],
        out_specs=pl.BlockSpec((8, bn), lambda j: (0, j)),
        out_shape=jax.ShapeDtypeStruct((8, n), F32),
        compiler_params=_cparams(("arbitrary",)),
        name="ada",
    )(cc, w_ada, b_ada.reshape(1, n))


def _rope(v, cos, sin, low_half):
    fwd = pltpu.roll(v, LANES - 16, axis=1)
    bwd = pltpu.roll(v, 16, axis=1)
    return v * cos + jnp.where(low_half, fwd, bwd) * sin


def _proj_kernel(n_lat_tiles, x_ref, ctx_ref, sh_ref, sc_ref, cos_ref, sin_ref, wg_ref, wgate_ref, wqk_ref, wvt_ref,
                 qkv_ref, z_ref, gt_ref, dq_ref, dk_ref, dvt_ref):
    rows = jnp.where(pl.program_id(1) >= n_lat_tiles, ctx_ref[0], x_ref[0])
    h = (_normalize(rows) * (1.0 + sc_ref[0]) + sh_ref[0]).astype(BF16)
    q_all = _dot(h, wqk_ref[:, :D_QK])
    k_all = _dot(h, wqk_ref[:, D_QK:])
    qkv_ref[0] = _dot(h, wg_ref[:, :G_QKV])
    z_ref[0] = _dot(h, wg_ref[:, G_QKV:]).astype(z_ref.dtype)
    gt_ref[0] = _dot(h, wgate_ref[...])
    cos = cos_ref[...]
    sin = sin_ref[...]
    lane = lax.broadcasted_iota(jnp.int32, cos.shape, 1)
    low_half = (lane % 32) < 16
    for j in range(D_QK // LANES):
        slab = slice(j * LANES, (j + 1) * LANES)
        dq_ref[0, :, slab] = (_rope(q_all[:, slab], cos, sin, low_half) * Q_SCALE).astype(BF16)
        dk_ref[0, :, slab] = _rope(k_all[:, slab], cos, sin, low_half).astype(BF16)
    dvt_ref[0] = _dot_nt(wvt_ref[...], h).astype(BF16)


def _proj(x, ctx, mod3, cos_t, sin_t, w_gdn, w_gate, w_qk, wv_t):
    bsz, seq, _ = x.shape
    lc = seq + ctx.shape[1]
    nt = lc // ROW_TILE
    n_lat_tiles = seq // ROW_TILE
    ctx_row = mod3.shape[0] - 1

    def mod_idx(col):
        return lambda b, i: (jnp.where(i >= n_lat_tiles, ctx_row, b), 0, col)

    def row_spec(width):
        return pl.BlockSpec((1, ROW_TILE, width), lambda b, i: (b, i, 0))

    outs = [(G_QKV, F32), (G_Z, BF16), (2 * LANES, F32), (D_QK, BF16), (D_QK, BF16)]
    return pl.pallas_call(
        functools.partial(_proj_kernel, n_lat_tiles),
        grid=(bsz, nt),
        in_specs=[pl.BlockSpec((1, ROW_TILE, D), lambda b, i: (b, jnp.minimum(i, n_lat_tiles - 1), 0)),
                  pl.BlockSpec((1, ROW_TILE, D), lambda b, i: (b, jnp.maximum(i - n_lat_tiles, 0), 0)),
                  pl.BlockSpec((1, 1, D), mod_idx(0)),
                  pl.BlockSpec((1, 1, D), mod_idx(1)),
                  pl.BlockSpec((ROW_TILE, LANES), lambda b, i: (i, 0)),
                  pl.BlockSpec((ROW_TILE, LANES), lambda b, i: (i, 0)),
                  pl.BlockSpec(w_gdn.shape, lambda b, i: (0, 0)),
                  pl.BlockSpec(w_gate.shape, lambda b, i: (0, 0)),
                  pl.BlockSpec(w_qk.shape, lambda b, i: (0, 0)),
                  pl.BlockSpec(wv_t.shape, lambda b, i: (0, 0))],
        out_specs=[row_spec(w) for w, _ in outs]
        + [pl.BlockSpec((1, DH * DDV, ROW_TILE), lambda b, i: (b, 0, i))],
        out_shape=[jax.ShapeDtypeStruct((bsz, lc, w), dt) for w, dt in outs]
        + [jax.ShapeDtypeStruct((bsz, DH * DDV, lc), BF16)],
        compiler_params=_cparams(("arbitrary", "arbitrary")),
        name="proj",
    )(x, ctx, mod3, mod3, cos_t, sin_t, w_gdn, w_gate, w_qk, wv_t)


def _prep_kernel(n_lat_tiles, nt, main_ref, prev_ref, next_ref, gt_ref, cw_ref, alog_ref, dtb_ref,
                 lo_ref, up_ref, q_ref, k_ref, v_ref, beta_ref, gc_ref, ext_ref):
    i = pl.program_id(1)
    halo = prev_ref.shape[1]
    has_prev = jnp.logical_and(i != 0, i != n_lat_tiles)
    has_next = jnp.logical_and(i != n_lat_tiles - 1, i != nt - 1)
    ext_ref[0:halo, :] = prev_ref[0] * has_prev.astype(F32)
    ext_ref[halo:halo + ROW_TILE, :] = main_ref[0]
    ext_ref[halo + ROW_TILE:, :] = next_ref[0] * has_next.astype(F32)
    pad = GCONV // 2
    acc = None
    ext = ext_ref[...]
    n_ext = ext.shape[0]
    for j in range(GCONV):
        shifted = ext if j == pad else pltpu.roll(ext, (pad - j) % n_ext, axis=0)
        term = shifted[halo:halo + ROW_TILE, :] * cw_ref[j:j + 1, :]
        acc = term if acc is None else acc + term
    qkv = _silu(acc)
    for h in range(GH):
        q = qkv[:, h * GDK:(h + 1) * GDK]
        q_ref[0, :, h * GDK:(h + 1) * GDK] = (q * (lax.rsqrt(jnp.sum(q * q, axis=-1, keepdims=True) + NORM_EPS)
                                                 * (GDK ** -0.5))).astype(q_ref.dtype)
        k = qkv[:, G_QK + h * GDK:G_QK + (h + 1) * GDK]
        k_ref[0, :, h * GDK:(h + 1) * GDK] = (
            k * lax.rsqrt(jnp.sum(k * k, axis=-1, keepdims=True) + NORM_EPS)).astype(k_ref.dtype)
    v_ref[0] = qkv[:, 2 * G_QK:].astype(v_ref.dtype)
    gt = gt_ref[0]
    beta_ref[0] = jax.nn.sigmoid(gt[:, :LANES])
    a = gt[:, LANES:] + dtb_ref[...]
    softplus = jnp.maximum(a, 0.0) + jnp.log(1.0 + jnp.exp(-jnp.abs(a)))
    g = -jnp.exp(alog_ref[...]) * softplus
    lo = lo_ref[...]
    up = up_ref[...]
    fwd = None
    bwd = None
    for part in _split3(g):
        f = _dot(lo, part)
        r = _dot(up, part)
        fwd = f if fwd is None else fwd + f
        bwd = r if bwd is None else bwd + r
    lane = lax.broadcasted_iota(jnp.int32, g.shape, 1)
    gc_ref[0] = jnp.where(lane < GH, fwd, bwd)


def _prep(qkv, gt, conv_w8, alog_v, dtb_v, tri_lo, tri_up, n_lat_tiles):
    bsz, lc, _ = qkv.shape
    nt = lc // ROW_TILE
    halo = 8
    per = ROW_TILE // halo
    last = lc // halo - 1

    def row_spec(width):
        return pl.BlockSpec((1, ROW_TILE, width), lambda b, i: (b, i, 0))

    def const_spec(shape):
        return pl.BlockSpec(shape, lambda b, i: tuple(0 for _ in shape))

    outs = [(G_QK, BF16), (G_QK, BF16), (GH * GDV, BF16), (LANES, F32), (LANES, F32)]
    return pl.pallas_call(
        functools.partial(_prep_kernel, n_lat_tiles, nt),
        grid=(bsz, nt),
        in_specs=[row_spec(G_QKV),
                  pl.BlockSpec((1, halo, G_QKV), lambda b, i: (b, jnp.maximum(i * per - 1, 0), 0)),
                  pl.BlockSpec((1, halo, G_QKV), lambda b, i: (b, jnp.minimum((i + 1) * per, last), 0)),
                  row_spec(2 * LANES),
                  const_spec((8, G_QKV)), const_spec((1, LANES)), const_spec((1, LANES)),
                  const_spec((ROW_TILE, ROW_TILE)), const_spec((ROW_TILE, ROW_TILE))],
        out_specs=[row_spec(w) for w, _ in outs],
        out_shape=[jax.ShapeDtypeStruct((bsz, lc, w), dt) for w, dt in outs],
        scratch_shapes=[pltpu.VMEM((ROW_TILE + 2 * halo, G_QKV), F32)],
        compiler_params=_cparams(("arbitrary", "arbitrary")),
        name="prep",
    )(qkv, qkv, qkv, gt, conv_w8, alog_v, dtb_v, tri_lo, tri_up)


def _chunk_masks(n, rev):
    row = lax.broadcasted_iota(jnp.int32, (n, n), 0)
    col = lax.broadcasted_iota(jnp.int32, (n, n), 1)
    same = (row // CHUNK) == (col // CHUNK)
    incl = jnp.logical_and(same, (col >= row) if rev else (col <= row))
    strict = jnp.logical_and(same, (col > row) if rev else (col < row))
    return incl, strict


def _decay(gcol, grow, incl):
    return jnp.where(incl, jnp.exp(jnp.minimum(gcol - grow, 0.0)), 0.0)


def _chunk_kernel(k_ref, v_ref, beta_ref, gc_ref, uw_ref):
    n = k_ref.shape[1]
    row = lax.broadcasted_iota(jnp.int32, (n, n), 0)
    col = lax.broadcasted_iota(jnp.int32, (n, n), 1)
    eye = (row == col).astype(F32)
    sizes = [2 ** j for j in range(1, int(math.log2(CHUNK)))]
    same = {r: (row // r) == (col // r) for r in sizes + [CHUNK]}
    a_all, t_all, x_all, where = [], [], [], []
    masks = [_chunk_masks(n, rev) for rev in (False, True)]
    for bb in range(k_ref.shape[0]):
        gc = gc_ref[bb]
        gct = gc.T
        beta = beta_ref[bb]
        gram = [_dot_nt(k_ref[bb, :, h * GDK:(h + 1) * GDK], k_ref[bb, :, h * GDK:(h + 1) * GDK]) for h in range(GH)]
        for d in range(2):
            incl, strict = masks[d]
            for h in range(GH):
                c = d * GH + h
                k = k_ref[bb, :, h * GDK:(h + 1) * GDK].astype(F32)
                v = v_ref[bb, :, h * GDV:(h + 1) * GDV].astype(F32)
                bcol = beta[:, c:c + 1]
                gcol = gc[:, c:c + 1]
                grow = gct[c:c + 1, :]
                kb = k * bcol
                a = jnp.where(strict, gram[h] * bcol * _decay(gcol, grow, incl), 0.0)
                a_all.append(a.astype(BF16))
                t_all.append((eye - jnp.where(same[2], a, 0.0)).astype(BF16))
                x_all.append(jnp.concatenate([v * bcol, kb * jnp.exp(gcol)], axis=1).astype(BF16))
                where.append((bb, c))
    n_prob = len(a_all)
    zero = jnp.zeros((n, n), BF16)
    for r in sizes:
        off_mask = jnp.logical_and(same[2 * r], jnp.logical_not(same[r]))
        inner = [_dot(jnp.where(off_mask, a_all[c], zero), t_all[c]).astype(BF16) for c in range(n_prob)]
        t_all = [t_all[c] - _dot(t_all[c], inner[c]).astype(BF16) for c in range(n_prob)]
    for p in range(n_prob):
        bb, c = where[p]
        uw_ref[bb, c] = _dot(t_all[p], x_all[p]).astype(uw_ref.dtype)


def _chunk(k, v, beta, gc):
    bsz, lc, _ = k.shape
    nt = lc // ROW_TILE
    nb = CHUNK_BATCH if bsz % CHUNK_BATCH == 0 else 1

    def row_spec(width):
        return pl.BlockSpec((nb, ROW_TILE, width), lambda b, i: (b, i, 0))

    return pl.pallas_call(
        _chunk_kernel,
        grid=(bsz // nb, nt),
        in_specs=[row_spec(G_QK), row_spec(GH * GDV), row_spec(LANES), row_spec(LANES)],
        out_specs=pl.BlockSpec((nb, 2 * GH, ROW_TILE, GDV + GDK), lambda b, i: (b, 0, i, 0)),
        out_shape=jax.ShapeDtypeStruct((bsz, 2 * GH, lc, GDV + GDK), BF16),
        compiler_params=_cparams(("arbitrary", "arbitrary")),
        name="chunk",
    )(k, v, beta, gc)


def _scan_kernel(qf_ref, kf_ref, gf_ref, uwf_ref, qr_ref, kr_ref, gr_ref, uwr_ref,
                 of_ref, or_ref, s_ref):
    @pl.when(pl.program_id(1) == 0)
    def _():
        s_ref[...] = jnp.zeros_like(s_ref)

    n_chunks = ROW_TILE // CHUNK
    n_batch = qf_ref.shape[0]
    row = lax.broadcasted_iota(jnp.int32, (CHUNK, CHUNK), 0)
    col = lax.broadcasted_iota(jnp.int32, (CHUNK, CHUNK), 1)
    dirs = ((qf_ref, kf_ref, gf_ref, uwf_ref, of_ref), (qr_ref, kr_ref, gr_ref, uwr_ref, or_ref))
    chains = [(bb, d, h) for bb in range(n_batch) for d in range(2) for h in range(GH)]
    state = [s_ref[i] for i in range(len(chains))]

    def chunk_rows(step, d):
        ci = n_chunks - 1 - step if d == 1 else step
        return slice(ci * CHUNK, (ci + 1) * CHUNK)

    pre = []
    for step in range(n_chunks):
        gates = {}
        for bb in range(n_batch):
            for d in range(2):
                gc = dirs[d][2][bb, chunk_rows(step, d), :]
                gates[bb, d] = (gc, gc.T)
        per_chain = []
        for bb, d, h in chains:
            q_ref, k_ref, _, uw_ref, _ = dirs[d]
            rev = d == 1
            incl = (col >= row) if rev else (col <= row)
            rows = chunk_rows(step, d)
            gc, gct = gates[bb, d]
            c = d * GH + h
            q_b = q_ref[bb, rows, h * GDK:(h + 1) * GDK]
            k_b = k_ref[bb, rows, h * GDK:(h + 1) * GDK]
            q = q_b.astype(F32)
            k = k_b.astype(F32)
            gcol = gc[:, c:c + 1]
            grow = gct[c:c + 1, :]
            last = 0 if rev else CHUNK - 1
            g_last = gcol[last:last + 1, :]
            a_qk = (_dot_nt(q_b, k_b) * _decay(gcol, grow, incl)).astype(BF16)
            wq = jnp.concatenate([uw_ref[bb, h, rows, GDV:], (q * jnp.exp(gcol)).astype(BF16)], axis=0)
            k_tail_t = (k * jnp.exp(g_last - gcol)).T.astype(BF16)
            per_chain.append((a_qk, wq, k_tail_t, jnp.exp(g_last)))
        pre.append(per_chain)

    for step in range(n_chunks):
        ws = [_dot(pre[step][i][1], state[i].astype(BF16)) for i in range(len(chains))]
        v_new = [(dirs[d][3][bb, h, chunk_rows(step, d), :GDV].astype(F32) - ws[i][:CHUNK]).astype(BF16)
                 for i, (bb, d, h) in enumerate(chains)]
        for i, (bb, d, h) in enumerate(chains):
            a_qk, _, k_tail_t, decay_last = pre[step][i]
            dirs[d][4][bb, chunk_rows(step, d), h * GDV:(h + 1) * GDV] = (
                ws[i][CHUNK:] + _dot(a_qk, v_new[i])).astype(BF16)
            state[i] = state[i] * decay_last + _dot(k_tail_t, v_new[i])
    for i in range(len(chains)):
        s_ref[i] = state[i]


def _scan(q, k, gc, uw, n_lat_tiles):
    bsz, lc, _ = q.shape
    nt = lc // ROW_TILE
    n_ctx_tiles = nt - n_lat_tiles

    def fwd(b, i):
        return jnp.where(i < n_ctx_tiles, n_lat_tiles + i, i - n_ctx_tiles)

    def bwd(b, i):
        return jnp.where(i < n_ctx_tiles, nt - 1 - i, n_lat_tiles - 1 - (i - n_ctx_tiles))

    nb = SCAN_BATCH if bsz % SCAN_BATCH == 0 else 1

    def specs(tile):
        return [pl.BlockSpec((nb, ROW_TILE, G_QK), lambda b, i: (b, tile(b, i), 0)),
                pl.BlockSpec((nb, ROW_TILE, G_QK), lambda b, i: (b, tile(b, i), 0)),
                pl.BlockSpec((nb, ROW_TILE, LANES), lambda b, i: (b, tile(b, i), 0))]

    def uw_spec(tile, d):
        return pl.BlockSpec((nb, GH, ROW_TILE, GDV + GDK), lambda b, i: (b, d, tile(b, i), 0))

    def out_spec(tile):
        return pl.BlockSpec((nb, ROW_TILE, GH * GDV), lambda b, i: (b, tile(b, i), 0))

    return pl.pallas_call(
        _scan_kernel,
        grid=(bsz // nb, nt),
        in_specs=specs(fwd) + [uw_spec(fwd, 0)] + specs(bwd) + [uw_spec(bwd, 1)],
        out_specs=[out_spec(fwd), out_spec(bwd)],
        out_shape=[jax.ShapeDtypeStruct((bsz, lc, GH * GDV), BF16)] * 2,
        scratch_shapes=[pltpu.VMEM((nb * 2 * GH, GDK, GDV), F32)],
        compiler_params=_cparams(("arbitrary", "arbitrary")),
        name="scan",
    )(q, k, gc, uw, q, k, gc, uw)


def _attn_kernel(lam_init, q_ref, k_ref, vt_ref, lam_ref, nw_ref, o_ref):
    lam = lam_ref[...]
    lam_full = (jnp.exp(jnp.sum(lam[0:1] * lam[1:2], axis=-1, keepdims=True))
                - jnp.exp(jnp.sum(lam[2:3] * lam[3:4], axis=-1, keepdims=True)) + lam_init)
    q = q_ref[0]
    tq = q.shape[0]
    n_keys = k_ref.shape[1]
    lane = lax.broadcasted_iota(jnp.int32, q.shape, 1)
    zero = jnp.zeros_like(q)
    qc = [jnp.where(lane < DDK, q, zero), jnp.where(lane >= DDK, q, zero)]
    m = [jnp.full((1, tq), -jnp.inf, F32) for _ in range(2)]
    l = [jnp.zeros((1, tq), F32) for _ in range(2)]
    acc = [jnp.zeros((DDV, tq), F32) for _ in range(2)]
    n_tiles = n_keys // KEY_TILE
    bounds = [t * KEY_TILE for t in range(n_tiles)] + [n_keys]

    def scores(t):
        k_t = k_ref[0, bounds[t]:bounds[t + 1], :]
        tiles = [_dot_nt(k_t, qc[c]) for c in range(2)]
        return [(s, jnp.max(s, axis=0, keepdims=True)) for s in tiles]

    pending = [scores(t) for t in range(min(SCORE_LOOKAHEAD, n_tiles))]
    for t in range(n_tiles):
        s_cur = pending.pop(0)
        if t + SCORE_LOOKAHEAD < n_tiles:
            pending.append(scores(t + SCORE_LOOKAHEAD))
        width = bounds[t + 1] - bounds[t]
        vt_t = jnp.concatenate([vt_ref[0, :, bounds[t]:bounds[t + 1]], jnp.ones((16, width), BF16)], axis=0)
        for c in range(2):
            s, s_max = s_cur[c]
            m_new = jnp.maximum(m[c], s_max)
            p = jnp.exp2(s - m_new)
            scale = jnp.exp2(m[c] - m_new)
            pv = _dot(vt_t, p.astype(BF16))
            l[c] = scale * l[c] + pv[DDV:DDV + 1]
            acc[c] = scale * acc[c] + pv[:DDV]
            m[c] = m_new
    o = acc[0] / l[0] - lam_full * (acc[1] / l[1])
    y = o * lax.rsqrt(jnp.mean(o * o, axis=0, keepdims=True) + NORM_EPS)
    o_ref[0] = (y.T * nw_ref[...] * (1.0 - lam_init)).astype(o_ref.dtype)


def _attn(dq, dk, dvt, lam, norm_w, lam_init, lq, tq):
    bsz, lc, _ = dq.shape
    return pl.pallas_call(
        functools.partial(_attn_kernel, lam_init),
        grid=(bsz, DH, lq // tq),
        in_specs=[pl.BlockSpec((1, tq, 2 * DDK), lambda b, h, i: (b, i, h)),
                  pl.BlockSpec((1, lc, 2 * DDK), lambda b, h, i: (b, 0, h)),
                  pl.BlockSpec((1, DDV, lc), lambda b, h, i: (b, h, 0)),
                  pl.BlockSpec((4, DDK), lambda b, h, i: (0, 0)),
                  pl.BlockSpec((1, DDV), lambda b, h, i: (0, 0))],
        out_specs=pl.BlockSpec((1, tq, DDV), lambda b, h, i: (b, i, h)),
        out_shape=jax.ShapeDtypeStruct((bsz, lq, DH * DDV), BF16),
        compiler_params=_cparams(("arbitrary", "arbitrary", "arbitrary")),
        name="attn",
    )(dq, dk, dvt, lam, norm_w.reshape(1, DDV))


def _out_kernel(alpha, of_ref, or_ref, z_ref, yd_ref, x_ref, g1_ref, sh2_ref, sc2_ref, gnw_ref,
                wo_ref, lng_ref, lnb_ref, wr_ref, br_ref, x1_ref, h2_ref, lg_ref):
    o = of_ref[0].astype(F32) + or_ref[0].astype(F32)
    z = z_ref[0].astype(F32)
    heads = []
    for h in range(GH):
        oh = o[:, h * GDV:(h + 1) * GDV]
        heads.append((oh * lax.rsqrt(jnp.mean(oh * oh, axis=-1, keepdims=True) + NORM_EPS) * gnw_ref[...]
                      * _silu(z[:, h * GDV:(h + 1) * GDV])).astype(BF16))
    mix = jnp.concatenate(heads + [yd_ref[0]], axis=1)
    x1 = _normalize(alpha * x_ref[0] + g1_ref[0] * _dot(mix, wo_ref[...])) * lng_ref[...] + lnb_ref[...]
    x1_ref[0] = x1
    h2 = _normalize(x1) * (1.0 + sc2_ref[0]) + sh2_ref[0]
    h2_ref[0] = h2
    h_hi = h2.astype(BF16)
    h_lo = (h2 - h_hi.astype(F32)).astype(BF16)
    both = _dot(h_hi, wr_ref[...])
    lg_ref[0] = br_ref[...] + both[:, :LANES] + both[:, LANES:] + _dot(h_lo, wr_ref[:, :LANES])


def _out(o_f, o_r, z, y_diff, xc, mod3, gdn_norm_w, w_out, ln_g, ln_b, w_router3, b_router, alpha):
    bsz, lq, _ = y_diff.shape
    tile = OUT_TILE if lq % OUT_TILE == 0 else ROW_TILE
    nt = lq // tile

    def row_spec(width):
        return pl.BlockSpec((1, tile, width), lambda b, i: (b, i, 0))

    def mod_spec(col):
        return pl.BlockSpec((1, 1, D), lambda b, i: (b, 0, col))

    def const_spec(shape):
        return pl.BlockSpec(shape, lambda b, i: tuple(0 for _ in shape))

    return pl.pallas_call(
        functools.partial(_out_kernel, alpha),
        grid=(bsz, nt),
        in_specs=[row_spec(GH * GDV), row_spec(GH * GDV), row_spec(G_Z), row_spec(DH * DDV), row_spec(D),
                  mod_spec(2), mod_spec(3), mod_spec(4),
                  const_spec((1, GDV)), const_spec((G_Z + DH * DDV, D)),
                  const_spec((1, D)), const_spec((1, D)),
                  const_spec((D, 2 * LANES)), const_spec((1, LANES))],
        out_specs=[row_spec(D), row_spec(D), row_spec(LANES)],
        out_shape=[jax.ShapeDtypeStruct((bsz, lq, D), F32), jax.ShapeDtypeStruct((bsz, lq, D), F32),
                   jax.ShapeDtypeStruct((bsz, lq, LANES), F32)],
        compiler_params=_cparams(("arbitrary", "arbitrary")),
        name="out",
    )(o_f, o_r, z, y_diff, xc, mod3, mod3, mod3, gdn_norm_w.reshape(1, GDV), w_out,
      ln_g.reshape(1, D), ln_b.reshape(1, D), w_router3, b_router)


def _route_kernel(lg_ref, tri_ref, ids_ref, gate_ref, cnt_ref, run_ref):
    @pl.when(pl.program_id(0) == 0)
    def _():
        run_ref[...] = jnp.zeros_like(run_ref)

    lgt = lg_ref[...].T[:ROUTE_ROWS]
    n = lgt.shape[1]
    row_i = lax.broadcasted_iota(jnp.int32, lgt.shape, 0)
    row = row_i.astype(F32)
    neg = jnp.float32(-jnp.inf)

    def first_max(mask):
        masked = jnp.where(mask, lgt, neg)
        m = jnp.max(masked, axis=0, keepdims=True)
        idx = jnp.min(jnp.where(jnp.logical_and(mask, masked == m), row, float(LANES)), axis=0, keepdims=True)
        return m, idx

    is_group = row_i < N_GROUPS
    m_g, grp = first_max(is_group)
    p_g = 1.0 / jnp.sum(jnp.where(is_group, jnp.exp(lgt - m_g), 0.0), axis=0, keepdims=True)
    e_row = row - float(N_GROUPS)
    lo_e = grp * float(E_PER_GROUP)
    in_group = jnp.logical_and(e_row >= lo_e, e_row < lo_e + float(E_PER_GROUP))
    l0, i0 = first_max(in_group)
    l1, i1 = first_max(jnp.logical_and(in_group, row != i0))
    r = jnp.exp(l1 - l0)
    gate0 = p_g / (1.0 + r)
    gate1 = p_g * r / (1.0 + r)
    e0 = i0 - float(N_GROUPS)
    e1 = i1 - float(N_GROUPS)
    oh0 = (e_row == e0).astype(F32)
    oh1 = (e_row == e1).astype(F32)
    both = oh0 + oh1
    before = _dot(both.astype(BF16), tri_ref[...]) + run_ref[...]
    rank0 = jnp.sum(oh0 * before, axis=0, keepdims=True)
    rank1 = jnp.sum(oh1 * before, axis=0, keepdims=True)
    run_ref[...] = run_ref[...] + jnp.sum(both, axis=1, keepdims=True)
    cnt_ref[...] = jnp.broadcast_to(run_ref[...], cnt_ref.shape)
    ids_ref[...] = jnp.concatenate([e0, e1, rank0, rank1, jnp.zeros((4, n), F32)], axis=0).astype(jnp.int32)
    gate_ref[...] = jnp.concatenate([gate0, gate1, jnp.zeros((LANES - 2, n), F32)], axis=0).T


def _route(logits, tri_strict):
    t = logits.shape[0]
    tile = tri_strict.shape[0]
    return pl.pallas_call(
        _route_kernel,
        grid=(t // tile,),
        in_specs=[pl.BlockSpec((tile, LANES), lambda i: (i, 0)),
                  pl.BlockSpec((tile, tile), lambda i: (0, 0))],
        out_specs=[pl.BlockSpec((8, tile), lambda i: (0, i)),
                   pl.BlockSpec((tile, LANES), lambda i: (i, 0)),
                   pl.BlockSpec((ROUTE_ROWS, LANES), lambda i: (0, 0))],
        out_shape=[jax.ShapeDtypeStruct((8, t), jnp.int32), jax.ShapeDtypeStruct((t, LANES), F32),
                   jax.ShapeDtypeStruct((ROUTE_ROWS, LANES), F32)],
        scratch_shapes=[pltpu.VMEM((ROUTE_ROWS, 1), F32)],
        compiler_params=_cparams(("arbitrary",)),
        name="route",
    )(logits, tri_strict)


ROUTE_TILE = 1024
ROUTE_ROWS = 40
DMA_TOKENS = 512
COMBINE_LAG = 2


def _dispatch_kernel(tail_ref, used_ref, dest_ref, h_hbm, buf_hbm, zero_ref, src_ref, load_sem, row_sem, zsem):
    i = pl.program_id(0)
    n_tiles = pl.num_programs(0)
    n_slots = src_ref.shape[0]
    groups = DMA_TOKENS // 8

    def load(tile):
        slot = lax.rem(tile, n_slots)
        return pltpu.make_async_copy(h_hbm.at[pl.ds(tile * groups, groups)], src_ref.at[slot], load_sem.at[slot])

    def wait_rows(tile):
        slot = lax.rem(tile, n_slots)
        for _ in range(2):
            pltpu.make_async_copy(buf_hbm.at[pl.ds(0, DMA_TOKENS)], buf_hbm.at[pl.ds(0, DMA_TOKENS)],
                                  row_sem.at[slot]).wait()

    @pl.when(i == 0)
    def _():
        load(0).start()
        zero_ref[...] = jnp.zeros_like(zero_ref)
        n_blocks = buf_hbm.shape[0] // FFN_BLOCK

        def block_copy(b):
            return pltpu.make_async_copy(zero_ref, buf_hbm.at[pl.ds(b * FFN_BLOCK, FFN_BLOCK)], zsem)

        for e in range(N_EXPERTS):
            @pl.when(tail_ref[e] >= 0)
            def _():
                block_copy(tail_ref[e]).start()
        lax.fori_loop(used_ref[0], n_blocks, lambda b, carry: (block_copy(b).start(), carry)[1], 0)
        for e in range(N_EXPERTS):
            @pl.when(tail_ref[e] >= 0)
            def _():
                block_copy(tail_ref[e]).wait()
        lax.fori_loop(used_ref[0], n_blocks, lambda b, carry: (block_copy(b).wait(), carry)[1], 0)

    @pl.when(i + 1 < n_tiles)
    def _():
        load(i + 1).start()

    load(i).wait()
    slot = lax.rem(i, n_slots)

    def start(grp, carry):
        for u in range(8):
            for j in range(2):
                dst = dest_ref[j * DMA_TOKENS + 8 * grp + u]
                pltpu.make_async_copy(src_ref.at[slot, grp, pl.ds(u, 1)], buf_hbm.at[pl.ds(dst, 1)],
                                      row_sem.at[slot]).start()
        return carry

    lax.fori_loop(0, groups, start, 0)

    @pl.when(i > 0)
    def _():
        wait_rows(i - 1)

    @pl.when(i == n_tiles - 1)
    def _():
        wait_rows(i)


def _dispatch(tail_block, n_used, dest_flat, h2, n_rows):
    t, d = h2.shape
    grid_spec = pltpu.PrefetchScalarGridSpec(
        num_scalar_prefetch=2,
        grid=(t // DMA_TOKENS,),
        in_specs=[pl.BlockSpec((2 * DMA_TOKENS,), lambda i, tail, used: (i,), memory_space=pltpu.SMEM),
                  pl.BlockSpec(memory_space=pl.ANY)],
        out_specs=pl.BlockSpec(memory_space=pl.ANY),
        scratch_shapes=[pltpu.VMEM((FFN_BLOCK, d), h2.dtype),
                        pltpu.VMEM((3, DMA_TOKENS // 8, 8, d), h2.dtype),
                        pltpu.SemaphoreType.DMA((3,)), pltpu.SemaphoreType.DMA((3,)),
                        pltpu.SemaphoreType.DMA(())],
    )
    return pl.pallas_call(
        _dispatch_kernel,
        grid_spec=grid_spec,
        out_shape=jax.ShapeDtypeStruct((n_rows, d), h2.dtype),
        compiler_params=_cparams(("arbitrary",)),
        name="dispatch",
    )(tail_block, n_used, dest_flat, h2.reshape(t // 8, 8, d))


def _combine_kernel(alpha, dest_ref, ys_hbm, gate_ref, x1_ref, g2_ref, lng_ref, lnb_ref, o_ref, rows_ref, sem):
    i = pl.program_id(0)
    n_tiles = pl.num_programs(0) - COMBINE_LAG
    n = x1_ref.shape[0]
    n_slots = rows_ref.shape[0]
    slot_in = lax.rem(i, n_slots)
    slot_out = lax.rem(i + n_slots - COMBINE_LAG, n_slots)

    def start_group(grp):
        for u in range(8):
            for j in range(2):
                src = dest_ref[j * DMA_TOKENS + 8 * grp + u]
                pltpu.make_async_copy(ys_hbm.at[pl.ds(src, 1)], rows_ref.at[slot_in, j, grp, pl.ds(u, 1)],
                                      sem.at[slot_in]).start()

    def finish():
        gate = gate_ref[...]
        r0 = rows_ref[slot_out, 0].reshape(n, -1)
        r1 = rows_ref[slot_out, 1].reshape(n, -1)
        y = gate[:, 0:1] * r0 + gate[:, 1:2] * r1
        o_ref[...] = _normalize(alpha * x1_ref[...] + g2_ref[0] * y) * lng_ref[...] + lnb_ref[...]

    @pl.when(i >= COMBINE_LAG)
    def _():
        for j in range(2):
            pltpu.make_async_copy(ys_hbm.at[pl.ds(0, n)], ys_hbm.at[pl.ds(0, n)], sem.at[slot_out]).wait()

    @pl.when(i < COMBINE_LAG)
    def _():
        lax.fori_loop(0, n // 8, lambda grp, carry: (start_group(grp), carry)[1], 0)

    @pl.when(jnp.logical_and(i >= COMBINE_LAG, i < n_tiles))
    def _():
        for grp in range(n // 8):
            start_group(grp)
        finish()

    @pl.when(i >= n_tiles)
    def _():
        finish()


def _combine(dest_flat, ys, gate, x1, mod3, ln_g, ln_b, alpha, rows_per_batch):
    t, d = x1.shape
    per_batch = rows_per_batch // DMA_TOKENS
    n_tiles = t // DMA_TOKENS

    def done(i):
        return jnp.maximum(i - COMBINE_LAG, 0)

    return pl.pallas_call(
        functools.partial(_combine_kernel, alpha),
        grid=(n_tiles + COMBINE_LAG,),
        in_specs=[pl.BlockSpec((2 * DMA_TOKENS,), lambda i: (jnp.minimum(i, n_tiles - 1),), memory_space=pltpu.SMEM),
                  pl.BlockSpec(memory_space=pl.ANY),
                  pl.BlockSpec((DMA_TOKENS, LANES), lambda i: (done(i), 0)),
                  pl.BlockSpec((DMA_TOKENS, d), lambda i: (done(i), 0)),
                  pl.BlockSpec((1, 1, d), lambda i: (done(i) // per_batch, 0, 5)),
                  pl.BlockSpec((1, d), lambda i: (0, 0)),
                  pl.BlockSpec((1, d), lambda i: (0, 0))],
        out_specs=pl.BlockSpec((DMA_TOKENS, d), lambda i: (done(i), 0)),
        out_shape=jax.ShapeDtypeStruct((t, d), F32),
        scratch_shapes=[pltpu.VMEM((COMBINE_LAG + 1, 2, DMA_TOKENS // 8, 8, d), F32),
                        pltpu.SemaphoreType.DMA((COMBINE_LAG + 1,))],
        compiler_params=_cparams(("arbitrary",)),
        name="combine",
    )(dest_flat, ys, gate, x1, mod3, ln_g.reshape(1, d), ln_b.reshape(1, d))


def _ffn_kernel(start_ref, nblk_ref, used_ref, buf_hbm, wg_ref, wu_ref, wd_ref, ys_hbm,
                wgb_ref, wub_ref, wdb_ref, x_buf, o_buf, in_sem, out_sem):
    e = pl.program_id(0)
    nb = nblk_ref[e]
    base = start_ref[e]
    used = used_ref[0]
    n_in = x_buf.shape[0]
    n_out = o_buf.shape[0]
    ahead = n_in - 1

    def rows(g):
        return pl.ds(g * FFN_BLOCK, FFN_BLOCK)

    def in_copy(g):
        slot = lax.rem(g, n_in)
        return pltpu.make_async_copy(buf_hbm.at[rows(g)], x_buf.at[slot], in_sem.at[slot])

    def out_copy(g):
        slot = lax.rem(g, n_out)
        return pltpu.make_async_copy(o_buf.at[slot], ys_hbm.at[rows(g)], out_sem.at[slot])

    @pl.when(e == 0)
    def _():
        for g in range(ahead):
            @pl.when(g < used)
            def _():
                in_copy(g).start()

    @pl.when(nb > 0)
    def _():
        wgb_ref[...] = wg_ref[0].astype(BF16)
        wub_ref[...] = wu_ref[0].astype(BF16)
        wdb_ref[...] = wd_ref[0].astype(BF16)

        def block(j, carry):
            g = base + j
            in_copy(g).wait()

            @pl.when(g + ahead < used)
            def _():
                in_copy(g + ahead).start()

            @pl.when(g >= n_out)
            def _():
                out_copy(g - n_out).wait()

            islot = lax.rem(g, n_in)
            oslot = lax.rem(g, n_out)
            slab = FFN_BLOCK // FFN_SLABS
            xs = [x_buf[islot, s * slab:(s + 1) * slab, :].astype(BF16) for s in range(FFN_SLABS)]
            gates = [_dot(x, wgb_ref[...]) for x in xs]
            ups = [_dot(x, wub_ref[...]) for x in xs]
            hidden = [(_silu(gt) * up).astype(BF16) for gt, up in zip(gates, ups)]
            for s in range(FFN_SLABS):
                o_buf[oslot, s * slab:(s + 1) * slab, :] = _dot(hidden[s], wdb_ref[...])
            out_copy(g).start()
            return carry

        lax.fori_loop(0, nb, block, 0)

    @pl.when(e == pl.num_programs(0) - 1)
    def _():
        for back in range(n_out):
            @pl.when(used - 1 - back >= 0)
            def _():
                out_copy(used - 1 - back).wait()
        n_blocks = ys_hbm.shape[0] // FFN_BLOCK
        o_buf[0] = jnp.zeros(o_buf.shape[1:], o_buf.dtype)

        def clear(b):
            return pltpu.make_async_copy(o_buf.at[0], ys_hbm.at[pl.ds(b * FFN_BLOCK, FFN_BLOCK)], out_sem.at[0])

        lax.fori_loop(used_ref[0], n_blocks, lambda b, carry: (clear(b).start(), carry)[1], 0)
        lax.fori_loop(used_ref[0], n_blocks, lambda b, carry: (clear(b).wait(), carry)[1], 0)


def _ffn(seg_start, seg_blocks, n_used, buf, w_gate, w_up, w_down):
    n_rows, d = buf.shape
    n_experts = w_gate.shape[0]
    grid_spec = pltpu.PrefetchScalarGridSpec(
        num_scalar_prefetch=3,
        grid=(n_experts,),
        in_specs=[pl.BlockSpec(memory_space=pl.ANY),
                  pl.BlockSpec((1, d, D_EXPERT), lambda e, *_: (e, 0, 0)),
                  pl.BlockSpec((1, d, D_EXPERT), lambda e, *_: (e, 0, 0)),
                  pl.BlockSpec((1, D_EXPERT, d), lambda e, *_: (e, 0, 0))],
        out_specs=pl.BlockSpec(memory_space=pl.ANY),
        scratch_shapes=[pltpu.VMEM((d, D_EXPERT), BF16), pltpu.VMEM((d, D_EXPERT), BF16),
                        pltpu.VMEM((D_EXPERT, d), BF16),
                        pltpu.VMEM((FFN_IN_SLOTS, FFN_BLOCK, d), F32), pltpu.VMEM((FFN_OUT_SLOTS, FFN_BLOCK, d), F32),
                        pltpu.SemaphoreType.DMA((FFN_IN_SLOTS,)), pltpu.SemaphoreType.DMA((FFN_OUT_SLOTS,))],
    )
    return pl.pallas_call(
        _ffn_kernel,
        grid_spec=grid_spec,
        out_shape=jax.ShapeDtypeStruct((n_rows, d), F32),
        compiler_params=_cparams(("arbitrary",)),
        name="ffn",
    )(seg_start, seg_blocks, n_used, buf, w_gate, w_up, w_down)


def _rope_tables(seq_len, n_ctx_rows):
    lane = np.arange(LANES)
    within = lane % DDK
    freq = within % 16
    use_col = within >= DDK // 2
    sign = np.where((lane % 32) < 16, -1.0, 1.0).astype(np.float32)
    half = DDK // 2
    inv_freq = (ROPE_THETA ** (-np.arange(0, half, 2, dtype=np.float32) / np.float32(half))).astype(np.float32)
    t = np.arange(seq_len)
    pos = np.where(use_col[None, :], (t % GRID_W)[:, None], (t // GRID_W)[:, None]).astype(np.float32)
    ang = pos * inv_freq[freq][None, :]
    cos = np.concatenate([np.cos(ang), np.ones((n_ctx_rows, LANES), np.float32)], axis=0)
    sin = np.concatenate([np.sin(ang) * sign[None, :], np.zeros((n_ctx_rows, LANES), np.float32)], axis=0)
    return jnp.asarray(cos, F32), jnp.asarray(sin, F32)


def _block_tri(n, block, upper):
    i = np.arange(n)
    same = (i[:, None] // block) == (i[None, :] // block)
    tri = (i[None, :] >= i[:, None]) if upper else (i[None, :] <= i[:, None])
    return jnp.asarray((same & tri).astype(np.float32), dtype=BF16)


def _pack_w_in(w_in):
    off = G_QKV + G_Z
    v_start = G_COLS + 2 * D_QK
    pad = jnp.zeros((D, LANES - G_GATES), w_in.dtype)
    w_gate = jnp.concatenate([w_in[:, off:off + G_GATES], pad, w_in[:, off + G_GATES:G_COLS], pad], axis=1)
    return (w_in[:, :off].astype(BF16), w_gate.astype(BF16), w_in[:, G_COLS:v_start].astype(BF16),
            w_in[:, v_start:].T.astype(BF16))


def kernel(x, c, ctx, c_ctx, w_ada, b_ada, w_in, conv_w, gdn_a_log, gdn_dt_bias, gdn_norm_w, diff_lambda,
           diff_norm_w, w_out, ln1_g, ln1_b, w_router_group, b_router_group, w_router_expert, b_router_expert,
           w_expert_gate, w_expert_up, w_expert_down, ln2_g, ln2_b):
    depth = w_ada.shape[0]
    assert depth == 1, "single-layer block: the context stream never feeds a later layer"
    bsz, seq, _ = x.shape
    n_ctx = ctx.shape[1]
    assert seq % ROW_TILE == 0 and n_ctx % ROW_TILE == 0 and seq % GRID_W == 0 and bsz < 8
    n_lat_tiles = seq // ROW_TILE
    alpha = (2.0 * depth) ** 0.25
    lam_init = 0.8 - 0.6 * math.exp(-0.3 * 0)
    i = 0

    cc = jnp.zeros((8, D), F32).at[:bsz].set(c).at[7].set(c_ctx)
    mod = _ada(cc, w_ada[i], b_ada[i])
    mod3 = mod.reshape(8, 1, 6 * D)

    cos_t, sin_t = _rope_tables(seq, n_ctx)
    qkv, z, gt, dq, dk, dvt = _proj(x, ctx, mod3, cos_t, sin_t, *_pack_w_in(w_in[i]))

    conv_w8 = jnp.zeros((8, G_QKV), F32).at[:GCONV].set(conv_w[i])
    alog_v = jnp.zeros((1, LANES), F32).at[0, :G_GATES].set(gdn_a_log[i].reshape(-1))
    dtb_v = jnp.zeros((1, LANES), F32).at[0, :G_GATES].set(gdn_dt_bias[i].reshape(-1))
    gq, gk, gv, beta, gc = _prep(qkv, gt, conv_w8, alog_v, dtb_v,
                                 _block_tri(ROW_TILE, CHUNK, False), _block_tri(ROW_TILE, CHUNK, True), n_lat_tiles)
    uw = _chunk(gk, gv, beta, gc)
    o_f, o_r = _scan(gq, gk, gc, uw, n_lat_tiles)

    y_diff = _attn(dq, dk, dvt, diff_lambda[i], diff_norm_w[i], lam_init, seq, tq=min(ATTN_Q_TILE, seq))

    w_router = jnp.zeros((D, LANES), F32).at[:, :N_GROUPS].set(w_router_group[i]) \
        .at[:, N_GROUPS:N_GROUPS + N_EXPERTS].set(w_router_expert[i])
    w_hi = w_router.astype(BF16)
    w_router3 = jnp.concatenate([w_hi, (w_router - w_hi.astype(F32)).astype(BF16)], axis=1)
    b_router = jnp.zeros((1, LANES), F32).at[0, :N_GROUPS].set(b_router_group[i]) \
        .at[0, N_GROUPS:N_GROUPS + N_EXPERTS].set(b_router_expert[i])
    x1, h2, logits = _out(o_f, o_r, z, y_diff, x, mod3, gdn_norm_w[i], w_out[i].astype(BF16), ln1_g[i], ln1_b[i],
                          w_router3, b_router, alpha)

    t = bsz * seq
    tri_strict = jnp.asarray(np.triu(np.ones((ROUTE_TILE, ROUTE_TILE), np.float32), 1), dtype=BF16)
    ids, gate, counts = _route(logits.reshape(t, LANES), tri_strict)

    cnt = counts[N_GROUPS:N_GROUPS + N_EXPERTS, 0].astype(jnp.int32)
    padded = (cnt + FFN_BLOCK - 1) // FFN_BLOCK * FFN_BLOCK
    p_end = jnp.cumsum(padded)
    p_start = p_end - padded
    n_blocks = (2 * t) // FFN_BLOCK + N_EXPERTS
    n_used = (p_end[-1:] // FFN_BLOCK).astype(jnp.int32)
    tail_block = jnp.where(cnt > 0, p_end // FFN_BLOCK - 1, -1).astype(jnp.int32)
    one_hot = ids[0:2, :, None] == jnp.arange(N_EXPERTS, dtype=jnp.int32)
    row_start = jnp.sum(jnp.where(one_hot, p_start, 0), axis=-1)
    dest = jnp.swapaxes((row_start + ids[2:4]).reshape(2, t // DMA_TOKENS, DMA_TOKENS), 0, 1)
    dest = dest.reshape(-1).astype(jnp.int32)

    buf = _dispatch(tail_block, n_used, dest, h2.reshape(t, D), n_blocks * FFN_BLOCK)
    ys = _ffn((p_start // FFN_BLOCK).astype(jnp.int32), (padded // FFN_BLOCK).astype(jnp.int32), n_used, buf,
              w_expert_gate[i], w_expert_up[i], w_expert_down[i])
    out = _combine(dest, ys, gate, x1.reshape(t, D), mod3, ln2_g[i], ln2_b[i], alpha, seq)
    return out.reshape(bsz, seq, D)
```

```python
import functools
import math

import numpy as np
import jax
import jax.numpy as jnp
from jax import lax
from jax.experimental import pallas as pl
from jax.experimental.pallas import tpu as pltpu

F32 = jnp.float32
BF16 = jnp.bfloat16

D = 1024
GRID_W = 64
ROPE_THETA = 10000.0
GH = 4
GDK = 128
GDV = 128
GCONV = 5
DH = 4
DDK = 64
DDV = 128
G_QK = GH * GDK
G_QKV = 2 * G_QK + GH * GDV
G_Z = GH * GDV
G_GATES = 2 * GH
G_COLS = G_QKV + G_Z + 2 * G_GATES
D_QK = DH * 2 * DDK
ROPE_PAIR = DDK // 4
N_GROUPS = 4
E_PER_GROUP = 8
N_EXPERTS = N_GROUPS * E_PER_GROUP
D_EXPERT = 512
LN_EPS = 1e-5
NORM_EPS = 1e-6

LANES = 128
SUBLANES = 8
BF16_ROWS = 16
ADA_BLOCK = 1024
ROW_TILE = 256
CHUNK = 64
OUT_TILE = 512
CHUNK_BATCH = 2
SCAN_BATCH = 2
FFN_BLOCK = 256
FFN_SLABS = 2
FFN_IN_SLOTS = 6
FFN_OUT_SLOTS = 4
KEY_TILE = 512
ATTN_Q_TILE = 1024
SCORE_LOOKAHEAD = 1
ROUTE_TILE = 1024
ROUTE_ROWS = 40
DMA_TOKENS = 512
COMBINE_LAG = 2
VMEM_LIMIT = 56 * 1024 * 1024

Q_SCALE = DDK ** -0.5 * math.log2(math.e)


def _cparams(sem):
    return pltpu.CompilerParams(dimension_semantics=sem, vmem_limit_bytes=VMEM_LIMIT)


def _dot(a, b):
    return jnp.dot(a, b, preferred_element_type=F32)


def _dot_nt(a, b):
    return lax.dot_general(a, b, (((1,), (1,)), ((), ())), preferred_element_type=F32)


def _normalize(x):
    mu = jnp.mean(x, axis=-1, keepdims=True)
    xc = x - mu
    var = jnp.mean(xc * xc, axis=-1, keepdims=True)
    return xc * lax.rsqrt(var + LN_EPS)


def _silu(x):
    return x * jax.nn.sigmoid(x)


def _split3(x):
    hi = x.astype(BF16)
    r = x - hi.astype(F32)
    mid = r.astype(BF16)
    lo = (r - mid.astype(F32)).astype(BF16)
    return hi, mid, lo


def _ada_kernel(c_ref, w_ref, b_ref, o_ref):
    s = _silu(c_ref[...])
    s_hi = s.astype(BF16)
    s_lo = (s - s_hi.astype(F32)).astype(BF16)
    w = w_ref[...]
    w_hi = w.astype(BF16)
    w_lo = (w - w_hi.astype(F32)).astype(BF16)
    rows = s.shape[0]
    both = _dot(jnp.concatenate([s_hi, s_lo], axis=0), w_hi)
    o_ref[...] = both[:rows] + both[rows:] + _dot(s_hi, w_lo) + b_ref[...]


def _ada(cc, w_ada, b_ada):
    n = w_ada.shape[1]
    bn = ADA_BLOCK
    return pl.pallas_call(
        _ada_kernel,
        grid=(n // bn,),
        in_specs=[pl.BlockSpec((SUBLANES, D), lambda j: (0, 0)),
                  pl.BlockSpec((D, bn), lambda j: (0, j)),
                  pl.BlockSpec((1, bn), lambda j: (0, j))],
        out_specs=pl.BlockSpec((SUBLANES, bn), lambda j: (0, j)),
        out_shape=jax.ShapeDtypeStruct((SUBLANES, n), F32),
        compiler_params=_cparams(("arbitrary",)),
        name="ada",
    )(cc, w_ada, b_ada.reshape(1, n))


def _rope(v, cos, sin, low_half):
    fwd = pltpu.roll(v, LANES - ROPE_PAIR, axis=1)
    bwd = pltpu.roll(v, ROPE_PAIR, axis=1)
    return v * cos + jnp.where(low_half, fwd, bwd) * sin


def _proj_kernel(n_lat_tiles, x_ref, ctx_ref, sh_ref, sc_ref, cos_ref, sin_ref, wg_ref, wgate_ref, wqk_ref, wvt_ref,
                 qkv_ref, z_ref, gt_ref, dq_ref, dk_ref, dvt_ref):
    rows = jnp.where(pl.program_id(1) >= n_lat_tiles, ctx_ref[0], x_ref[0])
    h = (_normalize(rows) * (1.0 + sc_ref[0]) + sh_ref[0]).astype(BF16)
    q_all = _dot(h, wqk_ref[:, :D_QK])
    k_all = _dot(h, wqk_ref[:, D_QK:])
    qkv_ref[0] = _dot(h, wg_ref[:, :G_QKV])
    z_ref[0] = _dot(h, wg_ref[:, G_QKV:]).astype(z_ref.dtype)
    gt_ref[0] = _dot(h, wgate_ref[...])
    cos = cos_ref[...]
    sin = sin_ref[...]
    lane = lax.broadcasted_iota(jnp.int32, cos.shape, 1)
    low_half = (lane % (2 * ROPE_PAIR)) < ROPE_PAIR
    for j in range(D_QK // LANES):
        slab = slice(j * LANES, (j + 1) * LANES)
        dq_ref[0, :, slab] = (_rope(q_all[:, slab], cos, sin, low_half) * Q_SCALE).astype(BF16)
        dk_ref[0, :, slab] = _rope(k_all[:, slab], cos, sin, low_half).astype(BF16)
    dvt_ref[0] = _dot_nt(wvt_ref[...], h).astype(BF16)


def _proj(x, ctx, mod3, cos_t, sin_t, w_gdn, w_gate, w_qk, wv_t):
    bsz, seq, _ = x.shape
    lc = seq + ctx.shape[1]
    nt = lc // ROW_TILE
    n_lat_tiles = seq // ROW_TILE
    ctx_row = mod3.shape[0] - 1

    def mod_idx(col):
        return lambda b, i: (jnp.where(i >= n_lat_tiles, ctx_row, b), 0, col)

    def row_spec(width):
        return pl.BlockSpec((1, ROW_TILE, width), lambda b, i: (b, i, 0))

    outs = [(G_QKV, F32), (G_Z, BF16), (2 * LANES, F32), (D_QK, BF16), (D_QK, BF16)]
    return pl.pallas_call(
        functools.partial(_proj_kernel, n_lat_tiles),
        grid=(bsz, nt),
        in_specs=[pl.BlockSpec((1, ROW_TILE, D), lambda b, i: (b, jnp.minimum(i, n_lat_tiles - 1), 0)),
                  pl.BlockSpec((1, ROW_TILE, D), lambda b, i: (b, jnp.maximum(i - n_lat_tiles, 0), 0)),
                  pl.BlockSpec((1, 1, D), mod_idx(0)),
                  pl.BlockSpec((1, 1, D), mod_idx(1)),
                  pl.BlockSpec((ROW_TILE, LANES), lambda b, i: (i, 0)),
                  pl.BlockSpec((ROW_TILE, LANES), lambda b, i: (i, 0)),
                  pl.BlockSpec(w_gdn.shape, lambda b, i: (0, 0)),
                  pl.BlockSpec(w_gate.shape, lambda b, i: (0, 0)),
                  pl.BlockSpec(w_qk.shape, lambda b, i: (0, 0)),
                  pl.BlockSpec(wv_t.shape, lambda b, i: (0, 0))],
        out_specs=[row_spec(w) for w, _ in outs]
        + [pl.BlockSpec((1, DH * DDV, ROW_TILE), lambda b, i: (b, 0, i))],
        out_shape=[jax.ShapeDtypeStruct((bsz, lc, w), dt) for w, dt in outs]
        + [jax.ShapeDtypeStruct((bsz, DH * DDV, lc), BF16)],
        compiler_params=_cparams(("arbitrary", "arbitrary")),
        name="proj",
    )(x, ctx, mod3, mod3, cos_t, sin_t, w_gdn, w_gate, w_qk, wv_t)


def _prep_kernel(n_lat_tiles, nt, main_ref, prev_ref, next_ref, gt_ref, cw_ref, alog_ref, dtb_ref,
                 lo_ref, up_ref, q_ref, k_ref, v_ref, beta_ref, gc_ref, ext_ref):
    i = pl.program_id(1)
    halo = prev_ref.shape[1]
    has_prev = jnp.logical_and(i != 0, i != n_lat_tiles)
    has_next = jnp.logical_and(i != n_lat_tiles - 1, i != nt - 1)
    ext_ref[0:halo, :] = prev_ref[0] * has_prev.astype(F32)
    ext_ref[halo:halo + ROW_TILE, :] = main_ref[0]
    ext_ref[halo + ROW_TILE:, :] = next_ref[0] * has_next.astype(F32)
    pad = GCONV // 2
    acc = None
    ext = ext_ref[...]
    n_ext = ext.shape[0]
    for j in range(GCONV):
        shifted = ext if j == pad else pltpu.roll(ext, (pad - j) % n_ext, axis=0)
        term = shifted[halo:halo + ROW_TILE, :] * cw_ref[j:j + 1, :]
        acc = term if acc is None else acc + term
    qkv = _silu(acc)
    for h in range(GH):
        q = qkv[:, h * GDK:(h + 1) * GDK]
        q_ref[0, :, h * GDK:(h + 1) * GDK] = (q * (lax.rsqrt(jnp.sum(q * q, axis=-1, keepdims=True) + NORM_EPS)
                                                 * (GDK ** -0.5))).astype(q_ref.dtype)
        k = qkv[:, G_QK + h * GDK:G_QK + (h + 1) * GDK]
        k_ref[0, :, h * GDK:(h + 1) * GDK] = (
            k * lax.rsqrt(jnp.sum(k * k, axis=-1, keepdims=True) + NORM_EPS)).astype(k_ref.dtype)
    v_ref[0] = qkv[:, 2 * G_QK:].astype(v_ref.dtype)
    gt = gt_ref[0]
    beta_ref[0] = jax.nn.sigmoid(gt[:, :LANES])
    a = gt[:, LANES:] + dtb_ref[...]
    softplus = jnp.maximum(a, 0.0) + jnp.log(1.0 + jnp.exp(-jnp.abs(a)))
    g = -jnp.exp(alog_ref[...]) * softplus
    lo = lo_ref[...]
    up = up_ref[...]
    fwd = None
    bwd = None
    for part in _split3(g):
        f = _dot(lo, part)
        r = _dot(up, part)
        fwd = f if fwd is None else fwd + f
        bwd = r if bwd is None else bwd + r
    lane = lax.broadcasted_iota(jnp.int32, g.shape, 1)
    gc_ref[0] = jnp.where(lane < GH, fwd, bwd)


def _prep(qkv, gt, conv_w8, alog_v, dtb_v, tri_lo, tri_up, n_lat_tiles):
    bsz, lc, _ = qkv.shape
    nt = lc // ROW_TILE
    halo = SUBLANES
    per = ROW_TILE // halo
    last = lc // halo - 1

    def row_spec(width):
        return pl.BlockSpec((1, ROW_TILE, width), lambda b, i: (b, i, 0))

    def const_spec(shape):
        return pl.BlockSpec(shape, lambda b, i: tuple(0 for _ in shape))

    outs = [(G_QK, BF16), (G_QK, BF16), (GH * GDV, BF16), (LANES, F32), (LANES, F32)]
    return pl.pallas_call(
        functools.partial(_prep_kernel, n_lat_tiles, nt),
        grid=(bsz, nt),
        in_specs=[row_spec(G_QKV),
                  pl.BlockSpec((1, halo, G_QKV), lambda b, i: (b, jnp.maximum(i * per - 1, 0), 0)),
                  pl.BlockSpec((1, halo, G_QKV), lambda b, i: (b, jnp.minimum((i + 1) * per, last), 0)),
                  row_spec(2 * LANES),
                  const_spec((SUBLANES, G_QKV)), const_spec((1, LANES)), const_spec((1, LANES)),
                  const_spec((ROW_TILE, ROW_TILE)), const_spec((ROW_TILE, ROW_TILE))],
        out_specs=[row_spec(w) for w, _ in outs],
        out_shape=[jax.ShapeDtypeStruct((bsz, lc, w), dt) for w, dt in outs],
        scratch_shapes=[pltpu.VMEM((ROW_TILE + 2 * halo, G_QKV), F32)],
        compiler_params=_cparams(("arbitrary", "arbitrary")),
        name="prep",
    )(qkv, qkv, qkv, gt, conv_w8, alog_v, dtb_v, tri_lo, tri_up)


def _chunk_masks(n, rev):
    row = lax.broadcasted_iota(jnp.int32, (n, n), 0)
    col = lax.broadcasted_iota(jnp.int32, (n, n), 1)
    same = (row // CHUNK) == (col // CHUNK)
    incl = jnp.logical_and(same, (col >= row) if rev else (col <= row))
    strict = jnp.logical_and(same, (col > row) if rev else (col < row))
    return incl, strict


def _decay(gcol, grow, incl):
    return jnp.where(incl, jnp.exp(jnp.minimum(gcol - grow, 0.0)), 0.0)


def _chunk_kernel(k_ref, v_ref, beta_ref, gc_ref, uw_ref):
    n = k_ref.shape[1]
    row = lax.broadcasted_iota(jnp.int32, (n, n), 0)
    col = lax.broadcasted_iota(jnp.int32, (n, n), 1)
    eye = (row == col).astype(F32)
    sizes = [2 ** j for j in range(1, int(math.log2(CHUNK)))]
    same = {r: (row // r) == (col // r) for r in sizes + [CHUNK]}
    a_all, t_all, x_all, where = [], [], [], []
    masks = [_chunk_masks(n, rev) for rev in (False, True)]
    for bb in range(k_ref.shape[0]):
        gc = gc_ref[bb]
        gct = gc.T
        beta = beta_ref[bb]
        gram = [_dot_nt(k_ref[bb, :, h * GDK:(h + 1) * GDK], k_ref[bb, :, h * GDK:(h + 1) * GDK]) for h in range(GH)]
        for d in range(2):
            incl, strict = masks[d]
            for h in range(GH):
                c = d * GH + h
                k = k_ref[bb, :, h * GDK:(h + 1) * GDK].astype(F32)
                v = v_ref[bb, :, h * GDV:(h + 1) * GDV].astype(F32)
                bcol = beta[:, c:c + 1]
                gcol = gc[:, c:c + 1]
                grow = gct[c:c + 1, :]
                kb = k * bcol
                a = jnp.where(strict, gram[h] * bcol * _decay(gcol, grow, incl), 0.0)
                a_all.append(a.astype(BF16))
                t_all.append((eye - jnp.where(same[2], a, 0.0)).astype(BF16))
                x_all.append(jnp.concatenate([v * bcol, kb * jnp.exp(gcol)], axis=1).astype(BF16))
                where.append((bb, c))
    n_prob = len(a_all)
    zero = jnp.zeros((n, n), BF16)
    for r in sizes:
        off_mask = jnp.logical_and(same[2 * r], jnp.logical_not(same[r]))
        inner = [_dot(jnp.where(off_mask, a_all[c], zero), t_all[c]).astype(BF16) for c in range(n_prob)]
        t_all = [t_all[c] - _dot(t_all[c], inner[c]).astype(BF16) for c in range(n_prob)]
    for p in range(n_prob):
        bb, c = where[p]
        uw_ref[bb, c] = _dot(t_all[p], x_all[p]).astype(uw_ref.dtype)


def _chunk(k, v, beta, gc):
    bsz, lc, _ = k.shape
    nt = lc // ROW_TILE
    nb = CHUNK_BATCH if bsz % CHUNK_BATCH == 0 else 1

    def row_spec(width):
        return pl.BlockSpec((nb, ROW_TILE, width), lambda b, i: (b, i, 0))

    return pl.pallas_call(
        _chunk_kernel,
        grid=(bsz // nb, nt),
        in_specs=[row_spec(G_QK), row_spec(GH * GDV), row_spec(LANES), row_spec(LANES)],
        out_specs=pl.BlockSpec((nb, 2 * GH, ROW_TILE, GDV + GDK), lambda b, i: (b, 0, i, 0)),
        out_shape=jax.ShapeDtypeStruct((bsz, 2 * GH, lc, GDV + GDK), BF16),
        compiler_params=_cparams(("arbitrary", "arbitrary")),
        name="chunk",
    )(k, v, beta, gc)


def _scan_kernel(qf_ref, kf_ref, gf_ref, uwf_ref, qr_ref, kr_ref, gr_ref, uwr_ref,
                 of_ref, or_ref, s_ref):
    @pl.when(pl.program_id(1) == 0)
    def _():
        s_ref[...] = jnp.zeros_like(s_ref)

    n_chunks = ROW_TILE // CHUNK
    n_batch = qf_ref.shape[0]
    row = lax.broadcasted_iota(jnp.int32, (CHUNK, CHUNK), 0)
    col = lax.broadcasted_iota(jnp.int32, (CHUNK, CHUNK), 1)
    dirs = ((qf_ref, kf_ref, gf_ref, uwf_ref, of_ref), (qr_ref, kr_ref, gr_ref, uwr_ref, or_ref))
    chains = [(bb, d, h) for bb in range(n_batch) for d in range(2) for h in range(GH)]
    state = [s_ref[i] for i in range(len(chains))]

    def chunk_rows(step, d):
        ci = n_chunks - 1 - step if d == 1 else step
        return slice(ci * CHUNK, (ci + 1) * CHUNK)

    pre = []
    for step in range(n_chunks):
        gates = {}
        for bb in range(n_batch):
            for d in range(2):
                gc = dirs[d][2][bb, chunk_rows(step, d), :]
                gates[bb, d] = (gc, gc.T)
        per_chain = []
        for bb, d, h in chains:
            q_ref, k_ref, _, uw_ref, _ = dirs[d]
            rev = d == 1
            incl = (col >= row) if rev else (col <= row)
            rows = chunk_rows(step, d)
            gc, gct = gates[bb, d]
            c = d * GH + h
            q_b = q_ref[bb, rows, h * GDK:(h + 1) * GDK]
            k_b = k_ref[bb, rows, h * GDK:(h + 1) * GDK]
            q = q_b.astype(F32)
            k = k_b.astype(F32)
            gcol = gc[:, c:c + 1]
            grow = gct[c:c + 1, :]
            last = 0 if rev else CHUNK - 1
            g_last = gcol[last:last + 1, :]
            a_qk = (_dot_nt(q_b, k_b) * _decay(gcol, grow, incl)).astype(BF16)
            wq = jnp.concatenate([uw_ref[bb, h, rows, GDV:], (q * jnp.exp(gcol)).astype(BF16)], axis=0)
            k_tail_t = (k * jnp.exp(g_last - gcol)).T.astype(BF16)
            per_chain.append((a_qk, wq, k_tail_t, jnp.exp(g_last)))
        pre.append(per_chain)

    for step in range(n_chunks):
        ws = [_dot(pre[step][i][1], state[i].astype(BF16)) for i in range(len(chains))]
        v_new = [(dirs[d][3][bb, h, chunk_rows(step, d), :GDV].astype(F32) - ws[i][:CHUNK]).astype(BF16)
                 for i, (bb, d, h) in enumerate(chains)]
        for i, (bb, d, h) in enumerate(chains):
            a_qk, _, k_tail_t, decay_last = pre[step][i]
            dirs[d][4][bb, chunk_rows(step, d), h * GDV:(h + 1) * GDV] = (
                ws[i][CHUNK:] + _dot(a_qk, v_new[i])).astype(BF16)
            state[i] = state[i] * decay_last + _dot(k_tail_t, v_new[i])
    for i in range(len(chains)):
        s_ref[i] = state[i]


def _scan(q, k, gc, uw, n_lat_tiles):
    bsz, lc, _ = q.shape
    nt = lc // ROW_TILE
    n_ctx_tiles = nt - n_lat_tiles

    def fwd(b, i):
        return jnp.where(i < n_ctx_tiles, n_lat_tiles + i, i - n_ctx_tiles)

    def bwd(b, i):
        return jnp.where(i < n_ctx_tiles, nt - 1 - i, n_lat_tiles - 1 - (i - n_ctx_tiles))

    nb = SCAN_BATCH if bsz % SCAN_BATCH == 0 else 1

    def specs(tile):
        return [pl.BlockSpec((nb, ROW_TILE, G_QK), lambda b, i: (b, tile(b, i), 0)),
                pl.BlockSpec((nb, ROW_TILE, G_QK), lambda b, i: (b, tile(b, i), 0)),
                pl.BlockSpec((nb, ROW_TILE, LANES), lambda b, i: (b, tile(b, i), 0))]

    def uw_spec(tile, d):
        return pl.BlockSpec((nb, GH, ROW_TILE, GDV + GDK), lambda b, i: (b, d, tile(b, i), 0))

    def out_spec(tile):
        return pl.BlockSpec((nb, ROW_TILE, GH * GDV), lambda b, i: (b, tile(b, i), 0))

    return pl.pallas_call(
        _scan_kernel,
        grid=(bsz // nb, nt),
        in_specs=specs(fwd) + [uw_spec(fwd, 0)] + specs(bwd) + [uw_spec(bwd, 1)],
        out_specs=[out_spec(fwd), out_spec(bwd)],
        out_shape=[jax.ShapeDtypeStruct((bsz, lc, GH * GDV), BF16)] * 2,
        scratch_shapes=[pltpu.VMEM((nb * 2 * GH, GDK, GDV), F32)],
        compiler_params=_cparams(("arbitrary", "arbitrary")),
        name="scan",
    )(q, k, gc, uw, q, k, gc, uw)


def _attn_kernel(lam_init, q_ref, k_ref, vt_ref, lam_ref, nw_ref, o_ref):
    lam = lam_ref[...]
    lam_full = (jnp.exp(jnp.sum(lam[0:1] * lam[1:2], axis=-1, keepdims=True))
                - jnp.exp(jnp.sum(lam[2:3] * lam[3:4], axis=-1, keepdims=True)) + lam_init)
    q = q_ref[0]
    tq = q.shape[0]
    n_keys = k_ref.shape[1]
    lane = lax.broadcasted_iota(jnp.int32, q.shape, 1)
    zero = jnp.zeros_like(q)
    qc = [jnp.where(lane < DDK, q, zero), jnp.where(lane >= DDK, q, zero)]
    m = [jnp.full((1, tq), -jnp.inf, F32) for _ in range(2)]
    l = [jnp.zeros((1, tq), F32) for _ in range(2)]
    acc = [jnp.zeros((DDV, tq), F32) for _ in range(2)]
    n_tiles = n_keys // KEY_TILE
    bounds = [t * KEY_TILE for t in range(n_tiles)] + [n_keys]

    def scores(t):
        k_t = k_ref[0, bounds[t]:bounds[t + 1], :]
        tiles = [_dot_nt(k_t, qc[c]) for c in range(2)]
        return [(s, jnp.max(s, axis=0, keepdims=True)) for s in tiles]

    pending = [scores(t) for t in range(min(SCORE_LOOKAHEAD, n_tiles))]
    for t in range(n_tiles):
        s_cur = pending.pop(0)
        if t + SCORE_LOOKAHEAD < n_tiles:
            pending.append(scores(t + SCORE_LOOKAHEAD))
        width = bounds[t + 1] - bounds[t]
        vt_t = jnp.concatenate([vt_ref[0, :, bounds[t]:bounds[t + 1]], jnp.ones((BF16_ROWS, width), BF16)], axis=0)
        for c in range(2):
            s, s_max = s_cur[c]
            m_new = jnp.maximum(m[c], s_max)
            p = jnp.exp2(s - m_new)
            scale = jnp.exp2(m[c] - m_new)
            pv = _dot(vt_t, p.astype(BF16))
            l[c] = scale * l[c] + pv[DDV:DDV + 1]
            acc[c] = scale * acc[c] + pv[:DDV]
            m[c] = m_new
    o = acc[0] / l[0] - lam_full * (acc[1] / l[1])
    y = o * lax.rsqrt(jnp.mean(o * o, axis=0, keepdims=True) + NORM_EPS)
    o_ref[0] = (y.T * nw_ref[...] * (1.0 - lam_init)).astype(o_ref.dtype)


def _attn(dq, dk, dvt, lam, norm_w, lam_init, lq, tq):
    bsz, lc, _ = dq.shape
    return pl.pallas_call(
        functools.partial(_attn_kernel, lam_init),
        grid=(bsz, DH, lq // tq),
        in_specs=[pl.BlockSpec((1, tq, 2 * DDK), lambda b, h, i: (b, i, h)),
                  pl.BlockSpec((1, lc, 2 * DDK), lambda b, h, i: (b, 0, h)),
                  pl.BlockSpec((1, DDV, lc), lambda b, h, i: (b, h, 0)),
                  pl.BlockSpec((4, DDK), lambda b, h, i: (0, 0)),
                  pl.BlockSpec((1, DDV), lambda b, h, i: (0, 0))],
        out_specs=pl.BlockSpec((1, tq, DDV), lambda b, h, i: (b, i, h)),
        out_shape=jax.ShapeDtypeStruct((bsz, lq, DH * DDV), BF16),
        compiler_params=_cparams(("arbitrary", "arbitrary", "arbitrary")),
        name="attn",
    )(dq, dk, dvt, lam, norm_w.reshape(1, DDV))


def _out_kernel(alpha, of_ref, or_ref, z_ref, yd_ref, x_ref, g1_ref, sh2_ref, sc2_ref, gnw_ref,
                wo_ref, lng_ref, lnb_ref, wr_ref, br_ref, x1_ref, h2_ref, lg_ref):
    o = of_ref[0].astype(F32) + or_ref[0].astype(F32)
    z = z_ref[0].astype(F32)
    heads = []
    for h in range(GH):
        oh = o[:, h * GDV:(h + 1) * GDV]
        heads.append((oh * lax.rsqrt(jnp.mean(oh * oh, axis=-1, keepdims=True) + NORM_EPS) * gnw_ref[...]
                      * _silu(z[:, h * GDV:(h + 1) * GDV])).astype(BF16))
    mix = jnp.concatenate(heads + [yd_ref[0]], axis=1)
    x1 = _normalize(alpha * x_ref[0] + g1_ref[0] * _dot(mix, wo_ref[...])) * lng_ref[...] + lnb_ref[...]
    x1_ref[0] = x1
    h2 = _normalize(x1) * (1.0 + sc2_ref[0]) + sh2_ref[0]
    h2_ref[0] = h2
    h_hi = h2.astype(BF16)
    h_lo = (h2 - h_hi.astype(F32)).astype(BF16)
    both = _dot(h_hi, wr_ref[...])
    lg_ref[0] = br_ref[...] + both[:, :LANES] + both[:, LANES:] + _dot(h_lo, wr_ref[:, :LANES])


def _out(o_f, o_r, z, y_diff, xc, mod3, gdn_norm_w, w_out, ln_g, ln_b, w_router3, b_router, alpha):
    bsz, lq, _ = y_diff.shape
    tile = OUT_TILE if lq % OUT_TILE == 0 else ROW_TILE
    nt = lq // tile

    def row_spec(width):
        return pl.BlockSpec((1, tile, width), lambda b, i: (b, i, 0))

    def mod_spec(col):
        return pl.BlockSpec((1, 1, D), lambda b, i: (b, 0, col))

    def const_spec(shape):
        return pl.BlockSpec(shape, lambda b, i: tuple(0 for _ in shape))

    return pl.pallas_call(
        functools.partial(_out_kernel, alpha),
        grid=(bsz, nt),
        in_specs=[row_spec(GH * GDV), row_spec(GH * GDV), row_spec(G_Z), row_spec(DH * DDV), row_spec(D),
                  mod_spec(2), mod_spec(3), mod_spec(4),
                  const_spec((1, GDV)), const_spec((G_Z + DH * DDV, D)),
                  const_spec((1, D)), const_spec((1, D)),
                  const_spec((D, 2 * LANES)), const_spec((1, LANES))],
        out_specs=[row_spec(D), row_spec(D), row_spec(LANES)],
        out_shape=[jax.ShapeDtypeStruct((bsz, lq, D), F32), jax.ShapeDtypeStruct((bsz, lq, D), F32),
                   jax.ShapeDtypeStruct((bsz, lq, LANES), F32)],
        compiler_params=_cparams(("arbitrary", "arbitrary")),
        name="out",
    )(o_f, o_r, z, y_diff, xc, mod3, mod3, mod3, gdn_norm_w.reshape(1, GDV), w_out,
      ln_g.reshape(1, D), ln_b.reshape(1, D), w_router3, b_router)


def _route_kernel(lg_ref, tri_ref, ids_ref, gate_ref, cnt_ref, run_ref):
    @pl.when(pl.program_id(0) == 0)
    def _():
        run_ref[...] = jnp.zeros_like(run_ref)

    lgt = lg_ref[...].T[:ROUTE_ROWS]
    n = lgt.shape[1]
    row_i = lax.broadcasted_iota(jnp.int32, lgt.shape, 0)
    row = row_i.astype(F32)
    neg = jnp.float32(-jnp.inf)

    def first_max(mask):
        masked = jnp.where(mask, lgt, neg)
        m = jnp.max(masked, axis=0, keepdims=True)
        idx = jnp.min(jnp.where(jnp.logical_and(mask, masked == m), row, float(LANES)), axis=0, keepdims=True)
        return m, idx

    is_group = row_i < N_GROUPS
    m_g, grp = first_max(is_group)
    p_g = 1.0 / jnp.sum(jnp.where(is_group, jnp.exp(lgt - m_g), 0.0), axis=0, keepdims=True)
    e_row = row - float(N_GROUPS)
    lo_e = grp * float(E_PER_GROUP)
    in_group = jnp.logical_and(e_row >= lo_e, e_row < lo_e + float(E_PER_GROUP))
    l0, i0 = first_max(in_group)
    l1, i1 = first_max(jnp.logical_and(in_group, row != i0))
    r = jnp.exp(l1 - l0)
    gate0 = p_g / (1.0 + r)
    gate1 = p_g * r / (1.0 + r)
    e0 = i0 - float(N_GROUPS)
    e1 = i1 - float(N_GROUPS)
    oh0 = (e_row == e0).astype(F32)
    oh1 = (e_row == e1).astype(F32)
    both = oh0 + oh1
    before = _dot(both.astype(BF16), tri_ref[...]) + run_ref[...]
    rank0 = jnp.sum(oh0 * before, axis=0, keepdims=True)
    rank1 = jnp.sum(oh1 * before, axis=0, keepdims=True)
    run_ref[...] = run_ref[...] + jnp.sum(both, axis=1, keepdims=True)
    cnt_ref[...] = jnp.broadcast_to(run_ref[...], cnt_ref.shape)
    ids_ref[...] = jnp.concatenate([e0, e1, rank0, rank1, jnp.zeros((SUBLANES - 4, n), F32)],
                                   axis=0).astype(jnp.int32)
    gate_ref[...] = jnp.concatenate([gate0, gate1, jnp.zeros((LANES - 2, n), F32)], axis=0).T


def _route(logits, tri_strict):
    t = logits.shape[0]
    tile = tri_strict.shape[0]
    return pl.pallas_call(
        _route_kernel,
        grid=(t // tile,),
        in_specs=[pl.BlockSpec((tile, LANES), lambda i: (i, 0)),
                  pl.BlockSpec((tile, tile), lambda i: (0, 0))],
        out_specs=[pl.BlockSpec((SUBLANES, tile), lambda i: (0, i)),
                   pl.BlockSpec((tile, LANES), lambda i: (i, 0)),
                   pl.BlockSpec((ROUTE_ROWS, LANES), lambda i: (0, 0))],
        out_shape=[jax.ShapeDtypeStruct((SUBLANES, t), jnp.int32), jax.ShapeDtypeStruct((t, LANES), F32),
                   jax.ShapeDtypeStruct((ROUTE_ROWS, LANES), F32)],
        scratch_shapes=[pltpu.VMEM((ROUTE_ROWS, 1), F32)],
        compiler_params=_cparams(("arbitrary",)),
        name="route",
    )(logits, tri_strict)


def _dispatch_kernel(tail_ref, used_ref, dest_ref, h_hbm, buf_hbm, zero_ref, src_ref, load_sem, row_sem, zsem):
    i = pl.program_id(0)
    n_tiles = pl.num_programs(0)
    n_slots = src_ref.shape[0]
    groups = DMA_TOKENS // SUBLANES

    def load(tile):
        slot = lax.rem(tile, n_slots)
        return pltpu.make_async_copy(h_hbm.at[pl.ds(tile * groups, groups)], src_ref.at[slot], load_sem.at[slot])

    def wait_rows(tile):
        slot = lax.rem(tile, n_slots)
        for _ in range(2):
            pltpu.make_async_copy(buf_hbm.at[pl.ds(0, DMA_TOKENS)], buf_hbm.at[pl.ds(0, DMA_TOKENS)],
                                  row_sem.at[slot]).wait()

    @pl.when(i == 0)
    def _():
        load(0).start()
        zero_ref[...] = jnp.zeros_like(zero_ref)
        n_blocks = buf_hbm.shape[0] // FFN_BLOCK

        def block_copy(b):
            return pltpu.make_async_copy(zero_ref, buf_hbm.at[pl.ds(b * FFN_BLOCK, FFN_BLOCK)], zsem)

        for e in range(N_EXPERTS):
            @pl.when(tail_ref[e] >= 0)
            def _():
                block_copy(tail_ref[e]).start()
        lax.fori_loop(used_ref[0], n_blocks, lambda b, carry: (block_copy(b).start(), carry)[1], 0)
        for e in range(N_EXPERTS):
            @pl.when(tail_ref[e] >= 0)
            def _():
                block_copy(tail_ref[e]).wait()
        lax.fori_loop(used_ref[0], n_blocks, lambda b, carry: (block_copy(b).wait(), carry)[1], 0)

    @pl.when(i + 1 < n_tiles)
    def _():
        load(i + 1).start()

    load(i).wait()
    slot = lax.rem(i, n_slots)

    def start(grp, carry):
        for u in range(SUBLANES):
            for j in range(2):
                dst = dest_ref[j * DMA_TOKENS + SUBLANES * grp + u]
                pltpu.make_async_copy(src_ref.at[slot, grp, pl.ds(u, 1)], buf_hbm.at[pl.ds(dst, 1)],
                                      row_sem.at[slot]).start()
        return carry

    lax.fori_loop(0, groups, start, 0)

    @pl.when(i > 0)
    def _():
        wait_rows(i - 1)

    @pl.when(i == n_tiles - 1)
    def _():
        wait_rows(i)


def _dispatch(tail_block, n_used, dest_flat, h2, n_rows):
    t, d = h2.shape
    grid_spec = pltpu.PrefetchScalarGridSpec(
        num_scalar_prefetch=2,
        grid=(t // DMA_TOKENS,),
        in_specs=[pl.BlockSpec((2 * DMA_TOKENS,), lambda i, tail, used: (i,), memory_space=pltpu.SMEM),
                  pl.BlockSpec(memory_space=pl.ANY)],
        out_specs=pl.BlockSpec(memory_space=pl.ANY),
        scratch_shapes=[pltpu.VMEM((FFN_BLOCK, d), h2.dtype),
                        pltpu.VMEM((3, DMA_TOKENS // SUBLANES, SUBLANES, d), h2.dtype),
                        pltpu.SemaphoreType.DMA((3,)), pltpu.SemaphoreType.DMA((3,)),
                        pltpu.SemaphoreType.DMA(())],
    )
    return pl.pallas_call(
        _dispatch_kernel,
        grid_spec=grid_spec,
        out_shape=jax.ShapeDtypeStruct((n_rows, d), h2.dtype),
        compiler_params=_cparams(("arbitrary",)),
        name="dispatch",
    )(tail_block, n_used, dest_flat, h2.reshape(t // SUBLANES, SUBLANES, d))


def _combine_kernel(alpha, dest_ref, ys_hbm, gate_ref, x1_ref, g2_ref, lng_ref, lnb_ref, o_ref, rows_ref, sem):
    i = pl.program_id(0)
    n_tiles = pl.num_programs(0) - COMBINE_LAG
    n = x1_ref.shape[0]
    n_slots = rows_ref.shape[0]
    slot_in = lax.rem(i, n_slots)
    slot_out = lax.rem(i + n_slots - COMBINE_LAG, n_slots)

    def start_group(grp):
        for u in range(SUBLANES):
            for j in range(2):
                src = dest_ref[j * DMA_TOKENS + SUBLANES * grp + u]
                pltpu.make_async_copy(ys_hbm.at[pl.ds(src, 1)], rows_ref.at[slot_in, j, grp, pl.ds(u, 1)],
                                      sem.at[slot_in]).start()

    def finish():
        gate = gate_ref[...]
        r0 = rows_ref[slot_out, 0].reshape(n, -1)
        r1 = rows_ref[slot_out, 1].reshape(n, -1)
        y = gate[:, 0:1] * r0 + gate[:, 1:2] * r1
        o_ref[...] = _normalize(alpha * x1_ref[...] + g2_ref[0] * y) * lng_ref[...] + lnb_ref[...]

    @pl.when(i >= COMBINE_LAG)
    def _():
        for j in range(2):
            pltpu.make_async_copy(ys_hbm.at[pl.ds(0, n)], ys_hbm.at[pl.ds(0, n)], sem.at[slot_out]).wait()

    @pl.when(i < COMBINE_LAG)
    def _():
        lax.fori_loop(0, n // SUBLANES, lambda grp, carry: (start_group(grp), carry)[1], 0)

    @pl.when(jnp.logical_and(i >= COMBINE_LAG, i < n_tiles))
    def _():
        for grp in range(n // SUBLANES):
            start_group(grp)
        finish()

    @pl.when(i >= n_tiles)
    def _():
        finish()


def _combine(dest_flat, ys, gate, x1, mod3, ln_g, ln_b, alpha, rows_per_batch):
    t, d = x1.shape
    per_batch = rows_per_batch // DMA_TOKENS
    n_tiles = t // DMA_TOKENS

    def done(i):
        return jnp.maximum(i - COMBINE_LAG, 0)

    return pl.pallas_call(
        functools.partial(_combine_kernel, alpha),
        grid=(n_tiles + COMBINE_LAG,),
        in_specs=[pl.BlockSpec((2 * DMA_TOKENS,), lambda i: (jnp.minimum(i, n_tiles - 1),), memory_space=pltpu.SMEM),
                  pl.BlockSpec(memory_space=pl.ANY),
                  pl.BlockSpec((DMA_TOKENS, LANES), lambda i: (done(i), 0)),
                  pl.BlockSpec((DMA_TOKENS, d), lambda i: (done(i), 0)),
                  pl.BlockSpec((1, 1, d), lambda i: (done(i) // per_batch, 0, 5)),
                  pl.BlockSpec((1, d), lambda i: (0, 0)),
                  pl.BlockSpec((1, d), lambda i: (0, 0))],
        out_specs=pl.BlockSpec((DMA_TOKENS, d), lambda i: (done(i), 0)),
        out_shape=jax.ShapeDtypeStruct((t, d), F32),
        scratch_shapes=[pltpu.VMEM((COMBINE_LAG + 1, 2, DMA_TOKENS // SUBLANES, SUBLANES, d), F32),
                        pltpu.SemaphoreType.DMA((COMBINE_LAG + 1,))],
        compiler_params=_cparams(("arbitrary",)),
        name="combine",
    )(dest_flat, ys, gate, x1, mod3, ln_g.reshape(1, d), ln_b.reshape(1, d))


def _ffn_kernel(start_ref, nblk_ref, used_ref, buf_hbm, wg_ref, wu_ref, wd_ref, ys_hbm,
                wgb_ref, wub_ref, wdb_ref, x_buf, o_buf, in_sem, out_sem):
    e = pl.program_id(0)
    nb = nblk_ref[e]
    base = start_ref[e]
    used = used_ref[0]
    n_in = x_buf.shape[0]
    n_out = o_buf.shape[0]
    ahead = n_in - 1

    def rows(g):
        return pl.ds(g * FFN_BLOCK, FFN_BLOCK)

    def in_copy(g):
        slot = lax.rem(g, n_in)
        return pltpu.make_async_copy(buf_hbm.at[rows(g)], x_buf.at[slot], in_sem.at[slot])

    def out_copy(g):
        slot = lax.rem(g, n_out)
        return pltpu.make_async_copy(o_buf.at[slot], ys_hbm.at[rows(g)], out_sem.at[slot])

    @pl.when(e == 0)
    def _():
        for g in range(ahead):
            @pl.when(g < used)
            def _():
                in_copy(g).start()

    @pl.when(nb > 0)
    def _():
        wgb_ref[...] = wg_ref[0].astype(BF16)
        wub_ref[...] = wu_ref[0].astype(BF16)
        wdb_ref[...] = wd_ref[0].astype(BF16)

        def block(j, carry):
            g = base + j
            in_copy(g).wait()

            @pl.when(g + ahead < used)
            def _():
                in_copy(g + ahead).start()

            @pl.when(g >= n_out)
            def _():
                out_copy(g - n_out).wait()

            islot = lax.rem(g, n_in)
            oslot = lax.rem(g, n_out)
            slab = FFN_BLOCK // FFN_SLABS
            xs = [x_buf[islot, s * slab:(s + 1) * slab, :].astype(BF16) for s in range(FFN_SLABS)]
            gates = [_dot(x, wgb_ref[...]) for x in xs]
            ups = [_dot(x, wub_ref[...]) for x in xs]
            hidden = [(_silu(gt) * up).astype(BF16) for gt, up in zip(gates, ups)]
            for s in range(FFN_SLABS):
                o_buf[oslot, s * slab:(s + 1) * slab, :] = _dot(hidden[s], wdb_ref[...])
            out_copy(g).start()
            return carry

        lax.fori_loop(0, nb, block, 0)

    @pl.when(e == pl.num_programs(0) - 1)
    def _():
        for back in range(n_out):
            @pl.when(used - 1 - back >= 0)
            def _():
                out_copy(used - 1 - back).wait()
        n_blocks = ys_hbm.shape[0] // FFN_BLOCK
        o_buf[0] = jnp.zeros(o_buf.shape[1:], o_buf.dtype)

        def clear(b):
            return pltpu.make_async_copy(o_buf.at[0], ys_hbm.at[pl.ds(b * FFN_BLOCK, FFN_BLOCK)], out_sem.at[0])

        lax.fori_loop(used_ref[0], n_blocks, lambda b, carry: (clear(b).start(), carry)[1], 0)
        lax.fori_loop(used_ref[0], n_blocks, lambda b, carry: (clear(b).wait(), carry)[1], 0)


def _ffn(seg_start, seg_blocks, n_used, buf, w_gate, w_up, w_down):
    n_rows, d = buf.shape
    n_experts = w_gate.shape[0]
    grid_spec = pltpu.PrefetchScalarGridSpec(
        num_scalar_prefetch=3,
        grid=(n_experts,),
        in_specs=[pl.BlockSpec(memory_space=pl.ANY),
                  pl.BlockSpec((1, d, D_EXPERT), lambda e, *_: (e, 0, 0)),
                  pl.BlockSpec((1, d, D_EXPERT), lambda e, *_: (e, 0, 0)),
                  pl.BlockSpec((1, D_EXPERT, d), lambda e, *_: (e, 0, 0))],
        out_specs=pl.BlockSpec(memory_space=pl.ANY),
        scratch_shapes=[pltpu.VMEM((d, D_EXPERT), BF16), pltpu.VMEM((d, D_EXPERT), BF16),
                        pltpu.VMEM((D_EXPERT, d), BF16),
                        pltpu.VMEM((FFN_IN_SLOTS, FFN_BLOCK, d), F32), pltpu.VMEM((FFN_OUT_SLOTS, FFN_BLOCK, d), F32),
                        pltpu.SemaphoreType.DMA((FFN_IN_SLOTS,)), pltpu.SemaphoreType.DMA((FFN_OUT_SLOTS,))],
    )
    return pl.pallas_call(
        _ffn_kernel,
        grid_spec=grid_spec,
        out_shape=jax.ShapeDtypeStruct((n_rows, d), F32),
        compiler_params=_cparams(("arbitrary",)),
        name="ffn",
    )(seg_start, seg_blocks, n_used, buf, w_gate, w_up, w_down)


def _rope_tables(seq_len, n_ctx_rows):
    lane = np.arange(LANES)
    within = lane % DDK
    freq = within % ROPE_PAIR
    use_col = within >= DDK // 2
    sign = np.where((lane % (2 * ROPE_PAIR)) < ROPE_PAIR, -1.0, 1.0).astype(np.float32)
    half = DDK // 2
    inv_freq = (ROPE_THETA ** (-np.arange(0, half, 2, dtype=np.float32) / np.float32(half))).astype(np.float32)
    t = np.arange(seq_len)
    pos = np.where(use_col[None, :], (t % GRID_W)[:, None], (t // GRID_W)[:, None]).astype(np.float32)
    ang = pos * inv_freq[freq][None, :]
    cos = np.concatenate([np.cos(ang), np.ones((n_ctx_rows, LANES), np.float32)], axis=0)
    sin = np.concatenate([np.sin(ang) * sign[None, :], np.zeros((n_ctx_rows, LANES), np.float32)], axis=0)
    return jnp.asarray(cos, F32), jnp.asarray(sin, F32)


def _block_tri(n, block, upper):
    i = np.arange(n)
    same = (i[:, None] // block) == (i[None, :] // block)
    tri = (i[None, :] >= i[:, None]) if upper else (i[None, :] <= i[:, None])
    return jnp.asarray((same & tri).astype(np.float32), dtype=BF16)


def _pack_w_in(w_in):
    off = G_QKV + G_Z
    v_start = G_COLS + 2 * D_QK
    pad = jnp.zeros((D, LANES - G_GATES), w_in.dtype)
    w_gate = jnp.concatenate([w_in[:, off:off + G_GATES], pad, w_in[:, off + G_GATES:G_COLS], pad], axis=1)
    return (w_in[:, :off].astype(BF16), w_gate.astype(BF16), w_in[:, G_COLS:v_start].astype(BF16),
            w_in[:, v_start:].T.astype(BF16))


def kernel(x, c, ctx, c_ctx, w_ada, b_ada, w_in, conv_w, gdn_a_log, gdn_dt_bias, gdn_norm_w, diff_lambda,
           diff_norm_w, w_out, ln1_g, ln1_b, w_router_group, b_router_group, w_router_expert, b_router_expert,
           w_expert_gate, w_expert_up, w_expert_down, ln2_g, ln2_b):
    depth = w_ada.shape[0]
    assert depth == 1, "single-layer block: the context stream never feeds a later layer"
    bsz, seq, _ = x.shape
    n_ctx = ctx.shape[1]
    assert seq % ROW_TILE == 0 and n_ctx % ROW_TILE == 0 and seq % GRID_W == 0 and bsz < SUBLANES
    assert (bsz * seq) % ROUTE_TILE == 0 and seq % DMA_TOKENS == 0
    n_lat_tiles = seq // ROW_TILE
    alpha = (2.0 * depth) ** 0.25
    lam_init = 0.8 - 0.6 * math.exp(-0.3 * 0)
    i = 0

    cc = jnp.zeros((SUBLANES, D), F32).at[:bsz].set(c).at[SUBLANES - 1].set(c_ctx)
    mod = _ada(cc, w_ada[i], b_ada[i])
    mod3 = mod.reshape(SUBLANES, 1, 6 * D)

    cos_t, sin_t = _rope_tables(seq, n_ctx)
    qkv, z, gt, dq, dk, dvt = _proj(x, ctx, mod3, cos_t, sin_t, *_pack_w_in(w_in[i]))

    conv_w8 = jnp.zeros((SUBLANES, G_QKV), F32).at[:GCONV].set(conv_w[i])
    alog_v = jnp.zeros((1, LANES), F32).at[0, :G_GATES].set(gdn_a_log[i].reshape(-1))
    dtb_v = jnp.zeros((1, LANES), F32).at[0, :G_GATES].set(gdn_dt_bias[i].reshape(-1))
    gq, gk, gv, beta, gc = _prep(qkv, gt, conv_w8, alog_v, dtb_v,
                                 _block_tri(ROW_TILE, CHUNK, False), _block_tri(ROW_TILE, CHUNK, True), n_lat_tiles)
    uw = _chunk(gk, gv, beta, gc)
    o_f, o_r = _scan(gq, gk, gc, uw, n_lat_tiles)

    y_diff = _attn(dq, dk, dvt, diff_lambda[i], diff_norm_w[i], lam_init, seq, tq=min(ATTN_Q_TILE, seq))

    w_router = jnp.zeros((D, LANES), F32).at[:, :N_GROUPS].set(w_router_group[i]) \
        .at[:, N_GROUPS:N_GROUPS + N_EXPERTS].set(w_router_expert[i])
    w_hi = w_router.astype(BF16)
    w_router3 = jnp.concatenate([w_hi, (w_router - w_hi.astype(F32)).astype(BF16)], axis=1)
    b_router = jnp.zeros((1, LANES), F32).at[0, :N_GROUPS].set(b_router_group[i]) \
        .at[0, N_GROUPS:N_GROUPS + N_EXPERTS].set(b_router_expert[i])
    x1, h2, logits = _out(o_f, o_r, z, y_diff, x, mod3, gdn_norm_w[i], w_out[i].astype(BF16), ln1_g[i], ln1_b[i],
                          w_router3, b_router, alpha)

    t = bsz * seq
    tri_strict = jnp.asarray(np.triu(np.ones((ROUTE_TILE, ROUTE_TILE), np.float32), 1), dtype=BF16)
    ids, gate, counts = _route(logits.reshape(t, LANES), tri_strict)

    cnt = counts[N_GROUPS:N_GROUPS + N_EXPERTS, 0].astype(jnp.int32)
    padded = (cnt + FFN_BLOCK - 1) // FFN_BLOCK * FFN_BLOCK
    p_end = jnp.cumsum(padded)
    p_start = p_end - padded
    n_blocks = (2 * t) // FFN_BLOCK + N_EXPERTS
    n_used = (p_end[-1:] // FFN_BLOCK).astype(jnp.int32)
    tail_block = jnp.where(cnt > 0, p_end // FFN_BLOCK - 1, -1).astype(jnp.int32)
    one_hot = ids[0:2, :, None] == jnp.arange(N_EXPERTS, dtype=jnp.int32)
    row_start = jnp.sum(jnp.where(one_hot, p_start, 0), axis=-1)
    dest = jnp.swapaxes((row_start + ids[2:4]).reshape(2, t // DMA_TOKENS, DMA_TOKENS), 0, 1)
    dest = dest.reshape(-1).astype(jnp.int32)

    buf = _dispatch(tail_block, n_used, dest, h2.reshape(t, D), n_blocks * FFN_BLOCK)
    ys = _ffn((p_start // FFN_BLOCK).astype(jnp.int32), (padded // FFN_BLOCK).astype(jnp.int32), n_used, buf,
              w_expert_gate[i], w_expert_up[i], w_expert_down[i])
    out = _combine(dest, ys, gate, x1.reshape(t, D), mod3, ln2_g[i], ln2_b[i], alpha, seq)
    return out.reshape(bsz, seq, D)
```

```python
import functools
import math

import numpy as np
import jax
import jax.numpy as jnp
from jax import lax
from jax.experimental import pallas as pl
from jax.experimental.pallas import tpu as pltpu

F32 = jnp.float32
BF16 = jnp.bfloat16

D = 1024
GRID_W = 64
ROPE_THETA = 10000.0
GH = 4
GDK = 128
GDV = 128
GCONV = 5
DH = 4
DDK = 64
DDV = 128
G_QK = GH * GDK
G_QKV = 2 * G_QK + GH * GDV
G_Z = GH * GDV
G_GATES = 2 * GH
G_COLS = G_QKV + G_Z + 2 * G_GATES
D_QK = DH * 2 * DDK
ROPE_PAIR = DDK // 4
N_GROUPS = 4
E_PER_GROUP = 8
N_EXPERTS = N_GROUPS * E_PER_GROUP
D_EXPERT = 512
LN_EPS = 1e-5
NORM_EPS = 1e-6

LANES = 128
SUBLANES = 8
BF16_ROWS = 16
ADA_BLOCK = 1024
ROW_TILE = 256
CHUNK = 64
OUT_TILE = 512
CHUNK_BATCH = 2
SCAN_BATCH = 2
FFN_BLOCK = 256
FFN_SLABS = 2
FFN_IN_SLOTS = 6
FFN_OUT_SLOTS = 4
KEY_TILE = 768
ATTN_Q_TILE = 1024
SCORE_LOOKAHEAD = 1
ROUTE_TILE = 1024
ROUTE_ROWS = 40
DMA_TOKENS = 512
COMBINE_LAG = 2
VMEM_LIMIT = 56 * 1024 * 1024

Q_SCALE = DDK ** -0.5 * math.log2(math.e)


def _cparams(sem):
    return pltpu.CompilerParams(dimension_semantics=sem, vmem_limit_bytes=VMEM_LIMIT)


def _dot(a, b):
    return jnp.dot(a, b, preferred_element_type=F32)


def _dot_nt(a, b):
    return lax.dot_general(a, b, (((1,), (1,)), ((), ())), preferred_element_type=F32)


def _normalize(x):
    mu = jnp.mean(x, axis=-1, keepdims=True)
    xc = x - mu
    var = jnp.mean(xc * xc, axis=-1, keepdims=True)
    return xc * lax.rsqrt(var + LN_EPS)


def _silu(x):
    return x * jax.nn.sigmoid(x)


def _split3(x):
    hi = x.astype(BF16)
    r = x - hi.astype(F32)
    mid = r.astype(BF16)
    lo = (r - mid.astype(F32)).astype(BF16)
    return hi, mid, lo


def _ada_kernel(c_ref, w_ref, b_ref, o_ref):
    s = _silu(c_ref[...])
    s_hi = s.astype(BF16)
    s_lo = (s - s_hi.astype(F32)).astype(BF16)
    w = w_ref[...]
    w_hi = w.astype(BF16)
    w_lo = (w - w_hi.astype(F32)).astype(BF16)
    rows = s.shape[0]
    both = _dot(jnp.concatenate([s_hi, s_lo], axis=0), w_hi)
    o_ref[...] = both[:rows] + both[rows:] + _dot(s_hi, w_lo) + b_ref[...]


def _ada(cc, w_ada, b_ada):
    n = w_ada.shape[1]
    bn = ADA_BLOCK
    return pl.pallas_call(
        _ada_kernel,
        grid=(n // bn,),
        in_specs=[pl.BlockSpec((SUBLANES, D), lambda j: (0, 0)),
                  pl.BlockSpec((D, bn), lambda j: (0, j)),
                  pl.BlockSpec((1, bn), lambda j: (0, j))],
        out_specs=pl.BlockSpec((SUBLANES, bn), lambda j: (0, j)),
        out_shape=jax.ShapeDtypeStruct((SUBLANES, n), F32),
        compiler_params=_cparams(("arbitrary",)),
        name="ada",
    )(cc, w_ada, b_ada.reshape(1, n))


def _rope(v, cos, sin, low_half):
    fwd = pltpu.roll(v, LANES - ROPE_PAIR, axis=1)
    bwd = pltpu.roll(v, ROPE_PAIR, axis=1)
    return v * cos + jnp.where(low_half, fwd, bwd) * sin


def _proj_kernel(n_lat_tiles, x_ref, ctx_ref, sh_ref, sc_ref, cos_ref, sin_ref, wg_ref, wgate_ref, wqk_ref, wvt_ref,
                 qkv_ref, z_ref, gt_ref, dq_ref, dk_ref, dvt_ref):
    rows = jnp.where(pl.program_id(1) >= n_lat_tiles, ctx_ref[0], x_ref[0])
    h = (_normalize(rows) * (1.0 + sc_ref[0]) + sh_ref[0]).astype(BF16)
    q_all = _dot(h, wqk_ref[:, :D_QK])
    k_all = _dot(h, wqk_ref[:, D_QK:])
    qkv_ref[0] = _dot(h, wg_ref[:, :G_QKV])
    z_ref[0] = _dot(h, wg_ref[:, G_QKV:]).astype(z_ref.dtype)
    gt_ref[0] = _dot(h, wgate_ref[...])
    cos = cos_ref[...]
    sin = sin_ref[...]
    lane = lax.broadcasted_iota(jnp.int32, cos.shape, 1)
    low_half = (lane % (2 * ROPE_PAIR)) < ROPE_PAIR
    for j in range(D_QK // LANES):
        slab = slice(j * LANES, (j + 1) * LANES)
        dq_ref[0, :, slab] = (_rope(q_all[:, slab], cos, sin, low_half) * Q_SCALE).astype(BF16)
        dk_ref[0, :, slab] = _rope(k_all[:, slab], cos, sin, low_half).astype(BF16)
    dvt_ref[0] = _dot_nt(wvt_ref[...], h).astype(BF16)


def _proj(x, ctx, mod3, cos_t, sin_t, w_gdn, w_gate, w_qk, wv_t):
    bsz, seq, _ = x.shape
    lc = seq + ctx.shape[1]
    nt = lc // ROW_TILE
    n_lat_tiles = seq // ROW_TILE
    ctx_row = mod3.shape[0] - 1

    def mod_idx(col):
        return lambda b, i: (jnp.where(i >= n_lat_tiles, ctx_row, b), 0, col)

    def row_spec(width):
        return pl.BlockSpec((1, ROW_TILE, width), lambda b, i: (b, i, 0))

    outs = [(G_QKV, F32), (G_Z, BF16), (2 * LANES, F32), (D_QK, BF16), (D_QK, BF16)]
    return pl.pallas_call(
        functools.partial(_proj_kernel, n_lat_tiles),
        grid=(bsz, nt),
        in_specs=[pl.BlockSpec((1, ROW_TILE, D), lambda b, i: (b, jnp.minimum(i, n_lat_tiles - 1), 0)),
                  pl.BlockSpec((1, ROW_TILE, D), lambda b, i: (b, jnp.maximum(i - n_lat_tiles, 0), 0)),
                  pl.BlockSpec((1, 1, D), mod_idx(0)),
                  pl.BlockSpec((1, 1, D), mod_idx(1)),
                  pl.BlockSpec((ROW_TILE, LANES), lambda b, i: (i, 0)),
                  pl.BlockSpec((ROW_TILE, LANES), lambda b, i: (i, 0)),
                  pl.BlockSpec(w_gdn.shape, lambda b, i: (0, 0)),
                  pl.BlockSpec(w_gate.shape, lambda b, i: (0, 0)),
                  pl.BlockSpec(w_qk.shape, lambda b, i: (0, 0)),
                  pl.BlockSpec(wv_t.shape, lambda b, i: (0, 0))],
        out_specs=[row_spec(w) for w, _ in outs]
        + [pl.BlockSpec((1, DH * DDV, ROW_TILE), lambda b, i: (b, 0, i))],
        out_shape=[jax.ShapeDtypeStruct((bsz, lc, w), dt) for w, dt in outs]
        + [jax.ShapeDtypeStruct((bsz, DH * DDV, lc), BF16)],
        compiler_params=_cparams(("arbitrary", "arbitrary")),
        name="proj",
    )(x, ctx, mod3, mod3, cos_t, sin_t, w_gdn, w_gate, w_qk, wv_t)


def _prep_kernel(n_lat_tiles, nt, main_ref, prev_ref, next_ref, gt_ref, cw_ref, alog_ref, dtb_ref,
                 lo_ref, up_ref, q_ref, k_ref, v_ref, beta_ref, gc_ref, ext_ref):
    i = pl.program_id(1)
    halo = prev_ref.shape[1]
    has_prev = jnp.logical_and(i != 0, i != n_lat_tiles)
    has_next = jnp.logical_and(i != n_lat_tiles - 1, i != nt - 1)
    ext_ref[0:halo, :] = prev_ref[0] * has_prev.astype(F32)
    ext_ref[halo:halo + ROW_TILE, :] = main_ref[0]
    ext_ref[halo + ROW_TILE:, :] = next_ref[0] * has_next.astype(F32)
    pad = GCONV // 2
    acc = None
    ext = ext_ref[...]
    n_ext = ext.shape[0]
    for j in range(GCONV):
        shifted = ext if j == pad else pltpu.roll(ext, (pad - j) % n_ext, axis=0)
        term = shifted[halo:halo + ROW_TILE, :] * cw_ref[j:j + 1, :]
        acc = term if acc is None else acc + term
    qkv = _silu(acc)
    for h in range(GH):
        q = qkv[:, h * GDK:(h + 1) * GDK]
        q_ref[0, :, h * GDK:(h + 1) * GDK] = (q * (lax.rsqrt(jnp.sum(q * q, axis=-1, keepdims=True) + NORM_EPS)
                                                 * (GDK ** -0.5))).astype(q_ref.dtype)
        k = qkv[:, G_QK + h * GDK:G_QK + (h + 1) * GDK]
        k_ref[0, :, h * GDK:(h + 1) * GDK] = (
            k * lax.rsqrt(jnp.sum(k * k, axis=-1, keepdims=True) + NORM_EPS)).astype(k_ref.dtype)
    v_ref[0] = qkv[:, 2 * G_QK:].astype(v_ref.dtype)
    gt = gt_ref[0]
    beta_ref[0] = jax.nn.sigmoid(gt[:, :LANES])
    a = gt[:, LANES:] + dtb_ref[...]
    softplus = jnp.maximum(a, 0.0) + jnp.log(1.0 + jnp.exp(-jnp.abs(a)))
    g = -jnp.exp(alog_ref[...]) * softplus
    lo = lo_ref[...]
    up = up_ref[...]
    fwd = None
    bwd = None
    for part in _split3(g):
        f = _dot(lo, part)
        r = _dot(up, part)
        fwd = f if fwd is None else fwd + f
        bwd = r if bwd is None else bwd + r
    lane = lax.broadcasted_iota(jnp.int32, g.shape, 1)
    gc_ref[0] = jnp.where(lane < GH, fwd, bwd)


def _prep(qkv, gt, conv_w8, alog_v, dtb_v, tri_lo, tri_up, n_lat_tiles):
    bsz, lc, _ = qkv.shape
    nt = lc // ROW_TILE
    halo = SUBLANES
    per = ROW_TILE // halo
    last = lc // halo - 1

    def row_spec(width):
        return pl.BlockSpec((1, ROW_TILE, width), lambda b, i: (b, i, 0))

    def const_spec(shape):
        return pl.BlockSpec(shape, lambda b, i: tuple(0 for _ in shape))

    outs = [(G_QK, BF16), (G_QK, BF16), (GH * GDV, BF16), (LANES, F32), (LANES, F32)]
    return pl.pallas_call(
        functools.partial(_prep_kernel, n_lat_tiles, nt),
        grid=(bsz, nt),
        in_specs=[row_spec(G_QKV),
                  pl.BlockSpec((1, halo, G_QKV), lambda b, i: (b, jnp.maximum(i * per - 1, 0), 0)),
                  pl.BlockSpec((1, halo, G_QKV), lambda b, i: (b, jnp.minimum((i + 1) * per, last), 0)),
                  row_spec(2 * LANES),
                  const_spec((SUBLANES, G_QKV)), const_spec((1, LANES)), const_spec((1, LANES)),
                  const_spec((ROW_TILE, ROW_TILE)), const_spec((ROW_TILE, ROW_TILE))],
        out_specs=[row_spec(w) for w, _ in outs],
        out_shape=[jax.ShapeDtypeStruct((bsz, lc, w), dt) for w, dt in outs],
        scratch_shapes=[pltpu.VMEM((ROW_TILE + 2 * halo, G_QKV), F32)],
        compiler_params=_cparams(("arbitrary", "arbitrary")),
        name="prep",
    )(qkv, qkv, qkv, gt, conv_w8, alog_v, dtb_v, tri_lo, tri_up)


def _chunk_masks(n, rev):
    row = lax.broadcasted_iota(jnp.int32, (n, n), 0)
    col = lax.broadcasted_iota(jnp.int32, (n, n), 1)
    same = (row // CHUNK) == (col // CHUNK)
    incl = jnp.logical_and(same, (col >= row) if rev else (col <= row))
    strict = jnp.logical_and(same, (col > row) if rev else (col < row))
    return incl, strict


def _decay(gcol, grow, incl):
    return jnp.where(incl, jnp.exp(jnp.minimum(gcol - grow, 0.0)), 0.0)


def _chunk_kernel(k_ref, v_ref, beta_ref, gc_ref, uw_ref):
    n = k_ref.shape[1]
    row = lax.broadcasted_iota(jnp.int32, (n, n), 0)
    col = lax.broadcasted_iota(jnp.int32, (n, n), 1)
    eye = (row == col).astype(F32)
    sizes = [2 ** j for j in range(1, int(math.log2(CHUNK)))]
    same = {r: (row // r) == (col // r) for r in sizes + [CHUNK]}
    a_all, t_all, x_all, where = [], [], [], []
    masks = [_chunk_masks(n, rev) for rev in (False, True)]
    for bb in range(k_ref.shape[0]):
        gc = gc_ref[bb]
        gct = gc.T
        beta = beta_ref[bb]
        gram = [_dot_nt(k_ref[bb, :, h * GDK:(h + 1) * GDK], k_ref[bb, :, h * GDK:(h + 1) * GDK]) for h in range(GH)]
        for d in range(2):
            incl, strict = masks[d]
            for h in range(GH):
                c = d * GH + h
                k = k_ref[bb, :, h * GDK:(h + 1) * GDK].astype(F32)
                v = v_ref[bb, :, h * GDV:(h + 1) * GDV].astype(F32)
                bcol = beta[:, c:c + 1]
                gcol = gc[:, c:c + 1]
                grow = gct[c:c + 1, :]
                kb = k * bcol
                a = jnp.where(strict, gram[h] * bcol * _decay(gcol, grow, incl), 0.0)
                a_all.append(a.astype(BF16))
                t_all.append((eye - jnp.where(same[2], a, 0.0)).astype(BF16))
                x_all.append(jnp.concatenate([v * bcol, kb * jnp.exp(gcol)], axis=1).astype(BF16))
                where.append((bb, c))
    n_prob = len(a_all)
    zero = jnp.zeros((n, n), BF16)
    for r in sizes:
        off_mask = jnp.logical_and(same[2 * r], jnp.logical_not(same[r]))
        inner = [_dot(jnp.where(off_mask, a_all[c], zero), t_all[c]).astype(BF16) for c in range(n_prob)]
        t_all = [t_all[c] - _dot(t_all[c], inner[c]).astype(BF16) for c in range(n_prob)]
    for p in range(n_prob):
        bb, c = where[p]
        uw_ref[bb, c] = _dot(t_all[p], x_all[p]).astype(uw_ref.dtype)


def _chunk(k, v, beta, gc):
    bsz, lc, _ = k.shape
    nt = lc // ROW_TILE
    nb = CHUNK_BATCH if bsz % CHUNK_BATCH == 0 else 1

    def row_spec(width):
        return pl.BlockSpec((nb, ROW_TILE, width), lambda b, i: (b, i, 0))

    return pl.pallas_call(
        _chunk_kernel,
        grid=(bsz // nb, nt),
        in_specs=[row_spec(G_QK), row_spec(GH * GDV), row_spec(LANES), row_spec(LANES)],
        out_specs=pl.BlockSpec((nb, 2 * GH, ROW_TILE, GDV + GDK), lambda b, i: (b, 0, i, 0)),
        out_shape=jax.ShapeDtypeStruct((bsz, 2 * GH, lc, GDV + GDK), BF16),
        compiler_params=_cparams(("arbitrary", "arbitrary")),
        name="chunk",
    )(k, v, beta, gc)


def _scan_kernel(qf_ref, kf_ref, gf_ref, uwf_ref, qr_ref, kr_ref, gr_ref, uwr_ref,
                 of_ref, or_ref, s_ref):
    @pl.when(pl.program_id(1) == 0)
    def _():
        s_ref[...] = jnp.zeros_like(s_ref)

    n_chunks = ROW_TILE // CHUNK
    n_batch = qf_ref.shape[0]
    row = lax.broadcasted_iota(jnp.int32, (CHUNK, CHUNK), 0)
    col = lax.broadcasted_iota(jnp.int32, (CHUNK, CHUNK), 1)
    dirs = ((qf_ref, kf_ref, gf_ref, uwf_ref, of_ref), (qr_ref, kr_ref, gr_ref, uwr_ref, or_ref))
    chains = [(bb, d, h) for bb in range(n_batch) for d in range(2) for h in range(GH)]
    state = [s_ref[i] for i in range(len(chains))]

    def chunk_rows(step, d):
        ci = n_chunks - 1 - step if d == 1 else step
        return slice(ci * CHUNK, (ci + 1) * CHUNK)

    pre = []
    for step in range(n_chunks):
        gates = {}
        for bb in range(n_batch):
            for d in range(2):
                gc = dirs[d][2][bb, chunk_rows(step, d), :]
                gates[bb, d] = (gc, gc.T)
        per_chain = []
        for bb, d, h in chains:
            q_ref, k_ref, _, uw_ref, _ = dirs[d]
            rev = d == 1
            incl = (col >= row) if rev else (col <= row)
            rows = chunk_rows(step, d)
            gc, gct = gates[bb, d]
            c = d * GH + h
            q_b = q_ref[bb, rows, h * GDK:(h + 1) * GDK]
            k_b = k_ref[bb, rows, h * GDK:(h + 1) * GDK]
            q = q_b.astype(F32)
            k = k_b.astype(F32)
            gcol = gc[:, c:c + 1]
            grow = gct[c:c + 1, :]
            last = 0 if rev else CHUNK - 1
            g_last = gcol[last:last + 1, :]
            a_qk = (_dot_nt(q_b, k_b) * _decay(gcol, grow, incl)).astype(BF16)
            wq = jnp.concatenate([uw_ref[bb, h, rows, GDV:], (q * jnp.exp(gcol)).astype(BF16)], axis=0)
            k_tail_t = (k * jnp.exp(g_last - gcol)).T.astype(BF16)
            per_chain.append((a_qk, wq, k_tail_t, jnp.exp(g_last)))
        pre.append(per_chain)

    for step in range(n_chunks):
        ws = [_dot(pre[step][i][1], state[i].astype(BF16)) for i in range(len(chains))]
        v_new = [(dirs[d][3][bb, h, chunk_rows(step, d), :GDV].astype(F32) - ws[i][:CHUNK]).astype(BF16)
                 for i, (bb, d, h) in enumerate(chains)]
        for i, (bb, d, h) in enumerate(chains):
            a_qk, _, k_tail_t, decay_last = pre[step][i]
            dirs[d][4][bb, chunk_rows(step, d), h * GDV:(h + 1) * GDV] = (
                ws[i][CHUNK:] + _dot(a_qk, v_new[i])).astype(BF16)
            state[i] = state[i] * decay_last + _dot(k_tail_t, v_new[i])
    for i in range(len(chains)):
        s_ref[i] = state[i]


def _scan(q, k, gc, uw, n_lat_tiles):
    bsz, lc, _ = q.shape
    nt = lc // ROW_TILE
    n_ctx_tiles = nt - n_lat_tiles

    def fwd(b, i):
        return jnp.where(i < n_ctx_tiles, n_lat_tiles + i, i - n_ctx_tiles)

    def bwd(b, i):
        return jnp.where(i < n_ctx_tiles, nt - 1 - i, n_lat_tiles - 1 - (i - n_ctx_tiles))

    nb = SCAN_BATCH if bsz % SCAN_BATCH == 0 else 1

    def specs(tile):
        return [pl.BlockSpec((nb, ROW_TILE, G_QK), lambda b, i: (b, tile(b, i), 0)),
                pl.BlockSpec((nb, ROW_TILE, G_QK), lambda b, i: (b, tile(b, i), 0)),
                pl.BlockSpec((nb, ROW_TILE, LANES), lambda b, i: (b, tile(b, i), 0))]

    def uw_spec(tile, d):
        return pl.BlockSpec((nb, GH, ROW_TILE, GDV + GDK), lambda b, i: (b, d, tile(b, i), 0))

    def out_spec(tile):
        return pl.BlockSpec((nb, ROW_TILE, GH * GDV), lambda b, i: (b, tile(b, i), 0))

    return pl.pallas_call(
        _scan_kernel,
        grid=(bsz // nb, nt),
        in_specs=specs(fwd) + [uw_spec(fwd, 0)] + specs(bwd) + [uw_spec(bwd, 1)],
        out_specs=[out_spec(fwd), out_spec(bwd)],
        out_shape=[jax.ShapeDtypeStruct((bsz, lc, GH * GDV), BF16)] * 2,
        scratch_shapes=[pltpu.VMEM((nb * 2 * GH, GDK, GDV), F32)],
        compiler_params=_cparams(("arbitrary", "arbitrary")),
        name="scan",
    )(q, k, gc, uw, q, k, gc, uw)


def _attn_kernel(lam_init, q_ref, k_ref, vt_ref, lam_ref, nw_ref, o_ref):
    lam = lam_ref[...]
    lam_full = (jnp.exp(jnp.sum(lam[0:1] * lam[1:2], axis=-1, keepdims=True))
                - jnp.exp(jnp.sum(lam[2:3] * lam[3:4], axis=-1, keepdims=True)) + lam_init)
    q = q_ref[0]
    tq = q.shape[0]
    n_keys = k_ref.shape[1]
    lane = lax.broadcasted_iota(jnp.int32, q.shape, 1)
    zero = jnp.zeros_like(q)
    qc = [jnp.where(lane < DDK, q, zero), jnp.where(lane >= DDK, q, zero)]
    m = [jnp.full((1, tq), -jnp.inf, F32) for _ in range(2)]
    l = [jnp.zeros((1, tq), F32) for _ in range(2)]
    acc = [jnp.zeros((DDV, tq), F32) for _ in range(2)]
    n_tiles = n_keys // KEY_TILE
    bounds = [t * KEY_TILE for t in range(n_tiles)] + [n_keys]

    def scores(t):
        k_t = k_ref[0, bounds[t]:bounds[t + 1], :]
        tiles = [_dot_nt(k_t, qc[c]) for c in range(2)]
        return [(s, jnp.max(s, axis=0, keepdims=True)) for s in tiles]

    pending = [scores(t) for t in range(min(SCORE_LOOKAHEAD, n_tiles))]
    for t in range(n_tiles):
        s_cur = pending.pop(0)
        if t + SCORE_LOOKAHEAD < n_tiles:
            pending.append(scores(t + SCORE_LOOKAHEAD))
        width = bounds[t + 1] - bounds[t]
        vt_t = jnp.concatenate([vt_ref[0, :, bounds[t]:bounds[t + 1]], jnp.ones((BF16_ROWS, width), BF16)], axis=0)
        for c in range(2):
            s, s_max = s_cur[c]
            m_new = jnp.maximum(m[c], s_max)
            p = jnp.exp2(s - m_new)
            scale = jnp.exp2(m[c] - m_new)
            pv = _dot(vt_t, p.astype(BF16))
            l[c] = scale * l[c] + pv[DDV:DDV + 1]
            acc[c] = scale * acc[c] + pv[:DDV]
            m[c] = m_new
    o = acc[0] / l[0] - lam_full * (acc[1] / l[1])
    y = o * lax.rsqrt(jnp.mean(o * o, axis=0, keepdims=True) + NORM_EPS)
    o_ref[0] = (y.T * nw_ref[...] * (1.0 - lam_init)).astype(o_ref.dtype)


def _attn(dq, dk, dvt, lam, norm_w, lam_init, lq, tq):
    bsz, lc, _ = dq.shape
    return pl.pallas_call(
        functools.partial(_attn_kernel, lam_init),
        grid=(bsz, DH, lq // tq),
        in_specs=[pl.BlockSpec((1, tq, 2 * DDK), lambda b, h, i: (b, i, h)),
                  pl.BlockSpec((1, lc, 2 * DDK), lambda b, h, i: (b, 0, h)),
                  pl.BlockSpec((1, DDV, lc), lambda b, h, i: (b, h, 0)),
                  pl.BlockSpec((4, DDK), lambda b, h, i: (0, 0)),
                  pl.BlockSpec((1, DDV), lambda b, h, i: (0, 0))],
        out_specs=pl.BlockSpec((1, tq, DDV), lambda b, h, i: (b, i, h)),
        out_shape=jax.ShapeDtypeStruct((bsz, lq, DH * DDV), BF16),
        compiler_params=_cparams(("arbitrary", "arbitrary", "arbitrary")),
        name="attn",
    )(dq, dk, dvt, lam, norm_w.reshape(1, DDV))


def _out_kernel(alpha, of_ref, or_ref, z_ref, yd_ref, x_ref, g1_ref, sh2_ref, sc2_ref, gnw_ref,
                wo_ref, lng_ref, lnb_ref, wr_ref, br_ref, x1_ref, h2_ref, lg_ref):
    o = of_ref[0].astype(F32) + or_ref[0].astype(F32)
    z = z_ref[0].astype(F32)
    heads = []
    for h in range(GH):
        oh = o[:, h * GDV:(h + 1) * GDV]
        heads.append((oh * lax.rsqrt(jnp.mean(oh * oh, axis=-1, keepdims=True) + NORM_EPS) * gnw_ref[...]
                      * _silu(z[:, h * GDV:(h + 1) * GDV])).astype(BF16))
    mix = jnp.concatenate(heads + [yd_ref[0]], axis=1)
    x1 = _normalize(alpha * x_ref[0] + g1_ref[0] * _dot(mix, wo_ref[...])) * lng_ref[...] + lnb_ref[...]
    x1_ref[0] = x1
    h2 = _normalize(x1) * (1.0 + sc2_ref[0]) + sh2_ref[0]
    h2_ref[0] = h2
    h_hi = h2.astype(BF16)
    h_lo = (h2 - h_hi.astype(F32)).astype(BF16)
    both = _dot(h_hi, wr_ref[...])
    lg_ref[0] = br_ref[...] + both[:, :LANES] + both[:, LANES:] + _dot(h_lo, wr_ref[:, :LANES])


def _out(o_f, o_r, z, y_diff, xc, mod3, gdn_norm_w, w_out, ln_g, ln_b, w_router3, b_router, alpha):
    bsz, lq, _ = y_diff.shape
    tile = OUT_TILE if lq % OUT_TILE == 0 else ROW_TILE
    nt = lq // tile

    def row_spec(width):
        return pl.BlockSpec((1, tile, width), lambda b, i: (b, i, 0))

    def mod_spec(col):
        return pl.BlockSpec((1, 1, D), lambda b, i: (b, 0, col))

    def const_spec(shape):
        return pl.BlockSpec(shape, lambda b, i: tuple(0 for _ in shape))

    return pl.pallas_call(
        functools.partial(_out_kernel, alpha),
        grid=(bsz, nt),
        in_specs=[row_spec(GH * GDV), row_spec(GH * GDV), row_spec(G_Z), row_spec(DH * DDV), row_spec(D),
                  mod_spec(2), mod_spec(3), mod_spec(4),
                  const_spec((1, GDV)), const_spec((G_Z + DH * DDV, D)),
                  const_spec((1, D)), const_spec((1, D)),
                  const_spec((D, 2 * LANES)), const_spec((1, LANES))],
        out_specs=[row_spec(D), row_spec(D), row_spec(LANES)],
        out_shape=[jax.ShapeDtypeStruct((bsz, lq, D), F32), jax.ShapeDtypeStruct((bsz, lq, D), F32),
                   jax.ShapeDtypeStruct((bsz, lq, LANES), F32)],
        compiler_params=_cparams(("arbitrary", "arbitrary")),
        name="out",
    )(o_f, o_r, z, y_diff, xc, mod3, mod3, mod3, gdn_norm_w.reshape(1, GDV), w_out,
      ln_g.reshape(1, D), ln_b.reshape(1, D), w_router3, b_router)


def _route_kernel(lg_ref, tri_ref, ids_ref, gate_ref, cnt_ref, run_ref):
    @pl.when(pl.program_id(0) == 0)
    def _():
        run_ref[...] = jnp.zeros_like(run_ref)

    lgt = lg_ref[...].T[:ROUTE_ROWS]
    n = lgt.shape[1]
    row_i = lax.broadcasted_iota(jnp.int32, lgt.shape, 0)
    row = row_i.astype(F32)
    neg = jnp.float32(-jnp.inf)

    def first_max(mask):
        masked = jnp.where(mask, lgt, neg)
        m = jnp.max(masked, axis=0, keepdims=True)
        idx = jnp.min(jnp.where(jnp.logical_and(mask, masked == m), row, float(LANES)), axis=0, keepdims=True)
        return m, idx

    is_group = row_i < N_GROUPS
    m_g, grp = first_max(is_group)
    p_g = 1.0 / jnp.sum(jnp.where(is_group, jnp.exp(lgt - m_g), 0.0), axis=0, keepdims=True)
    e_row = row - float(N_GROUPS)
    lo_e = grp * float(E_PER_GROUP)
    in_group = jnp.logical_and(e_row >= lo_e, e_row < lo_e + float(E_PER_GROUP))
    l0, i0 = first_max(in_group)
    l1, i1 = first_max(jnp.logical_and(in_group, row != i0))
    r = jnp.exp(l1 - l0)
    gate0 = p_g / (1.0 + r)
    gate1 = p_g * r / (1.0 + r)
    e0 = i0 - float(N_GROUPS)
    e1 = i1 - float(N_GROUPS)
    oh0 = (e_row == e0).astype(F32)
    oh1 = (e_row == e1).astype(F32)
    both = oh0 + oh1
    before = _dot(both.astype(BF16), tri_ref[...]) + run_ref[...]
    rank0 = jnp.sum(oh0 * before, axis=0, keepdims=True)
    rank1 = jnp.sum(oh1 * before, axis=0, keepdims=True)
    run_ref[...] = run_ref[...] + jnp.sum(both, axis=1, keepdims=True)
    cnt_ref[...] = jnp.broadcast_to(run_ref[...], cnt_ref.shape)
    ids_ref[...] = jnp.concatenate([e0, e1, rank0, rank1, jnp.zeros((SUBLANES - 4, n), F32)],
                                   axis=0).astype(jnp.int32)
    gate_ref[...] = jnp.concatenate([gate0, gate1, jnp.zeros((LANES - 2, n), F32)], axis=0).T


def _route(logits, tri_strict):
    t = logits.shape[0]
    tile = tri_strict.shape[0]
    return pl.pallas_call(
        _route_kernel,
        grid=(t // tile,),
        in_specs=[pl.BlockSpec((tile, LANES), lambda i: (i, 0)),
                  pl.BlockSpec((tile, tile), lambda i: (0, 0))],
        out_specs=[pl.BlockSpec((SUBLANES, tile), lambda i: (0, i)),
                   pl.BlockSpec((tile, LANES), lambda i: (i, 0)),
                   pl.BlockSpec((ROUTE_ROWS, LANES), lambda i: (0, 0))],
        out_shape=[jax.ShapeDtypeStruct((SUBLANES, t), jnp.int32), jax.ShapeDtypeStruct((t, LANES), F32),
                   jax.ShapeDtypeStruct((ROUTE_ROWS, LANES), F32)],
        scratch_shapes=[pltpu.VMEM((ROUTE_ROWS, 1), F32)],
        compiler_params=_cparams(("arbitrary",)),
        name="route",
    )(logits, tri_strict)


def _dispatch_kernel(tail_ref, used_ref, dest_ref, h_hbm, buf_hbm, zero_ref, src_ref, load_sem, row_sem, zsem):
    i = pl.program_id(0)
    n_tiles = pl.num_programs(0)
    n_slots = src_ref.shape[0]
    groups = DMA_TOKENS // SUBLANES

    def load(tile):
        slot = lax.rem(tile, n_slots)
        return pltpu.make_async_copy(h_hbm.at[pl.ds(tile * groups, groups)], src_ref.at[slot], load_sem.at[slot])

    def wait_rows(tile):
        slot = lax.rem(tile, n_slots)
        for _ in range(2):
            pltpu.make_async_copy(buf_hbm.at[pl.ds(0, DMA_TOKENS)], buf_hbm.at[pl.ds(0, DMA_TOKENS)],
                                  row_sem.at[slot]).wait()

    @pl.when(i == 0)
    def _():
        load(0).start()
        zero_ref[...] = jnp.zeros_like(zero_ref)
        n_blocks = buf_hbm.shape[0] // FFN_BLOCK

        def block_copy(b):
            return pltpu.make_async_copy(zero_ref, buf_hbm.at[pl.ds(b * FFN_BLOCK, FFN_BLOCK)], zsem)

        for e in range(N_EXPERTS):
            @pl.when(tail_ref[e] >= 0)
            def _():
                block_copy(tail_ref[e]).start()
        lax.fori_loop(used_ref[0], n_blocks, lambda b, carry: (block_copy(b).start(), carry)[1], 0)
        for e in range(N_EXPERTS):
            @pl.when(tail_ref[e] >= 0)
            def _():
                block_copy(tail_ref[e]).wait()
        lax.fori_loop(used_ref[0], n_blocks, lambda b, carry: (block_copy(b).wait(), carry)[1], 0)

    @pl.when(i + 1 < n_tiles)
    def _():
        load(i + 1).start()

    load(i).wait()
    slot = lax.rem(i, n_slots)

    def start(grp, carry):
        for u in range(SUBLANES):
            for j in range(2):
                dst = dest_ref[j * DMA_TOKENS + SUBLANES * grp + u]
                pltpu.make_async_copy(src_ref.at[slot, grp, pl.ds(u, 1)], buf_hbm.at[pl.ds(dst, 1)],
                                      row_sem.at[slot]).start()
        return carry

    lax.fori_loop(0, groups, start, 0)

    @pl.when(i > 0)
    def _():
        wait_rows(i - 1)

    @pl.when(i == n_tiles - 1)
    def _():
        wait_rows(i)


def _dispatch(tail_block, n_used, dest_flat, h2, n_rows):
    t, d = h2.shape
    grid_spec = pltpu.PrefetchScalarGridSpec(
        num_scalar_prefetch=2,
        grid=(t // DMA_TOKENS,),
        in_specs=[pl.BlockSpec((2 * DMA_TOKENS,), lambda i, tail, used: (i,), memory_space=pltpu.SMEM),
                  pl.BlockSpec(memory_space=pl.ANY)],
        out_specs=pl.BlockSpec(memory_space=pl.ANY),
        scratch_shapes=[pltpu.VMEM((FFN_BLOCK, d), h2.dtype),
                        pltpu.VMEM((3, DMA_TOKENS // SUBLANES, SUBLANES, d), h2.dtype),
                        pltpu.SemaphoreType.DMA((3,)), pltpu.SemaphoreType.DMA((3,)),
                        pltpu.SemaphoreType.DMA(())],
    )
    return pl.pallas_call(
        _dispatch_kernel,
        grid_spec=grid_spec,
        out_shape=jax.ShapeDtypeStruct((n_rows, d), h2.dtype),
        compiler_params=_cparams(("arbitrary",)),
        name="dispatch",
    )(tail_block, n_used, dest_flat, h2.reshape(t // SUBLANES, SUBLANES, d))


def _combine_kernel(alpha, dest_ref, ys_hbm, gate_ref, x1_ref, g2_ref, lng_ref, lnb_ref, o_ref, rows_ref, sem):
    i = pl.program_id(0)
    n_tiles = pl.num_programs(0) - COMBINE_LAG
    n = x1_ref.shape[0]
    n_slots = rows_ref.shape[0]
    slot_in = lax.rem(i, n_slots)
    slot_out = lax.rem(i + n_slots - COMBINE_LAG, n_slots)

    def start_group(grp):
        for u in range(SUBLANES):
            for j in range(2):
                src = dest_ref[j * DMA_TOKENS + SUBLANES * grp + u]
                pltpu.make_async_copy(ys_hbm.at[pl.ds(src, 1)], rows_ref.at[slot_in, j, grp, pl.ds(u, 1)],
                                      sem.at[slot_in]).start()

    def finish():
        gate = gate_ref[...]
        r0 = rows_ref[slot_out, 0].reshape(n, -1)
        r1 = rows_ref[slot_out, 1].reshape(n, -1)
        y = gate[:, 0:1] * r0 + gate[:, 1:2] * r1
        o_ref[...] = _normalize(alpha * x1_ref[...] + g2_ref[0] * y) * lng_ref[...] + lnb_ref[...]

    @pl.when(i >= COMBINE_LAG)
    def _():
        for j in range(2):
            pltpu.make_async_copy(ys_hbm.at[pl.ds(0, n)], ys_hbm.at[pl.ds(0, n)], sem.at[slot_out]).wait()

    @pl.when(i < COMBINE_LAG)
    def _():
        lax.fori_loop(0, n // SUBLANES, lambda grp, carry: (start_group(grp), carry)[1], 0)

    @pl.when(jnp.logical_and(i >= COMBINE_LAG, i < n_tiles))
    def _():
        for grp in range(n // SUBLANES):
            start_group(grp)
        finish()

    @pl.when(i >= n_tiles)
    def _():
        finish()


def _combine(dest_flat, ys, gate, x1, mod3, ln_g, ln_b, alpha, rows_per_batch):
    t, d = x1.shape
    per_batch = rows_per_batch // DMA_TOKENS
    n_tiles = t // DMA_TOKENS

    def done(i):
        return jnp.maximum(i - COMBINE_LAG, 0)

    return pl.pallas_call(
        functools.partial(_combine_kernel, alpha),
        grid=(n_tiles + COMBINE_LAG,),
        in_specs=[pl.BlockSpec((2 * DMA_TOKENS,), lambda i: (jnp.minimum(i, n_tiles - 1),), memory_space=pltpu.SMEM),
                  pl.BlockSpec(memory_space=pl.ANY),
                  pl.BlockSpec((DMA_TOKENS, LANES), lambda i: (done(i), 0)),
                  pl.BlockSpec((DMA_TOKENS, d), lambda i: (done(i), 0)),
                  pl.BlockSpec((1, 1, d), lambda i: (done(i) // per_batch, 0, 5)),
                  pl.BlockSpec((1, d), lambda i: (0, 0)),
                  pl.BlockSpec((1, d), lambda i: (0, 0))],
        out_specs=pl.BlockSpec((DMA_TOKENS, d), lambda i: (done(i), 0)),
        out_shape=jax.ShapeDtypeStruct((t, d), F32),
        scratch_shapes=[pltpu.VMEM((COMBINE_LAG + 1, 2, DMA_TOKENS // SUBLANES, SUBLANES, d), F32),
                        pltpu.SemaphoreType.DMA((COMBINE_LAG + 1,))],
        compiler_params=_cparams(("arbitrary",)),
        name="combine",
    )(dest_flat, ys, gate, x1, mod3, ln_g.reshape(1, d), ln_b.reshape(1, d))


def _ffn_kernel(start_ref, nblk_ref, used_ref, buf_hbm, wg_ref, wu_ref, wd_ref, ys_hbm,
                wgb_ref, wub_ref, wdb_ref, x_buf, o_buf, in_sem, out_sem):
    e = pl.program_id(0)
    nb = nblk_ref[e]
    base = start_ref[e]
    used = used_ref[0]
    n_in = x_buf.shape[0]
    n_out = o_buf.shape[0]
    ahead = n_in - 1

    def rows(g):
        return pl.ds(g * FFN_BLOCK, FFN_BLOCK)

    def in_copy(g):
        slot = lax.rem(g, n_in)
        return pltpu.make_async_copy(buf_hbm.at[rows(g)], x_buf.at[slot], in_sem.at[slot])

    def out_copy(g):
        slot = lax.rem(g, n_out)
        return pltpu.make_async_copy(o_buf.at[slot], ys_hbm.at[rows(g)], out_sem.at[slot])

    @pl.when(e == 0)
    def _():
        for g in range(ahead):
            @pl.when(g < used)
            def _():
                in_copy(g).start()

    @pl.when(nb > 0)
    def _():
        wgb_ref[...] = wg_ref[0].astype(BF16)
        wub_ref[...] = wu_ref[0].astype(BF16)
        wdb_ref[...] = wd_ref[0].astype(BF16)

        def block(j, carry):
            g = base + j
            in_copy(g).wait()

            @pl.when(g + ahead < used)
            def _():
                in_copy(g + ahead).start()

            @pl.when(g >= n_out)
            def _():
                out_copy(g - n_out).wait()

            islot = lax.rem(g, n_in)
            oslot = lax.rem(g, n_out)
            slab = FFN_BLOCK // FFN_SLABS
            xs = [x_buf[islot, s * slab:(s + 1) * slab, :].astype(BF16) for s in range(FFN_SLABS)]
            gates = [_dot(x, wgb_ref[...]) for x in xs]
            ups = [_dot(x, wub_ref[...]) for x in xs]
            hidden = [(_silu(gt) * up).astype(BF16) for gt, up in zip(gates, ups)]
            for s in range(FFN_SLABS):
                o_buf[oslot, s * slab:(s + 1) * slab, :] = _dot(hidden[s], wdb_ref[...])
            out_copy(g).start()
            return carry

        lax.fori_loop(0, nb, block, 0)

    @pl.when(e == pl.num_programs(0) - 1)
    def _():
        for back in range(n_out):
            @pl.when(used - 1 - back >= 0)
            def _():
                out_copy(used - 1 - back).wait()
        n_blocks = ys_hbm.shape[0] // FFN_BLOCK
        o_buf[0] = jnp.zeros(o_buf.shape[1:], o_buf.dtype)

        def clear(b):
            return pltpu.make_async_copy(o_buf.at[0], ys_hbm.at[pl.ds(b * FFN_BLOCK, FFN_BLOCK)], out_sem.at[0])

        lax.fori_loop(used_ref[0], n_blocks, lambda b, carry: (clear(b).start(), carry)[1], 0)
        lax.fori_loop(used_ref[0], n_blocks, lambda b, carry: (clear(b).wait(), carry)[1], 0)


def _ffn(seg_start, seg_blocks, n_used, buf, w_gate, w_up, w_down):
    n_rows, d = buf.shape
    n_experts = w_gate.shape[0]
    grid_spec = pltpu.PrefetchScalarGridSpec(
        num_scalar_prefetch=3,
        grid=(n_experts,),
        in_specs=[pl.BlockSpec(memory_space=pl.ANY),
                  pl.BlockSpec((1, d, D_EXPERT), lambda e, *_: (e, 0, 0)),
                  pl.BlockSpec((1, d, D_EXPERT), lambda e, *_: (e, 0, 0)),
                  pl.BlockSpec((1, D_EXPERT, d), lambda e, *_: (e, 0, 0))],
        out_specs=pl.BlockSpec(memory_space=pl.ANY),
        scratch_shapes=[pltpu.VMEM((d, D_EXPERT), BF16), pltpu.VMEM((d, D_EXPERT), BF16),
                        pltpu.VMEM((D_EXPERT, d), BF16),
                        pltpu.VMEM((FFN_IN_SLOTS, FFN_BLOCK, d), F32), pltpu.VMEM((FFN_OUT_SLOTS, FFN_BLOCK, d), F32),
                        pltpu.SemaphoreType.DMA((FFN_IN_SLOTS,)), pltpu.SemaphoreType.DMA((FFN_OUT_SLOTS,))],
    )
    return pl.pallas_call(
        _ffn_kernel,
        grid_spec=grid_spec,
        out_shape=jax.ShapeDtypeStruct((n_rows, d), F32),
        compiler_params=_cparams(("arbitrary",)),
        name="ffn",
    )(seg_start, seg_blocks, n_used, buf, w_gate, w_up, w_down)


def _rope_tables(seq_len, n_ctx_rows):
    lane = np.arange(LANES)
    within = lane % DDK
    freq = within % ROPE_PAIR
    use_col = within >= DDK // 2
    sign = np.where((lane % (2 * ROPE_PAIR)) < ROPE_PAIR, -1.0, 1.0).astype(np.float32)
    half = DDK // 2
    inv_freq = (ROPE_THETA ** (-np.arange(0, half, 2, dtype=np.float32) / np.float32(half))).astype(np.float32)
    t = np.arange(seq_len)
    pos = np.where(use_col[None, :], (t % GRID_W)[:, None], (t // GRID_W)[:, None]).astype(np.float32)
    ang = pos * inv_freq[freq][None, :]
    cos = np.concatenate([np.cos(ang), np.ones((n_ctx_rows, LANES), np.float32)], axis=0)
    sin = np.concatenate([np.sin(ang) * sign[None, :], np.zeros((n_ctx_rows, LANES), np.float32)], axis=0)
    return jnp.asarray(cos, F32), jnp.asarray(sin, F32)


def _block_tri(n, block, upper):
    i = np.arange(n)
    same = (i[:, None] // block) == (i[None, :] // block)
    tri = (i[None, :] >= i[:, None]) if upper else (i[None, :] <= i[:, None])
    return jnp.asarray((same & tri).astype(np.float32), dtype=BF16)


def _pack_w_in(w_in):
    off = G_QKV + G_Z
    v_start = G_COLS + 2 * D_QK
    pad = jnp.zeros((D, LANES - G_GATES), w_in.dtype)
    w_gate = jnp.concatenate([w_in[:, off:off + G_GATES], pad, w_in[:, off + G_GATES:G_COLS], pad], axis=1)
    return (w_in[:, :off].astype(BF16), w_gate.astype(BF16), w_in[:, G_COLS:v_start].astype(BF16),
            w_in[:, v_start:].T.astype(BF16))


def kernel(x, c, ctx, c_ctx, w_ada, b_ada, w_in, conv_w, gdn_a_log, gdn_dt_bias, gdn_norm_w, diff_lambda,
           diff_norm_w, w_out, ln1_g, ln1_b, w_router_group, b_router_group, w_router_expert, b_router_expert,
           w_expert_gate, w_expert_up, w_expert_down, ln2_g, ln2_b):
    depth = w_ada.shape[0]
    assert depth == 1, "single-layer block: the context stream never feeds a later layer"
    bsz, seq, _ = x.shape
    n_ctx = ctx.shape[1]
    assert seq % ROW_TILE == 0 and n_ctx % ROW_TILE == 0 and seq % GRID_W == 0 and bsz < SUBLANES
    assert (bsz * seq) % ROUTE_TILE == 0 and seq % DMA_TOKENS == 0
    n_lat_tiles = seq // ROW_TILE
    alpha = (2.0 * depth) ** 0.25
    lam_init = 0.8 - 0.6 * math.exp(-0.3 * 0)
    i = 0

    cc = jnp.zeros((SUBLANES, D), F32).at[:bsz].set(c).at[SUBLANES - 1].set(c_ctx)
    mod = _ada(cc, w_ada[i], b_ada[i])
    mod3 = mod.reshape(SUBLANES, 1, 6 * D)

    cos_t, sin_t = _rope_tables(seq, n_ctx)
    qkv, z, gt, dq, dk, dvt = _proj(x, ctx, mod3, cos_t, sin_t, *_pack_w_in(w_in[i]))

    conv_w8 = jnp.zeros((SUBLANES, G_QKV), F32).at[:GCONV].set(conv_w[i])
    alog_v = jnp.zeros((1, LANES), F32).at[0, :G_GATES].set(gdn_a_log[i].reshape(-1))
    dtb_v = jnp.zeros((1, LANES), F32).at[0, :G_GATES].set(gdn_dt_bias[i].reshape(-1))
    gq, gk, gv, beta, gc = _prep(qkv, gt, conv_w8, alog_v, dtb_v,
                                 _block_tri(ROW_TILE, CHUNK, False), _block_tri(ROW_TILE, CHUNK, True), n_lat_tiles)
    uw = _chunk(gk, gv, beta, gc)
    o_f, o_r = _scan(gq, gk, gc, uw, n_lat_tiles)

    y_diff = _attn(dq, dk, dvt, diff_lambda[i], diff_norm_w[i], lam_init, seq, tq=min(ATTN_Q_TILE, seq))

    w_router = jnp.zeros((D, LANES), F32).at[:, :N_GROUPS].set(w_router_group[i]) \
        .at[:, N_GROUPS:N_GROUPS + N_EXPERTS].set(w_router_expert[i])
    w_hi = w_router.astype(BF16)
    w_router3 = jnp.concatenate([w_hi, (w_router - w_hi.astype(F32)).astype(BF16)], axis=1)
    b_router = jnp.zeros((1, LANES), F32).at[0, :N_GROUPS].set(b_router_group[i]) \
        .at[0, N_GROUPS:N_GROUPS + N_EXPERTS].set(b_router_expert[i])
    x1, h2, logits = _out(o_f, o_r, z, y_diff, x, mod3, gdn_norm_w[i], w_out[i].astype(BF16), ln1_g[i], ln1_b[i],
                          w_router3, b_router, alpha)

    t = bsz * seq
    tri_strict = jnp.asarray(np.triu(np.ones((ROUTE_TILE, ROUTE_TILE), np.float32), 1), dtype=BF16)
    ids, gate, counts = _route(logits.reshape(t, LANES), tri_strict)

    cnt = counts[N_GROUPS:N_GROUPS + N_EXPERTS, 0].astype(jnp.int32)
    padded = (cnt + FFN_BLOCK - 1) // FFN_BLOCK * FFN_BLOCK
    p_end = jnp.cumsum(padded)
    p_start = p_end - padded
    n_blocks = (2 * t) // FFN_BLOCK + N_EXPERTS
    n_used = (p_end[-1:] // FFN_BLOCK).astype(jnp.int32)
    tail_block = jnp.where(cnt > 0, p_end // FFN_BLOCK - 1, -1).astype(jnp.int32)
    one_hot = ids[0:2, :, None] == jnp.arange(N_EXPERTS, dtype=jnp.int32)
    row_start = jnp.sum(jnp.where(one_hot, p_start, 0), axis=-1)
    dest = jnp.swapaxes((row_start + ids[2:4]).reshape(2, t // DMA_TOKENS, DMA_TOKENS), 0, 1)
    dest = dest.reshape(-1).astype(jnp.int32)

    buf = _dispatch(tail_block, n_used, dest, h2.reshape(t, D), n_blocks * FFN_BLOCK)
    ys = _ffn((p_start // FFN_BLOCK).astype(jnp.int32), (padded // FFN_BLOCK).astype(jnp.int32), n_used, buf,
              w_expert_gate[i], w_expert_up[i], w_expert_down[i])
    out = _combine(dest, ys, gate, x1.reshape(t, D), mod3, ln2_g[i], ln2_b[i], alpha, seq)
    return out.reshape(bsz, seq, D)
```

```python
import functools
import math

import numpy as np
import jax
import jax.numpy as jnp
from jax import lax
from jax.experimental import pallas as pl
from jax.experimental.pallas import tpu as pltpu

F32 = jnp.float32
BF16 = jnp.bfloat16

D = 1024
GRID_W = 64
ROPE_THETA = 10000.0
GH = 4
GDK = 128
GDV = 128
GCONV = 5
DH = 4
DDK = 64
DDV = 128
G_QK = GH * GDK
G_QKV = 2 * G_QK + GH * GDV
G_Z = GH * GDV
G_GATES = 2 * GH
G_COLS = G_QKV + G_Z + 2 * G_GATES
D_QK = DH * 2 * DDK
ROPE_PAIR = DDK // 4
N_GROUPS = 4
E_PER_GROUP = 8
N_EXPERTS = N_GROUPS * E_PER_GROUP
D_EXPERT = 512
LN_EPS = 1e-5
NORM_EPS = 1e-6

LANES = 128
SUBLANES = 8
BF16_ROWS = 16
ADA_BLOCK = 1024
ROW_TILE = 256
CHUNK = 64
OUT_TILE = 512
CHUNK_BATCH = 2
SCAN_BATCH = 2
FFN_BLOCK = 256
FFN_SLABS = 2
FFN_IN_SLOTS = 6
FFN_OUT_SLOTS = 4
KEY_TILE = 512
ATTN_Q_TILE = 1024
SCORE_LOOKAHEAD = 1
ROUTE_TILE = 1024
ROUTE_ROWS = 40
DMA_TOKENS = 512
COMBINE_LAG = 2
VMEM_LIMIT = 56 * 1024 * 1024

Q_SCALE = DDK ** -0.5 * math.log2(math.e)


def _cparams(sem):
    return pltpu.CompilerParams(dimension_semantics=sem, vmem_limit_bytes=VMEM_LIMIT)


def _dot(a, b):
    return jnp.dot(a, b, preferred_element_type=F32)


def _dot_nt(a, b):
    return lax.dot_general(a, b, (((1,), (1,)), ((), ())), preferred_element_type=F32)


def _normalize(x):
    mu = jnp.mean(x, axis=-1, keepdims=True)
    xc = x - mu
    var = jnp.mean(xc * xc, axis=-1, keepdims=True)
    return xc * lax.rsqrt(var + LN_EPS)


def _silu(x):
    return x * jax.nn.sigmoid(x)


def _split3(x):
    hi = x.astype(BF16)
    r = x - hi.astype(F32)
    mid = r.astype(BF16)
    lo = (r - mid.astype(F32)).astype(BF16)
    return hi, mid, lo


def _ada_kernel(c_ref, w_ref, b_ref, o_ref):
    s = _silu(c_ref[...])
    s_hi = s.astype(BF16)
    s_lo = (s - s_hi.astype(F32)).astype(BF16)
    w = w_ref[...]
    w_hi = w.astype(BF16)
    w_lo = (w - w_hi.astype(F32)).astype(BF16)
    rows = s.shape[0]
    both = _dot(jnp.concatenate([s_hi, s_lo], axis=0), w_hi)
    o_ref[...] = both[:rows] + both[rows:] + _dot(s_hi, w_lo) + b_ref[...]


def _ada(cc, w_ada, b_ada):
    n = w_ada.shape[1]
    bn = ADA_BLOCK
    return pl.pallas_call(
        _ada_kernel,
        grid=(n // bn,),
        in_specs=[pl.BlockSpec((SUBLANES, D), lambda j: (0, 0)),
                  pl.BlockSpec((D, bn), lambda j: (0, j)),
                  pl.BlockSpec((1, bn), lambda j: (0, j))],
        out_specs=pl.BlockSpec((SUBLANES, bn), lambda j: (0, j)),
        out_shape=jax.ShapeDtypeStruct((SUBLANES, n), F32),
        compiler_params=_cparams(("arbitrary",)),
        name="ada",
    )(cc, w_ada, b_ada.reshape(1, n))


def _rope(v, cos, sin, low_half):
    fwd = pltpu.roll(v, LANES - ROPE_PAIR, axis=1)
    bwd = pltpu.roll(v, ROPE_PAIR, axis=1)
    return v * cos + jnp.where(low_half, fwd, bwd) * sin


def _proj_kernel(n_lat_tiles, x_ref, ctx_ref, sh_ref, sc_ref, cos_ref, sin_ref, wg_ref, wgate_ref, wqk_ref, wvt_ref,
                 qkv_ref, z_ref, gt_ref, dq_ref, dk_ref, dvt_ref):
    rows = jnp.where(pl.program_id(1) >= n_lat_tiles, ctx_ref[0], x_ref[0])
    h = (_normalize(rows) * (1.0 + sc_ref[0]) + sh_ref[0]).astype(BF16)
    q_all = _dot(h, wqk_ref[:, :D_QK])
    k_all = _dot(h, wqk_ref[:, D_QK:])
    qkv_ref[0] = _dot(h, wg_ref[:, :G_QKV])
    z_ref[0] = _dot(h, wg_ref[:, G_QKV:]).astype(z_ref.dtype)
    gt_ref[0] = _dot(h, wgate_ref[...])
    cos = cos_ref[...]
    sin = sin_ref[...]
    lane = lax.broadcasted_iota(jnp.int32, cos.shape, 1)
    low_half = (lane % (2 * ROPE_PAIR)) < ROPE_PAIR
    for j in range(D_QK // LANES):
        slab = slice(j * LANES, (j + 1) * LANES)
        dq_ref[0, :, slab] = (_rope(q_all[:, slab], cos, sin, low_half) * Q_SCALE).astype(BF16)
        dk_ref[0, :, slab] = _rope(k_all[:, slab], cos, sin, low_half).astype(BF16)
    dvt_ref[0] = _dot_nt(wvt_ref[...], h).astype(BF16)


def _proj(x, ctx, mod3, cos_t, sin_t, w_gdn, w_gate, w_qk, wv_t):
    bsz, seq, _ = x.shape
    lc = seq + ctx.shape[1]
    nt = lc // ROW_TILE
    n_lat_tiles = seq // ROW_TILE
    ctx_row = mod3.shape[0] - 1

    def mod_idx(col):
        return lambda b, i: (jnp.where(i >= n_lat_tiles, ctx_row, b), 0, col)

    def row_spec(width):
        return pl.BlockSpec((1, ROW_TILE, width), lambda b, i: (b, i, 0))

    outs = [(G_QKV, F32), (G_Z, BF16), (2 * LANES, F32), (D_QK, BF16), (D_QK, BF16)]
    return pl.pallas_call(
        functools.partial(_proj_kernel, n_lat_tiles),
        grid=(bsz, nt),
        in_specs=[pl.BlockSpec((1, ROW_TILE, D), lambda b, i: (b, jnp.minimum(i, n_lat_tiles - 1), 0)),
                  pl.BlockSpec((1, ROW_TILE, D), lambda b, i: (b, jnp.maximum(i - n_lat_tiles, 0), 0)),
                  pl.BlockSpec((1, 1, D), mod_idx(0)),
                  pl.BlockSpec((1, 1, D), mod_idx(1)),
                  pl.BlockSpec((ROW_TILE, LANES), lambda b, i: (i, 0)),
                  pl.BlockSpec((ROW_TILE, LANES), lambda b, i: (i, 0)),
                  pl.BlockSpec(w_gdn.shape, lambda b, i: (0, 0)),
                  pl.BlockSpec(w_gate.shape, lambda b, i: (0, 0)),
                  pl.BlockSpec(w_qk.shape, lambda b, i: (0, 0)),
                  pl.BlockSpec(wv_t.shape, lambda b, i: (0, 0))],
        out_specs=[row_spec(w) for w, _ in outs]
        + [pl.BlockSpec((1, DH * DDV, ROW_TILE), lambda b, i: (b, 0, i))],
        out_shape=[jax.ShapeDtypeStruct((bsz, lc, w), dt) for w, dt in outs]
        + [jax.ShapeDtypeStruct((bsz, DH * DDV, lc), BF16)],
        compiler_params=_cparams(("arbitrary", "arbitrary")),
        name="proj",
    )(x, ctx, mod3, mod3, cos_t, sin_t, w_gdn, w_gate, w_qk, wv_t)


def _prep_kernel(n_lat_tiles, nt, main_ref, prev_ref, next_ref, gt_ref, cw_ref, alog_ref, dtb_ref,
                 lo_ref, up_ref, q_ref, k_ref, v_ref, beta_ref, gc_ref, ext_ref):
    i = pl.program_id(1)
    halo = prev_ref.shape[1]
    has_prev = jnp.logical_and(i != 0, i != n_lat_tiles)
    has_next = jnp.logical_and(i != n_lat_tiles - 1, i != nt - 1)
    ext_ref[0:halo, :] = prev_ref[0] * has_prev.astype(F32)
    ext_ref[halo:halo + ROW_TILE, :] = main_ref[0]
    ext_ref[halo + ROW_TILE:, :] = next_ref[0] * has_next.astype(F32)
    pad = GCONV // 2
    acc = None
    ext = ext_ref[...]
    n_ext = ext.shape[0]
    for j in range(GCONV):
        shifted = ext if j == pad else pltpu.roll(ext, (pad - j) % n_ext, axis=0)
        term = shifted[halo:halo + ROW_TILE, :] * cw_ref[j:j + 1, :]
        acc = term if acc is None else acc + term
    qkv = _silu(acc)
    for h in range(GH):
        q = qkv[:, h * GDK:(h + 1) * GDK]
        q_ref[0, :, h * GDK:(h + 1) * GDK] = (q * (lax.rsqrt(jnp.sum(q * q, axis=-1, keepdims=True) + NORM_EPS)
                                                 * (GDK ** -0.5))).astype(q_ref.dtype)
        k = qkv[:, G_QK + h * GDK:G_QK + (h + 1) * GDK]
        k_ref[0, :, h * GDK:(h + 1) * GDK] = (
            k * lax.rsqrt(jnp.sum(k * k, axis=-1, keepdims=True) + NORM_EPS)).astype(k_ref.dtype)
    v_ref[0] = qkv[:, 2 * G_QK:].astype(v_ref.dtype)
    gt = gt_ref[0]
    beta_ref[0] = jax.nn.sigmoid(gt[:, :LANES])
    a = gt[:, LANES:] + dtb_ref[...]
    softplus = jnp.maximum(a, 0.0) + jnp.log(1.0 + jnp.exp(-jnp.abs(a)))
    g = -jnp.exp(alog_ref[...]) * softplus
    lo = lo_ref[...]
    up = up_ref[...]
    fwd = None
    bwd = None
    for part in _split3(g):
        f = _dot(lo, part)
        r = _dot(up, part)
        fwd = f if fwd is None else fwd + f
        bwd = r if bwd is None else bwd + r
    lane = lax.broadcasted_iota(jnp.int32, g.shape, 1)
    gc_ref[0] = jnp.where(lane < GH, fwd, bwd)


def _prep(qkv, gt, conv_w8, alog_v, dtb_v, tri_lo, tri_up, n_lat_tiles):
    bsz, lc, _ = qkv.shape
    nt = lc // ROW_TILE
    halo = SUBLANES
    per = ROW_TILE // halo
    last = lc // halo - 1

    def row_spec(width):
        return pl.BlockSpec((1, ROW_TILE, width), lambda b, i: (b, i, 0))

    def const_spec(shape):
        return pl.BlockSpec(shape, lambda b, i: tuple(0 for _ in shape))

    outs = [(G_QK, BF16), (G_QK, BF16), (GH * GDV, BF16), (LANES, F32), (LANES, F32)]
    return pl.pallas_call(
        functools.partial(_prep_kernel, n_lat_tiles, nt),
        grid=(bsz, nt),
        in_specs=[row_spec(G_QKV),
                  pl.BlockSpec((1, halo, G_QKV), lambda b, i: (b, jnp.maximum(i * per - 1, 0), 0)),
                  pl.BlockSpec((1, halo, G_QKV), lambda b, i: (b, jnp.minimum((i + 1) * per, last), 0)),
                  row_spec(2 * LANES),
                  const_spec((SUBLANES, G_QKV)), const_spec((1, LANES)), const_spec((1, LANES)),
                  const_spec((ROW_TILE, ROW_TILE)), const_spec((ROW_TILE, ROW_TILE))],
        out_specs=[row_spec(w) for w, _ in outs],
        out_shape=[jax.ShapeDtypeStruct((bsz, lc, w), dt) for w, dt in outs],
        scratch_shapes=[pltpu.VMEM((ROW_TILE + 2 * halo, G_QKV), F32)],
        compiler_params=_cparams(("arbitrary", "arbitrary")),
        name="prep",
    )(qkv, qkv, qkv, gt, conv_w8, alog_v, dtb_v, tri_lo, tri_up)


def _chunk_masks(n, rev):
    row = lax.broadcasted_iota(jnp.int32, (n, n), 0)
    col = lax.broadcasted_iota(jnp.int32, (n, n), 1)
    same = (row // CHUNK) == (col // CHUNK)
    incl = jnp.logical_and(same, (col >= row) if rev else (col <= row))
    strict = jnp.logical_and(same, (col > row) if rev else (col < row))
    return incl, strict


def _decay(gcol, grow, incl):
    return jnp.where(incl, jnp.exp(jnp.minimum(gcol - grow, 0.0)), 0.0)


def _chunk_kernel(k_ref, v_ref, beta_ref, gc_ref, uw_ref):
    n = k_ref.shape[1]
    row = lax.broadcasted_iota(jnp.int32, (n, n), 0)
    col = lax.broadcasted_iota(jnp.int32, (n, n), 1)
    eye = (row == col).astype(F32)
    sizes = [2 ** j for j in range(1, int(math.log2(CHUNK)))]
    same = {r: (row // r) == (col // r) for r in sizes + [CHUNK]}
    a_all, t_all, x_all, where = [], [], [], []
    masks = [_chunk_masks(n, rev) for rev in (False, True)]
    for bb in range(k_ref.shape[0]):
        gc = gc_ref[bb]
        gct = gc.T
        beta = beta_ref[bb]
        gram = [_dot_nt(k_ref[bb, :, h * GDK:(h + 1) * GDK], k_ref[bb, :, h * GDK:(h + 1) * GDK]) for h in range(GH)]
        for d in range(2):
            incl, strict = masks[d]
            for h in range(GH):
                c = d * GH + h
                k = k_ref[bb, :, h * GDK:(h + 1) * GDK].astype(F32)
                v = v_ref[bb, :, h * GDV:(h + 1) * GDV].astype(F32)
                bcol = beta[:, c:c + 1]
                gcol = gc[:, c:c + 1]
                grow = gct[c:c + 1, :]
                kb = k * bcol
                a = jnp.where(strict, gram[h] * bcol * _decay(gcol, grow, incl), 0.0)
                a_all.append(a.astype(BF16))
                t_all.append((eye - jnp.where(same[2], a, 0.0)).astype(BF16))
                x_all.append(jnp.concatenate([v * bcol, kb * jnp.exp(gcol)], axis=1).astype(BF16))
                where.append((bb, c))
    n_prob = len(a_all)
    zero = jnp.zeros((n, n), BF16)
    for r in sizes:
        off_mask = jnp.logical_and(same[2 * r], jnp.logical_not(same[r]))
        inner = [_dot(jnp.where(off_mask, a_all[c], zero), t_all[c]).astype(BF16) for c in range(n_prob)]
        t_all = [t_all[c] - _dot(t_all[c], inner[c]).astype(BF16) for c in range(n_prob)]
    for p in range(n_prob):
        bb, c = where[p]
        uw_ref[bb, c] = _dot(t_all[p], x_all[p]).astype(uw_ref.dtype)


def _chunk(k, v, beta, gc):
    bsz, lc, _ = k.shape
    nt = lc // ROW_TILE
    nb = CHUNK_BATCH if bsz % CHUNK_BATCH == 0 else 1

    def row_spec(width):
        return pl.BlockSpec((nb, ROW_TILE, width), lambda b, i: (b, i, 0))

    return pl.pallas_call(
        _chunk_kernel,
        grid=(bsz // nb, nt),
        in_specs=[row_spec(G_QK), row_spec(GH * GDV), row_spec(LANES), row_spec(LANES)],
        out_specs=pl.BlockSpec((nb, 2 * GH, ROW_TILE, GDV + GDK), lambda b, i: (b, 0, i, 0)),
        out_shape=jax.ShapeDtypeStruct((bsz, 2 * GH, lc, GDV + GDK), BF16),
        compiler_params=_cparams(("arbitrary", "arbitrary")),
        name="chunk",
    )(k, v, beta, gc)


def _scan_kernel(qf_ref, kf_ref, gf_ref, uwf_ref, qr_ref, kr_ref, gr_ref, uwr_ref,
                 of_ref, or_ref, s_ref):
    @pl.when(pl.program_id(1) == 0)
    def _():
        s_ref[...] = jnp.zeros_like(s_ref)

    n_chunks = ROW_TILE // CHUNK
    n_batch = qf_ref.shape[0]
    row = lax.broadcasted_iota(jnp.int32, (CHUNK, CHUNK), 0)
    col = lax.broadcasted_iota(jnp.int32, (CHUNK, CHUNK), 1)
    dirs = ((qf_ref, kf_ref, gf_ref, uwf_ref, of_ref), (qr_ref, kr_ref, gr_ref, uwr_ref, or_ref))
    chains = [(bb, d, h) for bb in range(n_batch) for d in range(2) for h in range(GH)]
    state = [s_ref[i] for i in range(len(chains))]

    def chunk_rows(step, d):
        ci = n_chunks - 1 - step if d == 1 else step
        return slice(ci * CHUNK, (ci + 1) * CHUNK)

    pre = []
    for step in range(n_chunks):
        gates = {}
        for bb in range(n_batch):
            for d in range(2):
                gc = dirs[d][2][bb, chunk_rows(step, d), :]
                gates[bb, d] = (gc, gc.T)
        per_chain = []
        for bb, d, h in chains:
            q_ref, k_ref, _, uw_ref, _ = dirs[d]
            rev = d == 1
            incl = (col >= row) if rev else (col <= row)
            rows = chunk_rows(step, d)
            gc, gct = gates[bb, d]
            c = d * GH + h
            q_b = q_ref[bb, rows, h * GDK:(h + 1) * GDK]
            k_b = k_ref[bb, rows, h * GDK:(h + 1) * GDK]
            q = q_b.astype(F32)
            k = k_b.astype(F32)
            gcol = gc[:, c:c + 1]
            grow = gct[c:c + 1, :]
            last = 0 if rev else CHUNK - 1
            g_last = gcol[last:last + 1, :]
            a_qk = (_dot_nt(q_b, k_b) * _decay(gcol, grow, incl)).astype(BF16)
            wq = jnp.concatenate([uw_ref[bb, h, rows, GDV:], (q * jnp.exp(gcol)).astype(BF16)], axis=0)
            k_tail_t = (k * jnp.exp(g_last - gcol)).T.astype(BF16)
            per_chain.append((a_qk, wq, k_tail_t, jnp.exp(g_last)))
        pre.append(per_chain)

    for step in range(n_chunks):
        ws = [_dot(pre[step][i][1], state[i].astype(BF16)) for i in range(len(chains))]
        v_new = [(dirs[d][3][bb, h, chunk_rows(step, d), :GDV].astype(F32) - ws[i][:CHUNK]).astype(BF16)
                 for i, (bb, d, h) in enumerate(chains)]
        for i, (bb, d, h) in enumerate(chains):
            a_qk, _, k_tail_t, decay_last = pre[step][i]
            dirs[d][4][bb, chunk_rows(step, d), h * GDV:(h + 1) * GDV] = (
                ws[i][CHUNK:] + _dot(a_qk, v_new[i])).astype(BF16)
            state[i] = state[i] * decay_last + _dot(k_tail_t, v_new[i])
    for i in range(len(chains)):
        s_ref[i] = state[i]


def _scan(q, k, gc, uw, n_lat_tiles):
    bsz, lc, _ = q.shape
    nt = lc // ROW_TILE
    n_ctx_tiles = nt - n_lat_tiles

    def fwd(b, i):
        return jnp.where(i < n_ctx_tiles, n_lat_tiles + i, i - n_ctx_tiles)

    def bwd(b, i):
        return jnp.where(i < n_ctx_tiles, nt - 1 - i, n_lat_tiles - 1 - (i - n_ctx_tiles))

    nb = SCAN_BATCH if bsz % SCAN_BATCH == 0 else 1

    def specs(tile):
        return [pl.BlockSpec((nb, ROW_TILE, G_QK), lambda b, i: (b, tile(b, i), 0)),
                pl.BlockSpec((nb, ROW_TILE, G_QK), lambda b, i: (b, tile(b, i), 0)),
                pl.BlockSpec((nb, ROW_TILE, LANES), lambda b, i: (b, tile(b, i), 0))]

    def uw_spec(tile, d):
        return pl.BlockSpec((nb, GH, ROW_TILE, GDV + GDK), lambda b, i: (b, d, tile(b, i), 0))

    def out_spec(tile):
        return pl.BlockSpec((nb, ROW_TILE, GH * GDV), lambda b, i: (b, tile(b, i), 0))

    return pl.pallas_call(
        _scan_kernel,
        grid=(bsz // nb, nt),
        in_specs=specs(fwd) + [uw_spec(fwd, 0)] + specs(bwd) + [uw_spec(bwd, 1)],
        out_specs=[out_spec(fwd), out_spec(bwd)],
        out_shape=[jax.ShapeDtypeStruct((bsz, lc, GH * GDV), BF16)] * 2,
        scratch_shapes=[pltpu.VMEM((nb * 2 * GH, GDK, GDV), F32)],
        compiler_params=_cparams(("arbitrary", "arbitrary")),
        name="scan",
    )(q, k, gc, uw, q, k, gc, uw)


def _attn_kernel(lam_init, q_ref, k_ref, vt_ref, lam_ref, nw_ref, o_ref):
    lam = lam_ref[...]
    lam_full = (jnp.exp(jnp.sum(lam[0:1] * lam[1:2], axis=-1, keepdims=True))
                - jnp.exp(jnp.sum(lam[2:3] * lam[3:4], axis=-1, keepdims=True)) + lam_init)
    q = q_ref[0]
    tq = q.shape[0]
    n_keys = k_ref.shape[1]
    lane = lax.broadcasted_iota(jnp.int32, q.shape, 1)
    zero = jnp.zeros_like(q)
    qc = [jnp.where(lane < DDK, q, zero), jnp.where(lane >= DDK, q, zero)]
    m = [jnp.full((1, tq), -jnp.inf, F32) for _ in range(2)]
    l = [jnp.zeros((1, tq), F32) for _ in range(2)]
    acc = [jnp.zeros((DDV, tq), F32) for _ in range(2)]
    n_tiles = n_keys // KEY_TILE
    bounds = [t * KEY_TILE for t in range(n_tiles)] + [n_keys]

    def scores(pair):
        t, c = pair
        s = _dot_nt(k_ref[0, bounds[t]:bounds[t + 1], :], qc[c])
        return s, jnp.max(s, axis=0, keepdims=True)

    pairs = [(t, c) for t in range(n_tiles) for c in range(2)]
    pending = [scores(pr) for pr in pairs[:SCORE_LOOKAHEAD]]
    for idx, (t, c) in enumerate(pairs):
        s, s_max = pending.pop(0)
        if idx + SCORE_LOOKAHEAD < len(pairs):
            pending.append(scores(pairs[idx + SCORE_LOOKAHEAD]))
        width = bounds[t + 1] - bounds[t]
        vt_t = jnp.concatenate([vt_ref[0, :, bounds[t]:bounds[t + 1]], jnp.ones((BF16_ROWS, width), BF16)], axis=0)
        m_new = jnp.maximum(m[c], s_max)
        p = jnp.exp2(s - m_new)
        scale = jnp.exp2(m[c] - m_new)
        pv = _dot(vt_t, p.astype(BF16))
        l[c] = scale * l[c] + pv[DDV:DDV + 1]
        acc[c] = scale * acc[c] + pv[:DDV]
        m[c] = m_new
    o = acc[0] / l[0] - lam_full * (acc[1] / l[1])
    y = o * lax.rsqrt(jnp.mean(o * o, axis=0, keepdims=True) + NORM_EPS)
    o_ref[0] = (y.T * nw_ref[...] * (1.0 - lam_init)).astype(o_ref.dtype)


def _attn(dq, dk, dvt, lam, norm_w, lam_init, lq, tq):
    bsz, lc, _ = dq.shape
    return pl.pallas_call(
        functools.partial(_attn_kernel, lam_init),
        grid=(bsz, DH, lq // tq),
        in_specs=[pl.BlockSpec((1, tq, 2 * DDK), lambda b, h, i: (b, i, h)),
                  pl.BlockSpec((1, lc, 2 * DDK), lambda b, h, i: (b, 0, h)),
                  pl.BlockSpec((1, DDV, lc), lambda b, h, i: (b, h, 0)),
                  pl.BlockSpec((4, DDK), lambda b, h, i: (0, 0)),
                  pl.BlockSpec((1, DDV), lambda b, h, i: (0, 0))],
        out_specs=pl.BlockSpec((1, tq, DDV), lambda b, h, i: (b, i, h)),
        out_shape=jax.ShapeDtypeStruct((bsz, lq, DH * DDV), BF16),
        compiler_params=_cparams(("arbitrary", "arbitrary", "arbitrary")),
        name="attn",
    )(dq, dk, dvt, lam, norm_w.reshape(1, DDV))


def _out_kernel(alpha, of_ref, or_ref, z_ref, yd_ref, x_ref, g1_ref, sh2_ref, sc2_ref, gnw_ref,
                wo_ref, lng_ref, lnb_ref, wr_ref, br_ref, x1_ref, h2_ref, lg_ref):
    o = of_ref[0].astype(F32) + or_ref[0].astype(F32)
    z = z_ref[0].astype(F32)
    heads = []
    for h in range(GH):
        oh = o[:, h * GDV:(h + 1) * GDV]
        heads.append((oh * lax.rsqrt(jnp.mean(oh * oh, axis=-1, keepdims=True) + NORM_EPS) * gnw_ref[...]
                      * _silu(z[:, h * GDV:(h + 1) * GDV])).astype(BF16))
    mix = jnp.concatenate(heads + [yd_ref[0]], axis=1)
    x1 = _normalize(alpha * x_ref[0] + g1_ref[0] * _dot(mix, wo_ref[...])) * lng_ref[...] + lnb_ref[...]
    x1_ref[0] = x1
    h2 = _normalize(x1) * (1.0 + sc2_ref[0]) + sh2_ref[0]
    h2_ref[0] = h2
    h_hi = h2.astype(BF16)
    h_lo = (h2 - h_hi.astype(F32)).astype(BF16)
    both = _dot(h_hi, wr_ref[...])
    lg_ref[0] = br_ref[...] + both[:, :LANES] + both[:, LANES:] + _dot(h_lo, wr_ref[:, :LANES])


def _out(o_f, o_r, z, y_diff, xc, mod3, gdn_norm_w, w_out, ln_g, ln_b, w_router3, b_router, alpha):
    bsz, lq, _ = y_diff.shape
    tile = OUT_TILE if lq % OUT_TILE == 0 else ROW_TILE
    nt = lq // tile

    def row_spec(width):
        return pl.BlockSpec((1, tile, width), lambda b, i: (b, i, 0))

    def mod_spec(col):
        return pl.BlockSpec((1, 1, D), lambda b, i: (b, 0, col))

    def const_spec(shape):
        return pl.BlockSpec(shape, lambda b, i: tuple(0 for _ in shape))

    return pl.pallas_call(
        functools.partial(_out_kernel, alpha),
        grid=(bsz, nt),
        in_specs=[row_spec(GH * GDV), row_spec(GH * GDV), row_spec(G_Z), row_spec(DH * DDV), row_spec(D),
                  mod_spec(2), mod_spec(3), mod_spec(4),
                  const_spec((1, GDV)), const_spec((G_Z + DH * DDV, D)),
                  const_spec((1, D)), const_spec((1, D)),
                  const_spec((D, 2 * LANES)), const_spec((1, LANES))],
        out_specs=[row_spec(D), row_spec(D), row_spec(LANES)],
        out_shape=[jax.ShapeDtypeStruct((bsz, lq, D), F32), jax.ShapeDtypeStruct((bsz, lq, D), F32),
                   jax.ShapeDtypeStruct((bsz, lq, LANES), F32)],
        compiler_params=_cparams(("arbitrary", "arbitrary")),
        name="out",
    )(o_f, o_r, z, y_diff, xc, mod3, mod3, mod3, gdn_norm_w.reshape(1, GDV), w_out,
      ln_g.reshape(1, D), ln_b.reshape(1, D), w_router3, b_router)


def _route_kernel(lg_ref, tri_ref, ids_ref, gate_ref, cnt_ref, run_ref):
    @pl.when(pl.program_id(0) == 0)
    def _():
        run_ref[...] = jnp.zeros_like(run_ref)

    lgt = lg_ref[...].T[:ROUTE_ROWS]
    n = lgt.shape[1]
    row_i = lax.broadcasted_iota(jnp.int32, lgt.shape, 0)
    row = row_i.astype(F32)
    neg = jnp.float32(-jnp.inf)

    def first_max(mask):
        masked = jnp.where(mask, lgt, neg)
        m = jnp.max(masked, axis=0, keepdims=True)
        idx = jnp.min(jnp.where(jnp.logical_and(mask, masked == m), row, float(LANES)), axis=0, keepdims=True)
        return m, idx

    is_group = row_i < N_GROUPS
    m_g, grp = first_max(is_group)
    p_g = 1.0 / jnp.sum(jnp.where(is_group, jnp.exp(lgt - m_g), 0.0), axis=0, keepdims=True)
    e_row = row - float(N_GROUPS)
    lo_e = grp * float(E_PER_GROUP)
    in_group = jnp.logical_and(e_row >= lo_e, e_row < lo_e + float(E_PER_GROUP))
    l0, i0 = first_max(in_group)
    l1, i1 = first_max(jnp.logical_and(in_group, row != i0))
    r = jnp.exp(l1 - l0)
    gate0 = p_g / (1.0 + r)
    gate1 = p_g * r / (1.0 + r)
    e0 = i0 - float(N_GROUPS)
    e1 = i1 - float(N_GROUPS)
    oh0 = (e_row == e0).astype(F32)
    oh1 = (e_row == e1).astype(F32)
    both = oh0 + oh1
    before = _dot(both.astype(BF16), tri_ref[...]) + run_ref[...]
    rank0 = jnp.sum(oh0 * before, axis=0, keepdims=True)
    rank1 = jnp.sum(oh1 * before, axis=0, keepdims=True)
    run_ref[...] = run_ref[...] + jnp.sum(both, axis=1, keepdims=True)
    cnt_ref[...] = jnp.broadcast_to(run_ref[...], cnt_ref.shape)
    ids_ref[...] = jnp.concatenate([e0, e1, rank0, rank1, jnp.zeros((SUBLANES - 4, n), F32)],
                                   axis=0).astype(jnp.int32)
    gate_ref[...] = jnp.concatenate([gate0, gate1, jnp.zeros((LANES - 2, n), F32)], axis=0).T


def _route(logits, tri_strict):
    t = logits.shape[0]
    tile = tri_strict.shape[0]
    return pl.pallas_call(
        _route_kernel,
        grid=(t // tile,),
        in_specs=[pl.BlockSpec((tile, LANES), lambda i: (i, 0)),
                  pl.BlockSpec((tile, tile), lambda i: (0, 0))],
        out_specs=[pl.BlockSpec((SUBLANES, tile), lambda i: (0, i)),
                   pl.BlockSpec((tile, LANES), lambda i: (i, 0)),
                   pl.BlockSpec((ROUTE_ROWS, LANES), lambda i: (0, 0))],
        out_shape=[jax.ShapeDtypeStruct((SUBLANES, t), jnp.int32), jax.ShapeDtypeStruct((t, LANES), F32),
                   jax.ShapeDtypeStruct((ROUTE_ROWS, LANES), F32)],
        scratch_shapes=[pltpu.VMEM((ROUTE_ROWS, 1), F32)],
        compiler_params=_cparams(("arbitrary",)),
        name="route",
    )(logits, tri_strict)


def _dispatch_kernel(tail_ref, used_ref, dest_ref, h_hbm, buf_hbm, zero_ref, src_ref, load_sem, row_sem, zsem):
    i = pl.program_id(0)
    n_tiles = pl.num_programs(0)
    n_slots = src_ref.shape[0]
    groups = DMA_TOKENS // SUBLANES

    def load(tile):
        slot = lax.rem(tile, n_slots)
        return pltpu.make_async_copy(h_hbm.at[pl.ds(tile * groups, groups)], src_ref.at[slot], load_sem.at[slot])

    def wait_rows(tile):
        slot = lax.rem(tile, n_slots)
        for _ in range(2):
            pltpu.make_async_copy(buf_hbm.at[pl.ds(0, DMA_TOKENS)], buf_hbm.at[pl.ds(0, DMA_TOKENS)],
                                  row_sem.at[slot]).wait()

    @pl.when(i == 0)
    def _():
        load(0).start()
        zero_ref[...] = jnp.zeros_like(zero_ref)
        n_blocks = buf_hbm.shape[0] // FFN_BLOCK

        def block_copy(b):
            return pltpu.make_async_copy(zero_ref, buf_hbm.at[pl.ds(b * FFN_BLOCK, FFN_BLOCK)], zsem)

        for e in range(N_EXPERTS):
            @pl.when(tail_ref[e] >= 0)
            def _():
                block_copy(tail_ref[e]).start()
        lax.fori_loop(used_ref[0], n_blocks, lambda b, carry: (block_copy(b).start(), carry)[1], 0)
        for e in range(N_EXPERTS):
            @pl.when(tail_ref[e] >= 0)
            def _():
                block_copy(tail_ref[e]).wait()
        lax.fori_loop(used_ref[0], n_blocks, lambda b, carry: (block_copy(b).wait(), carry)[1], 0)

    @pl.when(i + 1 < n_tiles)
    def _():
        load(i + 1).start()

    load(i).wait()
    slot = lax.rem(i, n_slots)

    def start(grp, carry):
        for u in range(SUBLANES):
            for j in range(2):
                dst = dest_ref[j * DMA_TOKENS + SUBLANES * grp + u]
                pltpu.make_async_copy(src_ref.at[slot, grp, pl.ds(u, 1)], buf_hbm.at[pl.ds(dst, 1)],
                                      row_sem.at[slot]).start()
        return carry

    lax.fori_loop(0, groups, start, 0)

    @pl.when(i > 0)
    def _():
        wait_rows(i - 1)

    @pl.when(i == n_tiles - 1)
    def _():
        wait_rows(i)


def _dispatch(tail_block, n_used, dest_flat, h2, n_rows):
    t, d = h2.shape
    grid_spec = pltpu.PrefetchScalarGridSpec(
        num_scalar_prefetch=2,
        grid=(t // DMA_TOKENS,),
        in_specs=[pl.BlockSpec((2 * DMA_TOKENS,), lambda i, tail, used: (i,), memory_space=pltpu.SMEM),
                  pl.BlockSpec(memory_space=pl.ANY)],
        out_specs=pl.BlockSpec(memory_space=pl.ANY),
        scratch_shapes=[pltpu.VMEM((FFN_BLOCK, d), h2.dtype),
                        pltpu.VMEM((3, DMA_TOKENS // SUBLANES, SUBLANES, d), h2.dtype),
                        pltpu.SemaphoreType.DMA((3,)), pltpu.SemaphoreType.DMA((3,)),
                        pltpu.SemaphoreType.DMA(())],
    )
    return pl.pallas_call(
        _dispatch_kernel,
        grid_spec=grid_spec,
        out_shape=jax.ShapeDtypeStruct((n_rows, d), h2.dtype),
        compiler_params=_cparams(("arbitrary",)),
        name="dispatch",
    )(tail_block, n_used, dest_flat, h2.reshape(t // SUBLANES, SUBLANES, d))


def _combine_kernel(alpha, dest_ref, ys_hbm, gate_ref, x1_ref, g2_ref, lng_ref, lnb_ref, o_ref, rows_ref, sem):
    i = pl.program_id(0)
    n_tiles = pl.num_programs(0) - COMBINE_LAG
    n = x1_ref.shape[0]
    n_slots = rows_ref.shape[0]
    slot_in = lax.rem(i, n_slots)
    slot_out = lax.rem(i + n_slots - COMBINE_LAG, n_slots)

    def start_group(grp):
        for u in range(SUBLANES):
            for j in range(2):
                src = dest_ref[j * DMA_TOKENS + SUBLANES * grp + u]
                pltpu.make_async_copy(ys_hbm.at[pl.ds(src, 1)], rows_ref.at[slot_in, j, grp, pl.ds(u, 1)],
                                      sem.at[slot_in]).start()

    def finish():
        gate = gate_ref[...]
        r0 = rows_ref[slot_out, 0].reshape(n, -1)
        r1 = rows_ref[slot_out, 1].reshape(n, -1)
        y = gate[:, 0:1] * r0 + gate[:, 1:2] * r1
        o_ref[...] = _normalize(alpha * x1_ref[...] + g2_ref[0] * y) * lng_ref[...] + lnb_ref[...]

    @pl.when(i >= COMBINE_LAG)
    def _():
        for j in range(2):
            pltpu.make_async_copy(ys_hbm.at[pl.ds(0, n)], ys_hbm.at[pl.ds(0, n)], sem.at[slot_out]).wait()

    @pl.when(i < COMBINE_LAG)
    def _():
        lax.fori_loop(0, n // SUBLANES, lambda grp, carry: (start_group(grp), carry)[1], 0)

    @pl.when(jnp.logical_and(i >= COMBINE_LAG, i < n_tiles))
    def _():
        for grp in range(n // SUBLANES):
            start_group(grp)
        finish()

    @pl.when(i >= n_tiles)
    def _():
        finish()


def _combine(dest_flat, ys, gate, x1, mod3, ln_g, ln_b, alpha, rows_per_batch):
    t, d = x1.shape
    per_batch = rows_per_batch // DMA_TOKENS
    n_tiles = t // DMA_TOKENS

    def done(i):
        return jnp.maximum(i - COMBINE_LAG, 0)

    return pl.pallas_call(
        functools.partial(_combine_kernel, alpha),
        grid=(n_tiles + COMBINE_LAG,),
        in_specs=[pl.BlockSpec((2 * DMA_TOKENS,), lambda i: (jnp.minimum(i, n_tiles - 1),), memory_space=pltpu.SMEM),
                  pl.BlockSpec(memory_space=pl.ANY),
                  pl.BlockSpec((DMA_TOKENS, LANES), lambda i: (done(i), 0)),
                  pl.BlockSpec((DMA_TOKENS, d), lambda i: (done(i), 0)),
                  pl.BlockSpec((1, 1, d), lambda i: (done(i) // per_batch, 0, 5)),
                  pl.BlockSpec((1, d), lambda i: (0, 0)),
                  pl.BlockSpec((1, d), lambda i: (0, 0))],
        out_specs=pl.BlockSpec((DMA_TOKENS, d), lambda i: (done(i), 0)),
        out_shape=jax.ShapeDtypeStruct((t, d), F32),
        scratch_shapes=[pltpu.VMEM((COMBINE_LAG + 1, 2, DMA_TOKENS // SUBLANES, SUBLANES, d), F32),
                        pltpu.SemaphoreType.DMA((COMBINE_LAG + 1,))],
        compiler_params=_cparams(("arbitrary",)),
        name="combine",
    )(dest_flat, ys, gate, x1, mod3, ln_g.reshape(1, d), ln_b.reshape(1, d))


def _ffn_kernel(start_ref, nblk_ref, used_ref, buf_hbm, wg_ref, wu_ref, wd_ref, ys_hbm,
                wgb_ref, wub_ref, wdb_ref, x_buf, o_buf, in_sem, out_sem):
    e = pl.program_id(0)
    nb = nblk_ref[e]
    base = start_ref[e]
    used = used_ref[0]
    n_in = x_buf.shape[0]
    n_out = o_buf.shape[0]
    ahead = n_in - 1

    def rows(g):
        return pl.ds(g * FFN_BLOCK, FFN_BLOCK)

    def in_copy(g):
        slot = lax.rem(g, n_in)
        return pltpu.make_async_copy(buf_hbm.at[rows(g)], x_buf.at[slot], in_sem.at[slot])

    def out_copy(g):
        slot = lax.rem(g, n_out)
        return pltpu.make_async_copy(o_buf.at[slot], ys_hbm.at[rows(g)], out_sem.at[slot])

    @pl.when(e == 0)
    def _():
        for g in range(ahead):
            @pl.when(g < used)
            def _():
                in_copy(g).start()

    @pl.when(nb > 0)
    def _():
        wgb_ref[...] = wg_ref[0].astype(BF16)
        wub_ref[...] = wu_ref[0].astype(BF16)
        wdb_ref[...] = wd_ref[0].astype(BF16)

        def block(j, carry):
            g = base + j
            in_copy(g).wait()

            @pl.when(g + ahead < used)
            def _():
                in_copy(g + ahead).start()

            @pl.when(g >= n_out)
            def _():
                out_copy(g - n_out).wait()

            islot = lax.rem(g, n_in)
            oslot = lax.rem(g, n_out)
            slab = FFN_BLOCK // FFN_SLABS
            xs = [x_buf[islot, s * slab:(s + 1) * slab, :].astype(BF16) for s in range(FFN_SLABS)]
            gates = [_dot(x, wgb_ref[...]) for x in xs]
            ups = [_dot(x, wub_ref[...]) for x in xs]
            hidden = [(_silu(gt) * up).astype(BF16) for gt, up in zip(gates, ups)]
            for s in range(FFN_SLABS):
                o_buf[oslot, s * slab:(s + 1) * slab, :] = _dot(hidden[s], wdb_ref[...])
            out_copy(g).start()
            return carry

        lax.fori_loop(0, nb, block, 0)

    @pl.when(e == pl.num_programs(0) - 1)
    def _():
        for back in range(n_out):
            @pl.when(used - 1 - back >= 0)
            def _():
                out_copy(used - 1 - back).wait()
        n_blocks = ys_hbm.shape[0] // FFN_BLOCK
        o_buf[0] = jnp.zeros(o_buf.shape[1:], o_buf.dtype)

        def clear(b):
            return pltpu.make_async_copy(o_buf.at[0], ys_hbm.at[pl.ds(b * FFN_BLOCK, FFN_BLOCK)], out_sem.at[0])

        lax.fori_loop(used_ref[0], n_blocks, lambda b, carry: (clear(b).start(), carry)[1], 0)
        lax.fori_loop(used_ref[0], n_blocks, lambda b, carry: (clear(b).wait(), carry)[1], 0)


def _ffn(seg_start, seg_blocks, n_used, buf, w_gate, w_up, w_down):
    n_rows, d = buf.shape
    n_experts = w_gate.shape[0]
    grid_spec = pltpu.PrefetchScalarGridSpec(
        num_scalar_prefetch=3,
        grid=(n_experts,),
        in_specs=[pl.BlockSpec(memory_space=pl.ANY),
                  pl.BlockSpec((1, d, D_EXPERT), lambda e, *_: (e, 0, 0)),
                  pl.BlockSpec((1, d, D_EXPERT), lambda e, *_: (e, 0, 0)),
                  pl.BlockSpec((1, D_EXPERT, d), lambda e, *_: (e, 0, 0))],
        out_specs=pl.BlockSpec(memory_space=pl.ANY),
        scratch_shapes=[pltpu.VMEM((d, D_EXPERT), BF16), pltpu.VMEM((d, D_EXPERT), BF16),
                        pltpu.VMEM((D_EXPERT, d), BF16),
                        pltpu.VMEM((FFN_IN_SLOTS, FFN_BLOCK, d), F32), pltpu.VMEM((FFN_OUT_SLOTS, FFN_BLOCK, d), F32),
                        pltpu.SemaphoreType.DMA((FFN_IN_SLOTS,)), pltpu.SemaphoreType.DMA((FFN_OUT_SLOTS,))],
    )
    return pl.pallas_call(
        _ffn_kernel,
        grid_spec=grid_spec,
        out_shape=jax.ShapeDtypeStruct((n_rows, d), F32),
        compiler_params=_cparams(("arbitrary",)),
        name="ffn",
    )(seg_start, seg_blocks, n_used, buf, w_gate, w_up, w_down)


def _rope_tables(seq_len, n_ctx_rows):
    lane = np.arange(LANES)
    within = lane % DDK
    freq = within % ROPE_PAIR
    use_col = within >= DDK // 2
    sign = np.where((lane % (2 * ROPE_PAIR)) < ROPE_PAIR, -1.0, 1.0).astype(np.float32)
    half = DDK // 2
    inv_freq = (ROPE_THETA ** (-np.arange(0, half, 2, dtype=np.float32) / np.float32(half))).astype(np.float32)
    t = np.arange(seq_len)
    pos = np.where(use_col[None, :], (t % GRID_W)[:, None], (t // GRID_W)[:, None]).astype(np.float32)
    ang = pos * inv_freq[freq][None, :]
    cos = np.concatenate([np.cos(ang), np.ones((n_ctx_rows, LANES), np.float32)], axis=0)
    sin = np.concatenate([np.sin(ang) * sign[None, :], np.zeros((n_ctx_rows, LANES), np.float32)], axis=0)
    return jnp.asarray(cos, F32), jnp.asarray(sin, F32)


def _block_tri(n, block, upper):
    i = np.arange(n)
    same = (i[:, None] // block) == (i[None, :] // block)
    tri = (i[None, :] >= i[:, None]) if upper else (i[None, :] <= i[:, None])
    return jnp.asarray((same & tri).astype(np.float32), dtype=BF16)


def _pack_w_in(w_in):
    off = G_QKV + G_Z
    v_start = G_COLS + 2 * D_QK
    pad = jnp.zeros((D, LANES - G_GATES), w_in.dtype)
    w_gate = jnp.concatenate([w_in[:, off:off + G_GATES], pad, w_in[:, off + G_GATES:G_COLS], pad], axis=1)
    return (w_in[:, :off].astype(BF16), w_gate.astype(BF16), w_in[:, G_COLS:v_start].astype(BF16),
            w_in[:, v_start:].T.astype(BF16))


def kernel(x, c, ctx, c_ctx, w_ada, b_ada, w_in, conv_w, gdn_a_log, gdn_dt_bias, gdn_norm_w, diff_lambda,
           diff_norm_w, w_out, ln1_g, ln1_b, w_router_group, b_router_group, w_router_expert, b_router_expert,
           w_expert_gate, w_expert_up, w_expert_down, ln2_g, ln2_b):
    depth = w_ada.shape[0]
    assert depth == 1, "single-layer block: the context stream never feeds a later layer"
    bsz, seq, _ = x.shape
    n_ctx = ctx.shape[1]
    assert seq % ROW_TILE == 0 and n_ctx % ROW_TILE == 0 and seq % GRID_W == 0 and bsz < SUBLANES
    assert (bsz * seq) % ROUTE_TILE == 0 and seq % DMA_TOKENS == 0
    n_lat_tiles = seq // ROW_TILE
    alpha = (2.0 * depth) ** 0.25
    lam_init = 0.8 - 0.6 * math.exp(-0.3 * 0)
    i = 0

    cc = jnp.zeros((SUBLANES, D), F32).at[:bsz].set(c).at[SUBLANES - 1].set(c_ctx)
    mod = _ada(cc, w_ada[i], b_ada[i])
    mod3 = mod.reshape(SUBLANES, 1, 6 * D)

    cos_t, sin_t = _rope_tables(seq, n_ctx)
    qkv, z, gt, dq, dk, dvt = _proj(x, ctx, mod3, cos_t, sin_t, *_pack_w_in(w_in[i]))

    conv_w8 = jnp.zeros((SUBLANES, G_QKV), F32).at[:GCONV].set(conv_w[i])
    alog_v = jnp.zeros((1, LANES), F32).at[0, :G_GATES].set(gdn_a_log[i].reshape(-1))
    dtb_v = jnp.zeros((1, LANES), F32).at[0, :G_GATES].set(gdn_dt_bias[i].reshape(-1))
    gq, gk, gv, beta, gc = _prep(qkv, gt, conv_w8, alog_v, dtb_v,
                                 _block_tri(ROW_TILE, CHUNK, False), _block_tri(ROW_TILE, CHUNK, True), n_lat_tiles)
    uw = _chunk(gk, gv, beta, gc)
    o_f, o_r = _scan(gq, gk, gc, uw, n_lat_tiles)

    y_diff = _attn(dq, dk, dvt, diff_lambda[i], diff_norm_w[i], lam_init, seq, tq=min(ATTN_Q_TILE, seq))

    w_router = jnp.zeros((D, LANES), F32).at[:, :N_GROUPS].set(w_router_group[i]) \
        .at[:, N_GROUPS:N_GROUPS + N_EXPERTS].set(w_router_expert[i])
    w_hi = w_router.astype(BF16)
    w_router3 = jnp.concatenate([w_hi, (w_router - w_hi.astype(F32)).astype(BF16)], axis=1)
    b_router = jnp.zeros((1, LANES), F32).at[0, :N_GROUPS].set(b_router_group[i]) \
        .at[0, N_GROUPS:N_GROUPS + N_EXPERTS].set(b_router_expert[i])
    x1, h2, logits = _out(o_f, o_r, z, y_diff, x, mod3, gdn_norm_w[i], w_out[i].astype(BF16), ln1_g[i], ln1_b[i],
                          w_router3, b_router, alpha)

    t = bsz * seq
    tri_strict = jnp.asarray(np.triu(np.ones((ROUTE_TILE, ROUTE_TILE), np.float32), 1), dtype=BF16)
    ids, gate, counts = _route(logits.reshape(t, LANES), tri_strict)

    cnt = counts[N_GROUPS:N_GROUPS + N_EXPERTS, 0].astype(jnp.int32)
    padded = (cnt + FFN_BLOCK - 1) // FFN_BLOCK * FFN_BLOCK
    p_end = jnp.cumsum(padded)
    p_start = p_end - padded
    n_blocks = (2 * t) // FFN_BLOCK + N_EXPERTS
    n_used = (p_end[-1:] // FFN_BLOCK).astype(jnp.int32)
    tail_block = jnp.where(cnt > 0, p_end // FFN_BLOCK - 1, -1).astype(jnp.int32)
    one_hot = ids[0:2, :, None] == jnp.arange(N_EXPERTS, dtype=jnp.int32)
    row_start = jnp.sum(jnp.where(one_hot, p_start, 0), axis=-1)
    dest = jnp.swapaxes((row_start + ids[2:4]).reshape(2, t // DMA_TOKENS, DMA_TOKENS), 0, 1)
    dest = dest.reshape(-1).astype(jnp.int32)

    buf = _dispatch(tail_block, n_used, dest, h2.reshape(t, D), n_blocks * FFN_BLOCK)
    ys = _ffn((p_start // FFN_BLOCK).astype(jnp.int32), (padded // FFN_BLOCK).astype(jnp.int32), n_used, buf,
              w_expert_gate[i], w_expert_up[i], w_expert_down[i])
    out = _combine(dest, ys, gate, x1.reshape(t, D), mod3, ln2_g[i], ln2_b[i], alpha, seq)
    return out.reshape(bsz, seq, D)
```
